```python
import math
import jax, jax.numpy as jnp
from jax import lax
import numpy as np

D_MODEL = 2048
BATCH = 1
SEQ = 8192
DEPTH = 1

N_META = 16
D_FF = 5632
EPS = 1e-6
HG_HEADS = 12
HG_DK = 128
HG_DV = 128
HG_K = HG_HEADS * HG_DK
HG_WIDTH = HG_HEADS * HG_DV
HG_CHUNK = 64
S5_GROUP = 16
S5_GROUPS = 32
S5_WIDTH = S5_GROUP * S5_GROUPS
S5_STATE = 64
DT_MIN = 1e-3
DT_MAX = 1e-1
IN_COLS = 3 * HG_K + HG_WIDTH + S5_WIDTH + 2 * D_MODEL

kernel_name = "hybrid_hgrn2_s5_macaron_meta"


def _rms_norm(x, g):
    xf = x.astype(jnp.float32)
    y = xf * lax.rsqrt(jnp.mean(xf * xf, axis=-1, keepdims=True) + EPS)
    return (y * g.astype(jnp.float32)).astype(x.dtype)


def _swiglu(x, w_gate, w_up, w_down):
    return (jax.nn.silu(x @ w_gate) * (x @ w_up)) @ w_down


def _split_points():
    sizes = (HG_K, HG_K, HG_WIDTH, HG_WIDTH, S5_WIDTH, D_MODEL)
    return [int(v) for v in np.cumsum(sizes)]


def _hgrn2_chunkwise(q, logf, k, v):
    B, L, H, DK = q.shape
    DV = v.shape[-1]
    n = L // HG_CHUNK

    def to_chunks(a):
        return a.reshape(B, n, HG_CHUNK, H, a.shape[-1]).swapaxes(0, 1)

    causal = jnp.tril(jnp.ones((HG_CHUNK, HG_CHUNK), bool))[None, :, :, None, None]

    def step(S, inp):
        qi, gi, ki, vi = inp
        b = jnp.cumsum(gi, axis=1)
        b_last = b[:, -1]
        o_inter = jnp.einsum('bthk,bhkv->bthv', qi * jnp.exp(b), S)
        diff = b[:, :, None] - b[:, None, :]
        decay = jnp.exp(jnp.where(causal, diff, -jnp.inf))
        attn = jnp.einsum('bthk,bshk,btshk->btsh', qi, ki, decay)
        o_intra = jnp.einsum('btsh,bshv->bthv', attn, vi)
        k_dec = ki * jnp.exp(b_last[:, None] - b)
        S_new = jnp.exp(b_last)[..., None] * S + jnp.einsum('bchk,bchv->bhkv', k_dec, vi)
        return S_new, o_inter + o_intra

    S0 = jnp.zeros((B, H, DK, DV), jnp.float32)
    _, o = lax.scan(step, S0, tuple(map(to_chunks, (q, logf, k, v))))
    return o.swapaxes(0, 1).reshape(B, L, H, DV)


def _s5_scan(u, lam_re, lam_im, log_step, b_re, b_im, c_re, c_im, d_skip):
    f32 = jnp.float32
    lr, li = lam_re.astype(f32), lam_im.astype(f32)
    dt = jnp.exp(log_step.astype(f32))[:, None]
    mag = jnp.exp(lr * dt)
    abar_re, abar_im = mag * jnp.cos(li * dt), mag * jnp.sin(li * dt)
    num_re, num_im = abar_re - 1.0, abar_im
    den = lr * lr + li * li
    coef_re = (num_re * lr + num_im * li) / den
    coef_im = (num_im * lr - num_re * li) / den
    br, bi = b_re.astype(f32), b_im.astype(f32)
    bbar_re = coef_re[..., None] * br - coef_im[..., None] * bi
    bbar_im = coef_re[..., None] * bi + coef_im[..., None] * br
    bu_re = jnp.einsum('blgp,gnp->blgn', u, bbar_re)
    bu_im = jnp.einsum('blgp,gnp->blgn', u, bbar_im)
    a_re = jnp.broadcast_to(abar_re, bu_re.shape)
    a_im = jnp.broadcast_to(abar_im, bu_im.shape)

    def combine(e1, e2):
        ar1, ai1, br1, bi1 = e1
        ar2, ai2, br2, bi2 = e2
        return (ar2 * ar1 - ai2 * ai1,
                ar2 * ai1 + ai2 * ar1,
                ar2 * br1 - ai2 * bi1 + br2,
                ar2 * bi1 + ai2 * br1 + bi2)

    _, _, xr, xi = lax.associative_scan(combine, (a_re, a_im, bu_re, bu_im), axis=1)
    y = (jnp.einsum('gpn,blgn->blgp', c_re.astype(f32), xr)
         - jnp.einsum('gpn,blgn->blgp', c_im.astype(f32), xi))
    return y + d_skip.astype(f32) * u


def _mixer(u, lb, w_in, hg_norm, hg_w_proj, s5_lam_re, s5_lam_im, s5_log_step,
           s5_b_re, s5_b_im, s5_c_re, s5_c_im, s5_d, s5_w_glu_a, s5_w_glu_b, w_out):
    f32 = jnp.float32
    B, L, _ = u.shape
    proj = u @ w_in
    q, f_logit, i_in, g_out, s5_u, gate_hg, gate_s5 = jnp.split(proj, _split_points(), axis=-1)

    qh = (jax.nn.silu(q.astype(f32)) * HG_DK ** -0.5).reshape(B, L, HG_HEADS, HG_DK)
    f = lb + (1.0 - lb) * jax.nn.sigmoid(f_logit.astype(f32))
    logf = jnp.log(f).reshape(B, L, HG_HEADS, HG_DK)
    kh = (1.0 - f).reshape(B, L, HG_HEADS, HG_DK)
    vh = i_in.astype(f32).reshape(B, L, HG_HEADS, HG_DV)
    pad = HG_CHUNK - N_META
    padw = ((0, 0), (pad, 0), (0, 0), (0, 0))
    o = _hgrn2_chunkwise(jnp.pad(qh, padw), jnp.pad(logf, padw),
                         jnp.pad(kh, padw), jnp.pad(vh, padw))[:, pad:]
    o = o * lax.rsqrt(jnp.mean(o * o, axis=-1, keepdims=True) + EPS)
    o = o.reshape(B, L, HG_WIDTH) * hg_norm.astype(f32) * jax.nn.silu(g_out.astype(f32))
    y_hg = o.astype(u.dtype) @ hg_w_proj

    us = s5_u.astype(f32).reshape(B, L, S5_GROUPS, S5_GROUP)
    ys = _s5_scan(us, s5_lam_re, s5_lam_im, s5_log_step, s5_b_re, s5_b_im,
                  s5_c_re, s5_c_im, s5_d)
    z = jax.nn.gelu(ys.reshape(B, L, S5_WIDTH)).astype(u.dtype)
    y_s5 = (z @ s5_w_glu_a) * jax.nn.sigmoid(z @ s5_w_glu_b)

    merged = jax.nn.sigmoid(gate_hg) * y_hg + jax.nn.sigmoid(gate_s5) * y_s5
    return merged @ w_out


def setup_inputs(seed: int = 0) -> dict:
    key = jax.random.key(seed)
    ks = jax.random.split(key, 32)
    f32 = jnp.float32

    def nrm(k, shape, scale):
        return jax.random.normal(k, shape, f32) * scale

    def gain(k, shape):
        return 1.0 + 0.02 * jax.random.normal(k, shape, f32)

    Ld = DEPTH
    n_idx = jnp.arange(S5_STATE, dtype=f32)
    return {
        "x": nrm(ks[0], (BATCH, SEQ, D_MODEL), 1.0),
        "meta_tokens": nrm(ks[1], (N_META, D_MODEL), 1.0),
        "norm_ffn1": gain(ks[2], (Ld, D_MODEL)),
        "ffn1_w_gate": nrm(ks[3], (Ld, D_MODEL, D_FF), D_MODEL ** -0.5),
        "ffn1_w_up": nrm(ks[4], (Ld, D_MODEL, D_FF), D_MODEL ** -0.5),
        "ffn1_w_down": nrm(ks[5], (Ld, D_FF, D_MODEL), D_FF ** -0.5),
        "norm_mix": gain(ks[6], (Ld, D_MODEL)),
        "w_in": nrm(ks[7], (Ld, D_MODEL, IN_COLS), D_MODEL ** -0.5),
        "hg_lower_bounds": nrm(ks[8], (Ld + 1, HG_K), 0.1),
        "hg_norm": gain(ks[9], (Ld, HG_WIDTH)),
        "hg_w_proj": nrm(ks[10], (Ld, HG_WIDTH, D_MODEL), HG_WIDTH ** -0.5),
        "s5_lam_re": -0.5 + nrm(ks[11], (Ld, S5_GROUPS, S5_STATE), 0.01),
        "s5_lam_im": math.pi * n_idx + nrm(ks[12], (Ld, S5_GROUPS, S5_STATE), 0.01),
        "s5_log_step": jax.random.uniform(ks[13], (Ld, S5_GROUPS), f32,
                                          math.log(DT_MIN), math.log(DT_MAX)),
        "s5_b_re": nrm(ks[14], (Ld, S5_GROUPS, S5_STATE, S5_GROUP), (2 * S5_GROUP) ** -0.5),
        "s5_b_im": nrm(ks[15], (Ld, S5_GROUPS, S5_STATE, S5_GROUP), (2 * S5_GROUP) ** -0.5),
        "s5_c_re": nrm(ks[16], (Ld, S5_GROUPS, S5_GROUP, S5_STATE), (2 * S5_STATE) ** -0.5),
        "s5_c_im": nrm(ks[17], (Ld, S5_GROUPS, S5_GROUP, S5_STATE), (2 * S5_STATE) ** -0.5),
        "s5_d": nrm(ks[18], (Ld, S5_GROUPS, S5_GROUP), 1.0),
        "s5_w_glu_a": nrm(ks[19], (Ld, S5_WIDTH, D_MODEL), S5_WIDTH ** -0.5),
        "s5_w_glu_b": nrm(ks[20], (Ld, S5_WIDTH, D_MODEL), S5_WIDTH ** -0.5),
        "w_out": nrm(ks[21], (Ld, D_MODEL, D_MODEL), D_MODEL ** -0.5),
        "norm_ffn2": gain(ks[22], (Ld, D_MODEL)),
        "ffn2_w_gate": nrm(ks[23], (Ld, D_MODEL, D_FF), D_MODEL ** -0.5),
        "ffn2_w_up": nrm(ks[24], (Ld, D_MODEL, D_FF), D_MODEL ** -0.5),
        "ffn2_w_down": nrm(ks[25], (Ld, D_FF, D_MODEL), D_FF ** -0.5),
        "norm_final": gain(ks[26], (D_MODEL,)),
    }


def reference(x, meta_tokens, norm_ffn1, ffn1_w_gate, ffn1_w_up, ffn1_w_down, norm_mix,
              w_in, hg_lower_bounds, hg_norm, hg_w_proj, s5_lam_re, s5_lam_im, s5_log_step,
              s5_b_re, s5_b_im, s5_c_re, s5_c_im, s5_d, s5_w_glu_a, s5_w_glu_b, w_out,
              norm_ffn2, ffn2_w_gate, ffn2_w_up, ffn2_w_down, norm_final):
    B = x.shape[0]
    meta = jnp.broadcast_to(meta_tokens.astype(x.dtype)[None], (B, N_META, D_MODEL))
    h = jnp.concatenate([meta, x], axis=1)
    lbs = jnp.cumsum(jax.nn.softmax(hg_lower_bounds.astype(jnp.float32), axis=0), axis=0)
    for l in range(DEPTH):
        h = h + 0.5 * _swiglu(_rms_norm(h, norm_ffn1[l]), ffn1_w_gate[l], ffn1_w_up[l], ffn1_w_down[l])
        mix = _mixer(_rms_norm(h, norm_mix[l]), lbs[l], w_in[l], hg_norm[l], hg_w_proj[l],
                     s5_lam_re[l], s5_lam_im[l], s5_log_step[l], s5_b_re[l], s5_b_im[l],
                     s5_c_re[l], s5_c_im[l], s5_d[l], s5_w_glu_a[l], s5_w_glu_b[l], w_out[l])
        h = h + mix.astype(h.dtype)
        h = h + 0.5 * _swiglu(_rms_norm(h, norm_ffn2[l]), ffn2_w_gate[l], ffn2_w_up[l], ffn2_w_down[l])
    return _rms_norm(h, norm_final)[:, N_META:]
```

```python
import functools
import math

import jax
import jax.numpy as jnp
from jax import lax
from jax.experimental import pallas as pl
from jax.experimental.pallas import tpu as pltpu

EPS = 1e-6
HG_DK = 128
HG_CHUNK = 64
HG_SUB = 16
S5_GROUP = 16
S5_STATE = 64
S5_T = 16
S5_SG = 8
LANE = 128
VMEM_LIMIT = 56 * 1024 * 1024

F32 = jnp.float32
BF16 = jnp.bfloat16


def _cparams(sem):
    return pltpu.CompilerParams(dimension_semantics=sem, vmem_limit_bytes=VMEM_LIMIT)


def _rms(v, g):
    return v * lax.rsqrt(jnp.mean(v * v, axis=-1, keepdims=True) + EPS) * g


def _dot(a, b):
    return jnp.dot(a, b, preferred_element_type=F32)


def _dot_nt(a, b):
    return lax.dot_general(a, b, (((1,), (1,)), ((), ())), preferred_element_type=F32)


def _dot_tn(a, b):
    return lax.dot_general(a, b, (((0,), (0,)), ((), ())), preferred_element_type=F32)


def _ffn_kernel(*refs, tm, n_meta, final_norm):
    if n_meta:
        x_ref, m_ref, g_ref, wg_ref, wu_ref, wd_ref, ox_ref, om_ref, hn_ref, acc_ref = refs
    else:
        x_ref, g_ref, gf_ref, wg_ref, wu_ref, wd_ref, ox_ref, hn_ref, acc_ref = refs
    j = pl.program_id(1)

    @pl.when(j == 0)
    def _():
        hn_ref[0:tm, :] = _rms(x_ref[...], g_ref[...]).astype(BF16)
        if n_meta:
            hn_ref[tm:tm + n_meta, :] = _rms(m_ref[...], g_ref[...]).astype(BF16)
        acc_ref[...] = jnp.zeros_like(acc_ref)

    hn = hn_ref[...]
    gate = _dot(hn, wg_ref[...])
    up = _dot(hn, wu_ref[...])
    act = (gate * jax.nn.sigmoid(gate) * up).astype(BF16)
    acc_ref[...] += _dot(act, wd_ref[...])

    @pl.when(j == pl.num_programs(1) - 1)
    def _():
        hx = x_ref[...] + 0.5 * acc_ref[0:tm, :]
        if final_norm:
            hx = _rms(hx, gf_ref[...])
        ox_ref[...] = hx
        if n_meta:
            om_ref[...] = m_ref[...] + 0.5 * acc_ref[tm:tm + n_meta, :]


def _ffn(hx, hm, g, wg, wu, wd, g_final=None, *, tm=512, tf=512):
    n, d = hx.shape
    dff = wg.shape[1]
    n_meta = 0 if hm is None else hm.shape[0]
    grid = (n // tm, dff // tf)
    row = lambda i, j: (i, 0)
    const = lambda i, j: (0, 0)
    in_specs = [pl.BlockSpec((tm, d), row)]
    args = [hx]
    if n_meta:
        in_specs.append(pl.BlockSpec((n_meta, d), const))
        args.append(hm)
    in_specs.append(pl.BlockSpec((1, d), const))
    args.append(g.reshape(1, d))
    if g_final is not None:
        in_specs.append(pl.BlockSpec((1, d), const))
        args.append(g_final.reshape(1, d))
    in_specs += [pl.BlockSpec((d, tf), lambda i, j: (0, j)),
                 pl.BlockSpec((d, tf), lambda i, j: (0, j)),
                 pl.BlockSpec((tf, d), lambda i, j: (j, 0))]
    args += [wg, wu, wd]
    out_shape = [jax.ShapeDtypeStruct((n, d), F32)]
    out_specs = [pl.BlockSpec((tm, d), row)]
    if n_meta:
        out_shape.append(jax.ShapeDtypeStruct((n_meta, d), F32))
        out_specs.append(pl.BlockSpec((n_meta, d), const))
    outs = pl.pallas_call(
        functools.partial(_ffn_kernel, tm=tm, n_meta=n_meta, final_norm=g_final is not None),
        grid=grid, in_specs=in_specs, out_specs=out_specs, out_shape=out_shape,
        scratch_shapes=[pltpu.VMEM((tm + n_meta, d), BF16), pltpu.VMEM((tm + n_meta, d), F32)],
        compiler_params=_cparams(("arbitrary", "arbitrary")),
        name="ffn_meta" if n_meta else "ffn_final",
    )(*args)
    return outs if n_meta else outs[0]


def _inproj_kernel(x_ref, m_ref, g_ref, w_ref, lb_ref,
                   q_ref, qm_ref, f_ref, fm_ref, v_ref, vm_ref, go_ref, u_ref, um_ref,
                   gh_ref, gs_ref, un_ref, *, tm, n_meta, bounds, q_scale):
    j = pl.program_id(1)

    @pl.when(j == 0)
    def _():
        un_ref[0:tm, :] = _rms(x_ref[...], g_ref[...]).astype(BF16)
        un_ref[tm:tm + n_meta, :] = _rms(m_ref[...], g_ref[...]).astype(BF16)

    res = _dot(un_ref[...], w_ref[...])
    rx = res[0:tm, :]
    rm = res[tm:tm + n_meta, :]
    b_q, b_f, b_i, b_g, b_u, b_gh = bounds

    @pl.when(j < b_q)
    def _():
        q_ref[...] = rx * jax.nn.sigmoid(rx) * q_scale
        qm_ref[...] = rm * jax.nn.sigmoid(rm) * q_scale

    @pl.when((j >= b_q) & (j < b_f))
    def _():
        lbl = lb_ref[...]
        e = jnp.exp(lbl - jnp.max(lbl, axis=0, keepdims=True))
        lb = e[0:1, :] / jnp.sum(e, axis=0, keepdims=True)
        f_ref[...] = lb + (1.0 - lb) * jax.nn.sigmoid(rx)
        fm_ref[...] = lb + (1.0 - lb) * jax.nn.sigmoid(rm)

    @pl.when((j >= b_f) & (j < b_i))
    def _():
        v_ref[...] = rx.astype(BF16)
        vm_ref[...] = rm.astype(BF16)

    @pl.when((j >= b_i) & (j < b_g))
    def _():
        go_ref[...] = (rx * jax.nn.sigmoid(rx)).astype(BF16)

    @pl.when((j >= b_g) & (j < b_u))
    def _():
        u_ref[...] = rx
        um_ref[...] = rm

    @pl.when((j >= b_u) & (j < b_gh))
    def _():
        gh_ref[...] = jax.nn.sigmoid(rx).astype(BF16)

    @pl.when(j >= b_gh)
    def _():
        gs_ref[...] = jax.nn.sigmoid(rx).astype(BF16)


def _inproj(hx, hm, g, w_in, lower_bounds, hg_k, hg_w, s5_w, *, tm=512):
    n, d = hx.shape
    n_meta = hm.shape[0]
    n_cols = w_in.shape[1]
    sizes = (hg_k, hg_k, hg_w, hg_w, s5_w, d, d)
    tn = functools.reduce(math.gcd, sizes + (512,))
    assert sum(sizes) == n_cols and tn % LANE == 0
    starts = [0]
    for s in sizes:
        starts.append(starts[-1] + s // tn)
    bounds = tuple(starts[1:7])
    grid = (n // tm, n_cols // tn)

    def seg(k):
        lo, hi = starts[k], starts[k + 1]
        return lambda i, j: (i, jnp.clip(j - lo, 0, hi - lo - 1))

    def seg_m(k):
        lo, hi = starts[k], starts[k + 1]
        return lambda i, j: (i, 0, jnp.clip(j - lo, 0, hi - lo - 1))

    def xo(k, dt):
        return jax.ShapeDtypeStruct((n, sizes[k]), dt), pl.BlockSpec((tm, tn), seg(k))

    def mo(k, dt):
        return (jax.ShapeDtypeStruct((n // tm, n_meta, sizes[k]), dt),
                pl.BlockSpec((pl.Squeezed(), n_meta, tn), seg_m(k)))

    outs = [xo(0, F32), mo(0, F32), xo(1, F32), mo(1, F32), xo(2, BF16), mo(2, BF16),
            xo(3, BF16), xo(4, F32), mo(4, F32), xo(5, BF16), xo(6, BF16)]
    nlb = lower_bounds.shape[0]
    lo_f, hi_f = starts[1], starts[2]
    q, qm, f, fm, v, vm, go, u, um, gh, gs = pl.pallas_call(
        functools.partial(_inproj_kernel, tm=tm, n_meta=n_meta, bounds=bounds,
                          q_scale=HG_DK ** -0.5),
        grid=grid,
        in_specs=[pl.BlockSpec((tm, d), lambda i, j: (i, 0)),
                  pl.BlockSpec((n_meta, d), lambda i, j: (0, 0)),
                  pl.BlockSpec((1, d), lambda i, j: (0, 0)),
                  pl.BlockSpec((d, tn), lambda i, j: (0, j)),
                  pl.BlockSpec((nlb, tn), lambda i, j: (0, jnp.clip(j - lo_f, 0, hi_f - lo_f - 1)))],
        out_specs=[o[1] for o in outs], out_shape=[o[0] for o in outs],
        scratch_shapes=[pltpu.VMEM((tm + n_meta, d), BF16)],
        compiler_params=_cparams(("arbitrary", "arbitrary")),
        name="mixer_inproj",
    )(hx, hm, g.reshape(1, d), w_in, lower_bounds)
    return q, qm[0], f, fm[0], v, vm[0], go, u, um[0], gh, gs


def _cumsum_rows(x, tri):
    hi = x.astype(BF16)
    r1 = x - hi.astype(F32)
    mid = r1.astype(BF16)
    lo = (r1 - mid.astype(F32)).astype(BF16)
    return _dot(tri, hi) + _dot(tri, mid) + _dot(tri, lo)


def _tri(n):
    r = lax.broadcasted_iota(jnp.int32, (n, n), 0)
    c = lax.broadcasted_iota(jnp.int32, (n, n), 1)
    return jnp.where(r >= c, 1.0, 0.0).astype(BF16)


def _diag_block_t(q, k, b, lane0, width):
    n = q.shape[0]
    s_idx = lax.broadcasted_iota(jnp.int32, (n, 1), 0)
    lane = lax.broadcasted_iota(jnp.int32, (n, width), 1)
    out = jnp.zeros((n, width), F32)
    for t in range(n):
        diff = jnp.where(s_idx <= t, b[t:t + 1, :] - b, -jnp.inf)
        p = (q[t:t + 1, :] * k) * jnp.exp(diff)
        col = jnp.sum(p, axis=-1, keepdims=True)
        out = jnp.where(lane == lane0 + t, col, out)
    return out


def _hgrn_kernel(q_ref, f_ref, v_ref, go_ref, qm_ref, fm_ref, vm_ref, gn_ref, o_ref, st_ref,
                 *, n_chunks, n_meta):
    i = pl.program_id(0)
    h = pl.program_id(1)
    C, SB = HG_CHUNK, HG_SUB

    @pl.when(i == 0)
    def _():
        fm = fm_ref[...]
        bm = _cumsum_rows(jnp.log(fm), _tri(n_meta))
        kdec = ((1.0 - fm) * jnp.exp(bm[n_meta - 1:n_meta, :] - bm)).astype(BF16)
        st_ref[h] = _dot_tn(vm_ref[...], kdec)

    tri = _tri(C)
    row = lax.broadcasted_iota(jnp.int32, (C, 1), 0)

    def chunk(c, carry):
        r0 = pl.multiple_of(c * C, C)
        q = q_ref[pl.ds(r0, C), :]
        f = f_ref[pl.ds(r0, C), :]
        v = v_ref[pl.ds(r0, C), :]
        k = 1.0 - f
        b = _cumsum_rows(jnp.log(f), tri)
        b_last = b[C - 1:C, :]
        st = st_ref[h]
        o = _dot_nt((q * jnp.exp(b)).astype(BF16), st.astype(BF16))
        diag = []
        at = None
        for blk in range(C // SB):
            lo = blk * SB
            diag.append(_diag_block_t(q[lo:lo + SB], k[lo:lo + SB], b[lo:lo + SB], lo, C))
            if blk:
                b_ref = b[lo - 1:lo, :]
                qt = q * jnp.exp(jnp.where((row >= lo) & (row < lo + SB), b - b_ref, -jnp.inf))
                kt = k * jnp.exp(jnp.where(row < lo, b_ref - b, -jnp.inf))
                part = _dot_nt(kt.astype(BF16), qt.astype(BF16))
                at = part if at is None else at + part
        at = at + jnp.concatenate(diag, axis=0)
        o = o + _dot_tn(at.astype(BF16), v)
        kdec = (k * jnp.exp(b_last - b)).astype(BF16)
        st_ref[h] = st * jnp.exp(b_last) + _dot_tn(v, kdec)
        on = o * lax.rsqrt(jnp.mean(o * o, axis=-1, keepdims=True) + EPS)
        on = on * gn_ref[...] * go_ref[pl.ds(r0, C), :].astype(F32)
        o_ref[pl.ds(r0, C), :] = on.astype(BF16)
        return carry

    lax.fori_loop(0, n_chunks, chunk, 0)


def _hgrn(q, f, v, go, qm, fm, vm, hg_norm, *, rb=512):
    n, width = q.shape
    heads = width // HG_DK
    n_meta = qm.shape[0]
    blk = lambda i, h: (i, h)
    mblk = lambda i, h: (0, h)
    return pl.pallas_call(
        functools.partial(_hgrn_kernel, n_chunks=rb // HG_CHUNK, n_meta=n_meta),
        grid=(n // rb, heads),
        in_specs=[pl.BlockSpec((rb, HG_DK), blk)] * 4
        + [pl.BlockSpec((n_meta, HG_DK), mblk)] * 3
        + [pl.BlockSpec((1, HG_DK), mblk)],
        out_specs=pl.BlockSpec((rb, HG_DK), blk),
        out_shape=jax.ShapeDtypeStruct((n, width), BF16),
        scratch_shapes=[pltpu.VMEM((heads, HG_DK, HG_DK), F32)],
        compiler_params=_cparams(("arbitrary", "arbitrary")),
        name="hgrn2",
    )(q, f, v, go, qm, fm, vm, hg_norm.reshape(1, width))


def _s5_weights_kernel(lr_c_ref, li_c_ref, dt_c_ref, lr_r_ref, li_r_ref, dt_r_ref,
                       bre_ref, bim_ref, cre_ref, cim_ref, dsk_ref,
                       m_ref, win_ref, wout_ref, at_ref):
    T = S5_T
    ns = lr_r_ref.shape[1]
    nc = cre_ref.shape[1]

    def powers(lr, li, dt, tau):
        mag = jnp.exp(lr * dt * tau)
        ang = li * dt * tau
        return mag * jnp.cos(ang), mag * jnp.sin(ang)

    lr, li, dt = lr_r_ref[...], li_r_ref[...], dt_r_ref[...]
    a_re, a_im = powers(lr, li, dt, 1.0)
    num_re, num_im = a_re - 1.0, a_im
    den = lr * lr + li * li
    coef_re = (num_re * lr + num_im * li) / den
    coef_im = (num_im * lr - num_re * li) / den
    br, bi = bre_ref[...], bim_ref[...]
    bbar_re = coef_re * br - coef_im * bi
    bbar_im = coef_re * bi + coef_im * br

    for s in range(T):
        p_re, p_im = powers(lr, li, dt, float(T - 1 - s))
        win_ref[s * nc:(s + 1) * nc, 0:ns] = (bbar_re * p_re - bbar_im * p_im).astype(BF16)
        win_ref[s * nc:(s + 1) * nc, ns:2 * ns] = (bbar_re * p_im + bbar_im * p_re).astype(BF16)
    t_re, t_im = powers(lr, li, dt, float(T))
    at_ref[:, 0:ns] = t_re
    at_ref[:, ns:2 * ns] = t_im

    lrc, lic, dtc = lr_c_ref[...], li_c_ref[...], dt_c_ref[...]
    cr, ci = cre_ref[...], cim_ref[...]
    hp = lax.Precision.HIGHEST
    zero = jnp.zeros((nc, nc), BF16)
    for tau in range(T + 1):
        p_re, p_im = powers(lrc, lic, dtc, float(tau))
        ca_re = cr * p_re - ci * p_im
        ca_im = cr * p_im + ci * p_re
        if tau >= 1:
            t = tau - 1
            wout_ref[0:ns, t * nc:(t + 1) * nc] = ca_re.astype(BF16)
            wout_ref[ns:2 * ns, t * nc:(t + 1) * nc] = (-ca_im).astype(BF16)
        if tau < T:
            kt = (jnp.dot(bbar_re, ca_re, precision=hp, preferred_element_type=F32)
                  - jnp.dot(bbar_im, ca_im, precision=hp, preferred_element_type=F32))
            if tau == 0:
                kt = kt + dsk_ref[...]
            kt = kt.astype(BF16)
            for s in range(T - tau):
                m_ref[s * nc:(s + 1) * nc, (s + tau) * nc:(s + tau + 1) * nc] = kt
            if tau:
                for s in range(T - tau):
                    m_ref[(s + tau) * nc:(s + tau + 1) * nc, s * nc:(s + 1) * nc] = zero


def _s5_weights(lam_re, lam_im, log_step, b_re, b_im, c_re, c_im, d_skip):
    G, N = lam_re.shape
    P = d_skip.shape[1]
    nsg = G // S5_SG
    ns, nc = S5_SG * N, S5_SG * P
    T = S5_T
    dt = jnp.exp(log_step.astype(F32))
    eye = jnp.eye(S5_SG, dtype=F32)

    def rows(a):
        return a.astype(F32).reshape(nsg, 1, ns)

    def cols(a):
        return jnp.broadcast_to(a.astype(F32).reshape(nsg, ns, 1), (nsg, ns, nc))

    def embed_b(b):
        b = b.astype(F32).reshape(nsg, S5_SG, N, P)
        return jnp.einsum('zgnq,gh->zhqgn', b, eye).reshape(nsg, nc, ns)

    def embed_c(c):
        c = c.astype(F32).reshape(nsg, S5_SG, P, N)
        return jnp.einsum('zgpn,gh->zgnhp', c, eye).reshape(nsg, ns, nc)

    dtb = jnp.broadcast_to(dt[:, None], (G, N))
    d_diag = jax.vmap(jnp.diag)(d_skip.astype(F32).reshape(nsg, nc))
    sq = pl.Squeezed()
    spec_r = pl.BlockSpec((sq, 1, ns), lambda z: (z, 0, 0))
    spec_c = pl.BlockSpec((sq, ns, nc), lambda z: (z, 0, 0))
    spec_b = pl.BlockSpec((sq, nc, ns), lambda z: (z, 0, 0))
    return pl.pallas_call(
        _s5_weights_kernel,
        grid=(nsg,),
        in_specs=[spec_c, spec_c, spec_c, spec_r, spec_r, spec_r, spec_b, spec_b, spec_c, spec_c,
                  pl.BlockSpec((sq, nc, nc), lambda z: (z, 0, 0))],
        out_specs=[pl.BlockSpec((sq, T * nc, T * nc), lambda z: (z, 0, 0)),
                   pl.BlockSpec((sq, T * nc, 2 * ns), lambda z: (z, 0, 0)),
                   pl.BlockSpec((sq, 2 * ns, T * nc), lambda z: (z, 0, 0)),
                   pl.BlockSpec((sq, 1, 2 * ns), lambda z: (z, 0, 0))],
        out_shape=[jax.ShapeDtypeStruct((nsg, T * nc, T * nc), BF16),
                   jax.ShapeDtypeStruct((nsg, T * nc, 2 * ns), BF16),
                   jax.ShapeDtypeStruct((nsg, 2 * ns, T * nc), BF16),
                   jax.ShapeDtypeStruct((nsg, 1, 2 * ns), F32)],
        compiler_params=_cparams(("arbitrary",)),
        name="s5_weights",
    )(cols(lam_re), cols(lam_im), cols(dtb), rows(lam_re), rows(lam_im), rows(dtb),
      embed_b(b_re), embed_b(b_im), embed_c(c_re), embed_c(c_im), d_diag)


def _s5_scan_kernel(u_ref, um_ref, m_ref, win_ref, wout_ref, at_ref, y_ref,
                    uc_ref, v_ref, xp_ref, *, n_chunks):
    T = S5_T
    nc = u_ref.shape[1]
    ns = at_ref.shape[1] // 2
    for t in range(T):
        uc_ref[0:n_chunks, t * nc:(t + 1) * nc] = u_ref[pl.ds(t, n_chunks, stride=T), :].astype(BF16)
        uc_ref[n_chunks:n_chunks + 1, t * nc:(t + 1) * nc] = um_ref[t:t + 1, :].astype(BF16)
    pad = uc_ref.shape[0] - n_chunks - 1
    uc_ref[n_chunks + 1:, :] = jnp.zeros((pad, T * nc), BF16)
    uc = uc_ref[...]
    v_ref[...] = _dot(uc, win_ref[...])
    a_re, a_im = at_ref[:, 0:ns], at_ref[:, ns:2 * ns]

    x_re0 = v_ref[n_chunks:n_chunks + 1, 0:ns]
    x_im0 = v_ref[n_chunks:n_chunks + 1, ns:2 * ns]
    xp_ref[n_chunks:, :] = jnp.zeros((pad + 1, 2 * ns), F32)

    def step(c, carry):
        x_re, x_im = carry
        xp_ref[pl.ds(c, 1), 0:ns] = x_re
        xp_ref[pl.ds(c, 1), ns:2 * ns] = x_im
        v_re = v_ref[pl.ds(c, 1), 0:ns]
        v_im = v_ref[pl.ds(c, 1), ns:2 * ns]
        return (a_re * x_re - a_im * x_im + v_re, a_re * x_im + a_im * x_re + v_im)

    lax.fori_loop(0, n_chunks, step, (x_re0, x_im0))
    yc = _dot(uc, m_ref[...]) + _dot(xp_ref[...].astype(BF16), wout_ref[...])
    for t in range(T):
        y_ref[pl.ds(t, n_chunks, stride=T), :] = yc[0:n_chunks, t * nc:(t + 1) * nc]


def _s5_scan(u, um, m, win, wout, at):
    n, width = u.shape
    T = S5_T
    nsg = m.shape[0]
    nc = width // nsg
    ns2 = at.shape[2]
    assert um.shape[0] == T and n % T == 0
    n_chunks = n // T
    rows = -(-(n_chunks + 1) // 16) * 16
    sq = pl.Squeezed()
    one = pl.Buffered(1)
    return pl.pallas_call(
        functools.partial(_s5_scan_kernel, n_chunks=n_chunks),
        grid=(nsg,),
        in_specs=[pl.BlockSpec((n, nc), lambda z: (0, z)),
                  pl.BlockSpec((T, nc), lambda z: (0, z)),
                  pl.BlockSpec((sq, T * nc, T * nc), lambda z: (z, 0, 0), pipeline_mode=one),
                  pl.BlockSpec((sq, T * nc, ns2), lambda z: (z, 0, 0), pipeline_mode=one),
                  pl.BlockSpec((sq, ns2, T * nc), lambda z: (z, 0, 0), pipeline_mode=one),
                  pl.BlockSpec((sq, 1, ns2), lambda z: (z, 0, 0))],
        out_specs=pl.BlockSpec((n, nc), lambda z: (0, z)),
        out_shape=jax.ShapeDtypeStruct((n, width), F32),
        scratch_shapes=[pltpu.VMEM((rows, T * nc), BF16), pltpu.VMEM((rows, ns2), F32),
                        pltpu.VMEM((rows, ns2), F32)],
        compiler_params=_cparams(("arbitrary",)),
        name="s5_scan",
    )(u, um, m, win, wout, at)


def _merge_kernel(h_ref, o_ref, y_ref, gh_ref, gs_ref, wp_ref, wa_ref, wb_ref, wo_ref, out_ref):
    y_hg = _dot(o_ref[...], wp_ref[...])
    z = jax.nn.gelu(y_ref[...]).astype(BF16)
    y_s5 = _dot(z, wa_ref[...]) * jax.nn.sigmoid(_dot(z, wb_ref[...]))
    merged = gh_ref[...].astype(F32) * y_hg + gs_ref[...].astype(F32) * y_s5
    out_ref[...] = h_ref[...] + _dot(merged.astype(BF16), wo_ref[...])


def _merge(hx, o, y, gh, gs, wp, wa, wb, wo, *, tm=256):
    n, d = hx.shape
    row = lambda i: (i, 0)
    const = lambda i: (0, 0)

    def resident(shape):
        return pl.BlockSpec(shape, const, pipeline_mode=pl.Buffered(1))

    return pl.pallas_call(
        _merge_kernel,
        grid=(n // tm,),
        in_specs=[pl.BlockSpec((tm, d), row), pl.BlockSpec((tm, o.shape[1]), row),
                  pl.BlockSpec((tm, y.shape[1]), row), pl.BlockSpec((tm, d), row),
                  pl.BlockSpec((tm, d), row),
                  resident(wp.shape), resident(wa.shape), resident(wb.shape), resident(wo.shape)],
        out_specs=pl.BlockSpec((tm, d), row),
        out_shape=jax.ShapeDtypeStruct((n, d), F32),
        compiler_params=_cparams(("arbitrary",)),
        name="mixer_merge",
    )(hx, o, y, gh, gs, wp, wa, wb, wo)


def kernel(x, meta_tokens, norm_ffn1, ffn1_w_gate, ffn1_w_up, ffn1_w_down, norm_mix, w_in, hg_lower_bounds, hg_norm, hg_w_proj, s5_lam_re, s5_lam_im, s5_log_step, s5_b_re, s5_b_im, s5_c_re, s5_c_im, s5_d, s5_w_glu_a, s5_w_glu_b, w_out, norm_ffn2, ffn2_w_gate, ffn2_w_up, ffn2_w_down, norm_final):
    batch, seq, d = x.shape
    depth = norm_ffn1.shape[0]
    assert batch == 1 and depth == 1
    hg_k = hg_lower_bounds.shape[1]
    hg_w = hg_norm.shape[1]
    s5_w = s5_w_glu_a.shape[1]
    bf = lambda w: w.astype(BF16)

    hx = x[0].astype(F32)
    hm = meta_tokens.astype(F32)
    hx, hm = _ffn(hx, hm, norm_ffn1[0], bf(ffn1_w_gate[0]), bf(ffn1_w_up[0]), bf(ffn1_w_down[0]))

    (q, qm, f, fm, v, vm, go, u, um, gh, gs) = _inproj(
        hx, hm, norm_mix[0], bf(w_in[0]), hg_lower_bounds, hg_k, hg_w, s5_w)
    o = _hgrn(q, f, v, go, qm, fm, vm, hg_norm[0])
    m, win, wout, at = _s5_weights(s5_lam_re[0], s5_lam_im[0], s5_log_step[0], s5_b_re[0],
                                   s5_b_im[0], s5_c_re[0], s5_c_im[0], s5_d[0])
    y = _s5_scan(u, um, m, win, wout, at)
    hx = _merge(hx, o, y, gh, gs, bf(hg_w_proj[0]), bf(s5_w_glu_a[0]), bf(s5_w_glu_b[0]),
                bf(w_out[0]))
    out = _ffn(hx, None, norm_ffn2[0], bf(ffn2_w_gate[0]), bf(ffn2_w_up[0]), bf(ffn2_w_down[0]),
               g_final=norm_final)
    return out[None].astype(x.dtype)
```

```python
import functools
import math

import numpy as np
import jax
import jax.numpy as jnp
from jax import lax
from jax.experimental import pallas as pl
from jax.experimental.pallas import tpu as pltpu

EPS = 1e-6
HG_DK = 128
HG_CHUNK = 64
S5_GROUP = 16
S5_STATE = 64
S5_T = 16
S5_SG = 8
LANE = 128
VMEM_LIMIT = 56 * 1024 * 1024

F32 = jnp.float32
BF16 = jnp.bfloat16


def _cparams(sem):
    return pltpu.CompilerParams(dimension_semantics=sem, vmem_limit_bytes=VMEM_LIMIT)


def _rms(v, g):
    return v * lax.rsqrt(jnp.mean(v * v, axis=-1, keepdims=True) + EPS) * g


def _dot(a, b):
    return jnp.dot(a, b, preferred_element_type=F32)


def _dot_nt(a, b):
    return lax.dot_general(a, b, (((1,), (1,)), ((), ())), preferred_element_type=F32)


def _dot_tn(a, b):
    return lax.dot_general(a, b, (((0,), (0,)), ((), ())), preferred_element_type=F32)


def _ffn_kernel(*refs, tm, n_meta, final_norm):
    if n_meta:
        x_ref, m_ref, g_ref, wg_ref, wu_ref, wd_ref, ox_ref, om_ref, hn_ref, acc_ref = refs
    else:
        x_ref, g_ref, gf_ref, wg_ref, wu_ref, wd_ref, ox_ref, hn_ref, acc_ref = refs
    j = pl.program_id(1)

    @pl.when(j == 0)
    def _():
        hn_ref[0:tm, :] = _rms(x_ref[...], g_ref[...]).astype(BF16)
        if n_meta:
            hn_ref[tm:tm + n_meta, :] = _rms(m_ref[...], g_ref[...]).astype(BF16)
        acc_ref[...] = jnp.zeros_like(acc_ref)

    hn = hn_ref[...]
    gate = _dot(hn, wg_ref[...])
    up = _dot(hn, wu_ref[...])
    act = (gate * jax.nn.sigmoid(gate) * up).astype(BF16)
    acc_ref[...] += _dot(act, wd_ref[...])

    @pl.when(j == pl.num_programs(1) - 1)
    def _():
        hx = x_ref[...] + 0.5 * acc_ref[0:tm, :]
        if final_norm:
            hx = _rms(hx, gf_ref[...])
        ox_ref[...] = hx
        if n_meta:
            om_ref[...] = m_ref[...] + 0.5 * acc_ref[tm:tm + n_meta, :]


def _ffn(hx, hm, g, wg, wu, wd, g_final=None, *, tm=512, tf=512):
    n, d = hx.shape
    dff = wg.shape[1]
    n_meta = 0 if hm is None else hm.shape[0]
    grid = (n // tm, dff // tf)
    row = lambda i, j: (i, 0)
    const = lambda i, j: (0, 0)
    in_specs = [pl.BlockSpec((tm, d), row)]
    args = [hx]
    if n_meta:
        in_specs.append(pl.BlockSpec((n_meta, d), const))
        args.append(hm)
    in_specs.append(pl.BlockSpec((1, d), const))
    args.append(g.reshape(1, d))
    if g_final is not None:
        in_specs.append(pl.BlockSpec((1, d), const))
        args.append(g_final.reshape(1, d))
    in_specs += [pl.BlockSpec((d, tf), lambda i, j: (0, j)),
                 pl.BlockSpec((d, tf), lambda i, j: (0, j)),
                 pl.BlockSpec((tf, d), lambda i, j: (j, 0))]
    args += [wg, wu, wd]
    out_shape = [jax.ShapeDtypeStruct((n, d), F32)]
    out_specs = [pl.BlockSpec((tm, d), row)]
    if n_meta:
        out_shape.append(jax.ShapeDtypeStruct((n_meta, d), F32))
        out_specs.append(pl.BlockSpec((n_meta, d), const))
    outs = pl.pallas_call(
        functools.partial(_ffn_kernel, tm=tm, n_meta=n_meta, final_norm=g_final is not None),
        grid=grid, in_specs=in_specs, out_specs=out_specs, out_shape=out_shape,
        scratch_shapes=[pltpu.VMEM((tm + n_meta, d), BF16), pltpu.VMEM((tm + n_meta, d), F32)],
        compiler_params=_cparams(("arbitrary", "arbitrary")),
        name="ffn_meta" if n_meta else "ffn_final",
    )(*args)
    return outs if n_meta else outs[0]


def _inproj_kernel(x_ref, m_ref, g_ref, w_ref, lb_ref,
                   q_ref, f_ref, fm_ref, v_ref, vm_ref, go_ref, u_ref, um_ref,
                   gh_ref, gs_ref, un_ref, *, tm, n_meta, bounds, q_scale):
    j = pl.program_id(1)

    @pl.when(j == 0)
    def _():
        un_ref[0:tm, :] = _rms(x_ref[...], g_ref[...]).astype(BF16)
        un_ref[tm:tm + n_meta, :] = _rms(m_ref[...], g_ref[...]).astype(BF16)

    res = _dot(un_ref[...], w_ref[...])
    rx = res[0:tm, :]
    rm = res[tm:tm + n_meta, :]
    b_q, b_f, b_i, b_g, b_u, b_gh = bounds

    @pl.when(j < b_q)
    def _():
        q_ref[...] = rx * jax.nn.sigmoid(rx) * q_scale

    @pl.when((j >= b_q) & (j < b_f))
    def _():
        lbl = lb_ref[...]
        e = jnp.exp(lbl - jnp.max(lbl, axis=0, keepdims=True))
        lb = e[0:1, :] / jnp.sum(e, axis=0, keepdims=True)
        f_ref[...] = lb + (1.0 - lb) * jax.nn.sigmoid(rx)
        fm_ref[...] = lb + (1.0 - lb) * jax.nn.sigmoid(rm)

    @pl.when((j >= b_f) & (j < b_i))
    def _():
        v_ref[...] = rx.astype(BF16)
        vm_ref[...] = rm.astype(BF16)

    @pl.when((j >= b_i) & (j < b_g))
    def _():
        go_ref[...] = (rx * jax.nn.sigmoid(rx)).astype(BF16)

    @pl.when((j >= b_g) & (j < b_u))
    def _():
        u_ref[...] = rx
        um_ref[...] = rm

    @pl.when((j >= b_u) & (j < b_gh))
    def _():
        gh_ref[...] = jax.nn.sigmoid(rx).astype(BF16)

    @pl.when(j >= b_gh)
    def _():
        gs_ref[...] = jax.nn.sigmoid(rx).astype(BF16)


def _inproj(hx, hm, g, w_in, lower_bounds, hg_k, hg_w, s5_w, *, tm=512):
    n, d = hx.shape
    n_meta = hm.shape[0]
    n_cols = w_in.shape[1]
    sizes = (hg_k, hg_k, hg_w, hg_w, s5_w, d, d)
    tn = functools.reduce(math.gcd, sizes + (512,))
    assert sum(sizes) == n_cols and tn % LANE == 0
    starts = [0]
    for s in sizes:
        starts.append(starts[-1] + s // tn)
    bounds = tuple(starts[1:7])
    grid = (n // tm, n_cols // tn)

    def seg(k):
        lo, hi = starts[k], starts[k + 1]
        return lambda i, j: (i, jnp.clip(j - lo, 0, hi - lo - 1))

    def seg_m(k):
        lo, hi = starts[k], starts[k + 1]
        return lambda i, j: (i, 0, jnp.clip(j - lo, 0, hi - lo - 1))

    def xo(k, dt):
        return jax.ShapeDtypeStruct((n, sizes[k]), dt), pl.BlockSpec((tm, tn), seg(k))

    def mo(k, dt):
        return (jax.ShapeDtypeStruct((n // tm, n_meta, sizes[k]), dt),
                pl.BlockSpec((pl.Squeezed(), n_meta, tn), seg_m(k)))

    outs = [xo(0, F32), xo(1, F32), mo(1, F32), xo(2, BF16), mo(2, BF16),
            xo(3, BF16), xo(4, F32), mo(4, F32), xo(5, BF16), xo(6, BF16)]
    nlb = lower_bounds.shape[0]
    lo_f, hi_f = starts[1], starts[2]
    q, f, fm, v, vm, go, u, um, gh, gs = pl.pallas_call(
        functools.partial(_inproj_kernel, tm=tm, n_meta=n_meta, bounds=bounds,
                          q_scale=HG_DK ** -0.5),
        grid=grid,
        in_specs=[pl.BlockSpec((tm, d), lambda i, j: (i, 0)),
                  pl.BlockSpec((n_meta, d), lambda i, j: (0, 0)),
                  pl.BlockSpec((1, d), lambda i, j: (0, 0)),
                  pl.BlockSpec((d, tn), lambda i, j: (0, j)),
                  pl.BlockSpec((nlb, tn), lambda i, j: (0, jnp.clip(j - lo_f, 0, hi_f - lo_f - 1)))],
        out_specs=[o[1] for o in outs], out_shape=[o[0] for o in outs],
        scratch_shapes=[pltpu.VMEM((tm + n_meta, d), BF16)],
        compiler_params=_cparams(("arbitrary", "arbitrary")),
        name="mixer_inproj",
    )(hx, hm, g.reshape(1, d), w_in, lower_bounds)
    return q, f, fm[0], v, vm[0], go, u, um[0], gh, gs


def _split3(x):
    hi = x.astype(BF16)
    r1 = x - hi.astype(F32)
    mid = r1.astype(BF16)
    lo = (r1 - mid.astype(F32)).astype(BF16)
    return hi, mid, lo


def _dot3(m01, parts):
    return _dot(m01, parts[0]) + _dot(m01, parts[1]) + _dot(m01, parts[2])


def _tri(n, rep=1):
    r = lax.broadcasted_iota(jnp.int32, (n * rep, n), 0)
    c = lax.broadcasted_iota(jnp.int32, (n * rep, n), 1)
    return jnp.where(r >= c * rep, 1.0, 0.0).astype(BF16)


def _hgrn_tables(c):
    sums = [np.tril(np.ones((c, c)))]
    level = np.full((c, c), -1, np.int32)
    t_idx, s_idx = np.meshgrid(np.arange(c), np.arange(c), indexing="ij")
    halves = []
    h = c // 2
    while h >= 1:
        if h < 8:
            m = np.zeros((c, c))
            for r in range(c):
                mid = (r // (2 * h)) * 2 * h + h
                if r < mid:
                    m[r, r + 1:mid] = 1.0
                else:
                    m[r, mid:r + 1] = 1.0
            sums.append(m)
        pair = 2 * h
        level[(t_idx // pair == s_idx // pair) & (s_idx % pair < h) & (t_idx % pair >= h)] = len(halves)
        halves.append(h)
        h //= 2
    assert len(halves) % 2 == 0
    level[np.arange(c), np.arange(c)] = len(halves)
    owner = np.full((c, 2 * c), -1, np.int32)
    owner[:, :c] = np.where((level >= 0) & (level % 2 == 0), level // 2, -1)
    owner[:, c:] = np.where(level % 2 == 1, level // 2, -1)
    return np.concatenate(sums, axis=0), owner, tuple(halves)


def _hgrn_score_products(q, k, x, halves):
    C = HG_CHUNK
    b = x[0:C]
    zero = jnp.zeros((C, HG_DK), BF16)
    diag = jnp.sum(q * k, axis=-1, keepdims=True)
    qts, kts = [], []
    fine = 0
    for h in halves:
        if h >= 8:
            none = jnp.zeros((h, HG_DK), F32)
            qh, kh = [], []
            for r in range(0, C, 2 * h):
                b_mid = b[r + h - 1:r + h, :]
                kh += [k[r:r + h] * jnp.exp(b_mid - b[r:r + h]), none]
                qh += [none, q[r + h:r + 2 * h] * jnp.exp(b[r + h:r + 2 * h] - b_mid)]
            qts.append(jnp.concatenate(qh, axis=0).astype(BF16))
            kts.append(jnp.concatenate(kh, axis=0).astype(BF16))
        else:
            fine += 1
            e = jnp.exp(x[fine * C:(fine + 1) * C])
            qts.append((q * e).astype(BF16))
            kts.append((k * e).astype(BF16))
    prods = []
    for p in range(len(halves) // 2):
        lhs = jnp.concatenate([qts[2 * p], qts[2 * p + 1]], axis=1)
        rhs = jnp.concatenate([jnp.concatenate([kts[2 * p], zero], axis=1),
                               jnp.concatenate([zero, kts[2 * p + 1]], axis=1)], axis=0)
        prods.append(_dot_nt(lhs, rhs))
    return prods, diag


def _hgrn_pair(q2, f2, v2, gate2, st2, sums, owner, halves):
    C, D = HG_CHUNK, HG_DK
    halfs = (slice(0, D), slice(D, 2 * D))
    k2 = 1.0 - f2
    x2 = _dot(sums, jnp.concatenate(_split3(jnp.log(f2)), axis=0))
    yield
    b2 = x2[0:C]
    b_last = b2[C - 1:C, :]
    kdec2 = (k2 * jnp.exp(b_last - b2)).astype(BF16)
    qe2 = (q2 * jnp.exp(b2)).astype(BF16)
    st2b = st2.astype(BF16)
    scores = [_hgrn_score_products(q2[:, c], k2[:, c], x2[:, c], halves) for c in halfs]
    yield
    zero = jnp.zeros((2 * C, D), BF16)
    os, vvts, kds = [], [], []
    for j, c in enumerate(halfs):
        prods, diag = scores[j]
        a = jnp.where(owner == len(prods), diag, 0.0)
        for p, prod in enumerate(prods):
            a = jnp.where(owner == p, prod, a)
        v = v2[:, c]
        vvt = jnp.concatenate([v, v], axis=0).T
        os.append(_dot_nt(jnp.concatenate([qe2[:, c], a.astype(BF16)], axis=1),
                          jnp.concatenate([st2b[:, c], vvt], axis=1)))
        vvts.append(vvt)
        kd = jnp.concatenate([kdec2[:, c], jnp.zeros((C, D), BF16)], axis=0)
        kds.append(jnp.concatenate([kd, zero] if j == 0 else [zero, kd], axis=1))
    st_new = st2 * jnp.exp(b_last) + _dot(jnp.concatenate(vvts, axis=1), jnp.concatenate(kds, axis=0))
    yield
    ons = [o * lax.rsqrt(jnp.mean(o * o, axis=-1, keepdims=True) + EPS) for o in os]
    on2 = jnp.concatenate(ons, axis=1) * gate2
    return on2.astype(BF16), st_new


def _interleave(gens):
    out = [None] * len(gens)
    live = list(range(len(gens)))
    while live:
        for g in list(live):
            try:
                next(gens[g])
            except StopIteration as stop:
                out[g] = stop.value
                live.remove(g)
    return out


def _hgrn_kernel(q_ref, f_ref, v_ref, go_ref, fm_ref, vm_ref, gn_ref, sums_ref, owner_ref,
                 o_ref, st_ref, *, n_chunks, n_meta, hb, halves):
    i = pl.program_id(0)
    p0 = pl.program_id(1) * (hb // 2)
    C = HG_CHUNK
    W = 2 * HG_DK

    @pl.when(i == 0)
    def _():
        tri_m = _tri(n_meta)
        for pair in range(hb // 2):
            sts = []
            for hh in (2 * pair, 2 * pair + 1):
                cols = slice(hh * HG_DK, (hh + 1) * HG_DK)
                fm = fm_ref[:, cols]
                bm = _dot3(tri_m, _split3(jnp.log(fm)))
                kdec = ((1.0 - fm) * jnp.exp(bm[n_meta - 1:n_meta, :] - bm)).astype(BF16)
                sts.append(_dot_tn(vm_ref[:, cols], kdec))
            st_ref[p0 + pair] = jnp.concatenate(sts, axis=1)

    def chunk(c, states):
        r0 = pl.multiple_of(c * C, C)
        owner = owner_ref[...]
        sums = sums_ref[...]
        gens = []
        for pair in range(hb // 2):
            cols = slice(pair * W, (pair + 1) * W)
            gate2 = gn_ref[:, cols] * go_ref[pl.ds(r0, C), cols].astype(F32)
            gens.append(_hgrn_pair(q_ref[pl.ds(r0, C), cols], f_ref[pl.ds(r0, C), cols],
                                   v_ref[pl.ds(r0, C), cols], gate2, states[pair],
                                   sums, owner, halves))
        results = _interleave(gens)
        o_ref[pl.ds(r0, C), :] = jnp.concatenate([r[0] for r in results], axis=1)
        return tuple(r[1] for r in results)

    states = lax.fori_loop(0, n_chunks, chunk, tuple(st_ref[p0 + p] for p in range(hb // 2)))
    for pair in range(hb // 2):
        st_ref[p0 + pair] = states[pair]


def _hgrn(q, f, v, go, fm, vm, hg_norm, *, rb=512, hb=12):
    n, width = q.shape
    heads = width // HG_DK
    hb = min(hb, heads)
    n_meta = fm.shape[0]
    assert hb % 2 == 0 and heads % hb == 0
    sums, owner, halves = _hgrn_tables(HG_CHUNK)
    sums3 = jnp.asarray(np.concatenate([sums] * 3, axis=1), BF16)
    blk = lambda i, h: (i, h)
    mblk = lambda i, h: (0, h)
    const = lambda i, h: (0, 0)
    return pl.pallas_call(
        functools.partial(_hgrn_kernel, n_chunks=rb // HG_CHUNK, n_meta=n_meta, hb=hb,
                          halves=halves),
        grid=(n // rb, heads // hb),
        in_specs=[pl.BlockSpec((rb, hb * HG_DK), blk)] * 4
        + [pl.BlockSpec((n_meta, hb * HG_DK), mblk)] * 2
        + [pl.BlockSpec((1, hb * HG_DK), mblk),
           pl.BlockSpec(sums3.shape, const), pl.BlockSpec(owner.shape, const)],
        out_specs=pl.BlockSpec((rb, hb * HG_DK), blk),
        out_shape=jax.ShapeDtypeStruct((n, width), BF16),
        scratch_shapes=[pltpu.VMEM((heads // 2, HG_DK, 2 * HG_DK), F32)],
        compiler_params=_cparams(("arbitrary", "arbitrary")),
        name="hgrn2",
    )(q, f, v, go, fm, vm, hg_norm.reshape(1, width), sums3, jnp.asarray(owner))


def _s5_weights_kernel(lr_c_ref, li_c_ref, dt_c_ref, lr_r_ref, li_r_ref, dt_r_ref,
                       bre_ref, bim_ref, cre_ref, cim_ref, dsk_ref,
                       m_ref, win_ref, wout_ref, at_ref):
    T = S5_T
    ns = lr_r_ref.shape[1]
    nc = cre_ref.shape[1]

    def powers(lr, li, dt, tau):
        mag = jnp.exp(lr * dt * tau)
        ang = li * dt * tau
        return mag * jnp.cos(ang), mag * jnp.sin(ang)

    lr, li, dt = lr_r_ref[...], li_r_ref[...], dt_r_ref[...]
    a_re, a_im = powers(lr, li, dt, 1.0)
    num_re, num_im = a_re - 1.0, a_im
    den = lr * lr + li * li
    coef_re = (num_re * lr + num_im * li) / den
    coef_im = (num_im * lr - num_re * li) / den
    br, bi = bre_ref[...], bim_ref[...]
    bbar_re = coef_re * br - coef_im * bi
    bbar_im = coef_re * bi + coef_im * br

    for s in range(T):
        p_re, p_im = powers(lr, li, dt, float(T - 1 - s))
        win_ref[s * nc:(s + 1) * nc, 0:ns] = (bbar_re * p_re - bbar_im * p_im).astype(BF16)
        win_ref[s * nc:(s + 1) * nc, ns:2 * ns] = (bbar_re * p_im + bbar_im * p_re).astype(BF16)
    t_re, t_im = powers(lr, li, dt, float(T))
    at_ref[:, 0:ns] = t_re
    at_ref[:, ns:2 * ns] = t_im

    lrc, lic, dtc = lr_c_ref[...], li_c_ref[...], dt_c_ref[...]
    cr, ci = cre_ref[...], cim_ref[...]
    hp = lax.Precision.HIGHEST
    zero = jnp.zeros((nc, nc), BF16)
    for tau in range(T + 1):
        p_re, p_im = powers(lrc, lic, dtc, float(tau))
        ca_re = cr * p_re - ci * p_im
        ca_im = cr * p_im + ci * p_re
        if tau >= 1:
            t = tau - 1
            wout_ref[0:ns, t * nc:(t + 1) * nc] = ca_re.astype(BF16)
            wout_ref[ns:2 * ns, t * nc:(t + 1) * nc] = (-ca_im).astype(BF16)
        if tau < T:
            kt = (jnp.dot(bbar_re, ca_re, precision=hp, preferred_element_type=F32)
                  - jnp.dot(bbar_im, ca_im, precision=hp, preferred_element_type=F32))
            if tau == 0:
                kt = kt + dsk_ref[...]
            kt = kt.astype(BF16)
            for s in range(T - tau):
                m_ref[s * nc:(s + 1) * nc, (s + tau) * nc:(s + tau + 1) * nc] = kt
            if tau:
                for s in range(T - tau):
                    m_ref[(s + tau) * nc:(s + tau + 1) * nc, s * nc:(s + 1) * nc] = zero


def _s5_weights(lam_re, lam_im, log_step, b_re, b_im, c_re, c_im, d_skip):
    G, N = lam_re.shape
    P = d_skip.shape[1]
    nsg = G // S5_SG
    ns, nc = S5_SG * N, S5_SG * P
    T = S5_T
    dt = jnp.exp(log_step.astype(F32))
    eye = jnp.eye(S5_SG, dtype=F32)

    def rows(a):
        return a.astype(F32).reshape(nsg, 1, ns)

    def cols(a):
        return jnp.broadcast_to(a.astype(F32).reshape(nsg, ns, 1), (nsg, ns, nc))

    def embed_b(b):
        b = b.astype(F32).reshape(nsg, S5_SG, N, P)
        return jnp.einsum('zgnq,gh->zhqgn', b, eye).reshape(nsg, nc, ns)

    def embed_c(c):
        c = c.astype(F32).reshape(nsg, S5_SG, P, N)
        return jnp.einsum('zgpn,gh->zgnhp', c, eye).reshape(nsg, ns, nc)

    dtb = jnp.broadcast_to(dt[:, None], (G, N))
    d_diag = jax.vmap(jnp.diag)(d_skip.astype(F32).reshape(nsg, nc))
    sq = pl.Squeezed()
    spec_r = pl.BlockSpec((sq, 1, ns), lambda z: (z, 0, 0))
    spec_c = pl.BlockSpec((sq, ns, nc), lambda z: (z, 0, 0))
    spec_b = pl.BlockSpec((sq, nc, ns), lambda z: (z, 0, 0))
    return pl.pallas_call(
        _s5_weights_kernel,
        grid=(nsg,),
        in_specs=[spec_c, spec_c, spec_c, spec_r, spec_r, spec_r, spec_b, spec_b, spec_c, spec_c,
                  pl.BlockSpec((sq, nc, nc), lambda z: (z, 0, 0))],
        out_specs=[pl.BlockSpec((sq, T * nc, T * nc), lambda z: (z, 0, 0)),
                   pl.BlockSpec((sq, T * nc, 2 * ns), lambda z: (z, 0, 0)),
                   pl.BlockSpec((sq, 2 * ns, T * nc), lambda z: (z, 0, 0)),
                   pl.BlockSpec((sq, 1, 2 * ns), lambda z: (z, 0, 0))],
        out_shape=[jax.ShapeDtypeStruct((nsg, T * nc, T * nc), BF16),
                   jax.ShapeDtypeStruct((nsg, T * nc, 2 * ns), BF16),
                   jax.ShapeDtypeStruct((nsg, 2 * ns, T * nc), BF16),
                   jax.ShapeDtypeStruct((nsg, 1, 2 * ns), F32)],
        compiler_params=_cparams(("arbitrary",)),
        name="s5_weights",
    )(cols(lam_re), cols(lam_im), cols(dtb), rows(lam_re), rows(lam_im), rows(dtb),
      embed_b(b_re), embed_b(b_im), embed_c(c_re), embed_c(c_im), d_diag)


def _s5_scan_kernel(u_ref, um_ref, m_ref, win_ref, wout_ref, at_ref, y_ref,
                    uc_ref, v_ref, xp_ref, *, n_chunks):
    T = S5_T
    nc = u_ref.shape[1]
    ns = at_ref.shape[1] // 2
    for t in range(T):
        uc_ref[0:n_chunks, t * nc:(t + 1) * nc] = u_ref[pl.ds(t, n_chunks, stride=T), :].astype(BF16)
        uc_ref[n_chunks:n_chunks + 1, t * nc:(t + 1) * nc] = um_ref[t:t + 1, :].astype(BF16)
    pad = uc_ref.shape[0] - n_chunks - 1
    uc_ref[n_chunks + 1:, :] = jnp.zeros((pad, T * nc), BF16)
    uc = uc_ref[...]
    v_ref[...] = _dot(uc, win_ref[...])
    a_re, a_im = at_ref[:, 0:ns], at_ref[:, ns:2 * ns]

    x_re0 = v_ref[n_chunks:n_chunks + 1, 0:ns]
    x_im0 = v_ref[n_chunks:n_chunks + 1, ns:2 * ns]
    xp_ref[n_chunks:, :] = jnp.zeros((pad + 1, 2 * ns), F32)

    def step(c, carry):
        x_re, x_im = carry
        xp_ref[pl.ds(c, 1), 0:ns] = x_re
        xp_ref[pl.ds(c, 1), ns:2 * ns] = x_im
        v_re = v_ref[pl.ds(c, 1), 0:ns]
        v_im = v_ref[pl.ds(c, 1), ns:2 * ns]
        return (a_re * x_re - a_im * x_im + v_re, a_re * x_im + a_im * x_re + v_im)

    lax.fori_loop(0, n_chunks, step, (x_re0, x_im0))
    yc = _dot(uc, m_ref[...]) + _dot(xp_ref[...].astype(BF16), wout_ref[...])
    for t in range(T):
        y_ref[pl.ds(t, n_chunks, stride=T), :] = yc[0:n_chunks, t * nc:(t + 1) * nc]


def _s5_scan(u, um, m, win, wout, at):
    n, width = u.shape
    T = S5_T
    nsg = m.shape[0]
    nc = width // nsg
    ns2 = at.shape[2]
    assert um.shape[0] == T and n % T == 0
    n_chunks = n // T
    rows = -(-(n_chunks + 1) // 16) * 16
    sq = pl.Squeezed()
    one = pl.Buffered(1)
    return pl.pallas_call(
        functools.partial(_s5_scan_kernel, n_chunks=n_chunks),
        grid=(nsg,),
        in_specs=[pl.BlockSpec((n, nc), lambda z: (0, z)),
                  pl.BlockSpec((T, nc), lambda z: (0, z)),
                  pl.BlockSpec((sq, T * nc, T * nc), lambda z: (z, 0, 0), pipeline_mode=one),
                  pl.BlockSpec((sq, T * nc, ns2), lambda z: (z, 0, 0), pipeline_mode=one),
                  pl.BlockSpec((sq, ns2, T * nc), lambda z: (z, 0, 0), pipeline_mode=one),
                  pl.BlockSpec((sq, 1, ns2), lambda z: (z, 0, 0))],
        out_specs=pl.BlockSpec((n, nc), lambda z: (0, z)),
        out_shape=jax.ShapeDtypeStruct((n, width), F32),
        scratch_shapes=[pltpu.VMEM((rows, T * nc), BF16), pltpu.VMEM((rows, ns2), F32),
                        pltpu.VMEM((rows, ns2), F32)],
        compiler_params=_cparams(("arbitrary",)),
        name="s5_scan",
    )(u, um, m, win, wout, at)


def _merge_kernel(h_ref, o_ref, y_ref, gh_ref, gs_ref, wp_ref, wa_ref, wb_ref, wo_ref, out_ref):
    y_hg = _dot(o_ref[...], wp_ref[...])
    z = jax.nn.gelu(y_ref[...]).astype(BF16)
    y_s5 = _dot(z, wa_ref[...]) * jax.nn.sigmoid(_dot(z, wb_ref[...]))
    merged = gh_ref[...].astype(F32) * y_hg + gs_ref[...].astype(F32) * y_s5
    out_ref[...] = h_ref[...] + _dot(merged.astype(BF16), wo_ref[...])


def _merge(hx, o, y, gh, gs, wp, wa, wb, wo, *, tm=256):
    n, d = hx.shape
    row = lambda i: (i, 0)
    const = lambda i: (0, 0)

    def resident(shape):
        return pl.BlockSpec(shape, const, pipeline_mode=pl.Buffered(1))

    return pl.pallas_call(
        _merge_kernel,
        grid=(n // tm,),
        in_specs=[pl.BlockSpec((tm, d), row), pl.BlockSpec((tm, o.shape[1]), row),
                  pl.BlockSpec((tm, y.shape[1]), row), pl.BlockSpec((tm, d), row),
                  pl.BlockSpec((tm, d), row),
                  resident(wp.shape), resident(wa.shape), resident(wb.shape), resident(wo.shape)],
        out_specs=pl.BlockSpec((tm, d), row),
        out_shape=jax.ShapeDtypeStruct((n, d), F32),
        compiler_params=_cparams(("arbitrary",)),
        name="mixer_merge",
    )(hx, o, y, gh, gs, wp, wa, wb, wo)


def kernel(x, meta_tokens, norm_ffn1, ffn1_w_gate, ffn1_w_up, ffn1_w_down, norm_mix, w_in, hg_lower_bounds, hg_norm, hg_w_proj, s5_lam_re, s5_lam_im, s5_log_step, s5_b_re, s5_b_im, s5_c_re, s5_c_im, s5_d, s5_w_glu_a, s5_w_glu_b, w_out, norm_ffn2, ffn2_w_gate, ffn2_w_up, ffn2_w_down, norm_final):
    batch, seq, d = x.shape
    depth = norm_ffn1.shape[0]
    assert batch == 1 and depth == 1
    hg_k = hg_lower_bounds.shape[1]
    hg_w = hg_norm.shape[1]
    s5_w = s5_w_glu_a.shape[1]
    bf = lambda w: w.astype(BF16)

    hx = x[0].astype(F32)
    hm = meta_tokens.astype(F32)
    hx, hm = _ffn(hx, hm, norm_ffn1[0], bf(ffn1_w_gate[0]), bf(ffn1_w_up[0]), bf(ffn1_w_down[0]))

    (q, f, fm, v, vm, go, u, um, gh, gs) = _inproj(
        hx, hm, norm_mix[0], bf(w_in[0]), hg_lower_bounds, hg_k, hg_w, s5_w)
    o = _hgrn(q, f, v, go, fm, vm, hg_norm[0])
    m, win, wout, at = _s5_weights(s5_lam_re[0], s5_lam_im[0], s5_log_step[0], s5_b_re[0],
                                   s5_b_im[0], s5_c_re[0], s5_c_im[0], s5_d[0])
    y = _s5_scan(u, um, m, win, wout, at)
    hx = _merge(hx, o, y, gh, gs, bf(hg_w_proj[0]), bf(s5_w_glu_a[0]), bf(s5_w_glu_b[0]),
                bf(w_out[0]))
    out = _ffn(hx, None, norm_ffn2[0], bf(ffn2_w_gate[0]), bf(ffn2_w_up[0]), bf(ffn2_w_down[0]),
               g_final=norm_final)
    return out[None].astype(x.dtype)
```

```python
import functools
import math

import numpy as np
import jax
import jax.numpy as jnp
from jax import lax
from jax.experimental import pallas as pl
from jax.experimental.pallas import tpu as pltpu

EPS = 1e-6
HG_DK = 128
HG_CHUNK = 64
S5_GROUP = 16
S5_STATE = 64
S5_T = 16
S5_SG = 8
LANE = 128
VMEM_LIMIT = 56 * 1024 * 1024

F32 = jnp.float32
BF16 = jnp.bfloat16


def _cparams(sem):
    return pltpu.CompilerParams(dimension_semantics=sem, vmem_limit_bytes=VMEM_LIMIT)


def _rms(v, g):
    return v * lax.rsqrt(jnp.mean(v * v, axis=-1, keepdims=True) + EPS) * g


def _dot(a, b):
    return jnp.dot(a, b, preferred_element_type=F32)


def _dot_nt(a, b):
    return lax.dot_general(a, b, (((1,), (1,)), ((), ())), preferred_element_type=F32)


def _dot_tn(a, b):
    return lax.dot_general(a, b, (((0,), (0,)), ((), ())), preferred_element_type=F32)


def _ffn_kernel(*refs, tm, n_meta):
    if n_meta:
        (x_ref, m_ref, g_ref, gn_ref, wg_ref, wu_ref, wd_ref,
         ox_ref, om_ref, nx_ref, nm_ref, hn_ref, acc_ref) = refs
    else:
        x_ref, g_ref, gn_ref, wg_ref, wu_ref, wd_ref, ox_ref, hn_ref, acc_ref = refs
    j = pl.program_id(1)

    @pl.when(j == 0)
    def _():
        hn_ref[0:tm, :] = _rms(x_ref[...], g_ref[...]).astype(BF16)
        if n_meta:
            hn_ref[tm:tm + n_meta, :] = _rms(m_ref[...], g_ref[...]).astype(BF16)
        acc_ref[...] = jnp.zeros_like(acc_ref)

    hn = hn_ref[...]
    gate = _dot(hn, wg_ref[...])
    up = _dot(hn, wu_ref[...])
    act = (gate * jax.nn.sigmoid(gate) * up).astype(BF16)
    acc_ref[...] += _dot(act, wd_ref[...])

    @pl.when(j == pl.num_programs(1) - 1)
    def _():
        hx = x_ref[...] + 0.5 * acc_ref[0:tm, :]
        if n_meta:
            ox_ref[...] = hx
            nx_ref[...] = _rms(hx, gn_ref[...]).astype(BF16)
            hm = m_ref[...] + 0.5 * acc_ref[tm:tm + n_meta, :]
            om_ref[...] = hm
            nm_ref[...] = _rms(hm, gn_ref[...]).astype(BF16)
        else:
            ox_ref[...] = _rms(hx, gn_ref[...])


def _ffn(hx, hm, g, g_next, wg, wu, wd, *, tm=512, tf=512):
    n, d = hx.shape
    dff = wg.shape[1]
    n_meta = 0 if hm is None else hm.shape[0]
    grid = (n // tm, dff // tf)
    row = lambda i, j: (i, 0)
    const = lambda i, j: (0, 0)
    in_specs = [pl.BlockSpec((tm, d), row)]
    args = [hx]
    if n_meta:
        in_specs.append(pl.BlockSpec((n_meta, d), const))
        args.append(hm)
    in_specs += [pl.BlockSpec((1, d), const), pl.BlockSpec((1, d), const),
                 pl.BlockSpec((d, tf), lambda i, j: (0, j)),
                 pl.BlockSpec((d, tf), lambda i, j: (0, j)),
                 pl.BlockSpec((tf, d), lambda i, j: (j, 0))]
    args += [g.reshape(1, d), g_next.reshape(1, d), wg, wu, wd]
    out_shape = [jax.ShapeDtypeStruct((n, d), F32)]
    out_specs = [pl.BlockSpec((tm, d), row)]
    if n_meta:
        out_shape += [jax.ShapeDtypeStruct((n_meta, d), F32), jax.ShapeDtypeStruct((n, d), BF16),
                      jax.ShapeDtypeStruct((n_meta, d), BF16)]
        out_specs += [pl.BlockSpec((n_meta, d), const), pl.BlockSpec((tm, d), row),
                      pl.BlockSpec((n_meta, d), const)]
    outs = pl.pallas_call(
        functools.partial(_ffn_kernel, tm=tm, n_meta=n_meta),
        grid=grid, in_specs=in_specs, out_specs=out_specs, out_shape=out_shape,
        scratch_shapes=[pltpu.VMEM((tm + n_meta, d), BF16), pltpu.VMEM((tm + n_meta, d), F32)],
        compiler_params=_cparams(("arbitrary", "arbitrary")),
        name="ffn_meta" if n_meta else "ffn_final",
    )(*args)
    return outs if n_meta else outs[0]


def _inproj_kernel(*refs, tm, n_meta, with_lb):
    if with_lb:
        x_ref, m_ref, w_ref, coef_ref, lb_ref, ox_ref, om_ref, lhs_ref = refs
    else:
        x_ref, m_ref, w_ref, coef_ref, ox_ref, om_ref, lhs_ref = refs

    @pl.when(pl.program_id(1) == 0)
    def _():
        lhs_ref[0:tm, :] = x_ref[...]
        lhs_ref[tm:tm + n_meta, :] = m_ref[...]

    tn = w_ref.shape[1]
    sw = math.gcd(tn, 2 * LANE)
    for s in range(tn // sw):
        cs = slice(s * sw, (s + 1) * sw)
        res = _dot(lhs_ref[...], w_ref[:, cs])
        sig = jax.nn.sigmoid(res)
        coef = coef_ref[:, cs]
        a, c, b, d = coef[0:1, :], coef[1:2, :], coef[2:3, :], 0.0
        if with_lb:
            lbl = lb_ref[:, cs]
            e = jnp.exp(lbl - jnp.max(lbl, axis=0, keepdims=True))
            lb = e[0:1, :] / jnp.sum(e, axis=0, keepdims=True)
            flag = coef[3:4, :]
            b = b + flag * (1.0 - lb)
            d = flag * lb
        out = res * (a * sig + c) + (b * sig + d)
        ox_ref[:, cs] = out[0:tm, :].astype(ox_ref.dtype)
        om_ref[:, cs] = out[tm:tm + n_meta, :].astype(om_ref.dtype)


def _inproj(nx, nm, w, coef, lower_bounds, lb_cols, dtype, *, tm=1024, tn=1792):
    n, d = nx.shape
    n_meta = nm.shape[0]
    cols = w.shape[1]
    tm = min(tm, n)
    with_lb = lower_bounds is not None
    tn = math.gcd(tn, cols)
    assert tn % LANE == 0 and n % tm == 0
    in_specs = [pl.BlockSpec((tm, d), lambda i, j: (i, 0)),
                pl.BlockSpec((n_meta, d), lambda i, j: (0, 0)),
                pl.BlockSpec((d, tn), lambda i, j: (0, j)),
                pl.BlockSpec((4, tn), lambda i, j: (0, j))]
    args = [nx, nm, w, coef]
    if with_lb:
        nlb, k = lower_bounds.shape
        lb_full = jnp.pad(lower_bounds.astype(F32), ((0, 0), (lb_cols, cols - lb_cols - k)))
        in_specs.append(pl.BlockSpec((nlb, tn), lambda i, j: (0, j)))
        args.append(lb_full)
    ox, om = pl.pallas_call(
        functools.partial(_inproj_kernel, tm=tm, n_meta=n_meta, with_lb=with_lb),
        grid=(n // tm, cols // tn),
        in_specs=in_specs,
        out_specs=[pl.BlockSpec((tm, tn), lambda i, j: (i, j)),
                   pl.BlockSpec((pl.Squeezed(), n_meta, tn), lambda i, j: (i, 0, j))],
        out_shape=[jax.ShapeDtypeStruct((n, cols), dtype),
                   jax.ShapeDtypeStruct((n // tm, n_meta, cols), dtype)],
        scratch_shapes=[pltpu.VMEM((tm + n_meta, d), BF16)],
        compiler_params=_cparams(("arbitrary", "arbitrary")),
        name="mixer_inproj_" + jnp.dtype(dtype).name,
    )(*args)
    return ox, om[0]


def _split3(x):
    hi = x.astype(BF16)
    r1 = x - hi.astype(F32)
    mid = r1.astype(BF16)
    lo = (r1 - mid.astype(F32)).astype(BF16)
    return hi, mid, lo


def _dot3(m01, parts):
    return _dot(m01, parts[0]) + _dot(m01, parts[1]) + _dot(m01, parts[2])


def _tri(n, rep=1):
    r = lax.broadcasted_iota(jnp.int32, (n * rep, n), 0)
    c = lax.broadcasted_iota(jnp.int32, (n * rep, n), 1)
    return jnp.where(r >= c * rep, 1.0, 0.0).astype(BF16)


def _hgrn_tables(c):
    sums = [np.tril(np.ones((c, c)))]
    level = np.full((c, c), -1, np.int32)
    t_idx, s_idx = np.meshgrid(np.arange(c), np.arange(c), indexing="ij")
    halves = []
    h = c // 2
    while h >= 1:
        if h < 8:
            m = np.zeros((c, c))
            for r in range(c):
                mid = (r // (2 * h)) * 2 * h + h
                if r < mid:
                    m[r, r + 1:mid] = 1.0
                else:
                    m[r, mid:r + 1] = 1.0
            sums.append(m)
        pair = 2 * h
        level[(t_idx // pair == s_idx // pair) & (s_idx % pair < h) & (t_idx % pair >= h)] = len(halves)
        halves.append(h)
        h //= 2
    assert len(halves) % 2 == 0
    level[np.arange(c), np.arange(c)] = len(halves)
    owner = np.full((c, 2 * c), -1, np.int32)
    owner[:, :c] = np.where((level >= 0) & (level % 2 == 0), level // 2, -1)
    owner[:, c:] = np.where(level % 2 == 1, level // 2, -1)
    return np.concatenate(sums, axis=0), owner, tuple(halves)


def _hgrn_score_products(q, k, x, halves):
    C = HG_CHUNK
    b = x[0:C]
    zero = jnp.zeros((C, HG_DK), BF16)
    diag = jnp.sum(q * k, axis=-1, keepdims=True)
    qts, kts = [], []
    fine = 0
    for h in halves:
        if h >= 8:
            none = jnp.zeros((h, HG_DK), F32)
            qh, kh = [], []
            for r in range(0, C, 2 * h):
                b_mid = b[r + h - 1:r + h, :]
                kh += [k[r:r + h] * jnp.exp(b_mid - b[r:r + h]), none]
                qh += [none, q[r + h:r + 2 * h] * jnp.exp(b[r + h:r + 2 * h] - b_mid)]
            qts.append(jnp.concatenate(qh, axis=0).astype(BF16))
            kts.append(jnp.concatenate(kh, axis=0).astype(BF16))
        else:
            fine += 1
            e = jnp.exp(x[fine * C:(fine + 1) * C])
            qts.append((q * e).astype(BF16))
            kts.append((k * e).astype(BF16))
    prods = []
    for p in range(len(halves) // 2):
        lhs = jnp.concatenate([qts[2 * p], qts[2 * p + 1]], axis=1)
        rhs = jnp.concatenate([jnp.concatenate([kts[2 * p], zero], axis=1),
                               jnp.concatenate([zero, kts[2 * p + 1]], axis=1)], axis=0)
        prods.append(_dot_nt(lhs, rhs))
    return prods, diag


def _hgrn_pair(q2, f2, v2, gate2, st2, sums, owner, halves):
    C, D = HG_CHUNK, HG_DK
    halfs = (slice(0, D), slice(D, 2 * D))
    k2 = 1.0 - f2
    x2 = _dot(sums, jnp.concatenate(_split3(jnp.log(f2)), axis=0))
    yield
    b2 = x2[0:C]
    b_last = b2[C - 1:C, :]
    kdec2 = (k2 * jnp.exp(b_last - b2)).astype(BF16)
    qe2 = (q2 * jnp.exp(b2)).astype(BF16)
    st2b = st2.astype(BF16)
    scores = [_hgrn_score_products(q2[:, c], k2[:, c], x2[:, c], halves) for c in halfs]
    yield
    zero = jnp.zeros((2 * C, D), BF16)
    os, vvts, kds = [], [], []
    for j, c in enumerate(halfs):
        prods, diag = scores[j]
        a = jnp.where(owner == len(prods), diag, 0.0)
        for p, prod in enumerate(prods):
            a = jnp.where(owner == p, prod, a)
        v = v2[:, c]
        vvt = jnp.concatenate([v, v], axis=0).T
        os.append(_dot_nt(jnp.concatenate([qe2[:, c], a.astype(BF16)], axis=1),
                          jnp.concatenate([st2b[:, c], vvt], axis=1)))
        vvts.append(vvt)
        kd = jnp.concatenate([kdec2[:, c], jnp.zeros((C, D), BF16)], axis=0)
        kds.append(jnp.concatenate([kd, zero] if j == 0 else [zero, kd], axis=1))
    st_new = st2 * jnp.exp(b_last) + _dot(jnp.concatenate(vvts, axis=1), jnp.concatenate(kds, axis=0))
    yield
    ons = [o * lax.rsqrt(jnp.mean(o * o, axis=-1, keepdims=True) + EPS) for o in os]
    on2 = jnp.concatenate(ons, axis=1) * gate2
    return on2.astype(BF16), st_new


def _interleave(gens):
    out = [None] * len(gens)
    live = list(range(len(gens)))
    while live:
        for g in list(live):
            try:
                next(gens[g])
            except StopIteration as stop:
                out[g] = stop.value
                live.remove(g)
    return out


def _hgrn_kernel(q_ref, f_ref, v_ref, go_ref, fm_ref, vm_ref, gn_ref, sums_ref, owner_ref,
                 o_ref, st_ref, *, n_chunks, n_meta, hb, halves):
    i = pl.program_id(0)
    p0 = pl.program_id(1) * (hb // 2)
    C = HG_CHUNK
    W = 2 * HG_DK

    @pl.when(i == 0)
    def _():
        tri_m = _tri(n_meta)
        for pair in range(hb // 2):
            sts = []
            for hh in (2 * pair, 2 * pair + 1):
                cols = slice(hh * HG_DK, (hh + 1) * HG_DK)
                fm = fm_ref[:, cols]
                bm = _dot3(tri_m, _split3(jnp.log(fm)))
                kdec = ((1.0 - fm) * jnp.exp(bm[n_meta - 1:n_meta, :] - bm)).astype(BF16)
                sts.append(_dot_tn(vm_ref[:, cols], kdec))
            st_ref[p0 + pair] = jnp.concatenate(sts, axis=1)

    def chunk(c, states):
        r0 = pl.multiple_of(c * C, C)
        owner = owner_ref[...]
        sums = sums_ref[...]
        gens = []
        for pair in range(hb // 2):
            cols = slice(pair * W, (pair + 1) * W)
            gate2 = gn_ref[:, cols] * go_ref[pl.ds(r0, C), cols].astype(F32)
            gens.append(_hgrn_pair(q_ref[pl.ds(r0, C), cols], f_ref[pl.ds(r0, C), cols],
                                   v_ref[pl.ds(r0, C), cols], gate2, states[pair],
                                   sums, owner, halves))
        results = _interleave(gens)
        o_ref[pl.ds(r0, C), :] = jnp.concatenate([r[0] for r in results], axis=1)
        return tuple(r[1] for r in results)

    states = lax.fori_loop(0, n_chunks, chunk, tuple(st_ref[p0 + p] for p in range(hb // 2)))
    for pair in range(hb // 2):
        st_ref[p0 + pair] = states[pair]


def _hgrn(q, f, v, go, fm, vm, hg_norm, *, rb=512, hb=12):
    n, width = q[0].shape[0], hg_norm.shape[-1]
    heads = width // HG_DK
    hb = min(hb, heads)
    bw = hb * HG_DK
    n_meta = fm.shape[0]
    assert hb % 2 == 0 and heads % hb == 0
    sums, owner, halves = _hgrn_tables(HG_CHUNK)
    sums3 = jnp.asarray(np.concatenate([sums] * 3, axis=1), BF16)
    blk = lambda i, h: (i, h)
    mblk = lambda i, h: (0, h)
    const = lambda i, h: (0, 0)

    def view(col0):
        assert col0 % bw == 0
        return pl.BlockSpec((rb, bw), lambda i, h: (i, col0 // bw + h))

    return pl.pallas_call(
        functools.partial(_hgrn_kernel, n_chunks=rb // HG_CHUNK, n_meta=n_meta, hb=hb,
                          halves=halves),
        grid=(n // rb, heads // hb),
        in_specs=[view(q[1]), view(f[1]), view(v[1]), view(go[1])]
        + [pl.BlockSpec((n_meta, hb * HG_DK), mblk)] * 2
        + [pl.BlockSpec((1, hb * HG_DK), mblk),
           pl.BlockSpec(sums3.shape, const), pl.BlockSpec(owner.shape, const)],
        out_specs=pl.BlockSpec((rb, hb * HG_DK), blk),
        out_shape=jax.ShapeDtypeStruct((n, width), BF16),
        scratch_shapes=[pltpu.VMEM((heads // 2, HG_DK, 2 * HG_DK), F32)],
        compiler_params=_cparams(("arbitrary", "arbitrary")),
        name="hgrn2",
    )(q[0], f[0], v[0], go[0], fm, vm, hg_norm.reshape(1, width), sums3, jnp.asarray(owner))


def _s5_weights_kernel(lr_ref, li_ref, dt_ref, bre_ref, bim_ref, cre_ref, cim_ref, dsk_ref,
                       m_ref, win_ref, wout_ref, at_ref):
    T = S5_T
    ns = lr_ref.shape[1]
    nc = cre_ref.shape[0]
    lr, li, dt = lr_ref[...], li_ref[...], dt_ref[...]

    def powers(tau):
        mag = jnp.exp(lr * dt * tau)
        ang = li * dt * tau
        return mag * jnp.cos(ang), mag * jnp.sin(ang)

    a_re, a_im = powers(1.0)
    num_re, num_im = a_re - 1.0, a_im
    den = lr * lr + li * li
    coef_re = (num_re * lr + num_im * li) / den
    coef_im = (num_im * lr - num_re * li) / den
    br, bi = bre_ref[...], bim_ref[...]
    bbar_re = coef_re * br - coef_im * bi
    bbar_im = coef_re * bi + coef_im * br

    pw = [powers(float(tau)) for tau in range(T + 1)]
    for s in range(T):
        p_re, p_im = pw[T - 1 - s]
        win_ref[s * nc:(s + 1) * nc, 0:ns] = (bbar_re * p_re - bbar_im * p_im).astype(BF16)
        win_ref[s * nc:(s + 1) * nc, ns:2 * ns] = (bbar_re * p_im + bbar_im * p_re).astype(BF16)
    at_ref[:, 0:ns] = pw[T][0]
    at_ref[:, ns:2 * ns] = pw[T][1]

    cr, ci = cre_ref[...], cim_ref[...]
    hp = lax.Precision.HIGHEST
    nt = (((1,), (1,)), ((), ()))
    zero = jnp.zeros((nc, nc), BF16)
    for tau in range(T + 1):
        p_re, p_im = pw[tau]
        ca_re = cr * p_re - ci * p_im
        ca_im = cr * p_im + ci * p_re
        if tau >= 1:
            t = tau - 1
            wout_ref[0:ns, t * nc:(t + 1) * nc] = ca_re.T.astype(BF16)
            wout_ref[ns:2 * ns, t * nc:(t + 1) * nc] = (-ca_im).T.astype(BF16)
        if tau < T:
            kt = (lax.dot_general(bbar_re, ca_re, nt, precision=hp, preferred_element_type=F32)
                  - lax.dot_general(bbar_im, ca_im, nt, precision=hp, preferred_element_type=F32))
            if tau == 0:
                kt = kt + dsk_ref[...]
            kt = kt.astype(BF16)
            for s in range(T - tau):
                m_ref[s * nc:(s + 1) * nc, (s + tau) * nc:(s + tau + 1) * nc] = kt
            if tau:
                for s in range(T - tau):
                    m_ref[(s + tau) * nc:(s + tau + 1) * nc, s * nc:(s + 1) * nc] = zero


def _s5_weights(lam_re, lam_im, log_step, b_re, b_im, c_re, c_im, d_skip):
    G, N = lam_re.shape
    P = d_skip.shape[1]
    nsg = G // S5_SG
    ns, nc = S5_SG * N, S5_SG * P
    T = S5_T
    dt = jnp.exp(log_step.astype(F32))
    eye = jnp.eye(S5_SG, dtype=F32)

    def rows(a):
        return a.astype(F32).reshape(nsg, 1, ns)

    def embed_b(b):
        b = b.astype(F32).reshape(nsg, S5_SG, N, P)
        return jnp.einsum('zgnq,gh->zhqgn', b, eye).reshape(nsg, nc, ns)

    def embed_c(c):
        c = c.astype(F32).reshape(nsg, S5_SG, P, N)
        return jnp.einsum('zgpn,gh->zhpgn', c, eye).reshape(nsg, nc, ns)

    dtb = jnp.broadcast_to(dt[:, None], (G, N))
    d_diag = jax.vmap(jnp.diag)(d_skip.astype(F32).reshape(nsg, nc))
    sq = pl.Squeezed()
    spec_r = pl.BlockSpec((sq, 1, ns), lambda z: (z, 0, 0))
    spec_b = pl.BlockSpec((sq, nc, ns), lambda z: (z, 0, 0))
    return pl.pallas_call(
        _s5_weights_kernel,
        grid=(nsg,),
        in_specs=[spec_r, spec_r, spec_r, spec_b, spec_b, spec_b, spec_b,
                  pl.BlockSpec((sq, nc, nc), lambda z: (z, 0, 0))],
        out_specs=[pl.BlockSpec((sq, T * nc, T * nc), lambda z: (z, 0, 0)),
                   pl.BlockSpec((sq, T * nc, 2 * ns), lambda z: (z, 0, 0)),
                   pl.BlockSpec((sq, 2 * ns, T * nc), lambda z: (z, 0, 0)),
                   pl.BlockSpec((sq, 1, 2 * ns), lambda z: (z, 0, 0))],
        out_shape=[jax.ShapeDtypeStruct((nsg, T * nc, T * nc), BF16),
                   jax.ShapeDtypeStruct((nsg, T * nc, 2 * ns), BF16),
                   jax.ShapeDtypeStruct((nsg, 2 * ns, T * nc), BF16),
                   jax.ShapeDtypeStruct((nsg, 1, 2 * ns), F32)],
        compiler_params=_cparams(("arbitrary",)),
        name="s5_weights",
    )(rows(lam_re), rows(lam_im), rows(dtb),
      embed_b(b_re), embed_b(b_im), embed_c(c_re), embed_c(c_im), d_diag)


def _s5_scan_kernel(u_ref, um_ref, m_ref, win_ref, wout_ref, at_ref, y_ref,
                    uc_ref, v_ref, xp_ref, *, n_chunks):
    T = S5_T
    nc = u_ref.shape[1]
    ns = at_ref.shape[1] // 2
    for t in range(T):
        uc_ref[0:n_chunks, t * nc:(t + 1) * nc] = u_ref[pl.ds(t, n_chunks, stride=T), :].astype(BF16)
        uc_ref[n_chunks:n_chunks + 1, t * nc:(t + 1) * nc] = um_ref[t:t + 1, :].astype(BF16)
    pad = uc_ref.shape[0] - n_chunks - 1
    uc_ref[n_chunks + 1:, :] = jnp.zeros((pad, T * nc), BF16)
    uc = uc_ref[...]
    v_ref[...] = _dot(uc, win_ref[...])
    a_re, a_im = at_ref[:, 0:ns], at_ref[:, ns:2 * ns]

    x_re0 = v_ref[n_chunks:n_chunks + 1, 0:ns]
    x_im0 = v_ref[n_chunks:n_chunks + 1, ns:2 * ns]
    xp_ref[n_chunks:, :] = jnp.zeros((pad + 1, 2 * ns), F32)

    def step(c, carry):
        x_re, x_im = carry
        xp_ref[pl.ds(c, 1), 0:ns] = x_re
        xp_ref[pl.ds(c, 1), ns:2 * ns] = x_im
        v_re = v_ref[pl.ds(c, 1), 0:ns]
        v_im = v_ref[pl.ds(c, 1), ns:2 * ns]
        return (a_re * x_re - a_im * x_im + v_re, a_re * x_im + a_im * x_re + v_im)

    lax.fori_loop(0, n_chunks, step, (x_re0, x_im0))
    yc = _dot(uc, m_ref[...]) + _dot(xp_ref[...].astype(BF16), wout_ref[...])
    for t in range(T):
        y_ref[pl.ds(t, n_chunks, stride=T), :] = yc[0:n_chunks, t * nc:(t + 1) * nc]


def _s5_scan(u, um, m, win, wout, at):
    u, col0 = u
    n, width = u.shape[0], um.shape[1]
    T = S5_T
    nsg = m.shape[0]
    nc = width // nsg
    ns2 = at.shape[2]
    assert um.shape[0] == T and n % T == 0 and col0 % nc == 0
    n_chunks = n // T
    rows = -(-(n_chunks + 1) // 16) * 16
    sq = pl.Squeezed()
    one = pl.Buffered(1)
    return pl.pallas_call(
        functools.partial(_s5_scan_kernel, n_chunks=n_chunks),
        grid=(nsg,),
        in_specs=[pl.BlockSpec((n, nc), lambda z: (0, col0 // nc + z)),
                  pl.BlockSpec((T, nc), lambda z: (0, z)),
                  pl.BlockSpec((sq, T * nc, T * nc), lambda z: (z, 0, 0), pipeline_mode=one),
                  pl.BlockSpec((sq, T * nc, ns2), lambda z: (z, 0, 0), pipeline_mode=one),
                  pl.BlockSpec((sq, ns2, T * nc), lambda z: (z, 0, 0), pipeline_mode=one),
                  pl.BlockSpec((sq, 1, ns2), lambda z: (z, 0, 0))],
        out_specs=pl.BlockSpec((n, nc), lambda z: (0, z)),
        out_shape=jax.ShapeDtypeStruct((n, width), F32),
        scratch_shapes=[pltpu.VMEM((rows, T * nc), BF16), pltpu.VMEM((rows, ns2), F32),
                        pltpu.VMEM((rows, ns2), F32)],
        compiler_params=_cparams(("arbitrary",)),
        name="s5_scan",
    )(u, um, m, win, wout, at)


def _merge_kernel(*refs, pieces):
    h_ref, o_ref, y_ref = refs[0:3]
    gh_refs, gs_refs = refs[3:3 + pieces], refs[3 + pieces:3 + 2 * pieces]
    wp_ref, wa_ref, wb_ref, wo_ref, out_ref = refs[3 + 2 * pieces:]
    gh = jnp.concatenate([r[...] for r in gh_refs], axis=1).astype(F32)
    gs = jnp.concatenate([r[...] for r in gs_refs], axis=1).astype(F32)
    y_hg = _dot(o_ref[...], wp_ref[...])
    z = jax.nn.gelu(y_ref[...]).astype(BF16)
    y_s5 = _dot(z, wa_ref[...]) * jax.nn.sigmoid(_dot(z, wb_ref[...]))
    merged = gh * y_hg + gs * y_s5
    out_ref[...] = h_ref[...] + _dot(merged.astype(BF16), wo_ref[...])


def _merge(hx, o, y, gh, gs, wp, wa, wb, wo, *, tm=256):
    n, d = hx.shape
    row = lambda i: (i, 0)
    const = lambda i: (0, 0)
    gw = functools.reduce(math.gcd, (gh[1], gs[1], d))
    assert gw % LANE == 0
    pieces = d // gw

    def resident(shape):
        return pl.BlockSpec(shape, const, pipeline_mode=pl.Buffered(1))

    def view(col0):
        return [pl.BlockSpec((tm, gw), functools.partial(lambda i, k: (i, k), k=col0 // gw + k))
                for k in range(pieces)]

    return pl.pallas_call(
        functools.partial(_merge_kernel, pieces=pieces),
        grid=(n // tm,),
        in_specs=[pl.BlockSpec((tm, d), row), pl.BlockSpec((tm, o.shape[1]), row),
                  pl.BlockSpec((tm, y.shape[1]), row)] + view(gh[1]) + view(gs[1])
        + [resident(wp.shape), resident(wa.shape), resident(wb.shape), resident(wo.shape)],
        out_specs=pl.BlockSpec((tm, d), row),
        out_shape=jax.ShapeDtypeStruct((n, d), F32),
        compiler_params=_cparams(("arbitrary",)),
        name="mixer_merge",
    )(hx, o, y, *([gh[0]] * pieces), *([gs[0]] * pieces), wp, wa, wb, wo)


def kernel(x, meta_tokens, norm_ffn1, ffn1_w_gate, ffn1_w_up, ffn1_w_down, norm_mix, w_in, hg_lower_bounds, hg_norm, hg_w_proj, s5_lam_re, s5_lam_im, s5_log_step, s5_b_re, s5_b_im, s5_c_re, s5_c_im, s5_d, s5_w_glu_a, s5_w_glu_b, w_out, norm_ffn2, ffn2_w_gate, ffn2_w_up, ffn2_w_down, norm_final):
    batch, seq, d = x.shape
    depth = norm_ffn1.shape[0]
    assert batch == 1 and depth == 1
    hg_k = hg_lower_bounds.shape[1]
    hg_w = hg_norm.shape[1]
    s5_w = s5_w_glu_a.shape[1]
    bf = lambda w: w.astype(BF16)

    hx = x[0].astype(F32)
    hm = meta_tokens.astype(F32)
    hx, hm, nx, nm = _ffn(hx, hm, norm_ffn1[0], norm_mix[0],
                          bf(ffn1_w_gate[0]), bf(ffn1_w_up[0]), bf(ffn1_w_down[0]))

    sizes = (hg_k, hg_k, hg_w, hg_w, s5_w, d, d)
    assert sum(sizes) == w_in.shape[2]
    off = [0]
    for s in sizes:
        off.append(off[-1] + s)

    def group(ids, rows):
        w = jnp.concatenate([w_in[0][:, off[k]:off[k + 1]] for k in ids], axis=1).astype(BF16)
        coef = jnp.concatenate([jnp.broadcast_to(jnp.asarray(rows[k], F32)[:, None], (4, sizes[k]))
                                for k in ids], axis=1)
        starts = {}
        col = 0
        for k in ids:
            starts[k] = col
            col += sizes[k]
        return w, coef, starts

    silu, ident, sigm = (1.0, 0.0, 0.0, 0.0), (0.0, 1.0, 0.0, 0.0), (0.0, 0.0, 1.0, 0.0)
    rows = {0: (HG_DK ** -0.5, 0.0, 0.0, 0.0), 1: (0.0, 0.0, 0.0, 1.0), 2: ident, 3: silu,
            4: ident, 5: sigm, 6: sigm}
    w32, c32, s32 = group((0, 1, 4), rows)
    w16, c16, s16 = group((2, 3, 5, 6), rows)
    x32, m32 = _inproj(nx, nm, w32, c32, hg_lower_bounds, s32[1], F32)
    x16, m16 = _inproj(nx, nm, w16, c16, None, 0, BF16)

    o = _hgrn((x32, s32[0]), (x32, s32[1]), (x16, s16[2]), (x16, s16[3]),
              m32[:, s32[1]:s32[1] + hg_k], m16[:, s16[2]:s16[2] + hg_w], hg_norm[0])
    m, win, wout, at = _s5_weights(s5_lam_re[0], s5_lam_im[0], s5_log_step[0], s5_b_re[0],
                                   s5_b_im[0], s5_c_re[0], s5_c_im[0], s5_d[0])
    y = _s5_scan((x32, s32[4]), m32[:, s32[4]:s32[4] + s5_w], m, win, wout, at)
    hx = _merge(hx, o, y, (x16, s16[5]), (x16, s16[6]), bf(hg_w_proj[0]), bf(s5_w_glu_a[0]),
                bf(s5_w_glu_b[0]), bf(w_out[0]))
    out = _ffn(hx, None, norm_ffn2[0], norm_final,
               bf(ffn2_w_gate[0]), bf(ffn2_w_up[0]), bf(ffn2_w_down[0]))
    return out[None].astype(x.dtype)
```

```python
import functools
import math

import numpy as np
import jax
import jax.numpy as jnp
from jax import lax
from jax.experimental import pallas as pl
from jax.experimental.pallas import tpu as pltpu

EPS = 1e-6
HG_DK = 128
HG_CHUNK = 64
S5_GROUP = 16
S5_STATE = 64
S5_T = 16
S5_SG = 8
LANE = 128
VMEM_LIMIT = 56 * 1024 * 1024

F32 = jnp.float32
BF16 = jnp.bfloat16


def _cparams(sem):
    return pltpu.CompilerParams(dimension_semantics=sem, vmem_limit_bytes=VMEM_LIMIT)


def _rms(v, g):
    return v * lax.rsqrt(jnp.mean(v * v, axis=-1, keepdims=True) + EPS) * g


def _dot(a, b):
    return jnp.dot(a, b, preferred_element_type=F32)


def _dot_nt(a, b):
    return lax.dot_general(a, b, (((1,), (1,)), ((), ())), preferred_element_type=F32)


def _dot_tn(a, b):
    return lax.dot_general(a, b, (((0,), (0,)), ((), ())), preferred_element_type=F32)


def _ffn_kernel(*refs, tm, n_meta):
    if n_meta:
        (x_ref, m_ref, g_ref, gn_ref, wg_ref, wu_ref, wd_ref,
         ox_ref, om_ref, nx_ref, nm_ref, hn_ref, acc_ref) = refs
    else:
        x_ref, g_ref, gn_ref, wg_ref, wu_ref, wd_ref, ox_ref, hn_ref, acc_ref = refs
    j = pl.program_id(1)

    @pl.when(j == 0)
    def _():
        hn_ref[0:tm, :] = _rms(x_ref[...], g_ref[...]).astype(BF16)
        if n_meta:
            hn_ref[tm:tm + n_meta, :] = _rms(m_ref[...], g_ref[...]).astype(BF16)
        acc_ref[...] = jnp.zeros_like(acc_ref)

    hn = hn_ref[...]
    gate = _dot(hn, wg_ref[...].astype(BF16))
    up = _dot(hn, wu_ref[...].astype(BF16))
    act = (gate * jax.nn.sigmoid(gate) * up).astype(BF16)
    acc_ref[...] += _dot(act, wd_ref[...].astype(BF16))

    @pl.when(j == pl.num_programs(1) - 1)
    def _():
        hx = x_ref[...] + 0.5 * acc_ref[0:tm, :]
        if n_meta:
            ox_ref[...] = hx
            nx_ref[...] = _rms(hx, gn_ref[...]).astype(BF16)
            hm = m_ref[...] + 0.5 * acc_ref[tm:tm + n_meta, :]
            om_ref[...] = hm
            nm_ref[...] = _rms(hm, gn_ref[...]).astype(BF16)
        else:
            ox_ref[...] = _rms(hx, gn_ref[...])


def _ffn(hx, hm, g, g_next, wg, wu, wd, *, tm=512, tf=512):
    n, d = hx.shape
    dff = wg.shape[1]
    n_meta = 0 if hm is None else hm.shape[0]
    grid = (n // tm, dff // tf)
    row = lambda i, j: (i, 0)
    const = lambda i, j: (0, 0)
    in_specs = [pl.BlockSpec((tm, d), row)]
    args = [hx]
    if n_meta:
        in_specs.append(pl.BlockSpec((n_meta, d), const))
        args.append(hm)
    in_specs += [pl.BlockSpec((1, d), const), pl.BlockSpec((1, d), const),
                 pl.BlockSpec((d, tf), lambda i, j: (0, j)),
                 pl.BlockSpec((d, tf), lambda i, j: (0, j)),
                 pl.BlockSpec((tf, d), lambda i, j: (j, 0))]
    args += [g.reshape(1, d), g_next.reshape(1, d), wg, wu, wd]
    out_shape = [jax.ShapeDtypeStruct((n, d), F32)]
    out_specs = [pl.BlockSpec((tm, d), row)]
    if n_meta:
        out_shape += [jax.ShapeDtypeStruct((n_meta, d), F32), jax.ShapeDtypeStruct((n, d), BF16),
                      jax.ShapeDtypeStruct((n_meta, d), BF16)]
        out_specs += [pl.BlockSpec((n_meta, d), const), pl.BlockSpec((tm, d), row),
                      pl.BlockSpec((n_meta, d), const)]
    outs = pl.pallas_call(
        functools.partial(_ffn_kernel, tm=tm, n_meta=n_meta),
        grid=grid, in_specs=in_specs, out_specs=out_specs, out_shape=out_shape,
        scratch_shapes=[pltpu.VMEM((tm + n_meta, d), BF16), pltpu.VMEM((tm + n_meta, d), F32)],
        compiler_params=_cparams(("arbitrary", "arbitrary")),
        name="ffn_meta" if n_meta else "ffn_final",
    )(*args)
    return outs if n_meta else outs[0]


def _inproj_kernel(*refs, tm, n_meta, with_lb):
    if with_lb:
        x_ref, m_ref, w_ref, coef_ref, lb_ref, ox_ref, om_ref, lhs_ref = refs
    else:
        x_ref, m_ref, w_ref, coef_ref, ox_ref, om_ref, lhs_ref = refs

    @pl.when(pl.program_id(1) == 0)
    def _():
        lhs_ref[0:tm, :] = x_ref[...]
        lhs_ref[tm:tm + n_meta, :] = m_ref[...]

    tn = w_ref.shape[1]
    sw = math.gcd(tn, 2 * LANE)
    for s in range(tn // sw):
        cs = slice(s * sw, (s + 1) * sw)
        res = _dot(lhs_ref[...], w_ref[:, cs])
        sig = jax.nn.sigmoid(res)
        coef = coef_ref[:, cs]
        a, c, b, d = coef[0:1, :], coef[1:2, :], coef[2:3, :], 0.0
        if with_lb:
            lbl = lb_ref[:, cs]
            e = jnp.exp(lbl - jnp.max(lbl, axis=0, keepdims=True))
            lb = e[0:1, :] / jnp.sum(e, axis=0, keepdims=True)
            flag = coef[3:4, :]
            b = b + flag * (1.0 - lb)
            d = flag * lb
        out = res * (a * sig + c) + (b * sig + d)
        ox_ref[:, cs] = out[0:tm, :].astype(ox_ref.dtype)
        om_ref[:, cs] = out[tm:tm + n_meta, :].astype(om_ref.dtype)


def _inproj(nx, nm, w, coef, lower_bounds, lb_cols, dtype, *, tm=1024, tn=1792):
    n, d = nx.shape
    n_meta = nm.shape[0]
    cols = w.shape[1]
    tm = min(tm, n)
    with_lb = lower_bounds is not None
    tn = math.gcd(tn, cols)
    assert tn % LANE == 0 and n % tm == 0
    in_specs = [pl.BlockSpec((tm, d), lambda i, j: (i, 0)),
                pl.BlockSpec((n_meta, d), lambda i, j: (0, 0)),
                pl.BlockSpec((d, tn), lambda i, j: (0, j)),
                pl.BlockSpec((4, tn), lambda i, j: (0, j))]
    args = [nx, nm, w, coef]
    if with_lb:
        nlb, k = lower_bounds.shape
        lb_full = jnp.pad(lower_bounds.astype(F32), ((0, 0), (lb_cols, cols - lb_cols - k)))
        in_specs.append(pl.BlockSpec((nlb, tn), lambda i, j: (0, j)))
        args.append(lb_full)
    ox, om = pl.pallas_call(
        functools.partial(_inproj_kernel, tm=tm, n_meta=n_meta, with_lb=with_lb),
        grid=(n // tm, cols // tn),
        in_specs=in_specs,
        out_specs=[pl.BlockSpec((tm, tn), lambda i, j: (i, j)),
                   pl.BlockSpec((pl.Squeezed(), n_meta, tn), lambda i, j: (i, 0, j))],
        out_shape=[jax.ShapeDtypeStruct((n, cols), dtype),
                   jax.ShapeDtypeStruct((n // tm, n_meta, cols), dtype)],
        scratch_shapes=[pltpu.VMEM((tm + n_meta, d), BF16)],
        compiler_params=_cparams(("arbitrary", "arbitrary")),
        name="mixer_inproj_" + jnp.dtype(dtype).name,
    )(*args)
    return ox, om[0]


def _split3(x):
    hi = x.astype(BF16)
    r1 = x - hi.astype(F32)
    mid = r1.astype(BF16)
    lo = (r1 - mid.astype(F32)).astype(BF16)
    return hi, mid, lo


def _dot3(m01, parts):
    return _dot(m01, parts[0]) + _dot(m01, parts[1]) + _dot(m01, parts[2])


def _tri(n, rep=1):
    r = lax.broadcasted_iota(jnp.int32, (n * rep, n), 0)
    c = lax.broadcasted_iota(jnp.int32, (n * rep, n), 1)
    return jnp.where(r >= c * rep, 1.0, 0.0).astype(BF16)


def _hgrn_tables(c):
    sums = [np.tril(np.ones((c, c)))]
    level = np.full((c, c), -1, np.int32)
    t_idx, s_idx = np.meshgrid(np.arange(c), np.arange(c), indexing="ij")
    halves = []
    h = c // 2
    while h >= 1:
        if h < 8:
            m = np.zeros((c, c))
            for r in range(c):
                mid = (r // (2 * h)) * 2 * h + h
                if r < mid:
                    m[r, r + 1:mid] = 1.0
                else:
                    m[r, mid:r + 1] = 1.0
            sums.append(m)
        pair = 2 * h
        level[(t_idx // pair == s_idx // pair) & (s_idx % pair < h) & (t_idx % pair >= h)] = len(halves)
        halves.append(h)
        h //= 2
    assert len(halves) % 2 == 0
    level[np.arange(c), np.arange(c)] = len(halves)
    owner = np.full((c, 2 * c), -1, np.int32)
    owner[:, :c] = np.where((level >= 0) & (level % 2 == 0), level // 2, -1)
    owner[:, c:] = np.where(level % 2 == 1, level // 2, -1)
    return np.concatenate(sums, axis=0), owner, tuple(halves)


def _hgrn_score_products(q, k, x, halves):
    C = HG_CHUNK
    b = x[0:C]
    zero = jnp.zeros((C, HG_DK), BF16)
    diag = jnp.sum(q * k, axis=-1, keepdims=True)
    qts, kts = [], []
    fine = 0
    for h in halves:
        if h >= 8:
            none = jnp.zeros((h, HG_DK), F32)
            qh, kh = [], []
            for r in range(0, C, 2 * h):
                b_mid = b[r + h - 1:r + h, :]
                kh += [k[r:r + h] * jnp.exp(b_mid - b[r:r + h]), none]
                qh += [none, q[r + h:r + 2 * h] * jnp.exp(b[r + h:r + 2 * h] - b_mid)]
            qts.append(jnp.concatenate(qh, axis=0).astype(BF16))
            kts.append(jnp.concatenate(kh, axis=0).astype(BF16))
        else:
            fine += 1
            e = jnp.exp(x[fine * C:(fine + 1) * C])
            qts.append((q * e).astype(BF16))
            kts.append((k * e).astype(BF16))
    prods = []
    for p in range(len(halves) // 2):
        lhs = jnp.concatenate([qts[2 * p], qts[2 * p + 1]], axis=1)
        rhs = jnp.concatenate([jnp.concatenate([kts[2 * p], zero], axis=1),
                               jnp.concatenate([zero, kts[2 * p + 1]], axis=1)], axis=0)
        prods.append(_dot_nt(lhs, rhs))
    return prods, diag


def _hgrn_pair(q2, f2, v2, gate2, st2, sums, owner, halves):
    C, D = HG_CHUNK, HG_DK
    halfs = (slice(0, D), slice(D, 2 * D))
    k2 = 1.0 - f2
    x2 = _dot(sums, jnp.concatenate(_split3(jnp.log(f2)), axis=0))
    yield
    b2 = x2[0:C]
    b_last = b2[C - 1:C, :]
    kdec2 = (k2 * jnp.exp(b_last - b2)).astype(BF16)
    qe2 = (q2 * jnp.exp(b2)).astype(BF16)
    st2b = st2.astype(BF16)
    scores = [_hgrn_score_products(q2[:, c], k2[:, c], x2[:, c], halves) for c in halfs]
    yield
    zero = jnp.zeros((2 * C, D), BF16)
    os, vvts, kds = [], [], []
    for j, c in enumerate(halfs):
        prods, diag = scores[j]
        a = jnp.where(owner == len(prods), diag, 0.0)
        for p, prod in enumerate(prods):
            a = jnp.where(owner == p, prod, a)
        v = v2[:, c]
        vvt = jnp.concatenate([v, v], axis=0).T
        os.append(_dot_nt(jnp.concatenate([qe2[:, c], a.astype(BF16)], axis=1),
                          jnp.concatenate([st2b[:, c], vvt], axis=1)))
        vvts.append(vvt)
        kd = jnp.concatenate([kdec2[:, c], jnp.zeros((C, D), BF16)], axis=0)
        kds.append(jnp.concatenate([kd, zero] if j == 0 else [zero, kd], axis=1))
    st_new = st2 * jnp.exp(b_last) + _dot(jnp.concatenate(vvts, axis=1), jnp.concatenate(kds, axis=0))
    yield
    ons = [o * lax.rsqrt(jnp.mean(o * o, axis=-1, keepdims=True) + EPS) for o in os]
    on2 = jnp.concatenate(ons, axis=1) * gate2
    return on2.astype(BF16), st_new


def _interleave(gens):
    out = [None] * len(gens)
    live = list(range(len(gens)))
    while live:
        for g in list(live):
            try:
                next(gens[g])
            except StopIteration as stop:
                out[g] = stop.value
                live.remove(g)
    return out


def _hgrn_kernel(q_ref, f_ref, v_ref, go_ref, fm_ref, vm_ref, gn_ref, sums_ref, owner_ref,
                 o_ref, st_ref, *, n_chunks, n_meta, hb, halves):
    i = pl.program_id(0)
    p0 = pl.program_id(1) * (hb // 2)
    C = HG_CHUNK
    W = 2 * HG_DK

    @pl.when(i == 0)
    def _():
        tri_m = _tri(n_meta)
        for pair in range(hb // 2):
            sts = []
            for hh in (2 * pair, 2 * pair + 1):
                cols = slice(hh * HG_DK, (hh + 1) * HG_DK)
                fm = fm_ref[:, cols]
                bm = _dot3(tri_m, _split3(jnp.log(fm)))
                kdec = ((1.0 - fm) * jnp.exp(bm[n_meta - 1:n_meta, :] - bm)).astype(BF16)
                sts.append(_dot_tn(vm_ref[:, cols], kdec))
            st_ref[p0 + pair] = jnp.concatenate(sts, axis=1)

    def chunk(c, states):
        r0 = pl.multiple_of(c * C, C)
        owner = owner_ref[...]
        sums = sums_ref[...]
        gens = []
        for pair in range(hb // 2):
            cols = slice(pair * W, (pair + 1) * W)
            gate2 = gn_ref[:, cols] * go_ref[pl.ds(r0, C), cols].astype(F32)
            gens.append(_hgrn_pair(q_ref[pl.ds(r0, C), cols], f_ref[pl.ds(r0, C), cols],
                                   v_ref[pl.ds(r0, C), cols], gate2, states[pair],
                                   sums, owner, halves))
        results = _interleave(gens)
        o_ref[pl.ds(r0, C), :] = jnp.concatenate([r[0] for r in results], axis=1)
        return tuple(r[1] for r in results)

    states = lax.fori_loop(0, n_chunks, chunk, tuple(st_ref[p0 + p] for p in range(hb // 2)))
    for pair in range(hb // 2):
        st_ref[p0 + pair] = states[pair]


def _hgrn(q, f, v, go, fm, vm, hg_norm, *, rb=512, hb=12):
    n, width = q[0].shape[0], hg_norm.shape[-1]
    heads = width // HG_DK
    hb = min(hb, heads)
    bw = hb * HG_DK
    n_meta = fm.shape[0]
    assert hb % 2 == 0 and heads % hb == 0
    sums, owner, halves = _hgrn_tables(HG_CHUNK)
    sums3 = jnp.asarray(np.concatenate([sums] * 3, axis=1), BF16)
    blk = lambda i, h: (i, h)
    mblk = lambda i, h: (0, h)
    const = lambda i, h: (0, 0)

    def view(col0):
        assert col0 % bw == 0
        return pl.BlockSpec((rb, bw), lambda i, h: (i, col0 // bw + h))

    return pl.pallas_call(
        functools.partial(_hgrn_kernel, n_chunks=rb // HG_CHUNK, n_meta=n_meta, hb=hb,
                          halves=halves),
        grid=(n // rb, heads // hb),
        in_specs=[view(q[1]), view(f[1]), view(v[1]), view(go[1])]
        + [pl.BlockSpec((n_meta, hb * HG_DK), mblk)] * 2
        + [pl.BlockSpec((1, hb * HG_DK), mblk),
           pl.BlockSpec(sums3.shape, const), pl.BlockSpec(owner.shape, const)],
        out_specs=pl.BlockSpec((rb, hb * HG_DK), blk),
        out_shape=jax.ShapeDtypeStruct((n, width), BF16),
        scratch_shapes=[pltpu.VMEM((heads // 2, HG_DK, 2 * HG_DK), F32)],
        compiler_params=_cparams(("arbitrary", "arbitrary")),
        name="hgrn2",
    )(q[0], f[0], v[0], go[0], fm, vm, hg_norm.reshape(1, width), sums3, jnp.asarray(owner))


def _s5_weights_kernel(lr_ref, li_ref, dt_ref, bre_ref, bim_ref, cre_ref, cim_ref, dsk_ref,
                       m_ref, win_ref, wout_ref, at_ref):
    T = S5_T
    ns = lr_ref.shape[1]
    nc = cre_ref.shape[0]
    lr, li, dt = lr_ref[...], li_ref[...], dt_ref[...]

    def powers(tau):
        mag = jnp.exp(lr * dt * tau)
        ang = li * dt * tau
        return mag * jnp.cos(ang), mag * jnp.sin(ang)

    a_re, a_im = powers(1.0)
    num_re, num_im = a_re - 1.0, a_im
    den = lr * lr + li * li
    coef_re = (num_re * lr + num_im * li) / den
    coef_im = (num_im * lr - num_re * li) / den
    br, bi = bre_ref[...], bim_ref[...]
    bbar_re = coef_re * br - coef_im * bi
    bbar_im = coef_re * bi + coef_im * br

    pw = [powers(float(tau)) for tau in range(T + 1)]
    for s in range(T):
        p_re, p_im = pw[T - 1 - s]
        win_ref[s * nc:(s + 1) * nc, 0:ns] = (bbar_re * p_re - bbar_im * p_im).astype(BF16)
        win_ref[s * nc:(s + 1) * nc, ns:2 * ns] = (bbar_re * p_im + bbar_im * p_re).astype(BF16)
    at_ref[:, 0:ns] = pw[T][0]
    at_ref[:, ns:2 * ns] = pw[T][1]

    cr, ci = cre_ref[...], cim_ref[...]
    hp = lax.Precision.HIGHEST
    nt = (((1,), (1,)), ((), ()))
    zero = jnp.zeros((nc, nc), BF16)
    for tau in range(T + 1):
        p_re, p_im = pw[tau]
        ca_re = cr * p_re - ci * p_im
        ca_im = cr * p_im + ci * p_re
        if tau >= 1:
            t = tau - 1
            wout_ref[0:ns, t * nc:(t + 1) * nc] = ca_re.T.astype(BF16)
            wout_ref[ns:2 * ns, t * nc:(t + 1) * nc] = (-ca_im).T.astype(BF16)
        if tau < T:
            kt = (lax.dot_general(bbar_re, ca_re, nt, precision=hp, preferred_element_type=F32)
                  - lax.dot_general(bbar_im, ca_im, nt, precision=hp, preferred_element_type=F32))
            if tau == 0:
                kt = kt + dsk_ref[...]
            kt = kt.astype(BF16)
            for s in range(T - tau):
                m_ref[s * nc:(s + 1) * nc, (s + tau) * nc:(s + tau + 1) * nc] = kt
            if tau:
                for s in range(T - tau):
                    m_ref[(s + tau) * nc:(s + tau + 1) * nc, s * nc:(s + 1) * nc] = zero


def _s5_weights(lam_re, lam_im, log_step, b_re, b_im, c_re, c_im, d_skip):
    G, N = lam_re.shape
    P = d_skip.shape[1]
    nsg = G // S5_SG
    ns, nc = S5_SG * N, S5_SG * P
    T = S5_T
    dt = jnp.exp(log_step.astype(F32))
    eye = jnp.eye(S5_SG, dtype=F32)

    def rows(a):
        return a.astype(F32).reshape(nsg, 1, ns)

    def embed_b(b):
        b = b.astype(F32).reshape(nsg, S5_SG, N, P)
        return jnp.einsum('zgnq,gh->zhqgn', b, eye).reshape(nsg, nc, ns)

    def embed_c(c):
        c = c.astype(F32).reshape(nsg, S5_SG, P, N)
        return jnp.einsum('zgpn,gh->zhpgn', c, eye).reshape(nsg, nc, ns)

    dtb = jnp.broadcast_to(dt[:, None], (G, N))
    d_diag = jax.vmap(jnp.diag)(d_skip.astype(F32).reshape(nsg, nc))
    sq = pl.Squeezed()
    spec_r = pl.BlockSpec((sq, 1, ns), lambda z: (z, 0, 0))
    spec_b = pl.BlockSpec((sq, nc, ns), lambda z: (z, 0, 0))
    return pl.pallas_call(
        _s5_weights_kernel,
        grid=(nsg,),
        in_specs=[spec_r, spec_r, spec_r, spec_b, spec_b, spec_b, spec_b,
                  pl.BlockSpec((sq, nc, nc), lambda z: (z, 0, 0))],
        out_specs=[pl.BlockSpec((sq, T * nc, T * nc), lambda z: (z, 0, 0)),
                   pl.BlockSpec((sq, T * nc, 2 * ns), lambda z: (z, 0, 0)),
                   pl.BlockSpec((sq, 2 * ns, T * nc), lambda z: (z, 0, 0)),
                   pl.BlockSpec((sq, 1, 2 * ns), lambda z: (z, 0, 0))],
        out_shape=[jax.ShapeDtypeStruct((nsg, T * nc, T * nc), BF16),
                   jax.ShapeDtypeStruct((nsg, T * nc, 2 * ns), BF16),
                   jax.ShapeDtypeStruct((nsg, 2 * ns, T * nc), BF16),
                   jax.ShapeDtypeStruct((nsg, 1, 2 * ns), F32)],
        compiler_params=_cparams(("arbitrary",)),
        name="s5_weights",
    )(rows(lam_re), rows(lam_im), rows(dtb),
      embed_b(b_re), embed_b(b_im), embed_c(c_re), embed_c(c_im), d_diag)


def _s5_scan_kernel(u_ref, um_ref, m_ref, win_ref, wout_ref, at_ref, y_ref,
                    uc_ref, v_ref, xp_ref, *, n_chunks):
    T = S5_T
    nc = u_ref.shape[1]
    ns = at_ref.shape[1] // 2
    for t in range(T):
        uc_ref[0:n_chunks, t * nc:(t + 1) * nc] = u_ref[pl.ds(t, n_chunks, stride=T), :].astype(BF16)
        uc_ref[n_chunks:n_chunks + 1, t * nc:(t + 1) * nc] = um_ref[t:t + 1, :].astype(BF16)
    pad = uc_ref.shape[0] - n_chunks - 1
    uc_ref[n_chunks + 1:, :] = jnp.zeros((pad, T * nc), BF16)
    uc = uc_ref[...]
    v_ref[...] = _dot(uc, win_ref[...])
    a_re, a_im = at_ref[:, 0:ns], at_ref[:, ns:2 * ns]

    x_re0 = v_ref[n_chunks:n_chunks + 1, 0:ns]
    x_im0 = v_ref[n_chunks:n_chunks + 1, ns:2 * ns]
    xp_ref[n_chunks:, :] = jnp.zeros((pad + 1, 2 * ns), F32)

    def step(c, carry):
        x_re, x_im = carry
        xp_ref[pl.ds(c, 1), 0:ns] = x_re
        xp_ref[pl.ds(c, 1), ns:2 * ns] = x_im
        v_re = v_ref[pl.ds(c, 1), 0:ns]
        v_im = v_ref[pl.ds(c, 1), ns:2 * ns]
        return (a_re * x_re - a_im * x_im + v_re, a_re * x_im + a_im * x_re + v_im)

    lax.fori_loop(0, n_chunks, step, (x_re0, x_im0))
    yc = _dot(uc, m_ref[...]) + _dot(xp_ref[...].astype(BF16), wout_ref[...])
    for t in range(T):
        y_ref[pl.ds(t, n_chunks, stride=T), :] = yc[0:n_chunks, t * nc:(t + 1) * nc]


def _s5_scan(u, um, m, win, wout, at):
    u, col0 = u
    n, width = u.shape[0], um.shape[1]
    T = S5_T
    nsg = m.shape[0]
    nc = width // nsg
    ns2 = at.shape[2]
    assert um.shape[0] == T and n % T == 0 and col0 % nc == 0
    n_chunks = n // T
    rows = -(-(n_chunks + 1) // 16) * 16
    sq = pl.Squeezed()
    one = pl.Buffered(1)
    return pl.pallas_call(
        functools.partial(_s5_scan_kernel, n_chunks=n_chunks),
        grid=(nsg,),
        in_specs=[pl.BlockSpec((n, nc), lambda z: (0, col0 // nc + z)),
                  pl.BlockSpec((T, nc), lambda z: (0, z)),
                  pl.BlockSpec((sq, T * nc, T * nc), lambda z: (z, 0, 0), pipeline_mode=one),
                  pl.BlockSpec((sq, T * nc, ns2), lambda z: (z, 0, 0), pipeline_mode=one),
                  pl.BlockSpec((sq, ns2, T * nc), lambda z: (z, 0, 0), pipeline_mode=one),
                  pl.BlockSpec((sq, 1, ns2), lambda z: (z, 0, 0))],
        out_specs=pl.BlockSpec((n, nc), lambda z: (0, z)),
        out_shape=jax.ShapeDtypeStruct((n, width), F32),
        scratch_shapes=[pltpu.VMEM((rows, T * nc), BF16), pltpu.VMEM((rows, ns2), F32),
                        pltpu.VMEM((rows, ns2), F32)],
        compiler_params=_cparams(("arbitrary",)),
        name="s5_scan",
    )(u, um, m, win, wout, at)


def _merge_kernel(*refs, pieces):
    h_ref, o_ref, y_ref = refs[0:3]
    gh_refs, gs_refs = refs[3:3 + pieces], refs[3 + pieces:3 + 2 * pieces]
    wp_ref, wa_ref, wb_ref, wo_ref, out_ref = refs[3 + 2 * pieces:]
    gh = jnp.concatenate([r[...] for r in gh_refs], axis=1).astype(F32)
    gs = jnp.concatenate([r[...] for r in gs_refs], axis=1).astype(F32)
    y_hg = _dot(o_ref[...], wp_ref[...])
    z = jax.nn.gelu(y_ref[...]).astype(BF16)
    y_s5 = _dot(z, wa_ref[...]) * jax.nn.sigmoid(_dot(z, wb_ref[...]))
    merged = gh * y_hg + gs * y_s5
    out_ref[...] = h_ref[...] + _dot(merged.astype(BF16), wo_ref[...])


def _merge(hx, o, y, gh, gs, wp, wa, wb, wo, *, tm=256):
    n, d = hx.shape
    row = lambda i: (i, 0)
    const = lambda i: (0, 0)
    gw = functools.reduce(math.gcd, (gh[1], gs[1], d))
    assert gw % LANE == 0
    pieces = d // gw

    def resident(shape):
        return pl.BlockSpec(shape, const, pipeline_mode=pl.Buffered(1))

    def view(col0):
        return [pl.BlockSpec((tm, gw), functools.partial(lambda i, k: (i, k), k=col0 // gw + k))
                for k in range(pieces)]

    return pl.pallas_call(
        functools.partial(_merge_kernel, pieces=pieces),
        grid=(n // tm,),
        in_specs=[pl.BlockSpec((tm, d), row), pl.BlockSpec((tm, o.shape[1]), row),
                  pl.BlockSpec((tm, y.shape[1]), row)] + view(gh[1]) + view(gs[1])
        + [resident(wp.shape), resident(wa.shape), resident(wb.shape), resident(wo.shape)],
        out_specs=pl.BlockSpec((tm, d), row),
        out_shape=jax.ShapeDtypeStruct((n, d), F32),
        compiler_params=_cparams(("arbitrary",)),
        name="mixer_merge",
    )(hx, o, y, *([gh[0]] * pieces), *([gs[0]] * pieces), wp, wa, wb, wo)


def kernel(x, meta_tokens, norm_ffn1, ffn1_w_gate, ffn1_w_up, ffn1_w_down, norm_mix, w_in, hg_lower_bounds, hg_norm, hg_w_proj, s5_lam_re, s5_lam_im, s5_log_step, s5_b_re, s5_b_im, s5_c_re, s5_c_im, s5_d, s5_w_glu_a, s5_w_glu_b, w_out, norm_ffn2, ffn2_w_gate, ffn2_w_up, ffn2_w_down, norm_final):
    batch, seq, d = x.shape
    depth = norm_ffn1.shape[0]
    assert batch == 1 and depth == 1
    hg_k = hg_lower_bounds.shape[1]
    hg_w = hg_norm.shape[1]
    s5_w = s5_w_glu_a.shape[1]
    bf = lambda w: w.astype(BF16)

    hx = x[0].astype(F32)
    hm = meta_tokens.astype(F32)
    hx, hm, nx, nm = _ffn(hx, hm, norm_ffn1[0], norm_mix[0],
                          ffn1_w_gate[0], ffn1_w_up[0], bf(ffn1_w_down[0]))

    sizes = (hg_k, hg_k, hg_w, hg_w, s5_w, d, d)
    assert sum(sizes) == w_in.shape[2]
    off = [0]
    for s in sizes:
        off.append(off[-1] + s)

    def group(ids, rows):
        w = jnp.concatenate([w_in[0][:, off[k]:off[k + 1]] for k in ids], axis=1).astype(BF16)
        coef = jnp.concatenate([jnp.broadcast_to(jnp.asarray(rows[k], F32)[:, None], (4, sizes[k]))
                                for k in ids], axis=1)
        starts = {}
        col = 0
        for k in ids:
            starts[k] = col
            col += sizes[k]
        return w, coef, starts

    silu, ident, sigm = (1.0, 0.0, 0.0, 0.0), (0.0, 1.0, 0.0, 0.0), (0.0, 0.0, 1.0, 0.0)
    rows = {0: (HG_DK ** -0.5, 0.0, 0.0, 0.0), 1: (0.0, 0.0, 0.0, 1.0), 2: ident, 3: silu,
            4: ident, 5: sigm, 6: sigm}
    w32, c32, s32 = group((0, 1, 4), rows)
    w16, c16, s16 = group((2, 3, 5, 6), rows)
    x32, m32 = _inproj(nx, nm, w32, c32, hg_lower_bounds, s32[1], F32)
    x16, m16 = _inproj(nx, nm, w16, c16, None, 0, BF16)

    o = _hgrn((x32, s32[0]), (x32, s32[1]), (x16, s16[2]), (x16, s16[3]),
              m32[:, s32[1]:s32[1] + hg_k], m16[:, s16[2]:s16[2] + hg_w], hg_norm[0])
    m, win, wout, at = _s5_weights(s5_lam_re[0], s5_lam_im[0], s5_log_step[0], s5_b_re[0],
                                   s5_b_im[0], s5_c_re[0], s5_c_im[0], s5_d[0])
    y = _s5_scan((x32, s32[4]), m32[:, s32[4]:s32[4] + s5_w], m, win, wout, at)
    hx = _merge(hx, o, y, (x16, s16[5]), (x16, s16[6]), bf(hg_w_proj[0]), bf(s5_w_glu_a[0]),
                bf(s5_w_glu_b[0]), bf(w_out[0]))
    out = _ffn(hx, None, norm_ffn2[0], norm_final,
               ffn2_w_gate[0], ffn2_w_up[0], bf(ffn2_w_down[0]))
    return out[None].astype(x.dtype)
```

```python
import functools
import math

import numpy as np
import jax
import jax.numpy as jnp
from jax import lax
from jax.experimental import pallas as pl
from jax.experimental.pallas import tpu as pltpu

EPS = 1e-6
HG_DK = 128
HG_CHUNK = 64
S5_GROUP = 16
S5_STATE = 64
S5_T = 16
S5_SG = 8
LANE = 128
VMEM_LIMIT = 56 * 1024 * 1024

F32 = jnp.float32
BF16 = jnp.bfloat16


def _cparams(sem):
    return pltpu.CompilerParams(dimension_semantics=sem, vmem_limit_bytes=VMEM_LIMIT)


def _rms(v, g):
    return v * lax.rsqrt(jnp.mean(v * v, axis=-1, keepdims=True) + EPS) * g


def _dot(a, b):
    return jnp.dot(a, b, preferred_element_type=F32)


def _dot_nt(a, b):
    return lax.dot_general(a, b, (((1,), (1,)), ((), ())), preferred_element_type=F32)


def _dot_tn(a, b):
    return lax.dot_general(a, b, (((0,), (0,)), ((), ())), preferred_element_type=F32)


def _ffn_kernel(*refs, tm, n_meta, n_cast):
    n_in = 7 if n_meta else 6
    n_out = 4 if n_meta else 1
    cast_in = refs[n_in:n_in + n_cast]
    cast_out = refs[n_in + n_cast + n_out:n_in + 2 * n_cast + n_out]
    main = refs[:n_in] + refs[n_in + n_cast:n_in + n_cast + n_out] + refs[n_in + 2 * n_cast + n_out:]
    if n_meta:
        (x_ref, m_ref, g_ref, gn_ref, wg_ref, wu_ref, wd_ref,
         ox_ref, om_ref, nx_ref, nm_ref, hn_ref, acc_ref) = main
    else:
        x_ref, g_ref, gn_ref, wg_ref, wu_ref, wd_ref, ox_ref, hn_ref, acc_ref = main
    j = pl.program_id(1)
    for src, dst in zip(cast_in, cast_out):
        dst[...] = src[...].astype(BF16)

    @pl.when(j == 0)
    def _():
        hn_ref[0:tm, :] = _rms(x_ref[...], g_ref[...]).astype(BF16)
        if n_meta:
            hn_ref[tm:tm + n_meta, :] = _rms(m_ref[...], g_ref[...]).astype(BF16)
        acc_ref[...] = jnp.zeros_like(acc_ref)

    hn = hn_ref[...]
    gate = _dot(hn, wg_ref[...])
    up = _dot(hn, wu_ref[...])
    act = (gate * jax.nn.sigmoid(gate) * up).astype(BF16)
    acc_ref[...] += _dot(act, wd_ref[...])

    @pl.when(j == pl.num_programs(1) - 1)
    def _():
        hx = x_ref[...] + 0.5 * acc_ref[0:tm, :]
        if n_meta:
            ox_ref[...] = hx
            nx_ref[...] = _rms(hx, gn_ref[...]).astype(BF16)
            hm = m_ref[...] + 0.5 * acc_ref[tm:tm + n_meta, :]
            om_ref[...] = hm
            nm_ref[...] = _rms(hm, gn_ref[...]).astype(BF16)
        else:
            ox_ref[...] = _rms(hx, gn_ref[...])


def _cast_spec(shape, grid):
    (r, c), (gi, gj) = shape, grid
    if r % gi == 0 and c % gj == 0 and (r // gi) % 16 == 0 and (c // gj) % LANE == 0:
        return pl.BlockSpec((r // gi, c // gj), lambda i, j: (i, j))
    assert r % gj == 0 and c % gi == 0 and (r // gj) % 16 == 0 and (c // gi) % LANE == 0
    return pl.BlockSpec((r // gj, c // gi), lambda i, j: (j, i))


def _ffn(hx, hm, g, g_next, wg, wu, wd, cast=(), *, tm=512, tf=512):
    n, d = hx.shape
    dff = wg.shape[1]
    n_meta = 0 if hm is None else hm.shape[0]
    grid = (n // tm, dff // tf)
    cast_specs = [_cast_spec(w.shape, grid) for w in cast]
    row = lambda i, j: (i, 0)
    const = lambda i, j: (0, 0)
    in_specs = [pl.BlockSpec((tm, d), row)]
    args = [hx]
    if n_meta:
        in_specs.append(pl.BlockSpec((n_meta, d), const))
        args.append(hm)
    in_specs += [pl.BlockSpec((1, d), const), pl.BlockSpec((1, d), const),
                 pl.BlockSpec((d, tf), lambda i, j: (0, j)),
                 pl.BlockSpec((d, tf), lambda i, j: (0, j)),
                 pl.BlockSpec((tf, d), lambda i, j: (j, 0))]
    args += [g.reshape(1, d), g_next.reshape(1, d), wg, wu, wd]
    out_shape = [jax.ShapeDtypeStruct((n, d), F32)]
    out_specs = [pl.BlockSpec((tm, d), row)]
    if n_meta:
        out_shape += [jax.ShapeDtypeStruct((n_meta, d), F32), jax.ShapeDtypeStruct((n, d), BF16),
                      jax.ShapeDtypeStruct((n_meta, d), BF16)]
        out_specs += [pl.BlockSpec((n_meta, d), const), pl.BlockSpec((tm, d), row),
                      pl.BlockSpec((n_meta, d), const)]
    in_specs += cast_specs
    args += list(cast)
    out_specs += cast_specs
    out_shape += [jax.ShapeDtypeStruct(w.shape, BF16) for w in cast]
    outs = pl.pallas_call(
        functools.partial(_ffn_kernel, tm=tm, n_meta=n_meta, n_cast=len(cast)),
        grid=grid, in_specs=in_specs, out_specs=out_specs, out_shape=out_shape,
        scratch_shapes=[pltpu.VMEM((tm + n_meta, d), BF16), pltpu.VMEM((tm + n_meta, d), F32)],
        compiler_params=_cparams(("arbitrary", "arbitrary")),
        name="ffn_meta" if n_meta else "ffn_final",
    )(*args)
    return outs if (n_meta or cast) else outs[0]


def _inproj_kernel(*refs, tm, n_meta, with_lb):
    if with_lb:
        x_ref, m_ref, w_ref, coef_ref, lb_ref, ox_ref, om_ref, lhs_ref = refs
    else:
        x_ref, m_ref, w_ref, coef_ref, ox_ref, om_ref, lhs_ref = refs

    @pl.when(pl.program_id(1) == 0)
    def _():
        lhs_ref[0:tm, :] = x_ref[...]
        lhs_ref[tm:tm + n_meta, :] = m_ref[...]

    tn = w_ref.shape[1]
    sw = math.gcd(tn, 2 * LANE)
    for s in range(tn // sw):
        cs = slice(s * sw, (s + 1) * sw)
        res = _dot(lhs_ref[...], w_ref[:, cs])
        sig = jax.nn.sigmoid(res)
        coef = coef_ref[:, cs]
        a, c, b, d = coef[0:1, :], coef[1:2, :], coef[2:3, :], 0.0
        if with_lb:
            lbl = lb_ref[:, cs]
            e = jnp.exp(lbl - jnp.max(lbl, axis=0, keepdims=True))
            lb = e[0:1, :] / jnp.sum(e, axis=0, keepdims=True)
            flag = coef[3:4, :]
            b = b + flag * (1.0 - lb)
            d = flag * lb
        out = res * (a * sig + c) + (b * sig + d)
        ox_ref[:, cs] = out[0:tm, :].astype(ox_ref.dtype)
        om_ref[:, cs] = out[tm:tm + n_meta, :].astype(om_ref.dtype)


def _inproj(nx, nm, w, coef, lower_bounds, lb_cols, dtype, *, tm=1024, tn=1792):
    n, d = nx.shape
    n_meta = nm.shape[0]
    cols = w.shape[1]
    tm = min(tm, n)
    with_lb = lower_bounds is not None
    tn = math.gcd(tn, cols)
    assert tn % LANE == 0 and n % tm == 0
    in_specs = [pl.BlockSpec((tm, d), lambda i, j: (i, 0)),
                pl.BlockSpec((n_meta, d), lambda i, j: (0, 0)),
                pl.BlockSpec((d, tn), lambda i, j: (0, j)),
                pl.BlockSpec((4, tn), lambda i, j: (0, j))]
    args = [nx, nm, w, coef]
    if with_lb:
        nlb, k = lower_bounds.shape
        lb_full = jnp.pad(lower_bounds.astype(F32), ((0, 0), (lb_cols, cols - lb_cols - k)))
        in_specs.append(pl.BlockSpec((nlb, tn), lambda i, j: (0, j)))
        args.append(lb_full)
    ox, om = pl.pallas_call(
        functools.partial(_inproj_kernel, tm=tm, n_meta=n_meta, with_lb=with_lb),
        grid=(n // tm, cols // tn),
        in_specs=in_specs,
        out_specs=[pl.BlockSpec((tm, tn), lambda i, j: (i, j)),
                   pl.BlockSpec((pl.Squeezed(), n_meta, tn), lambda i, j: (i, 0, j))],
        out_shape=[jax.ShapeDtypeStruct((n, cols), dtype),
                   jax.ShapeDtypeStruct((n // tm, n_meta, cols), dtype)],
        scratch_shapes=[pltpu.VMEM((tm + n_meta, d), BF16)],
        compiler_params=_cparams(("arbitrary", "arbitrary")),
        name="mixer_inproj_" + jnp.dtype(dtype).name,
    )(*args)
    return ox, om[0]


def _split3(x):
    hi = x.astype(BF16)
    r1 = x - hi.astype(F32)
    mid = r1.astype(BF16)
    lo = (r1 - mid.astype(F32)).astype(BF16)
    return hi, mid, lo


def _dot3(m01, parts):
    return _dot(m01, parts[0]) + _dot(m01, parts[1]) + _dot(m01, parts[2])


def _tri(n, rep=1):
    r = lax.broadcasted_iota(jnp.int32, (n * rep, n), 0)
    c = lax.broadcasted_iota(jnp.int32, (n * rep, n), 1)
    return jnp.where(r >= c * rep, 1.0, 0.0).astype(BF16)


def _hgrn_tables(c):
    sums = [np.tril(np.ones((c, c)))]
    level = np.full((c, c), -1, np.int32)
    t_idx, s_idx = np.meshgrid(np.arange(c), np.arange(c), indexing="ij")
    halves = []
    h = c // 2
    while h >= 1:
        if h < 8:
            m = np.zeros((c, c))
            for r in range(c):
                mid = (r // (2 * h)) * 2 * h + h
                if r < mid:
                    m[r, r + 1:mid] = 1.0
                else:
                    m[r, mid:r + 1] = 1.0
            sums.append(m)
        pair = 2 * h
        level[(t_idx // pair == s_idx // pair) & (s_idx % pair < h) & (t_idx % pair >= h)] = len(halves)
        halves.append(h)
        h //= 2
    assert len(halves) % 2 == 0
    level[np.arange(c), np.arange(c)] = len(halves)
    owner = np.full((c, 2 * c), -1, np.int32)
    owner[:, :c] = np.where((level >= 0) & (level % 2 == 0), level // 2, -1)
    owner[:, c:] = np.where(level % 2 == 1, level // 2, -1)
    return np.concatenate(sums, axis=0), owner, tuple(halves)


def _hgrn_score_products(q, k, x, halves):
    C = HG_CHUNK
    b = x[0:C]
    zero = jnp.zeros((C, HG_DK), BF16)
    diag = jnp.sum(q * k, axis=-1, keepdims=True)
    qts, kts = [], []
    fine = 0
    for h in halves:
        if h >= 8:
            none = jnp.zeros((h, HG_DK), F32)
            qh, kh = [], []
            for r in range(0, C, 2 * h):
                b_mid = b[r + h - 1:r + h, :]
                kh += [k[r:r + h] * jnp.exp(b_mid - b[r:r + h]), none]
                qh += [none, q[r + h:r + 2 * h] * jnp.exp(b[r + h:r + 2 * h] - b_mid)]
            qts.append(jnp.concatenate(qh, axis=0).astype(BF16))
            kts.append(jnp.concatenate(kh, axis=0).astype(BF16))
        else:
            fine += 1
            e = jnp.exp(x[fine * C:(fine + 1) * C])
            qts.append((q * e).astype(BF16))
            kts.append((k * e).astype(BF16))
    prods = []
    for p in range(len(halves) // 2):
        lhs = jnp.concatenate([qts[2 * p], qts[2 * p + 1]], axis=1)
        rhs = jnp.concatenate([jnp.concatenate([kts[2 * p], zero], axis=1),
                               jnp.concatenate([zero, kts[2 * p + 1]], axis=1)], axis=0)
        prods.append(_dot_nt(lhs, rhs))
    return prods, diag


def _hgrn_pair(q2, f2, v2, gate2, st2, sums, owner, halves):
    C, D = HG_CHUNK, HG_DK
    halfs = (slice(0, D), slice(D, 2 * D))
    k2 = 1.0 - f2
    x2 = _dot(sums, jnp.concatenate(_split3(jnp.log(f2)), axis=0))
    yield
    b2 = x2[0:C]
    b_last = b2[C - 1:C, :]
    kdec2 = (k2 * jnp.exp(b_last - b2)).astype(BF16)
    qe2 = (q2 * jnp.exp(b2)).astype(BF16)
    st2b = st2.astype(BF16)
    scores = [_hgrn_score_products(q2[:, c], k2[:, c], x2[:, c], halves) for c in halfs]
    yield
    zero = jnp.zeros((2 * C, D), BF16)
    os, vvts, kds = [], [], []
    for j, c in enumerate(halfs):
        prods, diag = scores[j]
        a = jnp.where(owner == len(prods), diag, 0.0)
        for p, prod in enumerate(prods):
            a = jnp.where(owner == p, prod, a)
        v = v2[:, c]
        vvt = jnp.concatenate([v, v], axis=0).T
        os.append(_dot_nt(jnp.concatenate([qe2[:, c], a.astype(BF16)], axis=1),
                          jnp.concatenate([st2b[:, c], vvt], axis=1)))
        vvts.append(vvt)
        kd = jnp.concatenate([kdec2[:, c], jnp.zeros((C, D), BF16)], axis=0)
        kds.append(jnp.concatenate([kd, zero] if j == 0 else [zero, kd], axis=1))
    st_new = st2 * jnp.exp(b_last) + _dot(jnp.concatenate(vvts, axis=1), jnp.concatenate(kds, axis=0))
    yield
    ons = [o * lax.rsqrt(jnp.mean(o * o, axis=-1, keepdims=True) + EPS) for o in os]
    on2 = jnp.concatenate(ons, axis=1) * gate2
    return on2.astype(BF16), st_new


def _interleave(gens):
    out = [None] * len(gens)
    live = list(range(len(gens)))
    while live:
        for g in list(live):
            try:
                next(gens[g])
            except StopIteration as stop:
                out[g] = stop.value
                live.remove(g)
    return out


def _hgrn_kernel(q_ref, f_ref, v_ref, go_ref, fm_ref, vm_ref, gn_ref, sums_ref, owner_ref,
                 o_ref, st_ref, *, n_chunks, n_meta, hb, halves):
    i = pl.program_id(0)
    p0 = pl.program_id(1) * (hb // 2)
    C = HG_CHUNK
    W = 2 * HG_DK

    @pl.when(i == 0)
    def _():
        tri_m = _tri(n_meta)
        for pair in range(hb // 2):
            sts = []
            for hh in (2 * pair, 2 * pair + 1):
                cols = slice(hh * HG_DK, (hh + 1) * HG_DK)
                fm = fm_ref[:, cols]
                bm = _dot3(tri_m, _split3(jnp.log(fm)))
                kdec = ((1.0 - fm) * jnp.exp(bm[n_meta - 1:n_meta, :] - bm)).astype(BF16)
                sts.append(_dot_tn(vm_ref[:, cols], kdec))
            st_ref[p0 + pair] = jnp.concatenate(sts, axis=1)

    def chunk(c, states):
        r0 = pl.multiple_of(c * C, C)
        owner = owner_ref[...]
        sums = sums_ref[...]
        gens = []
        for pair in range(hb // 2):
            cols = slice(pair * W, (pair + 1) * W)
            gate2 = gn_ref[:, cols] * go_ref[pl.ds(r0, C), cols].astype(F32)
            gens.append(_hgrn_pair(q_ref[pl.ds(r0, C), cols], f_ref[pl.ds(r0, C), cols],
                                   v_ref[pl.ds(r0, C), cols], gate2, states[pair],
                                   sums, owner, halves))
        results = _interleave(gens)
        o_ref[pl.ds(r0, C), :] = jnp.concatenate([r[0] for r in results], axis=1)
        return tuple(r[1] for r in results)

    states = lax.fori_loop(0, n_chunks, chunk, tuple(st_ref[p0 + p] for p in range(hb // 2)))
    for pair in range(hb // 2):
        st_ref[p0 + pair] = states[pair]


def _hgrn(q, f, v, go, fm, vm, hg_norm, *, rb=512, hb=12):
    n, width = q[0].shape[0], hg_norm.shape[-1]
    heads = width // HG_DK
    hb = min(hb, heads)
    bw = hb * HG_DK
    n_meta = fm.shape[0]
    assert hb % 2 == 0 and heads % hb == 0
    sums, owner, halves = _hgrn_tables(HG_CHUNK)
    sums3 = jnp.asarray(np.concatenate([sums] * 3, axis=1), BF16)
    blk = lambda i, h: (i, h)
    mblk = lambda i, h: (0, h)
    const = lambda i, h: (0, 0)

    def view(col0):
        assert col0 % bw == 0
        return pl.BlockSpec((rb, bw), lambda i, h: (i, col0 // bw + h))

    return pl.pallas_call(
        functools.partial(_hgrn_kernel, n_chunks=rb // HG_CHUNK, n_meta=n_meta, hb=hb,
                          halves=halves),
        grid=(n // rb, heads // hb),
        in_specs=[view(q[1]), view(f[1]), view(v[1]), view(go[1])]
        + [pl.BlockSpec((n_meta, hb * HG_DK), mblk)] * 2
        + [pl.BlockSpec((1, hb * HG_DK), mblk),
           pl.BlockSpec(sums3.shape, const), pl.BlockSpec(owner.shape, const)],
        out_specs=pl.BlockSpec((rb, hb * HG_DK), blk),
        out_shape=jax.ShapeDtypeStruct((n, width), BF16),
        scratch_shapes=[pltpu.VMEM((heads // 2, HG_DK, 2 * HG_DK), F32)],
        compiler_params=_cparams(("arbitrary", "arbitrary")),
        name="hgrn2",
    )(q[0], f[0], v[0], go[0], fm, vm, hg_norm.reshape(1, width), sums3, jnp.asarray(owner))


def _s5_weights_kernel(lr_ref, li_ref, dt_ref, bre_ref, bim_ref, cre_ref, cim_ref, dsk_ref,
                       kt_ref, win_ref, wout_ref, at_ref):
    T = S5_T
    ns = lr_ref.shape[1]
    nc = cre_ref.shape[0]
    lr, li, dt = lr_ref[...], li_ref[...], dt_ref[...]

    def powers(tau):
        mag = jnp.exp(lr * dt * tau)
        ang = li * dt * tau
        return mag * jnp.cos(ang), mag * jnp.sin(ang)

    a_re, a_im = powers(1.0)
    num_re, num_im = a_re - 1.0, a_im
    den = lr * lr + li * li
    coef_re = (num_re * lr + num_im * li) / den
    coef_im = (num_im * lr - num_re * li) / den
    br, bi = bre_ref[...], bim_ref[...]
    bbar_re = coef_re * br - coef_im * bi
    bbar_im = coef_re * bi + coef_im * br

    pw = [powers(float(tau)) for tau in range(T + 1)]
    for s in range(T):
        p_re, p_im = pw[T - 1 - s]
        win_ref[s * nc:(s + 1) * nc, 0:ns] = (bbar_re * p_re - bbar_im * p_im).astype(BF16)
        win_ref[s * nc:(s + 1) * nc, ns:2 * ns] = (bbar_re * p_im + bbar_im * p_re).astype(BF16)
    at_ref[:, 0:ns] = pw[T][0]
    at_ref[:, ns:2 * ns] = pw[T][1]

    cr, ci = cre_ref[...], cim_ref[...]
    hp = lax.Precision.HIGHEST
    nt = (((1,), (1,)), ((), ()))
    for tau in range(T + 1):
        p_re, p_im = pw[tau]
        ca_re = cr * p_re - ci * p_im
        ca_im = cr * p_im + ci * p_re
        if tau >= 1:
            t = tau - 1
            wout_ref[0:ns, t * nc:(t + 1) * nc] = ca_re.T.astype(BF16)
            wout_ref[ns:2 * ns, t * nc:(t + 1) * nc] = (-ca_im).T.astype(BF16)
        if tau < T:
            kt = (lax.dot_general(bbar_re, ca_re, nt, precision=hp, preferred_element_type=F32)
                  - lax.dot_general(bbar_im, ca_im, nt, precision=hp, preferred_element_type=F32))
            if tau == 0:
                kt = kt + dsk_ref[...]
            kt_ref[tau] = kt.astype(BF16)


def _s5_weights(lam_re, lam_im, log_step, b_re, b_im, c_re, c_im, d_skip):
    G, N = lam_re.shape
    P = d_skip.shape[1]
    nsg = G // S5_SG
    ns, nc = S5_SG * N, S5_SG * P
    T = S5_T
    dt = jnp.exp(log_step.astype(F32))
    eye = jnp.eye(S5_SG, dtype=F32)

    def rows(a):
        return a.astype(F32).reshape(nsg, 1, ns)

    def embed_b(b):
        b = b.astype(F32).reshape(nsg, S5_SG, N, P)
        return jnp.einsum('zgnq,gh->zhqgn', b, eye).reshape(nsg, nc, ns)

    def embed_c(c):
        c = c.astype(F32).reshape(nsg, S5_SG, P, N)
        return jnp.einsum('zgpn,gh->zhpgn', c, eye).reshape(nsg, nc, ns)

    dtb = jnp.broadcast_to(dt[:, None], (G, N))
    d_diag = jax.vmap(jnp.diag)(d_skip.astype(F32).reshape(nsg, nc))
    sq = pl.Squeezed()
    spec_r = pl.BlockSpec((sq, 1, ns), lambda z: (z, 0, 0))
    spec_b = pl.BlockSpec((sq, nc, ns), lambda z: (z, 0, 0))
    return pl.pallas_call(
        _s5_weights_kernel,
        grid=(nsg,),
        in_specs=[spec_r, spec_r, spec_r, spec_b, spec_b, spec_b, spec_b,
                  pl.BlockSpec((sq, nc, nc), lambda z: (z, 0, 0))],
        out_specs=[pl.BlockSpec((sq, T, nc, nc), lambda z: (z, 0, 0, 0)),
                   pl.BlockSpec((sq, T * nc, 2 * ns), lambda z: (z, 0, 0)),
                   pl.BlockSpec((sq, 2 * ns, T * nc), lambda z: (z, 0, 0)),
                   pl.BlockSpec((sq, 1, 2 * ns), lambda z: (z, 0, 0))],
        out_shape=[jax.ShapeDtypeStruct((nsg, T, nc, nc), BF16),
                   jax.ShapeDtypeStruct((nsg, T * nc, 2 * ns), BF16),
                   jax.ShapeDtypeStruct((nsg, 2 * ns, T * nc), BF16),
                   jax.ShapeDtypeStruct((nsg, 1, 2 * ns), F32)],
        compiler_params=_cparams(("arbitrary",)),
        name="s5_weights",
    )(rows(lam_re), rows(lam_im), rows(dtb),
      embed_b(b_re), embed_b(b_im), embed_c(c_re), embed_c(c_im), d_diag)


def _s5_scan_kernel(u_ref, um_ref, kt_ref, win_ref, wout_ref, at_ref, y_ref,
                    uc_ref, v_ref, xp_ref, m_ref, *, n_chunks):
    T = S5_T
    nc = u_ref.shape[1]
    ns = at_ref.shape[1] // 2

    @pl.when(pl.program_id(0) == 0)
    def _():
        zero = jnp.zeros((nc, nc), BF16)
        for s in range(1, T):
            for t in range(s):
                m_ref[s * nc:(s + 1) * nc, t * nc:(t + 1) * nc] = zero

    for tau in range(T):
        tap = kt_ref[tau]
        for s in range(T - tau):
            m_ref[s * nc:(s + 1) * nc, (s + tau) * nc:(s + tau + 1) * nc] = tap

    for t in range(T):
        uc_ref[0:n_chunks, t * nc:(t + 1) * nc] = u_ref[pl.ds(t, n_chunks, stride=T), :].astype(BF16)
        uc_ref[n_chunks:n_chunks + 1, t * nc:(t + 1) * nc] = um_ref[t:t + 1, :].astype(BF16)
    pad = uc_ref.shape[0] - n_chunks - 1
    uc_ref[n_chunks + 1:, :] = jnp.zeros((pad, T * nc), BF16)
    uc = uc_ref[...]
    v_ref[...] = _dot(uc, win_ref[...])
    a_re, a_im = at_ref[:, 0:ns], at_ref[:, ns:2 * ns]

    x_re0 = v_ref[n_chunks:n_chunks + 1, 0:ns]
    x_im0 = v_ref[n_chunks:n_chunks + 1, ns:2 * ns]
    xp_ref[n_chunks:, :] = jnp.zeros((pad + 1, 2 * ns), F32)

    def step(c, carry):
        x_re, x_im = carry
        xp_ref[pl.ds(c, 1), 0:ns] = x_re
        xp_ref[pl.ds(c, 1), ns:2 * ns] = x_im
        v_re = v_ref[pl.ds(c, 1), 0:ns]
        v_im = v_ref[pl.ds(c, 1), ns:2 * ns]
        return (a_re * x_re - a_im * x_im + v_re, a_re * x_im + a_im * x_re + v_im)

    lax.fori_loop(0, n_chunks, step, (x_re0, x_im0))
    yc = _dot(uc, m_ref[...]) + _dot(xp_ref[...].astype(BF16), wout_ref[...])
    for t in range(T):
        y_ref[pl.ds(t, n_chunks, stride=T), :] = yc[0:n_chunks, t * nc:(t + 1) * nc]


def _s5_scan(u, um, kt, win, wout, at):
    u, col0 = u
    n, width = u.shape[0], um.shape[1]
    T = S5_T
    nsg = kt.shape[0]
    nc = width // nsg
    ns2 = at.shape[2]
    assert um.shape[0] == T and n % T == 0 and col0 % nc == 0
    n_chunks = n // T
    rows = -(-(n_chunks + 1) // 16) * 16
    sq = pl.Squeezed()
    return pl.pallas_call(
        functools.partial(_s5_scan_kernel, n_chunks=n_chunks),
        grid=(nsg,),
        in_specs=[pl.BlockSpec((n, nc), lambda z: (0, col0 // nc + z)),
                  pl.BlockSpec((T, nc), lambda z: (0, z)),
                  pl.BlockSpec((sq, T, nc, nc), lambda z: (z, 0, 0, 0)),
                  pl.BlockSpec((sq, T * nc, ns2), lambda z: (z, 0, 0)),
                  pl.BlockSpec((sq, ns2, T * nc), lambda z: (z, 0, 0)),
                  pl.BlockSpec((sq, 1, ns2), lambda z: (z, 0, 0))],
        out_specs=pl.BlockSpec((n, nc), lambda z: (0, z)),
        out_shape=jax.ShapeDtypeStruct((n, width), F32),
        scratch_shapes=[pltpu.VMEM((rows, T * nc), BF16), pltpu.VMEM((rows, ns2), F32),
                        pltpu.VMEM((rows, ns2), F32), pltpu.VMEM((T * nc, T * nc), BF16)],
        compiler_params=_cparams(("arbitrary",)),
        name="s5_scan",
    )(u, um, kt, win, wout, at)


def _merge_kernel(*refs, pieces):
    h_ref, o_ref, y_ref = refs[0:3]
    gh_refs, gs_refs = refs[3:3 + pieces], refs[3 + pieces:3 + 2 * pieces]
    wp_ref, wa_ref, wb_ref, wo_ref, out_ref = refs[3 + 2 * pieces:]
    gh = jnp.concatenate([r[...] for r in gh_refs], axis=1).astype(F32)
    gs = jnp.concatenate([r[...] for r in gs_refs], axis=1).astype(F32)
    y_hg = _dot(o_ref[...], wp_ref[...])
    z = jax.nn.gelu(y_ref[...]).astype(BF16)
    y_s5 = _dot(z, wa_ref[...]) * jax.nn.sigmoid(_dot(z, wb_ref[...]))
    merged = gh * y_hg + gs * y_s5
    out_ref[...] = h_ref[...] + _dot(merged.astype(BF16), wo_ref[...])


def _merge(hx, o, y, gh, gs, wp, wa, wb, wo, *, tm=256):
    n, d = hx.shape
    row = lambda i: (i, 0)
    const = lambda i: (0, 0)
    gw = functools.reduce(math.gcd, (gh[1], gs[1], d))
    assert gw % LANE == 0
    pieces = d // gw

    def resident(shape):
        return pl.BlockSpec(shape, const, pipeline_mode=pl.Buffered(1))

    def view(col0):
        return [pl.BlockSpec((tm, gw), functools.partial(lambda i, k: (i, k), k=col0 // gw + k))
                for k in range(pieces)]

    return pl.pallas_call(
        functools.partial(_merge_kernel, pieces=pieces),
        grid=(n // tm,),
        in_specs=[pl.BlockSpec((tm, d), row), pl.BlockSpec((tm, o.shape[1]), row),
                  pl.BlockSpec((tm, y.shape[1]), row)] + view(gh[1]) + view(gs[1])
        + [resident(wp.shape), resident(wa.shape), resident(wb.shape), resident(wo.shape)],
        out_specs=pl.BlockSpec((tm, d), row),
        out_shape=jax.ShapeDtypeStruct((n, d), F32),
        compiler_params=_cparams(("arbitrary",)),
        name="mixer_merge",
    )(hx, o, y, *([gh[0]] * pieces), *([gs[0]] * pieces), wp, wa, wb, wo)


def kernel(x, meta_tokens, norm_ffn1, ffn1_w_gate, ffn1_w_up, ffn1_w_down, norm_mix, w_in, hg_lower_bounds, hg_norm, hg_w_proj, s5_lam_re, s5_lam_im, s5_log_step, s5_b_re, s5_b_im, s5_c_re, s5_c_im, s5_d, s5_w_glu_a, s5_w_glu_b, w_out, norm_ffn2, ffn2_w_gate, ffn2_w_up, ffn2_w_down, norm_final):
    batch, seq, d = x.shape
    depth = norm_ffn1.shape[0]
    assert batch == 1 and depth == 1
    hg_k = hg_lower_bounds.shape[1]
    hg_w = hg_norm.shape[1]
    s5_w = s5_w_glu_a.shape[1]
    bf = lambda w: w.astype(BF16)

    hx = x[0].astype(F32)
    hm = meta_tokens.astype(F32)
    hx, hm, nx, nm, w2_gate, w2_up, w2_down = _ffn(
        hx, hm, norm_ffn1[0], norm_mix[0], bf(ffn1_w_gate[0]), bf(ffn1_w_up[0]), bf(ffn1_w_down[0]),
        cast=(ffn2_w_gate[0], ffn2_w_up[0], ffn2_w_down[0]))

    sizes = (hg_k, hg_k, hg_w, hg_w, s5_w, d, d)
    assert sum(sizes) == w_in.shape[2]
    off = [0]
    for s in sizes:
        off.append(off[-1] + s)

    def group(ids, rows):
        w = jnp.concatenate([w_in[0][:, off[k]:off[k + 1]] for k in ids], axis=1).astype(BF16)
        coef = jnp.concatenate([jnp.broadcast_to(jnp.asarray(rows[k], F32)[:, None], (4, sizes[k]))
                                for k in ids], axis=1)
        starts = {}
        col = 0
        for k in ids:
            starts[k] = col
            col += sizes[k]
        return w, coef, starts

    silu, ident, sigm = (1.0, 0.0, 0.0, 0.0), (0.0, 1.0, 0.0, 0.0), (0.0, 0.0, 1.0, 0.0)
    rows = {0: (HG_DK ** -0.5, 0.0, 0.0, 0.0), 1: (0.0, 0.0, 0.0, 1.0), 2: ident, 3: silu,
            4: ident, 5: sigm, 6: sigm}
    w32, c32, s32 = group((0, 1, 4), rows)
    w16, c16, s16 = group((2, 3, 5, 6), rows)
    x32, m32 = _inproj(nx, nm, w32, c32, hg_lower_bounds, s32[1], F32)
    x16, m16 = _inproj(nx, nm, w16, c16, None, 0, BF16)

    o = _hgrn((x32, s32[0]), (x32, s32[1]), (x16, s16[2]), (x16, s16[3]),
              m32[:, s32[1]:s32[1] + hg_k], m16[:, s16[2]:s16[2] + hg_w], hg_norm[0])
    m, win, wout, at = _s5_weights(s5_lam_re[0], s5_lam_im[0], s5_log_step[0], s5_b_re[0],
                                   s5_b_im[0], s5_c_re[0], s5_c_im[0], s5_d[0])
    y = _s5_scan((x32, s32[4]), m32[:, s32[4]:s32[4] + s5_w], m, win, wout, at)
    hx = _merge(hx, o, y, (x16, s16[5]), (x16, s16[6]), bf(hg_w_proj[0]), bf(s5_w_glu_a[0]),
                bf(s5_w_glu_b[0]), bf(w_out[0]))
    out = _ffn(hx, None, norm_ffn2[0], norm_final, w2_gate, w2_up, w2_down)
    return out[None].astype(x.dtype)
```

```python
import functools
import math

import numpy as np
import jax
import jax.numpy as jnp
from jax import lax
from jax.experimental import pallas as pl
from jax.experimental.pallas import tpu as pltpu

EPS = 1e-6
HG_DK = 128
HG_CHUNK = 64
S5_GROUP = 16
S5_STATE = 64
S5_T = 16
S5_SG = 8
LANE = 128
VMEM_LIMIT = 56 * 1024 * 1024

F32 = jnp.float32
BF16 = jnp.bfloat16


def _cparams(sem):
    return pltpu.CompilerParams(dimension_semantics=sem, vmem_limit_bytes=VMEM_LIMIT)


def _rms(v, g):
    return v * lax.rsqrt(jnp.mean(v * v, axis=-1, keepdims=True) + EPS) * g


def _dot(a, b):
    return jnp.dot(a, b, preferred_element_type=F32)


def _dot_nt(a, b):
    return lax.dot_general(a, b, (((1,), (1,)), ((), ())), preferred_element_type=F32)


def _dot_tn(a, b):
    return lax.dot_general(a, b, (((0,), (0,)), ((), ())), preferred_element_type=F32)


def _ffn_kernel(*refs, tm, n_meta, emit_norm, w_f32, head, n_cast):
    it = iter(refs)
    take = lambda k: [next(it) for _ in range(k)]
    x_ref, = take(1)
    m_ref = take(1)[0] if n_meta else None
    g_ref, gn_ref, wg_ref, wu_ref, wd_ref = take(5)
    hx_ref, hn16_ref = take(2) if head else (None, None)
    cast_in = take(n_cast)
    ox_ref, = take(1)
    nx_ref = take(1)[0] if emit_norm else None
    om_ref, nm_ref = take(2) if n_meta else (None, None)
    w16_out = take(3) if w_f32 else ()
    cast_out = take(n_cast)
    hn_ref, acc_ref = take(2)
    i = pl.program_id(0)
    j = pl.program_id(1)
    for src, dst in zip(cast_in, cast_out):
        dst[...] = src[...].astype(BF16)

    def body():
        @pl.when(j == 0)
        def _():
            hn_ref[0:tm, :] = _rms(x_ref[...], g_ref[...]).astype(BF16)
            if n_meta:
                hn_ref[tm:tm + n_meta, :] = _rms(m_ref[...], g_ref[...]).astype(BF16)
            acc_ref[...] = jnp.zeros_like(acc_ref)

        tiles = [w[...].astype(BF16) if w_f32 else w[...] for w in (wg_ref, wu_ref, wd_ref)]
        for dst, tile in zip(w16_out, tiles):
            dst[...] = tile
        hn = hn_ref[...]
        gate = _dot(hn, tiles[0])
        up = _dot(hn, tiles[1])
        act = (gate * jax.nn.sigmoid(gate) * up).astype(BF16)
        acc_ref[...] += _dot(act, tiles[2])

        @pl.when(j == pl.num_programs(1) - 1)
        def _():
            hx = x_ref[...] + 0.5 * acc_ref[0:tm, :]
            if emit_norm:
                ox_ref[...] = hx
                nx_ref[...] = _rms(hx, gn_ref[...]).astype(BF16)
            else:
                ox_ref[...] = _rms(hx, gn_ref[...])
            if n_meta:
                hm = m_ref[...] + 0.5 * acc_ref[tm:tm + n_meta, :]
                om_ref[...] = hm
                nm_ref[...] = _rms(hm, gn_ref[...]).astype(BF16)

    if head:
        @pl.when((i == 0) & (j == 0))
        def _():
            ox_ref[...] = hx_ref[...]
            nx_ref[...] = hn16_ref[...]

        pl.when(i > 0)(body)
    else:
        body()


def _cast_spec(shape, grid):
    (r, c), (gi, gj) = shape, grid
    if r % gi == 0 and c % gj == 0 and (r // gi) % 16 == 0 and (c // gj) % LANE == 0:
        return pl.BlockSpec((r // gi, c // gj), lambda i, j: (i, j))
    assert r % gj == 0 and c % gi == 0 and (r // gj) % 16 == 0 and (c // gi) % LANE == 0
    return pl.BlockSpec((r // gj, c // gi), lambda i, j: (j, i))


def _ffn(hx, hm, g, g_next, wg, wu, wd, *, emit_norm, head=None, cast=(), rows=None, name,
         tm=512, tf=512):
    n, d = hx.shape
    rows = n if rows is None else rows
    dff = wg.shape[1]
    n_meta = 0 if hm is None else hm.shape[0]
    w_f32 = wg.dtype == F32
    grid = (rows // tm, dff // tf)
    assert not w_f32 or grid[0] == 1
    row = lambda i, j: (i, 0)
    const = lambda i, j: (0, 0)
    in_specs = [pl.BlockSpec((tm, d), row)]
    args = [hx]
    if n_meta:
        in_specs.append(pl.BlockSpec((n_meta, d), const))
        args.append(hm)
    col = (lambda i, j: jnp.where(i == 0, 0, j)) if head else (lambda i, j: j)
    in_specs += [pl.BlockSpec((1, d), const), pl.BlockSpec((1, d), const),
                 pl.BlockSpec((d, tf), lambda i, j: (0, col(i, j))),
                 pl.BlockSpec((d, tf), lambda i, j: (0, col(i, j))),
                 pl.BlockSpec((tf, d), lambda i, j: (col(i, j), 0))]
    args += [g.reshape(1, d), g_next.reshape(1, d), wg, wu, wd]
    if head:
        in_specs += [pl.BlockSpec((tm, d), const), pl.BlockSpec((tm, d), const)]
        args += list(head)
    cast_specs = [_cast_spec(w.shape, grid) for w in cast]
    in_specs += cast_specs
    args += list(cast)
    out_shape = [jax.ShapeDtypeStruct((rows, d), F32)]
    out_specs = [pl.BlockSpec((tm, d), row)]
    if emit_norm:
        out_shape.append(jax.ShapeDtypeStruct((rows, d), BF16))
        out_specs.append(pl.BlockSpec((tm, d), row))
    if n_meta:
        out_shape += [jax.ShapeDtypeStruct((n_meta, d), F32), jax.ShapeDtypeStruct((n_meta, d), BF16)]
        out_specs += [pl.BlockSpec((n_meta, d), const)] * 2
    if w_f32:
        out_shape += [jax.ShapeDtypeStruct(w.shape, BF16) for w in (wg, wu, wd)]
        out_specs += [pl.BlockSpec((d, tf), lambda i, j: (0, j)),
                      pl.BlockSpec((d, tf), lambda i, j: (0, j)),
                      pl.BlockSpec((tf, d), lambda i, j: (j, 0))]
    out_specs += cast_specs
    out_shape += [jax.ShapeDtypeStruct(w.shape, BF16) for w in cast]
    return pl.pallas_call(
        functools.partial(_ffn_kernel, tm=tm, n_meta=n_meta, emit_norm=emit_norm, w_f32=w_f32,
                          head=head is not None, n_cast=len(cast)),
        grid=grid, in_specs=in_specs, out_specs=out_specs, out_shape=out_shape,
        scratch_shapes=[pltpu.VMEM((tm + n_meta, d), BF16), pltpu.VMEM((tm + n_meta, d), F32)],
        compiler_params=_cparams(("arbitrary", "arbitrary")),
        name=name,
    )(*args)


def _inproj_kernel(*refs, tm, n_meta, with_lb):
    if with_lb:
        x_ref, m_ref, w_ref, coef_ref, lb_ref, ox_ref, om_ref, lhs_ref = refs
    else:
        x_ref, m_ref, w_ref, coef_ref, ox_ref, om_ref, lhs_ref = refs

    @pl.when(pl.program_id(1) == 0)
    def _():
        lhs_ref[0:tm, :] = x_ref[...]
        lhs_ref[tm:tm + n_meta, :] = m_ref[...]

    tn = w_ref.shape[1]
    sw = math.gcd(tn, 2 * LANE)
    for s in range(tn // sw):
        cs = slice(s * sw, (s + 1) * sw)
        res = _dot(lhs_ref[...], w_ref[:, cs])
        sig = jax.nn.sigmoid(res)
        coef = coef_ref[:, cs]
        a, c, b, d = coef[0:1, :], coef[1:2, :], coef[2:3, :], 0.0
        if with_lb:
            lbl = lb_ref[:, cs]
            e = jnp.exp(lbl - jnp.max(lbl, axis=0, keepdims=True))
            lb = e[0:1, :] / jnp.sum(e, axis=0, keepdims=True)
            flag = coef[3:4, :]
            b = b + flag * (1.0 - lb)
            d = flag * lb
        out = res * (a * sig + c) + (b * sig + d)
        ox_ref[:, cs] = out[0:tm, :].astype(ox_ref.dtype)
        om_ref[:, cs] = out[tm:tm + n_meta, :].astype(om_ref.dtype)


def _inproj(nx, nm, w, coef, lower_bounds, lb_cols, dtype, *, tm=1024, tn=1792):
    n, d = nx.shape
    n_meta = nm.shape[0]
    cols = w.shape[1]
    tm = min(tm, n)
    with_lb = lower_bounds is not None
    tn = math.gcd(tn, cols)
    assert tn % LANE == 0 and n % tm == 0
    in_specs = [pl.BlockSpec((tm, d), lambda i, j: (i, 0)),
                pl.BlockSpec((n_meta, d), lambda i, j: (0, 0)),
                pl.BlockSpec((d, tn), lambda i, j: (0, j)),
                pl.BlockSpec((4, tn), lambda i, j: (0, j))]
    args = [nx, nm, w, coef]
    if with_lb:
        nlb, k = lower_bounds.shape
        lb_full = jnp.pad(lower_bounds.astype(F32), ((0, 0), (lb_cols, cols - lb_cols - k)))
        in_specs.append(pl.BlockSpec((nlb, tn), lambda i, j: (0, j)))
        args.append(lb_full)
    ox, om = pl.pallas_call(
        functools.partial(_inproj_kernel, tm=tm, n_meta=n_meta, with_lb=with_lb),
        grid=(n // tm, cols // tn),
        in_specs=in_specs,
        out_specs=[pl.BlockSpec((tm, tn), lambda i, j: (i, j)),
                   pl.BlockSpec((pl.Squeezed(), n_meta, tn), lambda i, j: (i, 0, j))],
        out_shape=[jax.ShapeDtypeStruct((n, cols), dtype),
                   jax.ShapeDtypeStruct((n // tm, n_meta, cols), dtype)],
        scratch_shapes=[pltpu.VMEM((tm + n_meta, d), BF16)],
        compiler_params=_cparams(("arbitrary", "arbitrary")),
        name="mixer_inproj_" + jnp.dtype(dtype).name,
    )(*args)
    return ox, om[0]


def _split3(x):
    hi = x.astype(BF16)
    r1 = x - hi.astype(F32)
    mid = r1.astype(BF16)
    lo = (r1 - mid.astype(F32)).astype(BF16)
    return hi, mid, lo


def _dot3(m01, parts):
    return _dot(m01, parts[0]) + _dot(m01, parts[1]) + _dot(m01, parts[2])


def _tri(n, rep=1):
    r = lax.broadcasted_iota(jnp.int32, (n * rep, n), 0)
    c = lax.broadcasted_iota(jnp.int32, (n * rep, n), 1)
    return jnp.where(r >= c * rep, 1.0, 0.0).astype(BF16)


def _hgrn_tables(c):
    sums = [np.tril(np.ones((c, c)))]
    level = np.full((c, c), -1, np.int32)
    t_idx, s_idx = np.meshgrid(np.arange(c), np.arange(c), indexing="ij")
    halves = []
    h = c // 2
    while h >= 1:
        if h < 8:
            m = np.zeros((c, c))
            for r in range(c):
                mid = (r // (2 * h)) * 2 * h + h
                if r < mid:
                    m[r, r + 1:mid] = 1.0
                else:
                    m[r, mid:r + 1] = 1.0
            sums.append(m)
        pair = 2 * h
        level[(t_idx // pair == s_idx // pair) & (s_idx % pair < h) & (t_idx % pair >= h)] = len(halves)
        halves.append(h)
        h //= 2
    assert len(halves) % 2 == 0
    level[np.arange(c), np.arange(c)] = len(halves)
    owner = np.full((c, 2 * c), -1, np.int32)
    owner[:, :c] = np.where((level >= 0) & (level % 2 == 0), level // 2, -1)
    owner[:, c:] = np.where(level % 2 == 1, level // 2, -1)
    return np.concatenate(sums, axis=0), owner, tuple(halves)


def _hgrn_score_products(q, k, x, halves):
    C = HG_CHUNK
    b = x[0:C]
    zero = jnp.zeros((C, HG_DK), BF16)
    diag = jnp.sum(q * k, axis=-1, keepdims=True)
    qts, kts = [], []
    fine = 0
    for h in halves:
        if h >= 8:
            none = jnp.zeros((h, HG_DK), F32)
            qh, kh = [], []
            for r in range(0, C, 2 * h):
                b_mid = b[r + h - 1:r + h, :]
                kh += [k[r:r + h] * jnp.exp(b_mid - b[r:r + h]), none]
                qh += [none, q[r + h:r + 2 * h] * jnp.exp(b[r + h:r + 2 * h] - b_mid)]
            qts.append(jnp.concatenate(qh, axis=0).astype(BF16))
            kts.append(jnp.concatenate(kh, axis=0).astype(BF16))
        else:
            fine += 1
            e = jnp.exp(x[fine * C:(fine + 1) * C])
            qts.append((q * e).astype(BF16))
            kts.append((k * e).astype(BF16))
    prods = []
    for p in range(len(halves) // 2):
        lhs = jnp.concatenate([qts[2 * p], qts[2 * p + 1]], axis=1)
        rhs = jnp.concatenate([jnp.concatenate([kts[2 * p], zero], axis=1),
                               jnp.concatenate([zero, kts[2 * p + 1]], axis=1)], axis=0)
        prods.append(_dot_nt(lhs, rhs))
    return prods, diag


def _hgrn_pair(q2, f2, v2, gate2, st2, sums, owner, halves):
    C, D = HG_CHUNK, HG_DK
    halfs = (slice(0, D), slice(D, 2 * D))
    k2 = 1.0 - f2
    x2 = _dot(sums, jnp.concatenate(_split3(jnp.log(f2)), axis=0))
    yield
    b2 = x2[0:C]
    b_last = b2[C - 1:C, :]
    kdec2 = (k2 * jnp.exp(b_last - b2)).astype(BF16)
    qe2 = (q2 * jnp.exp(b2)).astype(BF16)
    st2b = st2.astype(BF16)
    scores = [_hgrn_score_products(q2[:, c], k2[:, c], x2[:, c], halves) for c in halfs]
    yield
    zero = jnp.zeros((2 * C, D), BF16)
    os, vvts, kds = [], [], []
    for j, c in enumerate(halfs):
        prods, diag = scores[j]
        a = jnp.where(owner == len(prods), diag, 0.0)
        for p, prod in enumerate(prods):
            a = jnp.where(owner == p, prod, a)
        v = v2[:, c]
        vvt = jnp.concatenate([v, v], axis=0).T
        os.append(_dot_nt(jnp.concatenate([qe2[:, c], a.astype(BF16)], axis=1),
                          jnp.concatenate([st2b[:, c], vvt], axis=1)))
        vvts.append(vvt)
        kd = jnp.concatenate([kdec2[:, c], jnp.zeros((C, D), BF16)], axis=0)
        kds.append(jnp.concatenate([kd, zero] if j == 0 else [zero, kd], axis=1))
    st_new = st2 * jnp.exp(b_last) + _dot(jnp.concatenate(vvts, axis=1), jnp.concatenate(kds, axis=0))
    yield
    ons = [o * lax.rsqrt(jnp.mean(o * o, axis=-1, keepdims=True) + EPS) for o in os]
    on2 = jnp.concatenate(ons, axis=1) * gate2
    return on2.astype(BF16), st_new


def _interleave(gens):
    out = [None] * len(gens)
    live = list(range(len(gens)))
    while live:
        for g in list(live):
            try:
                next(gens[g])
            except StopIteration as stop:
                out[g] = stop.value
                live.remove(g)
    return out


def _hgrn_kernel(q_ref, f_ref, v_ref, go_ref, fm_ref, vm_ref, gn_ref, sums_ref, owner_ref,
                 o_ref, st_ref, *, n_chunks, n_meta, hb, halves):
    i = pl.program_id(0)
    p0 = pl.program_id(1) * (hb // 2)
    C = HG_CHUNK
    W = 2 * HG_DK

    @pl.when(i == 0)
    def _():
        tri_m = _tri(n_meta)
        for pair in range(hb // 2):
            sts = []
            for hh in (2 * pair, 2 * pair + 1):
                cols = slice(hh * HG_DK, (hh + 1) * HG_DK)
                fm = fm_ref[:, cols]
                bm = _dot3(tri_m, _split3(jnp.log(fm)))
                kdec = ((1.0 - fm) * jnp.exp(bm[n_meta - 1:n_meta, :] - bm)).astype(BF16)
                sts.append(_dot_tn(vm_ref[:, cols], kdec))
            st_ref[p0 + pair] = jnp.concatenate(sts, axis=1)

    def chunk(c, states):
        r0 = pl.multiple_of(c * C, C)
        owner = owner_ref[...]
        sums = sums_ref[...]
        gens = []
        for pair in range(hb // 2):
            cols = slice(pair * W, (pair + 1) * W)
            gate2 = gn_ref[:, cols] * go_ref[pl.ds(r0, C), cols].astype(F32)
            gens.append(_hgrn_pair(q_ref[pl.ds(r0, C), cols], f_ref[pl.ds(r0, C), cols],
                                   v_ref[pl.ds(r0, C), cols], gate2, states[pair],
                                   sums, owner, halves))
        results = _interleave(gens)
        o_ref[pl.ds(r0, C), :] = jnp.concatenate([r[0] for r in results], axis=1)
        return tuple(r[1] for r in results)

    states = lax.fori_loop(0, n_chunks, chunk, tuple(st_ref[p0 + p] for p in range(hb // 2)))
    for pair in range(hb // 2):
        st_ref[p0 + pair] = states[pair]


def _hgrn(q, f, v, go, fm, vm, hg_norm, *, rb=512, hb=12):
    n, width = q[0].shape[0], hg_norm.shape[-1]
    heads = width // HG_DK
    hb = min(hb, heads)
    bw = hb * HG_DK
    n_meta = fm.shape[0]
    assert hb % 2 == 0 and heads % hb == 0
    sums, owner, halves = _hgrn_tables(HG_CHUNK)
    sums3 = jnp.asarray(np.concatenate([sums] * 3, axis=1), BF16)
    blk = lambda i, h: (i, h)
    mblk = lambda i, h: (0, h)
    const = lambda i, h: (0, 0)

    def view(col0):
        assert col0 % bw == 0
        return pl.BlockSpec((rb, bw), lambda i, h: (i, col0 // bw + h))

    return pl.pallas_call(
        functools.partial(_hgrn_kernel, n_chunks=rb // HG_CHUNK, n_meta=n_meta, hb=hb,
                          halves=halves),
        grid=(n // rb, heads // hb),
        in_specs=[view(q[1]), view(f[1]), view(v[1]), view(go[1])]
        + [pl.BlockSpec((n_meta, hb * HG_DK), mblk)] * 2
        + [pl.BlockSpec((1, hb * HG_DK), mblk),
           pl.BlockSpec(sums3.shape, const), pl.BlockSpec(owner.shape, const)],
        out_specs=pl.BlockSpec((rb, hb * HG_DK), blk),
        out_shape=jax.ShapeDtypeStruct((n, width), BF16),
        scratch_shapes=[pltpu.VMEM((heads // 2, HG_DK, 2 * HG_DK), F32)],
        compiler_params=_cparams(("arbitrary", "arbitrary")),
        name="hgrn2",
    )(q[0], f[0], v[0], go[0], fm, vm, hg_norm.reshape(1, width), sums3, jnp.asarray(owner))


def _s5_weights_kernel(lr_ref, li_ref, dt_ref, bre_ref, bim_ref, cre_ref, cim_ref, dsk_ref,
                       kt_ref, win_ref, wout_ref, at_ref):
    T = S5_T
    ns = lr_ref.shape[1]
    nc = cre_ref.shape[0]
    lr, li, dt = lr_ref[...], li_ref[...], dt_ref[...]

    def powers(tau):
        mag = jnp.exp(lr * dt * tau)
        ang = li * dt * tau
        return mag * jnp.cos(ang), mag * jnp.sin(ang)

    a_re, a_im = powers(1.0)
    num_re, num_im = a_re - 1.0, a_im
    den = lr * lr + li * li
    coef_re = (num_re * lr + num_im * li) / den
    coef_im = (num_im * lr - num_re * li) / den
    br, bi = bre_ref[...], bim_ref[...]
    bbar_re = coef_re * br - coef_im * bi
    bbar_im = coef_re * bi + coef_im * br

    pw = [powers(float(tau)) for tau in range(T + 1)]
    for s in range(T):
        p_re, p_im = pw[T - 1 - s]
        win_ref[s * nc:(s + 1) * nc, 0:ns] = (bbar_re * p_re - bbar_im * p_im).astype(BF16)
        win_ref[s * nc:(s + 1) * nc, ns:2 * ns] = (bbar_re * p_im + bbar_im * p_re).astype(BF16)
    at_ref[:, 0:ns] = pw[T][0]
    at_ref[:, ns:2 * ns] = pw[T][1]

    cr, ci = cre_ref[...], cim_ref[...]
    hp = lax.Precision.HIGHEST
    nt = (((1,), (1,)), ((), ()))
    for tau in range(T + 1):
        p_re, p_im = pw[tau]
        ca_re = cr * p_re - ci * p_im
        ca_im = cr * p_im + ci * p_re
        if tau >= 1:
            t = tau - 1
            wout_ref[0:ns, t * nc:(t + 1) * nc] = ca_re.T.astype(BF16)
            wout_ref[ns:2 * ns, t * nc:(t + 1) * nc] = (-ca_im).T.astype(BF16)
        if tau < T:
            kt = (lax.dot_general(bbar_re, ca_re, nt, precision=hp, preferred_element_type=F32)
                  - lax.dot_general(bbar_im, ca_im, nt, precision=hp, preferred_element_type=F32))
            if tau == 0:
                kt = kt + dsk_ref[...]
            kt_ref[tau] = kt.astype(BF16)


def _s5_weights(lam_re, lam_im, log_step, b_re, b_im, c_re, c_im, d_skip):
    G, N = lam_re.shape
    P = d_skip.shape[1]
    nsg = G // S5_SG
    ns, nc = S5_SG * N, S5_SG * P
    T = S5_T
    dt = jnp.exp(log_step.astype(F32))
    eye = jnp.eye(S5_SG, dtype=F32)

    def rows(a):
        return a.astype(F32).reshape(nsg, 1, ns)

    def embed_b(b):
        b = b.astype(F32).reshape(nsg, S5_SG, N, P)
        return jnp.einsum('zgnq,gh->zhqgn', b, eye).reshape(nsg, nc, ns)

    def embed_c(c):
        c = c.astype(F32).reshape(nsg, S5_SG, P, N)
        return jnp.einsum('zgpn,gh->zhpgn', c, eye).reshape(nsg, nc, ns)

    dtb = jnp.broadcast_to(dt[:, None], (G, N))
    d_diag = jax.vmap(jnp.diag)(d_skip.astype(F32).reshape(nsg, nc))
    sq = pl.Squeezed()
    spec_r = pl.BlockSpec((sq, 1, ns), lambda z: (z, 0, 0))
    spec_b = pl.BlockSpec((sq, nc, ns), lambda z: (z, 0, 0))
    return pl.pallas_call(
        _s5_weights_kernel,
        grid=(nsg,),
        in_specs=[spec_r, spec_r, spec_r, spec_b, spec_b, spec_b, spec_b,
                  pl.BlockSpec((sq, nc, nc), lambda z: (z, 0, 0))],
        out_specs=[pl.BlockSpec((sq, T, nc, nc), lambda z: (z, 0, 0, 0)),
                   pl.BlockSpec((sq, T * nc, 2 * ns), lambda z: (z, 0, 0)),
                   pl.BlockSpec((sq, 2 * ns, T * nc), lambda z: (z, 0, 0)),
                   pl.BlockSpec((sq, 1, 2 * ns), lambda z: (z, 0, 0))],
        out_shape=[jax.ShapeDtypeStruct((nsg, T, nc, nc), BF16),
                   jax.ShapeDtypeStruct((nsg, T * nc, 2 * ns), BF16),
                   jax.ShapeDtypeStruct((nsg, 2 * ns, T * nc), BF16),
                   jax.ShapeDtypeStruct((nsg, 1, 2 * ns), F32)],
        compiler_params=_cparams(("arbitrary",)),
        name="s5_weights",
    )(rows(lam_re), rows(lam_im), rows(dtb),
      embed_b(b_re), embed_b(b_im), embed_c(c_re), embed_c(c_im), d_diag)


def _s5_scan_kernel(u_ref, um_ref, kt_ref, win_ref, wout_ref, at_ref, y_ref,
                    uc_ref, v_ref, xp_ref, m_ref, *, n_chunks):
    T = S5_T
    nc = u_ref.shape[1]
    ns = at_ref.shape[1] // 2

    @pl.when(pl.program_id(0) == 0)
    def _():
        zero = jnp.zeros((nc, nc), BF16)
        for s in range(1, T):
            for t in range(s):
                m_ref[s * nc:(s + 1) * nc, t * nc:(t + 1) * nc] = zero

    for tau in range(T):
        tap = kt_ref[tau]
        for s in range(T - tau):
            m_ref[s * nc:(s + 1) * nc, (s + tau) * nc:(s + tau + 1) * nc] = tap

    for t in range(T):
        uc_ref[0:n_chunks, t * nc:(t + 1) * nc] = u_ref[pl.ds(t, n_chunks, stride=T), :].astype(BF16)
        uc_ref[n_chunks:n_chunks + 1, t * nc:(t + 1) * nc] = um_ref[t:t + 1, :].astype(BF16)
    pad = uc_ref.shape[0] - n_chunks - 1
    uc_ref[n_chunks + 1:, :] = jnp.zeros((pad, T * nc), BF16)
    uc = uc_ref[...]
    v_ref[...] = _dot(uc, win_ref[...])
    a_re, a_im = at_ref[:, 0:ns], at_ref[:, ns:2 * ns]

    x_re0 = v_ref[n_chunks:n_chunks + 1, 0:ns]
    x_im0 = v_ref[n_chunks:n_chunks + 1, ns:2 * ns]
    xp_ref[n_chunks:, :] = jnp.zeros((pad + 1, 2 * ns), F32)

    def step(c, carry):
        x_re, x_im = carry
        xp_ref[pl.ds(c, 1), 0:ns] = x_re
        xp_ref[pl.ds(c, 1), ns:2 * ns] = x_im
        v_re = v_ref[pl.ds(c, 1), 0:ns]
        v_im = v_ref[pl.ds(c, 1), ns:2 * ns]
        return (a_re * x_re - a_im * x_im + v_re, a_re * x_im + a_im * x_re + v_im)

    lax.fori_loop(0, n_chunks, step, (x_re0, x_im0))
    yc = _dot(uc, m_ref[...]) + _dot(xp_ref[...].astype(BF16), wout_ref[...])
    for t in range(T):
        y_ref[pl.ds(t, n_chunks, stride=T), :] = yc[0:n_chunks, t * nc:(t + 1) * nc]


def _s5_scan(u, um, kt, win, wout, at):
    u, col0 = u
    n, width = u.shape[0], um.shape[1]
    T = S5_T
    nsg = kt.shape[0]
    nc = width // nsg
    ns2 = at.shape[2]
    assert um.shape[0] == T and n % T == 0 and col0 % nc == 0
    n_chunks = n // T
    rows = -(-(n_chunks + 1) // 16) * 16
    sq = pl.Squeezed()
    return pl.pallas_call(
        functools.partial(_s5_scan_kernel, n_chunks=n_chunks),
        grid=(nsg,),
        in_specs=[pl.BlockSpec((n, nc), lambda z: (0, col0 // nc + z)),
                  pl.BlockSpec((T, nc), lambda z: (0, z)),
                  pl.BlockSpec((sq, T, nc, nc), lambda z: (z, 0, 0, 0)),
                  pl.BlockSpec((sq, T * nc, ns2), lambda z: (z, 0, 0)),
                  pl.BlockSpec((sq, ns2, T * nc), lambda z: (z, 0, 0)),
                  pl.BlockSpec((sq, 1, ns2), lambda z: (z, 0, 0))],
        out_specs=pl.BlockSpec((n, nc), lambda z: (0, z)),
        out_shape=jax.ShapeDtypeStruct((n, width), F32),
        scratch_shapes=[pltpu.VMEM((rows, T * nc), BF16), pltpu.VMEM((rows, ns2), F32),
                        pltpu.VMEM((rows, ns2), F32), pltpu.VMEM((T * nc, T * nc), BF16)],
        compiler_params=_cparams(("arbitrary",)),
        name="s5_scan",
    )(u, um, kt, win, wout, at)


def _merge_kernel(*refs, pieces):
    h_ref, o_ref, y_ref = refs[0:3]
    gh_refs, gs_refs = refs[3:3 + pieces], refs[3 + pieces:3 + 2 * pieces]
    wp_ref, wa_ref, wb_ref, wo_ref, out_ref = refs[3 + 2 * pieces:]
    gh = jnp.concatenate([r[...] for r in gh_refs], axis=1).astype(F32)
    gs = jnp.concatenate([r[...] for r in gs_refs], axis=1).astype(F32)
    y_hg = _dot(o_ref[...], wp_ref[...])
    z = jax.nn.gelu(y_ref[...]).astype(BF16)
    y_s5 = _dot(z, wa_ref[...]) * jax.nn.sigmoid(_dot(z, wb_ref[...]))
    merged = gh * y_hg + gs * y_s5
    out_ref[...] = h_ref[...] + _dot(merged.astype(BF16), wo_ref[...])


def _merge(hx, o, y, gh, gs, wp, wa, wb, wo, *, tm=256):
    n, d = hx.shape
    row = lambda i: (i, 0)
    const = lambda i: (0, 0)
    gw = functools.reduce(math.gcd, (gh[1], gs[1], d))
    assert gw % LANE == 0
    pieces = d // gw

    def resident(shape):
        return pl.BlockSpec(shape, const, pipeline_mode=pl.Buffered(1))

    def view(col0):
        return [pl.BlockSpec((tm, gw), functools.partial(lambda i, k: (i, k), k=col0 // gw + k))
                for k in range(pieces)]

    return pl.pallas_call(
        functools.partial(_merge_kernel, pieces=pieces),
        grid=(n // tm,),
        in_specs=[pl.BlockSpec((tm, d), row), pl.BlockSpec((tm, o.shape[1]), row),
                  pl.BlockSpec((tm, y.shape[1]), row)] + view(gh[1]) + view(gs[1])
        + [resident(wp.shape), resident(wa.shape), resident(wb.shape), resident(wo.shape)],
        out_specs=pl.BlockSpec((tm, d), row),
        out_shape=jax.ShapeDtypeStruct((n, d), F32),
        compiler_params=_cparams(("arbitrary",)),
        name="mixer_merge",
    )(hx, o, y, *([gh[0]] * pieces), *([gs[0]] * pieces), wp, wa, wb, wo)


def kernel(x, meta_tokens, norm_ffn1, ffn1_w_gate, ffn1_w_up, ffn1_w_down, norm_mix, w_in, hg_lower_bounds, hg_norm, hg_w_proj, s5_lam_re, s5_lam_im, s5_log_step, s5_b_re, s5_b_im, s5_c_re, s5_c_im, s5_d, s5_w_glu_a, s5_w_glu_b, w_out, norm_ffn2, ffn2_w_gate, ffn2_w_up, ffn2_w_down, norm_final):
    batch, seq, d = x.shape
    depth = norm_ffn1.shape[0]
    assert batch == 1 and depth == 1
    hg_k = hg_lower_bounds.shape[1]
    hg_w = hg_norm.shape[1]
    s5_w = s5_w_glu_a.shape[1]
    bf = lambda w: w.astype(BF16)

    hx = x[0].astype(F32)
    hm = meta_tokens.astype(F32)
    tm = min(512, seq)
    h0, n0, hm, nm, w1_gate, w1_up, w1_down = _ffn(
        hx, hm, norm_ffn1[0], norm_mix[0], ffn1_w_gate[0], ffn1_w_up[0], ffn1_w_down[0],
        emit_norm=True, rows=tm, tm=tm, tf=256, name="ffn_head")
    hx, nx, w2_gate, w2_up, w2_down = _ffn(
        hx, None, norm_ffn1[0], norm_mix[0], w1_gate, w1_up, w1_down, emit_norm=True, head=(h0, n0),
        cast=(ffn2_w_gate[0], ffn2_w_up[0], ffn2_w_down[0]), tm=tm, name="ffn_body")

    sizes = (hg_k, hg_k, hg_w, hg_w, s5_w, d, d)
    assert sum(sizes) == w_in.shape[2]
    off = [0]
    for s in sizes:
        off.append(off[-1] + s)

    def group(ids, rows):
        w = jnp.concatenate([w_in[0][:, off[k]:off[k + 1]] for k in ids], axis=1).astype(BF16)
        coef = jnp.concatenate([jnp.broadcast_to(jnp.asarray(rows[k], F32)[:, None], (4, sizes[k]))
                                for k in ids], axis=1)
        starts = {}
        col = 0
        for k in ids:
            starts[k] = col
            col += sizes[k]
        return w, coef, starts

    silu, ident, sigm = (1.0, 0.0, 0.0, 0.0), (0.0, 1.0, 0.0, 0.0), (0.0, 0.0, 1.0, 0.0)
    rows = {0: (HG_DK ** -0.5, 0.0, 0.0, 0.0), 1: (0.0, 0.0, 0.0, 1.0), 2: ident, 3: silu,
            4: ident, 5: sigm, 6: sigm}
    w32, c32, s32 = group((0, 1, 4), rows)
    w16, c16, s16 = group((2, 3, 5, 6), rows)
    x32, m32 = _inproj(nx, nm, w32, c32, hg_lower_bounds, s32[1], F32)
    x16, m16 = _inproj(nx, nm, w16, c16, None, 0, BF16)

    o = _hgrn((x32, s32[0]), (x32, s32[1]), (x16, s16[2]), (x16, s16[3]),
              m32[:, s32[1]:s32[1] + hg_k], m16[:, s16[2]:s16[2] + hg_w], hg_norm[0])
    m, win, wout, at = _s5_weights(s5_lam_re[0], s5_lam_im[0], s5_log_step[0], s5_b_re[0],
                                   s5_b_im[0], s5_c_re[0], s5_c_im[0], s5_d[0])
    y = _s5_scan((x32, s32[4]), m32[:, s32[4]:s32[4] + s5_w], m, win, wout, at)
    hx = _merge(hx, o, y, (x16, s16[5]), (x16, s16[6]), bf(hg_w_proj[0]), bf(s5_w_glu_a[0]),
                bf(s5_w_glu_b[0]), bf(w_out[0]))
    out, = _ffn(hx, None, norm_ffn2[0], norm_final, w2_gate, w2_up, w2_down, emit_norm=False,
                tm=tm, name="ffn_final")
    return out[None].astype(x.dtype)
```

```python
import functools
import math

import numpy as np
import jax
import jax.numpy as jnp
from jax import lax
from jax.experimental import pallas as pl
from jax.experimental.pallas import tpu as pltpu

EPS = 1e-6
HG_DK = 128
HG_CHUNK = 64
S5_GROUP = 16
S5_STATE = 64
S5_T = 16
S5_SG = 8
LANE = 128
VMEM_LIMIT = 56 * 1024 * 1024

F32 = jnp.float32
BF16 = jnp.bfloat16


def _cparams(sem):
    return pltpu.CompilerParams(dimension_semantics=sem, vmem_limit_bytes=VMEM_LIMIT)


def _rms(v, g):
    return v * lax.rsqrt(jnp.mean(v * v, axis=-1, keepdims=True) + EPS) * g


def _dot(a, b):
    return jnp.dot(a, b, preferred_element_type=F32)


def _dot_nt(a, b):
    return lax.dot_general(a, b, (((1,), (1,)), ((), ())), preferred_element_type=F32)


def _dot_tn(a, b):
    return lax.dot_general(a, b, (((0,), (0,)), ((), ())), preferred_element_type=F32)


def _ffn_kernel(*refs, tm, n_meta, emit_norm, w_f32, head, jobs):
    it = iter(refs)
    take = lambda k: [next(it) for _ in range(k)]
    x_ref, = take(1)
    m_ref = take(1)[0] if n_meta else None
    g_ref, gn_ref, wg_ref, wu_ref, wd_ref = take(5)
    hx_ref, hn16_ref = take(2) if head else (None, None)
    cast_in = take(len(jobs))
    ox_ref, = take(1)
    nx_ref = take(1)[0] if emit_norm else None
    om_ref, nm_ref = take(2) if n_meta else (None, None)
    w16_out = take(3) if w_f32 else ()
    cast_out = [take(len(dests)) for _, _, dests in jobs]
    hn_ref, = take(1)
    acc_ref = take(1)[0] if n_meta else ox_ref
    i = pl.program_id(0)
    j = pl.program_id(1)
    step = i * pl.num_programs(1) + j
    for src, dsts, (n_tiles, ncb, dests) in zip(cast_in, cast_out, jobs):
        tile = src[...].astype(BF16)
        if len(dsts) == 1:
            dsts[0][...] = tile
        else:
            c = jnp.minimum(step, n_tiles - 1) % ncb
            for dst, cols in zip(dsts, dests):
                owns = functools.reduce(jnp.logical_or, [c == m for m in cols])

                @pl.when(owns)
                def _(dst=dst):
                    dst[...] = tile

    def body():
        @pl.when(j == 0)
        def _():
            hn_ref[0:tm, :] = _rms(x_ref[...], g_ref[...]).astype(BF16)
            if n_meta:
                hn_ref[tm:tm + n_meta, :] = _rms(m_ref[...], g_ref[...]).astype(BF16)
            acc_ref[...] = jnp.zeros_like(acc_ref)

        tiles = [w[...].astype(BF16) if w_f32 else w[...] for w in (wg_ref, wu_ref, wd_ref)]
        for dst, tile in zip(w16_out, tiles):
            dst[...] = tile
        hn = hn_ref[...]
        gate = _dot(hn, tiles[0])
        up = _dot(hn, tiles[1])
        act = (gate * jax.nn.sigmoid(gate) * up).astype(BF16)
        acc_ref[...] += _dot(act, tiles[2])

        @pl.when(j == pl.num_programs(1) - 1)
        def _():
            hx = x_ref[...] + 0.5 * acc_ref[0:tm, :]
            if emit_norm:
                ox_ref[...] = hx
                nx_ref[...] = _rms(hx, gn_ref[...]).astype(BF16)
            else:
                ox_ref[...] = _rms(hx, gn_ref[...])
            if n_meta:
                hm = m_ref[...] + 0.5 * acc_ref[tm:tm + n_meta, :]
                om_ref[...] = hm
                nm_ref[...] = _rms(hm, gn_ref[...]).astype(BF16)

    if head:
        @pl.when((i == 0) & (j == 0))
        def _():
            ox_ref[...] = hx_ref[...]
            nx_ref[...] = hn16_ref[...]

        pl.when(i > 0)(body)
    else:
        body()


def _cast_jobs(cast, grid):
    nj = grid[1]
    in_specs, out_specs, out_shapes, static = [], [], [], []
    for src, (tr, tc), dests in cast:
        rows, cols = src.shape
        ncb = cols // tc
        n_tiles = (rows // tr) * ncb
        assert rows % tr == 0 and cols % tc == 0 and n_tiles <= grid[0] * nj

        def tile(i, j, n_tiles=n_tiles, ncb=ncb):
            t = jnp.minimum(i * nj + j, n_tiles - 1)
            return t // ncb, t % ncb

        in_specs.append(pl.BlockSpec((tr, tc), tile))
        for blocks in dests:
            park, k = [], 0
            for c in range(ncb):
                k = blocks.index(c) if c in blocks else k
                park.append(k)
            jumps = [(m, park[m] - park[m - 1]) for m in range(1, ncb) if park[m] != park[m - 1]]

            def dest(i, j, tile=tile, first=park[0], jumps=jumps):
                r, c = tile(i, j)
                return r, first + sum(jnp.where(c >= m, dk, 0) for m, dk in jumps)

            out_specs.append(pl.BlockSpec((tr, tc), tile if park == list(range(ncb)) else dest))
            out_shapes.append(jax.ShapeDtypeStruct((rows, len(blocks) * tc), BF16))
        static.append((n_tiles, ncb, tuple(tuple(b) for b in dests)))
    return in_specs, out_specs, out_shapes, tuple(static)


def _ffn(hx, hm, g, g_next, wg, wu, wd, *, emit_norm, head=None, cast=(), rows=None, name,
         tm=512, tf=512):
    n, d = hx.shape
    rows = n if rows is None else rows
    dff = wg.shape[1]
    n_meta = 0 if hm is None else hm.shape[0]
    w_f32 = wg.dtype == F32
    grid = (rows // tm, dff // tf)
    assert not w_f32 or grid[0] == 1
    row = lambda i, j: (i, 0)
    const = lambda i, j: (0, 0)
    in_specs = [pl.BlockSpec((tm, d), row)]
    args = [hx]
    if n_meta:
        in_specs.append(pl.BlockSpec((n_meta, d), const))
        args.append(hm)
    col = (lambda i, j: jnp.where(i == 0, 0, j)) if head else (lambda i, j: j)
    in_specs += [pl.BlockSpec((1, d), const), pl.BlockSpec((1, d), const),
                 pl.BlockSpec((d, tf), lambda i, j: (0, col(i, j))),
                 pl.BlockSpec((d, tf), lambda i, j: (0, col(i, j))),
                 pl.BlockSpec((tf, d), lambda i, j: (col(i, j), 0))]
    args += [g.reshape(1, d), g_next.reshape(1, d), wg, wu, wd]
    if head:
        in_specs += [pl.BlockSpec((tm, d), const, pipeline_mode=pl.Buffered(1))] * 2
        args += list(head)
    job_in, job_out, job_shapes, jobs = _cast_jobs(cast, grid)
    in_specs += job_in
    args += [src for src, _, _ in cast]
    out_shape = [jax.ShapeDtypeStruct((rows, d), F32)]
    out_specs = [pl.BlockSpec((tm, d), row)]
    if emit_norm:
        out_shape.append(jax.ShapeDtypeStruct((rows, d), BF16))
        out_specs.append(pl.BlockSpec((tm, d), row))
    if n_meta:
        out_shape += [jax.ShapeDtypeStruct((n_meta, d), F32), jax.ShapeDtypeStruct((n_meta, d), BF16)]
        out_specs += [pl.BlockSpec((n_meta, d), const)] * 2
    if w_f32:
        out_shape += [jax.ShapeDtypeStruct(w.shape, BF16) for w in (wg, wu, wd)]
        out_specs += [pl.BlockSpec((d, tf), lambda i, j: (0, j)),
                      pl.BlockSpec((d, tf), lambda i, j: (0, j)),
                      pl.BlockSpec((tf, d), lambda i, j: (j, 0))]
    out_specs += job_out
    out_shape += job_shapes
    return pl.pallas_call(
        functools.partial(_ffn_kernel, tm=tm, n_meta=n_meta, emit_norm=emit_norm, w_f32=w_f32,
                          head=head is not None, jobs=jobs),
        grid=grid, in_specs=in_specs, out_specs=out_specs, out_shape=out_shape,
        scratch_shapes=[pltpu.VMEM((tm + n_meta, d), BF16)]
        + ([pltpu.VMEM((tm + n_meta, d), F32)] if n_meta else []),
        compiler_params=_cparams(("arbitrary", "arbitrary")),
        name=name,
    )(*args)


def _inproj_kernel(*refs, tm, n_meta, with_lb):
    if with_lb:
        x_ref, m_ref, w_ref, coef_ref, lb_ref, ox_ref, om_ref, lhs_ref = refs
    else:
        x_ref, m_ref, w_ref, coef_ref, ox_ref, om_ref, lhs_ref = refs

    @pl.when(pl.program_id(1) == 0)
    def _():
        lhs_ref[0:tm, :] = x_ref[...]
        lhs_ref[tm:tm + n_meta, :] = m_ref[...]

    tn = w_ref.shape[1]
    sw = math.gcd(tn, 2 * LANE)
    for s in range(tn // sw):
        cs = slice(s * sw, (s + 1) * sw)
        res = _dot(lhs_ref[...], w_ref[:, cs])
        sig = jax.nn.sigmoid(res)
        coef = coef_ref[:, cs]
        a, c, b, d = coef[0:1, :], coef[1:2, :], coef[2:3, :], 0.0
        if with_lb:
            lbl = lb_ref[:, cs]
            e = jnp.exp(lbl - jnp.max(lbl, axis=0, keepdims=True))
            lb = e[0:1, :] / jnp.sum(e, axis=0, keepdims=True)
            flag = coef[3:4, :]
            b = b + flag * (1.0 - lb)
            d = flag * lb
        out = res * (a * sig + c) + (b * sig + d)
        ox_ref[:, cs] = out[0:tm, :].astype(ox_ref.dtype)
        om_ref[:, cs] = out[tm:tm + n_meta, :].astype(om_ref.dtype)


def _inproj(nx, nm, w, coef, lower_bounds, lb_cols, dtype, *, tm=1024, tn=1792):
    n, d = nx.shape
    n_meta = nm.shape[0]
    cols = w.shape[1]
    tm = min(tm, n)
    with_lb = lower_bounds is not None
    tn = math.gcd(tn, cols)
    assert tn % LANE == 0 and n % tm == 0
    in_specs = [pl.BlockSpec((tm, d), lambda i, j: (i, 0)),
                pl.BlockSpec((n_meta, d), lambda i, j: (0, 0)),
                pl.BlockSpec((d, tn), lambda i, j: (0, j)),
                pl.BlockSpec((4, tn), lambda i, j: (0, j))]
    args = [nx, nm, w, coef]
    if with_lb:
        nlb, k = lower_bounds.shape
        lb_full = jnp.pad(lower_bounds.astype(F32), ((0, 0), (lb_cols, cols - lb_cols - k)))
        in_specs.append(pl.BlockSpec((nlb, tn), lambda i, j: (0, j)))
        args.append(lb_full)
    ox, om = pl.pallas_call(
        functools.partial(_inproj_kernel, tm=tm, n_meta=n_meta, with_lb=with_lb),
        grid=(n // tm, cols // tn),
        in_specs=in_specs,
        out_specs=[pl.BlockSpec((tm, tn), lambda i, j: (i, j)),
                   pl.BlockSpec((pl.Squeezed(), n_meta, tn), lambda i, j: (i, 0, j))],
        out_shape=[jax.ShapeDtypeStruct((n, cols), dtype),
                   jax.ShapeDtypeStruct((n // tm, n_meta, cols), dtype)],
        scratch_shapes=[pltpu.VMEM((tm + n_meta, d), BF16)],
        compiler_params=_cparams(("arbitrary", "arbitrary")),
        name="mixer_inproj_" + jnp.dtype(dtype).name,
    )(*args)
    return ox, om[0]


def _split3(x):
    hi = x.astype(BF16)
    r1 = x - hi.astype(F32)
    mid = r1.astype(BF16)
    lo = (r1 - mid.astype(F32)).astype(BF16)
    return hi, mid, lo


def _dot3(m01, parts):
    return _dot(m01, parts[0]) + _dot(m01, parts[1]) + _dot(m01, parts[2])


def _tri(n, rep=1):
    r = lax.broadcasted_iota(jnp.int32, (n * rep, n), 0)
    c = lax.broadcasted_iota(jnp.int32, (n * rep, n), 1)
    return jnp.where(r >= c * rep, 1.0, 0.0).astype(BF16)


def _hgrn_tables(c):
    sums = [np.tril(np.ones((c, c)))]
    level = np.full((c, c), -1, np.int32)
    t_idx, s_idx = np.meshgrid(np.arange(c), np.arange(c), indexing="ij")
    halves = []
    h = c // 2
    while h >= 1:
        if h < 8:
            m = np.zeros((c, c))
            for r in range(c):
                mid = (r // (2 * h)) * 2 * h + h
                if r < mid:
                    m[r, r + 1:mid] = 1.0
                else:
                    m[r, mid:r + 1] = 1.0
            sums.append(m)
        pair = 2 * h
        level[(t_idx // pair == s_idx // pair) & (s_idx % pair < h) & (t_idx % pair >= h)] = len(halves)
        halves.append(h)
        h //= 2
    assert len(halves) % 2 == 0
    level[np.arange(c), np.arange(c)] = len(halves)
    owner = np.full((c, 2 * c), -1, np.int32)
    owner[:, :c] = np.where((level >= 0) & (level % 2 == 0), level // 2, -1)
    owner[:, c:] = np.where(level % 2 == 1, level // 2, -1)
    return np.concatenate(sums, axis=0), owner, tuple(halves)


def _hgrn_score_products(q, k, x, halves):
    C = HG_CHUNK
    b = x[0:C]
    zero = jnp.zeros((C, HG_DK), BF16)
    diag = jnp.sum(q * k, axis=-1, keepdims=True)
    qts, kts = [], []
    fine = 0
    for h in halves:
        if h >= 8:
            none = jnp.zeros((h, HG_DK), F32)
            qh, kh = [], []
            for r in range(0, C, 2 * h):
                b_mid = b[r + h - 1:r + h, :]
                kh += [k[r:r + h] * jnp.exp(b_mid - b[r:r + h]), none]
                qh += [none, q[r + h:r + 2 * h] * jnp.exp(b[r + h:r + 2 * h] - b_mid)]
            qts.append(jnp.concatenate(qh, axis=0).astype(BF16))
            kts.append(jnp.concatenate(kh, axis=0).astype(BF16))
        else:
            fine += 1
            e = jnp.exp(x[fine * C:(fine + 1) * C])
            qts.append((q * e).astype(BF16))
            kts.append((k * e).astype(BF16))
    prods = []
    for p in range(len(halves) // 2):
        lhs = jnp.concatenate([qts[2 * p], qts[2 * p + 1]], axis=1)
        rhs = jnp.concatenate([jnp.concatenate([kts[2 * p], zero], axis=1),
                               jnp.concatenate([zero, kts[2 * p + 1]], axis=1)], axis=0)
        prods.append(_dot_nt(lhs, rhs))
    return prods, diag


def _hgrn_pair(q2, f2, v2, gate2, st2, sums, owner, halves):
    C, D = HG_CHUNK, HG_DK
    halfs = (slice(0, D), slice(D, 2 * D))
    k2 = 1.0 - f2
    x2 = _dot(sums, jnp.concatenate(_split3(jnp.log(f2)), axis=0))
    yield
    b2 = x2[0:C]
    b_last = b2[C - 1:C, :]
    kdec2 = (k2 * jnp.exp(b_last - b2)).astype(BF16)
    qe2 = (q2 * jnp.exp(b2)).astype(BF16)
    st2b = st2.astype(BF16)
    scores = [_hgrn_score_products(q2[:, c], k2[:, c], x2[:, c], halves) for c in halfs]
    yield
    zero = jnp.zeros((2 * C, D), BF16)
    os, vvts, kds = [], [], []
    for j, c in enumerate(halfs):
        prods, diag = scores[j]
        a = jnp.where(owner == len(prods), diag, 0.0)
        for p, prod in enumerate(prods):
            a = jnp.where(owner == p, prod, a)
        v = v2[:, c]
        vvt = jnp.concatenate([v, v], axis=0).T
        os.append(_dot_nt(jnp.concatenate([qe2[:, c], a.astype(BF16)], axis=1),
                          jnp.concatenate([st2b[:, c], vvt], axis=1)))
        vvts.append(vvt)
        kd = jnp.concatenate([kdec2[:, c], jnp.zeros((C, D), BF16)], axis=0)
        kds.append(jnp.concatenate([kd, zero] if j == 0 else [zero, kd], axis=1))
    st_new = st2 * jnp.exp(b_last) + _dot(jnp.concatenate(vvts, axis=1), jnp.concatenate(kds, axis=0))
    yield
    ons = [o * lax.rsqrt(jnp.mean(o * o, axis=-1, keepdims=True) + EPS) for o in os]
    on2 = jnp.concatenate(ons, axis=1) * gate2
    return on2.astype(BF16), st_new


def _interleave(gens):
    out = [None] * len(gens)
    live = list(range(len(gens)))
    while live:
        for g in list(live):
            try:
                next(gens[g])
            except StopIteration as stop:
                out[g] = stop.value
                live.remove(g)
    return out


def _hgrn_kernel(q_ref, f_ref, v_ref, go_ref, fm_ref, vm_ref, gn_ref, sums_ref, owner_ref,
                 o_ref, st_ref, *, n_chunks, n_meta, hb, halves):
    i = pl.program_id(0)
    p0 = pl.program_id(1) * (hb // 2)
    C = HG_CHUNK
    W = 2 * HG_DK

    @pl.when(i == 0)
    def _():
        tri_m = _tri(n_meta)
        for pair in range(hb // 2):
            sts = []
            for hh in (2 * pair, 2 * pair + 1):
                cols = slice(hh * HG_DK, (hh + 1) * HG_DK)
                fm = fm_ref[:, cols]
                bm = _dot3(tri_m, _split3(jnp.log(fm)))
                kdec = ((1.0 - fm) * jnp.exp(bm[n_meta - 1:n_meta, :] - bm)).astype(BF16)
                sts.append(_dot_tn(vm_ref[:, cols], kdec))
            st_ref[p0 + pair] = jnp.concatenate(sts, axis=1)

    def chunk(c, states):
        r0 = pl.multiple_of(c * C, C)
        owner = owner_ref[...]
        sums = sums_ref[...]
        gens = []
        for pair in range(hb // 2):
            cols = slice(pair * W, (pair + 1) * W)
            gate2 = gn_ref[:, cols] * go_ref[pl.ds(r0, C), cols].astype(F32)
            gens.append(_hgrn_pair(q_ref[pl.ds(r0, C), cols], f_ref[pl.ds(r0, C), cols],
                                   v_ref[pl.ds(r0, C), cols], gate2, states[pair],
                                   sums, owner, halves))
        results = _interleave(gens)
        o_ref[pl.ds(r0, C), :] = jnp.concatenate([r[0] for r in results], axis=1)
        return tuple(r[1] for r in results)

    states = lax.fori_loop(0, n_chunks, chunk, tuple(st_ref[p0 + p] for p in range(hb // 2)))
    for pair in range(hb // 2):
        st_ref[p0 + pair] = states[pair]


def _hgrn(q, f, v, go, fm, vm, hg_norm, *, rb=512, hb=12):
    n, width = q[0].shape[0], hg_norm.shape[-1]
    heads = width // HG_DK
    hb = min(hb, heads)
    bw = hb * HG_DK
    n_meta = fm.shape[0]
    assert hb % 2 == 0 and heads % hb == 0
    sums, owner, halves = _hgrn_tables(HG_CHUNK)
    sums3 = jnp.asarray(np.concatenate([sums] * 3, axis=1), BF16)
    blk = lambda i, h: (i, h)
    mblk = lambda i, h: (0, h)
    const = lambda i, h: (0, 0)

    def view(col0):
        assert col0 % bw == 0
        return pl.BlockSpec((rb, bw), lambda i, h: (i, col0 // bw + h))

    return pl.pallas_call(
        functools.partial(_hgrn_kernel, n_chunks=rb // HG_CHUNK, n_meta=n_meta, hb=hb,
                          halves=halves),
        grid=(n // rb, heads // hb),
        in_specs=[view(q[1]), view(f[1]), view(v[1]), view(go[1])]
        + [pl.BlockSpec((n_meta, hb * HG_DK), mblk)] * 2
        + [pl.BlockSpec((1, hb * HG_DK), mblk),
           pl.BlockSpec(sums3.shape, const), pl.BlockSpec(owner.shape, const)],
        out_specs=pl.BlockSpec((rb, hb * HG_DK), blk),
        out_shape=jax.ShapeDtypeStruct((n, width), BF16),
        scratch_shapes=[pltpu.VMEM((heads // 2, HG_DK, 2 * HG_DK), F32)],
        compiler_params=_cparams(("arbitrary", "arbitrary")),
        name="hgrn2",
    )(q[0], f[0], v[0], go[0], fm, vm, hg_norm.reshape(1, width), sums3, jnp.asarray(owner))


def _s5_weights_kernel(lr_ref, li_ref, dt_ref, bre_ref, bim_ref, cre_ref, cim_ref, dsk_ref,
                       kt_ref, win_ref, wout_ref, at_ref):
    T = S5_T
    ns = lr_ref.shape[1]
    nc = cre_ref.shape[0]
    lr, li, dt = lr_ref[...], li_ref[...], dt_ref[...]

    def powers(tau):
        mag = jnp.exp(lr * dt * tau)
        ang = li * dt * tau
        return mag * jnp.cos(ang), mag * jnp.sin(ang)

    a_re, a_im = powers(1.0)
    num_re, num_im = a_re - 1.0, a_im
    den = lr * lr + li * li
    coef_re = (num_re * lr + num_im * li) / den
    coef_im = (num_im * lr - num_re * li) / den
    br, bi = bre_ref[...], bim_ref[...]
    bbar_re = coef_re * br - coef_im * bi
    bbar_im = coef_re * bi + coef_im * br

    pw = [powers(float(tau)) for tau in range(T + 1)]
    for s in range(T):
        p_re, p_im = pw[T - 1 - s]
        win_ref[s * nc:(s + 1) * nc, 0:ns] = (bbar_re * p_re - bbar_im * p_im).astype(BF16)
        win_ref[s * nc:(s + 1) * nc, ns:2 * ns] = (bbar_re * p_im + bbar_im * p_re).astype(BF16)
    at_ref[:, 0:ns] = pw[T][0]
    at_ref[:, ns:2 * ns] = pw[T][1]

    cr, ci = cre_ref[...], cim_ref[...]
    hp = lax.Precision.HIGHEST
    nt = (((1,), (1,)), ((), ()))
    for tau in range(T + 1):
        p_re, p_im = pw[tau]
        ca_re = cr * p_re - ci * p_im
        ca_im = cr * p_im + ci * p_re
        if tau >= 1:
            t = tau - 1
            wout_ref[0:ns, t * nc:(t + 1) * nc] = ca_re.T.astype(BF16)
            wout_ref[ns:2 * ns, t * nc:(t + 1) * nc] = (-ca_im).T.astype(BF16)
        if tau < T:
            kt = (lax.dot_general(bbar_re, ca_re, nt, precision=hp, preferred_element_type=F32)
                  - lax.dot_general(bbar_im, ca_im, nt, precision=hp, preferred_element_type=F32))
            if tau == 0:
                kt = kt + dsk_ref[...]
            kt_ref[tau] = kt.astype(BF16)


def _s5_weights(lam_re, lam_im, log_step, b_re, b_im, c_re, c_im, d_skip):
    G, N = lam_re.shape
    P = d_skip.shape[1]
    nsg = G // S5_SG
    ns, nc = S5_SG * N, S5_SG * P
    T = S5_T
    dt = jnp.exp(log_step.astype(F32))
    eye = jnp.eye(S5_SG, dtype=F32)

    def rows(a):
        return a.astype(F32).reshape(nsg, 1, ns)

    def embed_b(b):
        b = b.astype(F32).reshape(nsg, S5_SG, N, P)
        return jnp.einsum('zgnq,gh->zhqgn', b, eye).reshape(nsg, nc, ns)

    def embed_c(c):
        c = c.astype(F32).reshape(nsg, S5_SG, P, N)
        return jnp.einsum('zgpn,gh->zhpgn', c, eye).reshape(nsg, nc, ns)

    dtb = jnp.broadcast_to(dt[:, None], (G, N))
    d_diag = jax.vmap(jnp.diag)(d_skip.astype(F32).reshape(nsg, nc))
    sq = pl.Squeezed()
    spec_r = pl.BlockSpec((sq, 1, ns), lambda z: (z, 0, 0))
    spec_b = pl.BlockSpec((sq, nc, ns), lambda z: (z, 0, 0))
    return pl.pallas_call(
        _s5_weights_kernel,
        grid=(nsg,),
        in_specs=[spec_r, spec_r, spec_r, spec_b, spec_b, spec_b, spec_b,
                  pl.BlockSpec((sq, nc, nc), lambda z: (z, 0, 0))],
        out_specs=[pl.BlockSpec((sq, T, nc, nc), lambda z: (z, 0, 0, 0)),
                   pl.BlockSpec((sq, T * nc, 2 * ns), lambda z: (z, 0, 0)),
                   pl.BlockSpec((sq, 2 * ns, T * nc), lambda z: (z, 0, 0)),
                   pl.BlockSpec((sq, 1, 2 * ns), lambda z: (z, 0, 0))],
        out_shape=[jax.ShapeDtypeStruct((nsg, T, nc, nc), BF16),
                   jax.ShapeDtypeStruct((nsg, T * nc, 2 * ns), BF16),
                   jax.ShapeDtypeStruct((nsg, 2 * ns, T * nc), BF16),
                   jax.ShapeDtypeStruct((nsg, 1, 2 * ns), F32)],
        compiler_params=_cparams(("arbitrary",)),
        name="s5_weights",
    )(rows(lam_re), rows(lam_im), rows(dtb),
      embed_b(b_re), embed_b(b_im), embed_c(c_re), embed_c(c_im), d_diag)


def _s5_scan_kernel(u_ref, um_ref, kt_ref, win_ref, wout_ref, at_ref, y_ref,
                    uc_ref, v_ref, xp_ref, m_ref, *, n_chunks):
    T = S5_T
    nc = u_ref.shape[1]
    ns = at_ref.shape[1] // 2

    @pl.when(pl.program_id(0) == 0)
    def _():
        zero = jnp.zeros((nc, nc), BF16)
        for s in range(1, T):
            for t in range(s):
                m_ref[s * nc:(s + 1) * nc, t * nc:(t + 1) * nc] = zero

    for tau in range(T):
        tap = kt_ref[tau]
        for s in range(T - tau):
            m_ref[s * nc:(s + 1) * nc, (s + tau) * nc:(s + tau + 1) * nc] = tap

    for t in range(T):
        uc_ref[0:n_chunks, t * nc:(t + 1) * nc] = u_ref[pl.ds(t, n_chunks, stride=T), :].astype(BF16)
        uc_ref[n_chunks:n_chunks + 1, t * nc:(t + 1) * nc] = um_ref[t:t + 1, :].astype(BF16)
    pad = uc_ref.shape[0] - n_chunks - 1
    uc_ref[n_chunks + 1:, :] = jnp.zeros((pad, T * nc), BF16)
    uc = uc_ref[...]
    v_ref[...] = _dot(uc, win_ref[...])
    a_re, a_im = at_ref[:, 0:ns], at_ref[:, ns:2 * ns]

    x_re0 = v_ref[n_chunks:n_chunks + 1, 0:ns]
    x_im0 = v_ref[n_chunks:n_chunks + 1, ns:2 * ns]
    xp_ref[n_chunks:, :] = jnp.zeros((pad + 1, 2 * ns), F32)

    def step(c, carry):
        x_re, x_im = carry
        xp_ref[pl.ds(c, 1), 0:ns] = x_re
        xp_ref[pl.ds(c, 1), ns:2 * ns] = x_im
        v_re = v_ref[pl.ds(c, 1), 0:ns]
        v_im = v_ref[pl.ds(c, 1), ns:2 * ns]
        return (a_re * x_re - a_im * x_im + v_re, a_re * x_im + a_im * x_re + v_im)

    lax.fori_loop(0, n_chunks, step, (x_re0, x_im0))
    yc = _dot(uc, m_ref[...]) + _dot(xp_ref[...].astype(BF16), wout_ref[...])
    for t in range(T):
        y_ref[pl.ds(t, n_chunks, stride=T), :] = yc[0:n_chunks, t * nc:(t + 1) * nc]


def _s5_scan(u, um, kt, win, wout, at):
    u, col0 = u
    n, width = u.shape[0], um.shape[1]
    T = S5_T
    nsg = kt.shape[0]
    nc = width // nsg
    ns2 = at.shape[2]
    assert um.shape[0] == T and n % T == 0 and col0 % nc == 0
    n_chunks = n // T
    rows = -(-(n_chunks + 1) // 16) * 16
    sq = pl.Squeezed()
    return pl.pallas_call(
        functools.partial(_s5_scan_kernel, n_chunks=n_chunks),
        grid=(nsg,),
        in_specs=[pl.BlockSpec((n, nc), lambda z: (0, col0 // nc + z)),
                  pl.BlockSpec((T, nc), lambda z: (0, z)),
                  pl.BlockSpec((sq, T, nc, nc), lambda z: (z, 0, 0, 0)),
                  pl.BlockSpec((sq, T * nc, ns2), lambda z: (z, 0, 0)),
                  pl.BlockSpec((sq, ns2, T * nc), lambda z: (z, 0, 0)),
                  pl.BlockSpec((sq, 1, ns2), lambda z: (z, 0, 0))],
        out_specs=pl.BlockSpec((n, nc), lambda z: (0, z)),
        out_shape=jax.ShapeDtypeStruct((n, width), F32),
        scratch_shapes=[pltpu.VMEM((rows, T * nc), BF16), pltpu.VMEM((rows, ns2), F32),
                        pltpu.VMEM((rows, ns2), F32), pltpu.VMEM((T * nc, T * nc), BF16)],
        compiler_params=_cparams(("arbitrary",)),
        name="s5_scan",
    )(u, um, kt, win, wout, at)


def _merge_kernel(*refs, pieces):
    h_ref, o_ref, y_ref = refs[0:3]
    gh_refs, gs_refs = refs[3:3 + pieces], refs[3 + pieces:3 + 2 * pieces]
    wp_ref, wa_ref, wb_ref, wo_ref, out_ref = refs[3 + 2 * pieces:]
    gh = jnp.concatenate([r[...] for r in gh_refs], axis=1).astype(F32)
    gs = jnp.concatenate([r[...] for r in gs_refs], axis=1).astype(F32)
    y_hg = _dot(o_ref[...], wp_ref[...])
    z = jax.nn.gelu(y_ref[...]).astype(BF16)
    y_s5 = _dot(z, wa_ref[...]) * jax.nn.sigmoid(_dot(z, wb_ref[...]))
    merged = gh * y_hg + gs * y_s5
    out_ref[...] = h_ref[...] + _dot(merged.astype(BF16), wo_ref[...])


def _merge(hx, o, y, gh, gs, wp, wa, wb, wo, *, tm=256):
    n, d = hx.shape
    row = lambda i: (i, 0)
    const = lambda i: (0, 0)
    gw = functools.reduce(math.gcd, (gh[1], gs[1], d))
    assert gw % LANE == 0
    pieces = d // gw

    def resident(shape):
        return pl.BlockSpec(shape, const, pipeline_mode=pl.Buffered(1))

    def view(col0):
        return [pl.BlockSpec((tm, gw), functools.partial(lambda i, k: (i, k), k=col0 // gw + k))
                for k in range(pieces)]

    return pl.pallas_call(
        functools.partial(_merge_kernel, pieces=pieces),
        grid=(n // tm,),
        in_specs=[pl.BlockSpec((tm, d), row), pl.BlockSpec((tm, o.shape[1]), row),
                  pl.BlockSpec((tm, y.shape[1]), row)] + view(gh[1]) + view(gs[1])
        + [resident(wp.shape), resident(wa.shape), resident(wb.shape), resident(wo.shape)],
        out_specs=pl.BlockSpec((tm, d), row),
        out_shape=jax.ShapeDtypeStruct((n, d), F32),
        compiler_params=_cparams(("arbitrary",)),
        name="mixer_merge",
    )(hx, o, y, *([gh[0]] * pieces), *([gs[0]] * pieces), wp, wa, wb, wo)


def kernel(x, meta_tokens, norm_ffn1, ffn1_w_gate, ffn1_w_up, ffn1_w_down, norm_mix, w_in, hg_lower_bounds, hg_norm, hg_w_proj, s5_lam_re, s5_lam_im, s5_log_step, s5_b_re, s5_b_im, s5_c_re, s5_c_im, s5_d, s5_w_glu_a, s5_w_glu_b, w_out, norm_ffn2, ffn2_w_gate, ffn2_w_up, ffn2_w_down, norm_final):
    batch, seq, d = x.shape
    depth = norm_ffn1.shape[0]
    assert batch == 1 and depth == 1
    hg_k = hg_lower_bounds.shape[1]
    hg_w = hg_norm.shape[1]
    s5_w = s5_w_glu_a.shape[1]
    hx = x[0].astype(F32)
    hm = meta_tokens.astype(F32)
    tm = min(512, seq)
    h0, n0, hm, nm, w1_gate, w1_up, w1_down = _ffn(
        hx, hm, norm_ffn1[0], norm_mix[0], ffn1_w_gate[0], ffn1_w_up[0], ffn1_w_down[0],
        emit_norm=True, rows=tm, tm=tm, tf=256, name="ffn_head")
    sizes = (hg_k, hg_k, hg_w, hg_w, s5_w, d, d)
    assert sum(sizes) == w_in.shape[2]
    off = [0]
    for s in sizes:
        off.append(off[-1] + s)
    ids32, ids16 = (0, 1, 4), (2, 3, 5, 6)
    tc_in = functools.reduce(math.gcd, sizes + (512,))

    def starts(ids):
        out, col = {}, 0
        for k in ids:
            out[k] = col
            col += sizes[k]
        return out

    def blocks(ids):
        return [c for k in ids for c in range(off[k] // tc_in, off[k + 1] // tc_in)]

    n_steps = (seq // tm) * (ffn1_w_gate.shape[2] // 512)

    def job(w, tc=512, dests=None):
        rows, cols = w.shape
        tc = math.gcd(tc, cols)
        tr = 128
        while (rows // tr) * (cols // tc) > n_steps:
            tr *= 2
        return w, (tr, tc), dests or [list(range(cols // tc))]

    (hx, nx, w2_gate, w2_up, w2_down, w32, w16, w_proj, w_glu_a, w_glu_b, w_o) = _ffn(
        hx, None, norm_ffn1[0], norm_mix[0], w1_gate, w1_up, w1_down, emit_norm=True, head=(h0, n0),
        cast=(job(ffn2_w_gate[0]), job(ffn2_w_up[0]), job(ffn2_w_down[0]),
              job(w_in[0], tc_in, [blocks(ids32), blocks(ids16)]), job(hg_w_proj[0]),
              job(s5_w_glu_a[0]), job(s5_w_glu_b[0]), job(w_out[0])),
        tm=tm, name="ffn_body")

    silu, ident, sigm = (1.0, 0.0, 0.0, 0.0), (0.0, 1.0, 0.0, 0.0), (0.0, 0.0, 1.0, 0.0)
    rows = {0: (HG_DK ** -0.5, 0.0, 0.0, 0.0), 1: (0.0, 0.0, 0.0, 1.0), 2: ident, 3: silu,
            4: ident, 5: sigm, 6: sigm}

    def coef(ids):
        return jnp.concatenate([jnp.broadcast_to(jnp.asarray(rows[k], F32)[:, None], (4, sizes[k]))
                                for k in ids], axis=1)

    s32, s16 = starts(ids32), starts(ids16)
    x32, m32 = _inproj(nx, nm, w32, coef(ids32), hg_lower_bounds, s32[1], F32)
    x16, m16 = _inproj(nx, nm, w16, coef(ids16), None, 0, BF16)

    o = _hgrn((x32, s32[0]), (x32, s32[1]), (x16, s16[2]), (x16, s16[3]),
              m32[:, s32[1]:s32[1] + hg_k], m16[:, s16[2]:s16[2] + hg_w], hg_norm[0])
    m, win, wout, at = _s5_weights(s5_lam_re[0], s5_lam_im[0], s5_log_step[0], s5_b_re[0],
                                   s5_b_im[0], s5_c_re[0], s5_c_im[0], s5_d[0])
    y = _s5_scan((x32, s32[4]), m32[:, s32[4]:s32[4] + s5_w], m, win, wout, at)
    hx = _merge(hx, o, y, (x16, s16[5]), (x16, s16[6]), w_proj, w_glu_a, w_glu_b, w_o)
    out, = _ffn(hx, None, norm_ffn2[0], norm_final, w2_gate, w2_up, w2_down, emit_norm=False,
                tm=tm, name="ffn_final")
    return out[None].astype(x.dtype)
```

```python
import functools
import math

import numpy as np
import jax
import jax.numpy as jnp
from jax import lax
from jax.experimental import pallas as pl
from jax.experimental.pallas import tpu as pltpu

EPS = 1e-6
HG_DK = 128
HG_CHUNK = 64
S5_GROUP = 16
S5_STATE = 64
S5_T = 16
S5_SG = 8
LANE = 128
VMEM_LIMIT = 56 * 1024 * 1024

F32 = jnp.float32
BF16 = jnp.bfloat16


def _cparams(sem):
    return pltpu.CompilerParams(dimension_semantics=sem, vmem_limit_bytes=VMEM_LIMIT)


def _rms(v, g):
    return v * lax.rsqrt(jnp.mean(v * v, axis=-1, keepdims=True) + EPS) * g


def _dot(a, b):
    return jnp.dot(a, b, preferred_element_type=F32)


def _dot_nt(a, b):
    return lax.dot_general(a, b, (((1,), (1,)), ((), ())), preferred_element_type=F32)


def _dot_tn(a, b):
    return lax.dot_general(a, b, (((0,), (0,)), ((), ())), preferred_element_type=F32)


def _cast_jobs(cast, grid):
    nj = grid[1]
    in_specs, out_specs, out_shapes, static = [], [], [], []
    for src, (tr, tc), dests in cast:
        rows, cols = src.shape
        ncb = cols // tc
        n_tiles = (rows // tr) * ncb
        assert rows % tr == 0 and cols % tc == 0 and n_tiles <= grid[0] * nj

        def tile(i, j, n_tiles=n_tiles, ncb=ncb):
            t = jnp.minimum(i * nj + j, n_tiles - 1)
            return t // ncb, t % ncb

        in_specs.append(pl.BlockSpec((tr, tc), tile))
        for blocks in dests:
            park, k = [], 0
            for c in range(ncb):
                k = blocks.index(c) if c in blocks else k
                park.append(k)
            jumps = [(m, park[m] - park[m - 1]) for m in range(1, ncb) if park[m] != park[m - 1]]

            def dest(i, j, tile=tile, first=park[0], jumps=jumps):
                r, c = tile(i, j)
                return r, first + sum(jnp.where(c >= m, dk, 0) for m, dk in jumps)

            out_specs.append(pl.BlockSpec((tr, tc), tile if park == list(range(ncb)) else dest))
            out_shapes.append(jax.ShapeDtypeStruct((rows, len(blocks) * tc), BF16))
        static.append((n_tiles, ncb, tuple(tuple(b) for b in dests)))
    return in_specs, out_specs, out_shapes, tuple(static)


def _n_cast_out(jobs):
    return sum(len(dests) for _, _, dests in jobs)


def _run_cast_jobs(cast_in, cast_out, jobs):
    step = pl.program_id(0) * pl.num_programs(1) + pl.program_id(1)
    outs = iter(cast_out)
    for src, (n_tiles, ncb, dests) in zip(cast_in, jobs):
        dsts = [next(outs) for _ in dests]
        tile = src[...].astype(BF16)
        if len(dsts) == 1:
            dsts[0][...] = tile
        else:
            c = jnp.minimum(step, n_tiles - 1) % ncb
            for dst, cols in zip(dsts, dests):
                owns = functools.reduce(jnp.logical_or, [c == m for m in cols])

                @pl.when(owns)
                def _(dst=dst):
                    dst[...] = tile


def _cast_tile(w, n_steps, tc=None, dests=None):
    rows, cols = w.shape
    if tc is None and rows % n_steps == 0 and (rows // n_steps) % 16 == 0:
        return w, (rows // n_steps, cols), [[0]]
    tc = math.gcd(tc or 512, cols)
    tr = 128
    while (rows // tr) * (cols // tc) > n_steps:
        tr *= 2
    return w, (tr, tc), dests or [list(range(cols // tc))]


def _ffn_kernel(*refs, tm, n_meta, emit_norm, w_f32, head, jobs):
    it = iter(refs)
    take = lambda k: [next(it) for _ in range(k)]
    x_ref, = take(1)
    m_ref = take(1)[0] if n_meta else None
    g_ref, gn_ref, wg_ref, wu_ref, wd_ref = take(5)
    hx_ref, hn16_ref = take(2) if head else (None, None)
    cast_in = take(len(jobs))
    ox_ref, = take(1)
    nx_ref = take(1)[0] if emit_norm else None
    om_ref, nm_ref = take(2) if n_meta else (None, None)
    w16_out = take(3) if w_f32 else ()
    cast_out = take(_n_cast_out(jobs))
    hn_ref, = take(1)
    acc_ref = take(1)[0] if n_meta else ox_ref
    i = pl.program_id(0)
    j = pl.program_id(1)
    _run_cast_jobs(cast_in, cast_out, jobs)

    def body():
        @pl.when(j == 0)
        def _():
            hn_ref[0:tm, :] = _rms(x_ref[...], g_ref[...]).astype(BF16)
            if n_meta:
                hn_ref[tm:tm + n_meta, :] = _rms(m_ref[...], g_ref[...]).astype(BF16)
            acc_ref[...] = jnp.zeros_like(acc_ref)

        tiles = [w[...].astype(BF16) if w_f32 else w[...] for w in (wg_ref, wu_ref, wd_ref)]
        for dst, tile in zip(w16_out, tiles):
            dst[...] = tile
        hn = hn_ref[...]
        gate = _dot(hn, tiles[0])
        up = _dot(hn, tiles[1])
        act = (gate * jax.nn.sigmoid(gate) * up).astype(BF16)
        acc_ref[...] += _dot(act, tiles[2])

        @pl.when(j == pl.num_programs(1) - 1)
        def _():
            hx = x_ref[...] + 0.5 * acc_ref[0:tm, :]
            if emit_norm:
                ox_ref[...] = hx
                nx_ref[...] = _rms(hx, gn_ref[...]).astype(BF16)
            else:
                ox_ref[...] = _rms(hx, gn_ref[...])
            if n_meta:
                hm = m_ref[...] + 0.5 * acc_ref[tm:tm + n_meta, :]
                om_ref[...] = hm
                nm_ref[...] = _rms(hm, gn_ref[...]).astype(BF16)

    if head:
        @pl.when((i == 0) & (j == 0))
        def _():
            ox_ref[...] = hx_ref[...]
            nx_ref[...] = hn16_ref[...]

        pl.when(i > 0)(body)
    else:
        body()


def _ffn(hx, hm, g, g_next, wg, wu, wd, *, emit_norm, head=None, cast=(), rows=None, name,
         tm=512, tf=512):
    n, d = hx.shape
    rows = n if rows is None else rows
    dff = wg.shape[1]
    n_meta = 0 if hm is None else hm.shape[0]
    w_f32 = wg.dtype == F32
    grid = (rows // tm, dff // tf)
    assert not w_f32 or grid[0] == 1
    row = lambda i, j: (i, 0)
    const = lambda i, j: (0, 0)
    in_specs = [pl.BlockSpec((tm, d), row)]
    args = [hx]
    if n_meta:
        in_specs.append(pl.BlockSpec((n_meta, d), const))
        args.append(hm)
    col = (lambda i, j: jnp.where(i == 0, 0, j)) if head else (lambda i, j: j)
    in_specs += [pl.BlockSpec((1, d), const), pl.BlockSpec((1, d), const),
                 pl.BlockSpec((d, tf), lambda i, j: (0, col(i, j))),
                 pl.BlockSpec((d, tf), lambda i, j: (0, col(i, j))),
                 pl.BlockSpec((tf, d), lambda i, j: (col(i, j), 0))]
    args += [g.reshape(1, d), g_next.reshape(1, d), wg, wu, wd]
    if head:
        in_specs += [pl.BlockSpec((tm, d), const, pipeline_mode=pl.Buffered(1))] * 2
        args += list(head)
    job_in, job_out, job_shapes, jobs = _cast_jobs(cast, grid)
    in_specs += job_in
    args += [src for src, _, _ in cast]
    out_shape = [jax.ShapeDtypeStruct((rows, d), F32)]
    out_specs = [pl.BlockSpec((tm, d), row)]
    if emit_norm:
        out_shape.append(jax.ShapeDtypeStruct((rows, d), BF16))
        out_specs.append(pl.BlockSpec((tm, d), row))
    if n_meta:
        out_shape += [jax.ShapeDtypeStruct((n_meta, d), F32), jax.ShapeDtypeStruct((n_meta, d), BF16)]
        out_specs += [pl.BlockSpec((n_meta, d), const)] * 2
    if w_f32:
        out_shape += [jax.ShapeDtypeStruct(w.shape, BF16) for w in (wg, wu, wd)]
        out_specs += [pl.BlockSpec((d, tf), lambda i, j: (0, j)),
                      pl.BlockSpec((d, tf), lambda i, j: (0, j)),
                      pl.BlockSpec((tf, d), lambda i, j: (j, 0))]
    out_specs += job_out
    out_shape += job_shapes
    return pl.pallas_call(
        functools.partial(_ffn_kernel, tm=tm, n_meta=n_meta, emit_norm=emit_norm, w_f32=w_f32,
                          head=head is not None, jobs=jobs),
        grid=grid, in_specs=in_specs, out_specs=out_specs, out_shape=out_shape,
        scratch_shapes=[pltpu.VMEM((tm + n_meta, d), BF16)]
        + ([pltpu.VMEM((tm + n_meta, d), F32)] if n_meta else []),
        compiler_params=_cparams(("arbitrary", "arbitrary")),
        name=name,
    )(*args)


def _inproj_kernel(*refs, tm, n_meta, with_lb, jobs):
    it = iter(refs)
    take = lambda k: [next(it) for _ in range(k)]
    x_ref, m_ref, w_ref, coef_ref = take(4)
    lb_ref = take(1)[0] if with_lb else None
    cast_in = take(len(jobs))
    ox_ref, om_ref = take(2)
    cast_out = take(_n_cast_out(jobs))
    lhs_ref, = take(1)
    _run_cast_jobs(cast_in, cast_out, jobs)

    @pl.when(pl.program_id(1) == 0)
    def _():
        lhs_ref[0:tm, :] = x_ref[...]
        lhs_ref[tm:tm + n_meta, :] = m_ref[...]

    tn = w_ref.shape[1]
    sw = math.gcd(tn, 2 * LANE)
    for s in range(tn // sw):
        cs = slice(s * sw, (s + 1) * sw)
        res = _dot(lhs_ref[...], w_ref[:, cs])
        sig = jax.nn.sigmoid(res)
        coef = coef_ref[:, cs]
        a, c, b, d = coef[0:1, :], coef[1:2, :], coef[2:3, :], 0.0
        if with_lb:
            lbl = lb_ref[:, cs]
            e = jnp.exp(lbl - jnp.max(lbl, axis=0, keepdims=True))
            lb = e[0:1, :] / jnp.sum(e, axis=0, keepdims=True)
            flag = coef[3:4, :]
            b = b + flag * (1.0 - lb)
            d = flag * lb
        out = res * (a * sig + c) + (b * sig + d)
        ox_ref[:, cs] = out[0:tm, :].astype(ox_ref.dtype)
        om_ref[:, cs] = out[tm:tm + n_meta, :].astype(om_ref.dtype)


def _inproj_grid(n, cols, tm=1024, tn=1792):
    tm = min(tm, n)
    tn = math.gcd(tn, cols)
    assert tn % LANE == 0 and n % tm == 0
    return tm, tn, (n // tm, cols // tn)


def _inproj(nx, nm, w, coef, lower_bounds, lb_cols, dtype, cast=()):
    n, d = nx.shape
    n_meta = nm.shape[0]
    cols = w.shape[1]
    with_lb = lower_bounds is not None
    tm, tn, grid = _inproj_grid(n, cols)
    job_in, job_out, job_shapes, jobs = _cast_jobs(cast, grid)
    in_specs = [pl.BlockSpec((tm, d), lambda i, j: (i, 0)),
                pl.BlockSpec((n_meta, d), lambda i, j: (0, 0)),
                pl.BlockSpec((d, tn), lambda i, j: (0, j)),
                pl.BlockSpec((4, tn), lambda i, j: (0, j))]
    args = [nx, nm, w, coef]
    if with_lb:
        nlb, k = lower_bounds.shape
        lb_full = jnp.pad(lower_bounds.astype(F32), ((0, 0), (lb_cols, cols - lb_cols - k)))
        in_specs.append(pl.BlockSpec((nlb, tn), lambda i, j: (0, j)))
        args.append(lb_full)
    ox, om, *copies = pl.pallas_call(
        functools.partial(_inproj_kernel, tm=tm, n_meta=n_meta, with_lb=with_lb, jobs=jobs),
        grid=grid,
        in_specs=in_specs + job_in,
        out_specs=[pl.BlockSpec((tm, tn), lambda i, j: (i, j)),
                   pl.BlockSpec((pl.Squeezed(), n_meta, tn), lambda i, j: (i, 0, j))] + job_out,
        out_shape=[jax.ShapeDtypeStruct((n, cols), dtype),
                   jax.ShapeDtypeStruct((n // tm, n_meta, cols), dtype)] + job_shapes,
        scratch_shapes=[pltpu.VMEM((tm + n_meta, d), BF16)],
        compiler_params=_cparams(("arbitrary", "arbitrary")),
        name="mixer_inproj_" + jnp.dtype(dtype).name,
    )(*args, *[src for src, _, _ in cast])
    return (ox, om[0], *copies)


def _split3(x):
    hi = x.astype(BF16)
    r1 = x - hi.astype(F32)
    mid = r1.astype(BF16)
    lo = (r1 - mid.astype(F32)).astype(BF16)
    return hi, mid, lo


def _dot3(m01, parts):
    return _dot(m01, parts[0]) + _dot(m01, parts[1]) + _dot(m01, parts[2])


def _tri(n, rep=1):
    r = lax.broadcasted_iota(jnp.int32, (n * rep, n), 0)
    c = lax.broadcasted_iota(jnp.int32, (n * rep, n), 1)
    return jnp.where(r >= c * rep, 1.0, 0.0).astype(BF16)


def _hgrn_tables(c):
    sums = [np.tril(np.ones((c, c)))]
    level = np.full((c, c), -1, np.int32)
    t_idx, s_idx = np.meshgrid(np.arange(c), np.arange(c), indexing="ij")
    halves = []
    h = c // 2
    while h >= 1:
        if h < 8:
            m = np.zeros((c, c))
            for r in range(c):
                mid = (r // (2 * h)) * 2 * h + h
                if r < mid:
                    m[r, r + 1:mid] = 1.0
                else:
                    m[r, mid:r + 1] = 1.0
            sums.append(m)
        pair = 2 * h
        level[(t_idx // pair == s_idx // pair) & (s_idx % pair < h) & (t_idx % pair >= h)] = len(halves)
        halves.append(h)
        h //= 2
    assert len(halves) % 2 == 0
    level[np.arange(c), np.arange(c)] = len(halves)
    owner = np.full((c, 2 * c), -1, np.int32)
    owner[:, :c] = np.where((level >= 0) & (level % 2 == 0), level // 2, -1)
    owner[:, c:] = np.where(level % 2 == 1, level // 2, -1)
    return np.concatenate(sums, axis=0), owner, tuple(halves)


def _hgrn_score_products(q, k, x, halves):
    C = HG_CHUNK
    b = x[0:C]
    zero = jnp.zeros((C, HG_DK), BF16)
    diag = jnp.sum(q * k, axis=-1, keepdims=True)
    qts, kts = [], []
    fine = 0
    for h in halves:
        if h >= 8:
            none = jnp.zeros((h, HG_DK), F32)
            qh, kh = [], []
            for r in range(0, C, 2 * h):
                b_mid = b[r + h - 1:r + h, :]
                kh += [k[r:r + h] * jnp.exp(b_mid - b[r:r + h]), none]
                qh += [none, q[r + h:r + 2 * h] * jnp.exp(b[r + h:r + 2 * h] - b_mid)]
            qts.append(jnp.concatenate(qh, axis=0).astype(BF16))
            kts.append(jnp.concatenate(kh, axis=0).astype(BF16))
        else:
            fine += 1
            e = jnp.exp(x[fine * C:(fine + 1) * C])
            qts.append((q * e).astype(BF16))
            kts.append((k * e).astype(BF16))
    prods = []
    for p in range(len(halves) // 2):
        lhs = jnp.concatenate([qts[2 * p], qts[2 * p + 1]], axis=1)
        rhs = jnp.concatenate([jnp.concatenate([kts[2 * p], zero], axis=1),
                               jnp.concatenate([zero, kts[2 * p + 1]], axis=1)], axis=0)
        prods.append(_dot_nt(lhs, rhs))
    return prods, diag


def _hgrn_pair(q2, f2, v2, gate2, st2, sums, owner, halves):
    C, D = HG_CHUNK, HG_DK
    halfs = (slice(0, D), slice(D, 2 * D))
    k2 = 1.0 - f2
    x2 = _dot(sums, jnp.concatenate(_split3(jnp.log(f2)), axis=0))
    yield
    b2 = x2[0:C]
    b_last = b2[C - 1:C, :]
    kdec2 = (k2 * jnp.exp(b_last - b2)).astype(BF16)
    qe2 = (q2 * jnp.exp(b2)).astype(BF16)
    st2b = st2.astype(BF16)
    scores = [_hgrn_score_products(q2[:, c], k2[:, c], x2[:, c], halves) for c in halfs]
    yield
    zero = jnp.zeros((2 * C, D), BF16)
    os, vvts, kds = [], [], []
    for j, c in enumerate(halfs):
        prods, diag = scores[j]
        a = jnp.where(owner == len(prods), diag, 0.0)
        for p, prod in enumerate(prods):
            a = jnp.where(owner == p, prod, a)
        v = v2[:, c]
        vvt = jnp.concatenate([v, v], axis=0).T
        os.append(_dot_nt(jnp.concatenate([qe2[:, c], a.astype(BF16)], axis=1),
                          jnp.concatenate([st2b[:, c], vvt], axis=1)))
        vvts.append(vvt)
        kd = jnp.concatenate([kdec2[:, c], jnp.zeros((C, D), BF16)], axis=0)
        kds.append(jnp.concatenate([kd, zero] if j == 0 else [zero, kd], axis=1))
    st_new = st2 * jnp.exp(b_last) + _dot(jnp.concatenate(vvts, axis=1), jnp.concatenate(kds, axis=0))
    yield
    ons = [o * lax.rsqrt(jnp.mean(o * o, axis=-1, keepdims=True) + EPS) for o in os]
    on2 = jnp.concatenate(ons, axis=1) * gate2
    return on2.astype(BF16), st_new


def _interleave(gens):
    out = [None] * len(gens)
    live = list(range(len(gens)))
    while live:
        for g in list(live):
            try:
                next(gens[g])
            except StopIteration as stop:
                out[g] = stop.value
                live.remove(g)
    return out


def _hgrn_kernel(q_ref, f_ref, v_ref, go_ref, fm_ref, vm_ref, gn_ref, sums_ref, owner_ref,
                 *rest, n_chunks, n_meta, hb, halves, jobs):
    cast_in, o_ref = rest[:len(jobs)], rest[len(jobs)]
    cast_out, st_ref = rest[len(jobs) + 1:-1], rest[-1]
    _run_cast_jobs(cast_in, cast_out, jobs)
    i = pl.program_id(0)
    p0 = pl.program_id(1) * (hb // 2)
    C = HG_CHUNK
    W = 2 * HG_DK

    @pl.when(i == 0)
    def _():
        tri_m = _tri(n_meta)
        for pair in range(hb // 2):
            sts = []
            for hh in (2 * pair, 2 * pair + 1):
                cols = slice(hh * HG_DK, (hh + 1) * HG_DK)
                fm = fm_ref[:, cols]
                bm = _dot3(tri_m, _split3(jnp.log(fm)))
                kdec = ((1.0 - fm) * jnp.exp(bm[n_meta - 1:n_meta, :] - bm)).astype(BF16)
                sts.append(_dot_tn(vm_ref[:, cols], kdec))
            st_ref[p0 + pair] = jnp.concatenate(sts, axis=1)

    def chunk(c, states):
        r0 = pl.multiple_of(c * C, C)
        owner = owner_ref[...]
        sums = sums_ref[...]
        gens = []
        for pair in range(hb // 2):
            cols = slice(pair * W, (pair + 1) * W)
            gate2 = gn_ref[:, cols] * go_ref[pl.ds(r0, C), cols].astype(F32)
            gens.append(_hgrn_pair(q_ref[pl.ds(r0, C), cols], f_ref[pl.ds(r0, C), cols],
                                   v_ref[pl.ds(r0, C), cols], gate2, states[pair],
                                   sums, owner, halves))
        results = _interleave(gens)
        o_ref[pl.ds(r0, C), :] = jnp.concatenate([r[0] for r in results], axis=1)
        return tuple(r[1] for r in results)

    states = lax.fori_loop(0, n_chunks, chunk, tuple(st_ref[p0 + p] for p in range(hb // 2)))
    for pair in range(hb // 2):
        st_ref[p0 + pair] = states[pair]


def _hgrn_grid(n, width, rb=512, hb=12):
    heads = width // HG_DK
    hb = min(hb, heads)
    assert hb % 2 == 0 and heads % hb == 0
    return rb, hb, (n // rb, heads // hb)


def _hgrn(q, f, v, go, fm, vm, hg_norm, cast=()):
    n, width = q[0].shape[0], hg_norm.shape[-1]
    heads = width // HG_DK
    rb, hb, grid = _hgrn_grid(n, width)
    bw = hb * HG_DK
    n_meta = fm.shape[0]
    job_in, job_out, job_shapes, jobs = _cast_jobs(cast, grid)
    sums, owner, halves = _hgrn_tables(HG_CHUNK)
    sums3 = jnp.asarray(np.concatenate([sums] * 3, axis=1), BF16)
    blk = lambda i, h: (i, h)
    mblk = lambda i, h: (0, h)
    const = lambda i, h: (0, 0)

    def view(col0):
        assert col0 % bw == 0
        return pl.BlockSpec((rb, bw), lambda i, h: (i, col0 // bw + h))

    return pl.pallas_call(
        functools.partial(_hgrn_kernel, n_chunks=rb // HG_CHUNK, n_meta=n_meta, hb=hb,
                          halves=halves, jobs=jobs),
        grid=grid,
        in_specs=[view(q[1]), view(f[1]), view(v[1]), view(go[1])]
        + [pl.BlockSpec((n_meta, hb * HG_DK), mblk)] * 2
        + [pl.BlockSpec((1, hb * HG_DK), mblk),
           pl.BlockSpec(sums3.shape, const), pl.BlockSpec(owner.shape, const)] + job_in,
        out_specs=[pl.BlockSpec((rb, hb * HG_DK), blk)] + job_out,
        out_shape=[jax.ShapeDtypeStruct((n, width), BF16)] + job_shapes,
        scratch_shapes=[pltpu.VMEM((heads // 2, HG_DK, 2 * HG_DK), F32)],
        compiler_params=_cparams(("arbitrary", "arbitrary")),
        name="hgrn2",
    )(q[0], f[0], v[0], go[0], fm, vm, hg_norm.reshape(1, width), sums3, jnp.asarray(owner),
      *[src for src, _, _ in cast])


def _s5_weights_kernel(lr_ref, li_ref, dt_ref, bre_ref, bim_ref, cre_ref, cim_ref, dsk_ref,
                       kt_ref, win_ref, wout_ref, at_ref):
    T = S5_T
    ns = lr_ref.shape[1]
    nc = cre_ref.shape[0]
    lr, li, dt = lr_ref[...], li_ref[...], dt_ref[...]

    def powers(tau):
        mag = jnp.exp(lr * dt * tau)
        ang = li * dt * tau
        return mag * jnp.cos(ang), mag * jnp.sin(ang)

    a_re, a_im = powers(1.0)
    num_re, num_im = a_re - 1.0, a_im
    den = lr * lr + li * li
    coef_re = (num_re * lr + num_im * li) / den
    coef_im = (num_im * lr - num_re * li) / den
    br, bi = bre_ref[...], bim_ref[...]
    bbar_re = coef_re * br - coef_im * bi
    bbar_im = coef_re * bi + coef_im * br

    pw = [powers(float(tau)) for tau in range(T + 1)]
    for s in range(T):
        p_re, p_im = pw[T - 1 - s]
        win_ref[s * nc:(s + 1) * nc, 0:ns] = (bbar_re * p_re - bbar_im * p_im).astype(BF16)
        win_ref[s * nc:(s + 1) * nc, ns:2 * ns] = (bbar_re * p_im + bbar_im * p_re).astype(BF16)
    at_ref[:, 0:ns] = pw[T][0]
    at_ref[:, ns:2 * ns] = pw[T][1]

    cr, ci = cre_ref[...], cim_ref[...]
    hp = lax.Precision.HIGHEST
    nt = (((1,), (1,)), ((), ()))
    for tau in range(T + 1):
        p_re, p_im = pw[tau]
        ca_re = cr * p_re - ci * p_im
        ca_im = cr * p_im + ci * p_re
        if tau >= 1:
            t = tau - 1
            wout_ref[0:ns, t * nc:(t + 1) * nc] = ca_re.T.astype(BF16)
            wout_ref[ns:2 * ns, t * nc:(t + 1) * nc] = (-ca_im).T.astype(BF16)
        if tau < T:
            kt = (lax.dot_general(bbar_re, ca_re, nt, precision=hp, preferred_element_type=F32)
                  - lax.dot_general(bbar_im, ca_im, nt, precision=hp, preferred_element_type=F32))
            if tau == 0:
                kt = kt + dsk_ref[...]
            kt_ref[tau] = kt.astype(BF16)


def _s5_weights(lam_re, lam_im, log_step, b_re, b_im, c_re, c_im, d_skip):
    G, N = lam_re.shape
    P = d_skip.shape[1]
    nsg = G // S5_SG
    ns, nc = S5_SG * N, S5_SG * P
    T = S5_T
    dt = jnp.exp(log_step.astype(F32))
    eye = jnp.eye(S5_SG, dtype=F32)

    def rows(a):
        return a.astype(F32).reshape(nsg, 1, ns)

    def embed_b(b):
        b = b.astype(F32).reshape(nsg, S5_SG, N, P)
        return jnp.einsum('zgnq,gh->zhqgn', b, eye).reshape(nsg, nc, ns)

    def embed_c(c):
        c = c.astype(F32).reshape(nsg, S5_SG, P, N)
        return jnp.einsum('zgpn,gh->zhpgn', c, eye).reshape(nsg, nc, ns)

    dtb = jnp.broadcast_to(dt[:, None], (G, N))
    d_diag = jax.vmap(jnp.diag)(d_skip.astype(F32).reshape(nsg, nc))
    sq = pl.Squeezed()
    spec_r = pl.BlockSpec((sq, 1, ns), lambda z: (z, 0, 0))
    spec_b = pl.BlockSpec((sq, nc, ns), lambda z: (z, 0, 0))
    return pl.pallas_call(
        _s5_weights_kernel,
        grid=(nsg,),
        in_specs=[spec_r, spec_r, spec_r, spec_b, spec_b, spec_b, spec_b,
                  pl.BlockSpec((sq, nc, nc), lambda z: (z, 0, 0))],
        out_specs=[pl.BlockSpec((sq, T, nc, nc), lambda z: (z, 0, 0, 0)),
                   pl.BlockSpec((sq, T * nc, 2 * ns), lambda z: (z, 0, 0)),
                   pl.BlockSpec((sq, 2 * ns, T * nc), lambda z: (z, 0, 0)),
                   pl.BlockSpec((sq, 1, 2 * ns), lambda z: (z, 0, 0))],
        out_shape=[jax.ShapeDtypeStruct((nsg, T, nc, nc), BF16),
                   jax.ShapeDtypeStruct((nsg, T * nc, 2 * ns), BF16),
                   jax.ShapeDtypeStruct((nsg, 2 * ns, T * nc), BF16),
                   jax.ShapeDtypeStruct((nsg, 1, 2 * ns), F32)],
        compiler_params=_cparams(("arbitrary",)),
        name="s5_weights",
    )(rows(lam_re), rows(lam_im), rows(dtb),
      embed_b(b_re), embed_b(b_im), embed_c(c_re), embed_c(c_im), d_diag)


def _s5_scan_kernel(u_ref, um_ref, kt_ref, win_ref, wout_ref, at_ref, y_ref,
                    uc_ref, v_ref, xp_ref, m_ref, *, n_chunks):
    T = S5_T
    nc = u_ref.shape[1]
    ns = at_ref.shape[1] // 2

    @pl.when(pl.program_id(0) == 0)
    def _():
        zero = jnp.zeros((nc, nc), BF16)
        for s in range(1, T):
            for t in range(s):
                m_ref[s * nc:(s + 1) * nc, t * nc:(t + 1) * nc] = zero

    for tau in range(T):
        tap = kt_ref[tau]
        for s in range(T - tau):
            m_ref[s * nc:(s + 1) * nc, (s + tau) * nc:(s + tau + 1) * nc] = tap

    for t in range(T):
        uc_ref[0:n_chunks, t * nc:(t + 1) * nc] = u_ref[pl.ds(t, n_chunks, stride=T), :].astype(BF16)
        uc_ref[n_chunks:n_chunks + 1, t * nc:(t + 1) * nc] = um_ref[t:t + 1, :].astype(BF16)
    pad = uc_ref.shape[0] - n_chunks - 1
    uc_ref[n_chunks + 1:, :] = jnp.zeros((pad, T * nc), BF16)
    uc = uc_ref[...]
    v_ref[...] = _dot(uc, win_ref[...])
    a_re, a_im = at_ref[:, 0:ns], at_ref[:, ns:2 * ns]

    x_re0 = v_ref[n_chunks:n_chunks + 1, 0:ns]
    x_im0 = v_ref[n_chunks:n_chunks + 1, ns:2 * ns]
    xp_ref[n_chunks:, :] = jnp.zeros((pad + 1, 2 * ns), F32)

    def step(c, carry):
        x_re, x_im = carry
        xp_ref[pl.ds(c, 1), 0:ns] = x_re
        xp_ref[pl.ds(c, 1), ns:2 * ns] = x_im
        v_re = v_ref[pl.ds(c, 1), 0:ns]
        v_im = v_ref[pl.ds(c, 1), ns:2 * ns]
        return (a_re * x_re - a_im * x_im + v_re, a_re * x_im + a_im * x_re + v_im)

    lax.fori_loop(0, n_chunks, step, (x_re0, x_im0))
    yc = _dot(uc, m_ref[...]) + _dot(xp_ref[...].astype(BF16), wout_ref[...])
    for t in range(T):
        y_ref[pl.ds(t, n_chunks, stride=T), :] = yc[0:n_chunks, t * nc:(t + 1) * nc]


def _s5_scan(u, um, kt, win, wout, at):
    u, col0 = u
    n, width = u.shape[0], um.shape[1]
    T = S5_T
    nsg = kt.shape[0]
    nc = width // nsg
    ns2 = at.shape[2]
    assert um.shape[0] == T and n % T == 0 and col0 % nc == 0
    n_chunks = n // T
    rows = -(-(n_chunks + 1) // 16) * 16
    sq = pl.Squeezed()
    return pl.pallas_call(
        functools.partial(_s5_scan_kernel, n_chunks=n_chunks),
        grid=(nsg,),
        in_specs=[pl.BlockSpec((n, nc), lambda z: (0, col0 // nc + z)),
                  pl.BlockSpec((T, nc), lambda z: (0, z)),
                  pl.BlockSpec((sq, T, nc, nc), lambda z: (z, 0, 0, 0)),
                  pl.BlockSpec((sq, T * nc, ns2), lambda z: (z, 0, 0)),
                  pl.BlockSpec((sq, ns2, T * nc), lambda z: (z, 0, 0)),
                  pl.BlockSpec((sq, 1, ns2), lambda z: (z, 0, 0))],
        out_specs=pl.BlockSpec((n, nc), lambda z: (0, z)),
        out_shape=jax.ShapeDtypeStruct((n, width), F32),
        scratch_shapes=[pltpu.VMEM((rows, T * nc), BF16), pltpu.VMEM((rows, ns2), F32),
                        pltpu.VMEM((rows, ns2), F32), pltpu.VMEM((T * nc, T * nc), BF16)],
        compiler_params=_cparams(("arbitrary",)),
        name="s5_scan",
    )(u, um, kt, win, wout, at)


def _merge_kernel(*refs, pieces):
    h_ref, o_ref, y_ref = refs[0:3]
    gh_refs, gs_refs = refs[3:3 + pieces], refs[3 + pieces:3 + 2 * pieces]
    wp_ref, wa_ref, wb_ref, wo_ref, out_ref = refs[3 + 2 * pieces:]
    gh = jnp.concatenate([r[...] for r in gh_refs], axis=1).astype(F32)
    gs = jnp.concatenate([r[...] for r in gs_refs], axis=1).astype(F32)
    y_hg = _dot(o_ref[...], wp_ref[...])
    z = jax.nn.gelu(y_ref[...]).astype(BF16)
    y_s5 = _dot(z, wa_ref[...]) * jax.nn.sigmoid(_dot(z, wb_ref[...]))
    merged = gh * y_hg + gs * y_s5
    out_ref[...] = h_ref[...] + _dot(merged.astype(BF16), wo_ref[...])


def _merge(hx, o, y, gh, gs, wp, wa, wb, wo, *, tm=256):
    n, d = hx.shape
    row = lambda i: (i, 0)
    const = lambda i: (0, 0)
    gw = functools.reduce(math.gcd, (gh[1], gs[1], d))
    assert gw % LANE == 0
    pieces = d // gw

    def resident(shape):
        return pl.BlockSpec(shape, const, pipeline_mode=pl.Buffered(1))

    def view(col0):
        return [pl.BlockSpec((tm, gw), functools.partial(lambda i, k: (i, k), k=col0 // gw + k))
                for k in range(pieces)]

    return pl.pallas_call(
        functools.partial(_merge_kernel, pieces=pieces),
        grid=(n // tm,),
        in_specs=[pl.BlockSpec((tm, d), row), pl.BlockSpec((tm, o.shape[1]), row),
                  pl.BlockSpec((tm, y.shape[1]), row)] + view(gh[1]) + view(gs[1])
        + [resident(wp.shape), resident(wa.shape), resident(wb.shape), resident(wo.shape)],
        out_specs=pl.BlockSpec((tm, d), row),
        out_shape=jax.ShapeDtypeStruct((n, d), F32),
        compiler_params=_cparams(("arbitrary",)),
        name="mixer_merge",
    )(hx, o, y, *([gh[0]] * pieces), *([gs[0]] * pieces), wp, wa, wb, wo)


def kernel(x, meta_tokens, norm_ffn1, ffn1_w_gate, ffn1_w_up, ffn1_w_down, norm_mix, w_in, hg_lower_bounds, hg_norm, hg_w_proj, s5_lam_re, s5_lam_im, s5_log_step, s5_b_re, s5_b_im, s5_c_re, s5_c_im, s5_d, s5_w_glu_a, s5_w_glu_b, w_out, norm_ffn2, ffn2_w_gate, ffn2_w_up, ffn2_w_down, norm_final):
    batch, seq, d = x.shape
    depth = norm_ffn1.shape[0]
    assert batch == 1 and depth == 1
    hg_k = hg_lower_bounds.shape[1]
    hg_w = hg_norm.shape[1]
    s5_w = s5_w_glu_a.shape[1]
    hx = x[0].astype(F32)
    hm = meta_tokens.astype(F32)
    tm = min(512, seq)
    h0, n0, hm, nm, w1_gate, w1_up, w1_down = _ffn(
        hx, hm, norm_ffn1[0], norm_mix[0], ffn1_w_gate[0], ffn1_w_up[0], ffn1_w_down[0],
        emit_norm=True, rows=tm, tm=tm, tf=256, name="ffn_head")
    sizes = (hg_k, hg_k, hg_w, hg_w, s5_w, d, d)
    assert sum(sizes) == w_in.shape[2]
    off = [0]
    for s in sizes:
        off.append(off[-1] + s)
    ids32, ids16 = (0, 1, 4), (2, 3, 5, 6)
    tc_in = functools.reduce(math.gcd, sizes + (512,))

    def starts(ids):
        out, col = {}, 0
        for k in ids:
            out[k] = col
            col += sizes[k]
        return out

    def blocks(ids):
        return [c for k in ids for c in range(off[k] // tc_in, off[k + 1] // tc_in)]

    steps = lambda grid: grid[0] * grid[1]
    n_body = (seq // tm) * (ffn1_w_gate.shape[2] // 512)
    n_inp = steps(_inproj_grid(seq, sum(sizes[k] for k in ids16))[2])
    n_hg = steps(_hgrn_grid(seq, hg_w)[2])
    hx, nx, w32, w16 = _ffn(
        hx, None, norm_ffn1[0], norm_mix[0], w1_gate, w1_up, w1_down, emit_norm=True, head=(h0, n0),
        cast=(_cast_tile(w_in[0], n_body, tc_in, [blocks(ids32), blocks(ids16)]),),
        tm=tm, name="ffn_body")

    silu, ident, sigm = (1.0, 0.0, 0.0, 0.0), (0.0, 1.0, 0.0, 0.0), (0.0, 0.0, 1.0, 0.0)
    rows = {0: (HG_DK ** -0.5, 0.0, 0.0, 0.0), 1: (0.0, 0.0, 0.0, 1.0), 2: ident, 3: silu,
            4: ident, 5: sigm, 6: sigm}

    def coef(ids):
        return jnp.concatenate([jnp.broadcast_to(jnp.asarray(rows[k], F32)[:, None], (4, sizes[k]))
                                for k in ids], axis=1)

    s32, s16 = starts(ids32), starts(ids16)
    x32, m32 = _inproj(nx, nm, w32, coef(ids32), hg_lower_bounds, s32[1], F32)
    x16, m16, w_proj, w_glu_a, w_glu_b, w_o = _inproj(
        nx, nm, w16, coef(ids16), None, 0, BF16,
        cast=[_cast_tile(w, n_inp) for w in (hg_w_proj[0], s5_w_glu_a[0], s5_w_glu_b[0], w_out[0])])

    o, w2_gate, w2_up, w2_down = _hgrn(
        (x32, s32[0]), (x32, s32[1]), (x16, s16[2]), (x16, s16[3]),
        m32[:, s32[1]:s32[1] + hg_k], m16[:, s16[2]:s16[2] + hg_w], hg_norm[0],
        cast=[_cast_tile(w, n_hg) for w in (ffn2_w_gate[0], ffn2_w_up[0], ffn2_w_down[0])])
    m, win, wout, at = _s5_weights(s5_lam_re[0], s5_lam_im[0], s5_log_step[0], s5_b_re[0],
                                   s5_b_im[0], s5_c_re[0], s5_c_im[0], s5_d[0])
    y = _s5_scan((x32, s32[4]), m32[:, s32[4]:s32[4] + s5_w], m, win, wout, at)
    hx = _merge(hx, o, y, (x16, s16[5]), (x16, s16[6]), w_proj, w_glu_a, w_glu_b, w_o)
    out, = _ffn(hx, None, norm_ffn2[0], norm_final, w2_gate, w2_up, w2_down, emit_norm=False,
                tm=tm, name="ffn_final")
    return out[None].astype(x.dtype)
```

```python
import functools
import math

import numpy as np
import jax
import jax.numpy as jnp
from jax import lax
from jax.experimental import pallas as pl
from jax.experimental.pallas import tpu as pltpu

EPS = 1e-6
HG_DK = 128
HG_CHUNK = 64
S5_GROUP = 16
S5_STATE = 64
S5_T = 16
S5_SG = 8
LANE = 128
INPROJ_SUB = 256
VMEM_LIMIT = 56 * 1024 * 1024

F32 = jnp.float32
BF16 = jnp.bfloat16


def _cparams(sem):
    return pltpu.CompilerParams(dimension_semantics=sem, vmem_limit_bytes=VMEM_LIMIT)


def _rms(v, g):
    return v * lax.rsqrt(jnp.mean(v * v, axis=-1, keepdims=True) + EPS) * g


def _dot(a, b):
    return jnp.dot(a, b, preferred_element_type=F32)


def _dot_nt(a, b):
    return lax.dot_general(a, b, (((1,), (1,)), ((), ())), preferred_element_type=F32)


def _dot_tn(a, b):
    return lax.dot_general(a, b, (((0,), (0,)), ((), ())), preferred_element_type=F32)


def _cast_jobs(cast, grid):
    nj = grid[1]
    in_specs, out_specs, out_shapes, static = [], [], [], []
    for src, (tr, tc), dests in cast:
        rows, cols = src.shape
        ncb = cols // tc
        n_tiles = (rows // tr) * ncb
        assert rows % tr == 0 and cols % tc == 0 and n_tiles <= grid[0] * nj

        def tile(i, j, n_tiles=n_tiles, ncb=ncb):
            t = jnp.minimum(i * nj + j, n_tiles - 1)
            return t // ncb, t % ncb

        in_specs.append(pl.BlockSpec((tr, tc), tile))
        for blocks in dests:
            park, k = [], 0
            for c in range(ncb):
                k = blocks.index(c) if c in blocks else k
                park.append(k)
            jumps = [(m, park[m] - park[m - 1]) for m in range(1, ncb) if park[m] != park[m - 1]]

            def dest(i, j, tile=tile, first=park[0], jumps=jumps):
                r, c = tile(i, j)
                return r, first + sum(jnp.where(c >= m, dk, 0) for m, dk in jumps)

            out_specs.append(pl.BlockSpec((tr, tc), tile if park == list(range(ncb)) else dest))
            out_shapes.append(jax.ShapeDtypeStruct((rows, len(blocks) * tc), BF16))
        static.append((n_tiles, ncb, tuple(tuple(b) for b in dests)))
    return in_specs, out_specs, out_shapes, tuple(static)


def _n_cast_out(jobs):
    return sum(len(dests) for _, _, dests in jobs)


def _run_cast_jobs(cast_in, cast_out, jobs):
    step = pl.program_id(0) * pl.num_programs(1) + pl.program_id(1)
    outs = iter(cast_out)
    for src, (n_tiles, ncb, dests) in zip(cast_in, jobs):
        dsts = [next(outs) for _ in dests]
        tile = src[...].astype(BF16)
        if len(dsts) == 1:
            dsts[0][...] = tile
        else:
            c = jnp.minimum(step, n_tiles - 1) % ncb
            for dst, cols in zip(dsts, dests):
                owns = functools.reduce(jnp.logical_or, [c == m for m in cols])

                @pl.when(owns)
                def _(dst=dst):
                    dst[...] = tile


def _cast_tile(w, n_steps, tc=None, dests=None):
    rows, cols = w.shape
    if tc is None and rows % n_steps == 0 and (rows // n_steps) % 16 == 0:
        return w, (rows // n_steps, cols), [[0]]
    tc = math.gcd(tc or 512, cols)
    tr = 128
    while (rows // tr) * (cols // tc) > n_steps:
        tr *= 2
    return w, (tr, tc), dests or [list(range(cols // tc))]


def _ffn_kernel(*refs, tm, n_meta, normed, emit_norm, w_f32, head, jobs):
    it = iter(refs)
    take = lambda k: [next(it) for _ in range(k)]
    x_ref, = take(1)
    m_ref = take(1)[0] if n_meta else None
    g_ref, gn_ref, wg_ref, wu_ref, wd_ref = take(5)
    hx_ref, hn16_ref = take(2) if head else (None, None)
    cast_in = take(len(jobs))
    ox_ref, = take(1)
    nx_ref = take(1)[0] if emit_norm else None
    om_ref, nm_ref = take(2) if n_meta else (None, None)
    w16_out = take(3) if w_f32 else ()
    cast_out = take(_n_cast_out(jobs))
    hn_ref = g_ref if normed else take(1)[0]
    acc_ref = take(1)[0] if n_meta else ox_ref
    i = pl.program_id(0)
    j = pl.program_id(1)
    _run_cast_jobs(cast_in, cast_out, jobs)

    def body():
        @pl.when(j == 0)
        def _():
            if not normed:
                hn_ref[0:tm, :] = _rms(x_ref[...], g_ref[...]).astype(BF16)
            if n_meta:
                hn_ref[tm:tm + n_meta, :] = _rms(m_ref[...], g_ref[...]).astype(BF16)
            acc_ref[...] = jnp.zeros_like(acc_ref)

        tiles = [w[...].astype(BF16) if w_f32 else w[...] for w in (wg_ref, wu_ref, wd_ref)]
        for dst, tile in zip(w16_out, tiles):
            dst[...] = tile
        hn = hn_ref[...]
        gate = _dot(hn, tiles[0])
        up = _dot(hn, tiles[1])
        act = (gate * jax.nn.sigmoid(gate) * up).astype(BF16)
        acc_ref[...] += _dot(act, tiles[2])

        @pl.when(j == pl.num_programs(1) - 1)
        def _():
            hx = x_ref[...] + 0.5 * acc_ref[0:tm, :]
            if emit_norm:
                ox_ref[...] = hx
                nx_ref[...] = _rms(hx, gn_ref[...]).astype(BF16)
            else:
                ox_ref[...] = _rms(hx, gn_ref[...])
            if n_meta:
                hm = m_ref[...] + 0.5 * acc_ref[tm:tm + n_meta, :]
                om_ref[...] = hm
                nm_ref[...] = _rms(hm, gn_ref[...]).astype(BF16)

    if head:
        @pl.when((i == 0) & (j == 0))
        def _():
            ox_ref[...] = hx_ref[...]
            nx_ref[...] = hn16_ref[...]

        pl.when(i > 0)(body)
    else:
        body()


def _ffn(hx, hm, g, g_next, wg, wu, wd, *, emit_norm, head=None, cast=(), rows=None, name,
         tm=512, tf=512):
    n, d = hx.shape
    rows = n if rows is None else rows
    dff = wg.shape[1]
    n_meta = 0 if hm is None else hm.shape[0]
    normed = g.ndim == 2
    w_f32 = wg.dtype == F32
    grid = (rows // tm, dff // tf)
    assert not w_f32 or grid[0] == 1
    assert not (normed and n_meta)
    row = lambda i, j: (i, 0)
    const = lambda i, j: (0, 0)
    in_specs = [pl.BlockSpec((tm, d), row)]
    args = [hx]
    if n_meta:
        in_specs.append(pl.BlockSpec((n_meta, d), const))
        args.append(hm)
    col = (lambda i, j: jnp.where(i == 0, 0, j)) if head else (lambda i, j: j)
    in_specs += [pl.BlockSpec((tm, d), row) if normed else pl.BlockSpec((1, d), const),
                 pl.BlockSpec((1, d), const),
                 pl.BlockSpec((d, tf), lambda i, j: (0, col(i, j))),
                 pl.BlockSpec((d, tf), lambda i, j: (0, col(i, j))),
                 pl.BlockSpec((tf, d), lambda i, j: (col(i, j), 0))]
    args += [g if normed else g.reshape(1, d), g_next.reshape(1, d), wg, wu, wd]
    if head:
        in_specs += [pl.BlockSpec((tm, d), const, pipeline_mode=pl.Buffered(1))] * 2
        args += list(head)
    job_in, job_out, job_shapes, jobs = _cast_jobs(cast, grid)
    in_specs += job_in
    args += [src for src, _, _ in cast]
    out_shape = [jax.ShapeDtypeStruct((rows, d), F32)]
    out_specs = [pl.BlockSpec((tm, d), row)]
    if emit_norm:
        out_shape.append(jax.ShapeDtypeStruct((rows, d), BF16))
        out_specs.append(pl.BlockSpec((tm, d), row))
    if n_meta:
        out_shape += [jax.ShapeDtypeStruct((n_meta, d), F32), jax.ShapeDtypeStruct((n_meta, d), BF16)]
        out_specs += [pl.BlockSpec((n_meta, d), const)] * 2
    if w_f32:
        out_shape += [jax.ShapeDtypeStruct(w.shape, BF16) for w in (wg, wu, wd)]
        out_specs += [pl.BlockSpec((d, tf), lambda i, j: (0, j)),
                      pl.BlockSpec((d, tf), lambda i, j: (0, j)),
                      pl.BlockSpec((tf, d), lambda i, j: (j, 0))]
    out_specs += job_out
    out_shape += job_shapes
    return pl.pallas_call(
        functools.partial(_ffn_kernel, tm=tm, n_meta=n_meta, normed=normed, emit_norm=emit_norm,
                          w_f32=w_f32, head=head is not None, jobs=jobs),
        grid=grid, in_specs=in_specs, out_specs=out_specs, out_shape=out_shape,
        scratch_shapes=([] if normed else [pltpu.VMEM((tm + n_meta, d), BF16)])
        + ([pltpu.VMEM((tm + n_meta, d), F32)] if n_meta else []),
        compiler_params=_cparams(("arbitrary", "arbitrary")),
        name=name,
    )(*args)


def _inproj_kernel(*refs, tm, n_meta, with_lb, jobs):
    it = iter(refs)
    take = lambda k: [next(it) for _ in range(k)]
    x_ref, m_ref, w_ref, coef_ref = take(4)
    lb_ref = take(1)[0] if with_lb else None
    cast_in = take(len(jobs))
    ox_ref, om_ref = take(2)
    cast_out = take(_n_cast_out(jobs))
    lhs_ref, = take(1)
    _run_cast_jobs(cast_in, cast_out, jobs)

    @pl.when(pl.program_id(1) == 0)
    def _():
        lhs_ref[0:tm, :] = x_ref[...]
        lhs_ref[tm:tm + n_meta, :] = m_ref[...]

    tn = w_ref.shape[1]
    sw = math.gcd(tn, INPROJ_SUB)
    for s in range(tn // sw):
        cs = slice(s * sw, (s + 1) * sw)
        res = _dot(lhs_ref[...], w_ref[:, cs])
        sig = jax.nn.sigmoid(res)
        coef = coef_ref[:, cs]
        a, c, b, d = coef[0:1, :], coef[1:2, :], coef[2:3, :], 0.0
        if with_lb:
            lbl = lb_ref[:, cs]
            e = jnp.exp(lbl - jnp.max(lbl, axis=0, keepdims=True))
            lb = e[0:1, :] / jnp.sum(e, axis=0, keepdims=True)
            flag = coef[3:4, :]
            b = b + flag * (1.0 - lb)
            d = flag * lb
        out = res * (a * sig + c) + (b * sig + d)
        ox_ref[:, cs] = out[0:tm, :].astype(ox_ref.dtype)
        om_ref[:, cs] = out[tm:tm + n_meta, :].astype(om_ref.dtype)


def _inproj_grid(n, cols, tm=1024, tn=1792):
    tm = min(tm, n)
    tn = math.gcd(tn, cols)
    assert tn % LANE == 0 and n % tm == 0
    return tm, tn, (n // tm, cols // tn)


def _inproj(nx, nm, w, coef, lower_bounds, lb_cols, dtype, cast=()):
    n, d = nx.shape
    n_meta = nm.shape[0]
    cols = w.shape[1]
    with_lb = lower_bounds is not None
    tm, tn, grid = _inproj_grid(n, cols)
    job_in, job_out, job_shapes, jobs = _cast_jobs(cast, grid)
    in_specs = [pl.BlockSpec((tm, d), lambda i, j: (i, 0)),
                pl.BlockSpec((n_meta, d), lambda i, j: (0, 0)),
                pl.BlockSpec((d, tn), lambda i, j: (0, j)),
                pl.BlockSpec((4, tn), lambda i, j: (0, j))]
    args = [nx, nm, w, coef]
    if with_lb:
        nlb, k = lower_bounds.shape
        lb_full = jnp.pad(lower_bounds.astype(F32), ((0, 0), (lb_cols, cols - lb_cols - k)))
        in_specs.append(pl.BlockSpec((nlb, tn), lambda i, j: (0, j)))
        args.append(lb_full)
    ox, om, *copies = pl.pallas_call(
        functools.partial(_inproj_kernel, tm=tm, n_meta=n_meta, with_lb=with_lb, jobs=jobs),
        grid=grid,
        in_specs=in_specs + job_in,
        out_specs=[pl.BlockSpec((tm, tn), lambda i, j: (i, j)),
                   pl.BlockSpec((pl.Squeezed(), n_meta, tn), lambda i, j: (i, 0, j))] + job_out,
        out_shape=[jax.ShapeDtypeStruct((n, cols), dtype),
                   jax.ShapeDtypeStruct((n // tm, n_meta, cols), dtype)] + job_shapes,
        scratch_shapes=[pltpu.VMEM((tm + n_meta, d), BF16)],
        compiler_params=_cparams(("arbitrary", "arbitrary")),
        name="mixer_inproj_" + jnp.dtype(dtype).name,
    )(*args, *[src for src, _, _ in cast])
    return (ox, om[0], *copies)


def _split3(x):
    hi = x.astype(BF16)
    r1 = x - hi.astype(F32)
    mid = r1.astype(BF16)
    lo = (r1 - mid.astype(F32)).astype(BF16)
    return hi, mid, lo


def _dot3(m01, parts):
    return _dot(m01, parts[0]) + _dot(m01, parts[1]) + _dot(m01, parts[2])


def _tri(n, rep=1):
    r = lax.broadcasted_iota(jnp.int32, (n * rep, n), 0)
    c = lax.broadcasted_iota(jnp.int32, (n * rep, n), 1)
    return jnp.where(r >= c * rep, 1.0, 0.0).astype(BF16)


def _hgrn_tables(c):
    sums = [np.tril(np.ones((c, c)))]
    level = np.full((c, c), -1, np.int32)
    t_idx, s_idx = np.meshgrid(np.arange(c), np.arange(c), indexing="ij")
    halves = []
    h = c // 2
    while h >= 1:
        if h < 8:
            m = np.zeros((c, c))
            for r in range(c):
                mid = (r // (2 * h)) * 2 * h + h
                if r < mid:
                    m[r, r + 1:mid] = 1.0
                else:
                    m[r, mid:r + 1] = 1.0
            sums.append(m)
        pair = 2 * h
        level[(t_idx // pair == s_idx // pair) & (s_idx % pair < h) & (t_idx % pair >= h)] = len(halves)
        halves.append(h)
        h //= 2
    assert len(halves) % 2 == 0
    level[np.arange(c), np.arange(c)] = len(halves)
    owner = np.full((c, 2 * c), -1, np.int32)
    owner[:, :c] = np.where((level >= 0) & (level % 2 == 0), level // 2, -1)
    owner[:, c:] = np.where(level % 2 == 1, level // 2, -1)
    return np.concatenate(sums, axis=0), owner, tuple(halves)


def _hgrn_score_products(q, k, x, halves):
    C = HG_CHUNK
    b = x[0:C]
    zero = jnp.zeros((C, HG_DK), BF16)
    diag = jnp.sum(q * k, axis=-1, keepdims=True)
    qts, kts = [], []
    fine = 0
    for h in halves:
        if h >= 8:
            none = jnp.zeros((h, HG_DK), F32)
            qh, kh = [], []
            for r in range(0, C, 2 * h):
                b_mid = b[r + h - 1:r + h, :]
                kh += [k[r:r + h] * jnp.exp(b_mid - b[r:r + h]), none]
                qh += [none, q[r + h:r + 2 * h] * jnp.exp(b[r + h:r + 2 * h] - b_mid)]
            qts.append(jnp.concatenate(qh, axis=0).astype(BF16))
            kts.append(jnp.concatenate(kh, axis=0).astype(BF16))
        else:
            fine += 1
            e = jnp.exp(x[fine * C:(fine + 1) * C])
            qts.append((q * e).astype(BF16))
            kts.append((k * e).astype(BF16))
    prods = []
    for p in range(len(halves) // 2):
        lhs = jnp.concatenate([qts[2 * p], qts[2 * p + 1]], axis=1)
        rhs = jnp.concatenate([jnp.concatenate([kts[2 * p], zero], axis=1),
                               jnp.concatenate([zero, kts[2 * p + 1]], axis=1)], axis=0)
        prods.append(_dot_nt(lhs, rhs))
    return prods, diag


def _hgrn_pair(q2, f2, v2, gate2, st2, sums, owns, halves):
    C, D = HG_CHUNK, HG_DK
    halfs = (slice(0, D), slice(D, 2 * D))
    k2 = 1.0 - f2
    x2 = _dot(sums, jnp.concatenate(_split3(jnp.log(f2)), axis=0))
    yield
    b2 = x2[0:C]
    b_last = b2[C - 1:C, :]
    kdec2 = (k2 * jnp.exp(b_last - b2)).astype(BF16)
    qe2 = (q2 * jnp.exp(b2)).astype(BF16)
    st2b = st2.astype(BF16)
    scores = [_hgrn_score_products(q2[:, c], k2[:, c], x2[:, c], halves) for c in halfs]
    yield
    zero = jnp.zeros((2 * C, D), BF16)
    os, vvts, kds = [], [], []
    for j, c in enumerate(halfs):
        prods, diag = scores[j]
        a = jnp.where(owns[len(prods)], diag, 0.0)
        for p, prod in enumerate(prods):
            a = jnp.where(owns[p], prod, a)
        v = v2[:, c]
        vvt = jnp.concatenate([v, v], axis=0).T
        os.append(_dot_nt(jnp.concatenate([qe2[:, c], a.astype(BF16)], axis=1),
                          jnp.concatenate([st2b[:, c], vvt], axis=1)))
        vvts.append(vvt)
        kd = jnp.concatenate([kdec2[:, c], jnp.zeros((C, D), BF16)], axis=0)
        kds.append(jnp.concatenate([kd, zero] if j == 0 else [zero, kd], axis=1))
    st_new = st2 * jnp.exp(b_last) + _dot(jnp.concatenate(vvts, axis=1), jnp.concatenate(kds, axis=0))
    yield
    ons = [o * lax.rsqrt(jnp.mean(o * o, axis=-1, keepdims=True) + EPS) for o in os]
    on2 = jnp.concatenate(ons, axis=1) * gate2
    return on2.astype(BF16), st_new


def _interleave(gens):
    out = [None] * len(gens)
    live = list(range(len(gens)))
    while live:
        for g in list(live):
            try:
                next(gens[g])
            except StopIteration as stop:
                out[g] = stop.value
                live.remove(g)
    return out


def _hgrn_kernel(q_ref, f_ref, v_ref, go_ref, fm_ref, vm_ref, sums_ref, owner_ref,
                 *rest, n_chunks, n_meta, hb, halves, jobs):
    cast_in, o_ref = rest[:len(jobs)], rest[len(jobs)]
    cast_out, st_ref = rest[len(jobs) + 1:-1], rest[-1]
    _run_cast_jobs(cast_in, cast_out, jobs)
    i = pl.program_id(0)
    p0 = pl.program_id(1) * (hb // 2)
    C = HG_CHUNK
    W = 2 * HG_DK

    @pl.when(i == 0)
    def _():
        tri_m = _tri(n_meta)
        for pair in range(hb // 2):
            sts = []
            for hh in (2 * pair, 2 * pair + 1):
                cols = slice(hh * HG_DK, (hh + 1) * HG_DK)
                fm = fm_ref[:, cols]
                bm = _dot3(tri_m, _split3(jnp.log(fm)))
                kdec = ((1.0 - fm) * jnp.exp(bm[n_meta - 1:n_meta, :] - bm)).astype(BF16)
                sts.append(_dot_tn(vm_ref[:, cols], kdec))
            st_ref[p0 + pair] = jnp.concatenate(sts, axis=1)

    def chunk(c, states):
        r0 = pl.multiple_of(c * C, C)
        owner = owner_ref[...]
        owns = [owner == p for p in range(len(halves) // 2 + 1)]
        sums = sums_ref[...]
        gens = []
        for pair in range(hb // 2):
            cols = slice(pair * W, (pair + 1) * W)
            gate2 = go_ref[pl.ds(r0, C), cols].astype(F32)
            gens.append(_hgrn_pair(q_ref[pl.ds(r0, C), cols], f_ref[pl.ds(r0, C), cols],
                                   v_ref[pl.ds(r0, C), cols], gate2, states[pair],
                                   sums, owns, halves))
        results = _interleave(gens)
        o_ref[pl.ds(r0, C), :] = jnp.concatenate([r[0] for r in results], axis=1)
        return tuple(r[1] for r in results)

    states = lax.fori_loop(0, n_chunks, chunk, tuple(st_ref[p0 + p] for p in range(hb // 2)))
    for pair in range(hb // 2):
        st_ref[p0 + pair] = states[pair]


def _hgrn_grid(n, width, rb=512, hb=12):
    heads = width // HG_DK
    hb = min(hb, heads)
    assert hb % 2 == 0 and heads % hb == 0
    return rb, hb, (n // rb, heads // hb)


def _hgrn(q, f, v, go, fm, vm, cast=()):
    n, width = q[0].shape[0], fm.shape[1]
    heads = width // HG_DK
    rb, hb, grid = _hgrn_grid(n, width)
    bw = hb * HG_DK
    n_meta = fm.shape[0]
    job_in, job_out, job_shapes, jobs = _cast_jobs(cast, grid)
    sums, owner, halves = _hgrn_tables(HG_CHUNK)
    sums3 = jnp.asarray(np.concatenate([sums] * 3, axis=1), BF16)
    blk = lambda i, h: (i, h)
    mblk = lambda i, h: (0, h)
    const = lambda i, h: (0, 0)

    def view(col0):
        assert col0 % bw == 0
        return pl.BlockSpec((rb, bw), lambda i, h: (i, col0 // bw + h))

    return pl.pallas_call(
        functools.partial(_hgrn_kernel, n_chunks=rb // HG_CHUNK, n_meta=n_meta, hb=hb,
                          halves=halves, jobs=jobs),
        grid=grid,
        in_specs=[view(q[1]), view(f[1]), view(v[1]), view(go[1])]
        + [pl.BlockSpec((n_meta, hb * HG_DK), mblk)] * 2
        + [pl.BlockSpec(sums3.shape, const), pl.BlockSpec(owner.shape, const)] + job_in,
        out_specs=[pl.BlockSpec((rb, hb * HG_DK), blk)] + job_out,
        out_shape=[jax.ShapeDtypeStruct((n, width), BF16)] + job_shapes,
        scratch_shapes=[pltpu.VMEM((heads // 2, HG_DK, 2 * HG_DK), F32)],
        compiler_params=_cparams(("arbitrary", "arbitrary")),
        name="hgrn2",
    )(q[0], f[0], v[0], go[0], fm, vm, sums3, jnp.asarray(owner), *[src for src, _, _ in cast])


def _s5_weights_kernel(lr_ref, li_ref, dt_ref, bre_ref, bim_ref, cre_ref, cim_ref, dsk_ref,
                       kt_ref, win_ref, wout_ref, at_ref):
    T = S5_T
    ns = lr_ref.shape[1]
    nc = cre_ref.shape[0]
    lr, li, dt = lr_ref[...], li_ref[...], dt_ref[...]

    def powers(tau):
        mag = jnp.exp(lr * dt * tau)
        ang = li * dt * tau
        return mag * jnp.cos(ang), mag * jnp.sin(ang)

    a_re, a_im = powers(1.0)
    num_re, num_im = a_re - 1.0, a_im
    den = lr * lr + li * li
    coef_re = (num_re * lr + num_im * li) / den
    coef_im = (num_im * lr - num_re * li) / den
    br, bi = bre_ref[...], bim_ref[...]
    bbar_re = coef_re * br - coef_im * bi
    bbar_im = coef_re * bi + coef_im * br

    pw = [powers(float(tau)) for tau in range(T + 1)]
    for s in range(T):
        p_re, p_im = pw[T - 1 - s]
        win_ref[s * nc:(s + 1) * nc, 0:ns] = (bbar_re * p_re - bbar_im * p_im).astype(BF16)
        win_ref[s * nc:(s + 1) * nc, ns:2 * ns] = (bbar_re * p_im + bbar_im * p_re).astype(BF16)
    at_ref[:, 0:ns] = pw[T][0]
    at_ref[:, ns:2 * ns] = pw[T][1]

    cr, ci = cre_ref[...], cim_ref[...]
    def split2(a):
        hi = a.astype(BF16)
        return hi, (a - hi.astype(F32)).astype(BF16)

    bb_hi, bb_lo = split2(jnp.concatenate([bbar_re, -bbar_im], axis=1))
    for tau in range(T + 1):
        p_re, p_im = pw[tau]
        ca_re = cr * p_re - ci * p_im
        ca_im = cr * p_im + ci * p_re
        if tau >= 1:
            t = tau - 1
            wout_ref[0:ns, t * nc:(t + 1) * nc] = ca_re.T.astype(BF16)
            wout_ref[ns:2 * ns, t * nc:(t + 1) * nc] = (-ca_im).T.astype(BF16)
        if tau < T:
            ca_hi, ca_lo = split2(jnp.concatenate([ca_re, ca_im], axis=1))
            kt = _dot_nt(bb_hi, ca_hi) + _dot_nt(bb_hi, ca_lo) + _dot_nt(bb_lo, ca_hi)
            if tau == 0:
                kt = kt + dsk_ref[...]
            kt_ref[tau] = kt.astype(BF16)


def _s5_weights(lam_re, lam_im, log_step, b_re, b_im, c_re, c_im, d_skip):
    G, N = lam_re.shape
    P = d_skip.shape[1]
    nsg = G // S5_SG
    ns, nc = S5_SG * N, S5_SG * P
    T = S5_T
    dt = jnp.exp(log_step.astype(F32))
    eye = jnp.eye(S5_SG, dtype=F32)

    def rows(a):
        return a.astype(F32).reshape(nsg, 1, ns)

    def embed_b(b):
        b = b.astype(F32).reshape(nsg, S5_SG, N, P)
        return jnp.einsum('zgnq,gh->zhqgn', b, eye).reshape(nsg, nc, ns)

    def embed_c(c):
        c = c.astype(F32).reshape(nsg, S5_SG, P, N)
        return jnp.einsum('zgpn,gh->zhpgn', c, eye).reshape(nsg, nc, ns)

    dtb = jnp.broadcast_to(dt[:, None], (G, N))
    d_diag = jax.vmap(jnp.diag)(d_skip.astype(F32).reshape(nsg, nc))
    sq = pl.Squeezed()
    spec_r = pl.BlockSpec((sq, 1, ns), lambda z: (z, 0, 0))
    spec_b = pl.BlockSpec((sq, nc, ns), lambda z: (z, 0, 0))
    return pl.pallas_call(
        _s5_weights_kernel,
        grid=(nsg,),
        in_specs=[spec_r, spec_r, spec_r, spec_b, spec_b, spec_b, spec_b,
                  pl.BlockSpec((sq, nc, nc), lambda z: (z, 0, 0))],
        out_specs=[pl.BlockSpec((sq, T, nc, nc), lambda z: (z, 0, 0, 0)),
                   pl.BlockSpec((sq, T * nc, 2 * ns), lambda z: (z, 0, 0)),
                   pl.BlockSpec((sq, 2 * ns, T * nc), lambda z: (z, 0, 0)),
                   pl.BlockSpec((sq, 1, 2 * ns), lambda z: (z, 0, 0))],
        out_shape=[jax.ShapeDtypeStruct((nsg, T, nc, nc), BF16),
                   jax.ShapeDtypeStruct((nsg, T * nc, 2 * ns), BF16),
                   jax.ShapeDtypeStruct((nsg, 2 * ns, T * nc), BF16),
                   jax.ShapeDtypeStruct((nsg, 1, 2 * ns), F32)],
        compiler_params=_cparams(("arbitrary",)),
        name="s5_weights",
    )(rows(lam_re), rows(lam_im), rows(dtb),
      embed_b(b_re), embed_b(b_im), embed_c(c_re), embed_c(c_im), d_diag)


def _s5_scan_kernel(u_ref, um_ref, kt_ref, win_ref, wout_ref, at_ref, y_ref,
                    uc_ref, v_ref, xp_ref, m_ref, *, n_chunks):
    T = S5_T
    nc = u_ref.shape[1]
    ns = at_ref.shape[1] // 2

    @pl.when(pl.program_id(0) == 0)
    def _():
        zero = jnp.zeros((nc, nc), BF16)
        for s in range(1, T):
            for t in range(s):
                m_ref[s * nc:(s + 1) * nc, t * nc:(t + 1) * nc] = zero

    for tau in range(T):
        tap = kt_ref[tau]
        for s in range(T - tau):
            m_ref[s * nc:(s + 1) * nc, (s + tau) * nc:(s + tau + 1) * nc] = tap

    for t in range(T):
        uc_ref[0:n_chunks, t * nc:(t + 1) * nc] = u_ref[pl.ds(t, n_chunks, stride=T), :].astype(BF16)
        uc_ref[n_chunks:n_chunks + 1, t * nc:(t + 1) * nc] = um_ref[t:t + 1, :].astype(BF16)
    pad = uc_ref.shape[0] - n_chunks - 1
    uc_ref[n_chunks + 1:, :] = jnp.zeros((pad, T * nc), BF16)
    uc = uc_ref[...]
    v_ref[...] = _dot(uc, win_ref[...])
    a_re, a_im = at_ref[:, 0:ns], at_ref[:, ns:2 * ns]

    x_re0 = v_ref[n_chunks:n_chunks + 1, 0:ns]
    x_im0 = v_ref[n_chunks:n_chunks + 1, ns:2 * ns]
    xp_ref[n_chunks:, :] = jnp.zeros((pad + 1, 2 * ns), F32)

    def step(c, carry):
        x_re, x_im = carry
        xp_ref[pl.ds(c, 1), 0:ns] = x_re
        xp_ref[pl.ds(c, 1), ns:2 * ns] = x_im
        v_re = v_ref[pl.ds(c, 1), 0:ns]
        v_im = v_ref[pl.ds(c, 1), ns:2 * ns]
        return (a_re * x_re - a_im * x_im + v_re, a_re * x_im + a_im * x_re + v_im)

    lax.fori_loop(0, n_chunks, step, (x_re0, x_im0))
    yc = _dot(uc, m_ref[...]) + _dot(xp_ref[...].astype(BF16), wout_ref[...])
    for t in range(T):
        y_ref[pl.ds(t, n_chunks, stride=T), :] = yc[0:n_chunks, t * nc:(t + 1) * nc]


def _s5_scan(u, um, kt, win, wout, at):
    u, col0 = u
    n, width = u.shape[0], um.shape[1]
    T = S5_T
    nsg = kt.shape[0]
    nc = width // nsg
    ns2 = at.shape[2]
    assert um.shape[0] == T and n % T == 0 and col0 % nc == 0
    n_chunks = n // T
    rows = -(-(n_chunks + 1) // 16) * 16
    sq = pl.Squeezed()
    return pl.pallas_call(
        functools.partial(_s5_scan_kernel, n_chunks=n_chunks),
        grid=(nsg,),
        in_specs=[pl.BlockSpec((n, nc), lambda z: (0, col0 // nc + z)),
                  pl.BlockSpec((T, nc), lambda z: (0, z)),
                  pl.BlockSpec((sq, T, nc, nc), lambda z: (z, 0, 0, 0)),
                  pl.BlockSpec((sq, T * nc, ns2), lambda z: (z, 0, 0)),
                  pl.BlockSpec((sq, ns2, T * nc), lambda z: (z, 0, 0)),
                  pl.BlockSpec((sq, 1, ns2), lambda z: (z, 0, 0))],
        out_specs=pl.BlockSpec((n, nc), lambda z: (0, z)),
        out_shape=jax.ShapeDtypeStruct((n, width), F32),
        scratch_shapes=[pltpu.VMEM((rows, T * nc), BF16), pltpu.VMEM((rows, ns2), F32),
                        pltpu.VMEM((rows, ns2), F32), pltpu.VMEM((T * nc, T * nc), BF16)],
        compiler_params=_cparams(("arbitrary",)),
        name="s5_scan",
    )(u, um, kt, win, wout, at)


def _merge_kernel(*refs, pieces):
    h_ref, o_ref, y_ref = refs[0:3]
    gh_refs, gs_refs = refs[3:3 + pieces], refs[3 + pieces:3 + 2 * pieces]
    wp_ref, wa_ref, wb_ref, wo_ref, gn_ref, out_ref, norm_ref = refs[3 + 2 * pieces:]
    gh = jnp.concatenate([r[...] for r in gh_refs], axis=1).astype(F32)
    gs = jnp.concatenate([r[...] for r in gs_refs], axis=1).astype(F32)
    y_hg = _dot(o_ref[...], wp_ref[...])
    z = jax.nn.gelu(y_ref[...]).astype(BF16)
    y_s5 = _dot(z, wa_ref[...]) * jax.nn.sigmoid(_dot(z, wb_ref[...]))
    merged = gh * y_hg + gs * y_s5
    h = h_ref[...] + _dot(merged.astype(BF16), wo_ref[...])
    out_ref[...] = h
    norm_ref[...] = _rms(h, gn_ref[...]).astype(BF16)


def _merge(hx, o, y, gh, gs, wp, wa, wb, wo, g_next, *, tm=256):
    n, d = hx.shape
    row = lambda i: (i, 0)
    const = lambda i: (0, 0)
    gw = functools.reduce(math.gcd, (gh[1], gs[1], d))
    assert gw % LANE == 0
    pieces = d // gw

    def resident(shape):
        return pl.BlockSpec(shape, const, pipeline_mode=pl.Buffered(1))

    def view(col0):
        return [pl.BlockSpec((tm, gw), functools.partial(lambda i, k: (i, k), k=col0 // gw + k))
                for k in range(pieces)]

    return pl.pallas_call(
        functools.partial(_merge_kernel, pieces=pieces),
        grid=(n // tm,),
        in_specs=[pl.BlockSpec((tm, d), row), pl.BlockSpec((tm, o.shape[1]), row),
                  pl.BlockSpec((tm, y.shape[1]), row)] + view(gh[1]) + view(gs[1])
        + [resident(wp.shape), resident(wa.shape), resident(wb.shape), resident(wo.shape),
           pl.BlockSpec((1, d), const)],
        out_specs=[pl.BlockSpec((tm, d), row)] * 2,
        out_shape=[jax.ShapeDtypeStruct((n, d), F32), jax.ShapeDtypeStruct((n, d), BF16)],
        compiler_params=_cparams(("arbitrary",)),
        name="mixer_merge",
    )(hx, o, y, *([gh[0]] * pieces), *([gs[0]] * pieces), wp, wa, wb, wo, g_next.reshape(1, d))


def kernel(x, meta_tokens, norm_ffn1, ffn1_w_gate, ffn1_w_up, ffn1_w_down, norm_mix, w_in, hg_lower_bounds, hg_norm, hg_w_proj, s5_lam_re, s5_lam_im, s5_log_step, s5_b_re, s5_b_im, s5_c_re, s5_c_im, s5_d, s5_w_glu_a, s5_w_glu_b, w_out, norm_ffn2, ffn2_w_gate, ffn2_w_up, ffn2_w_down, norm_final):
    batch, seq, d = x.shape
    depth = norm_ffn1.shape[0]
    assert batch == 1 and depth == 1
    hg_k = hg_lower_bounds.shape[1]
    hg_w = hg_norm.shape[1]
    s5_w = s5_w_glu_a.shape[1]
    hx = x[0].astype(F32)
    hm = meta_tokens.astype(F32)
    tm = min(512, seq)
    h0, n0, hm, nm, w1_gate, w1_up, w1_down = _ffn(
        hx, hm, norm_ffn1[0], norm_mix[0], ffn1_w_gate[0], ffn1_w_up[0], ffn1_w_down[0],
        emit_norm=True, rows=tm, tm=tm, tf=256, name="ffn_head")
    sizes = (hg_k, hg_k, hg_w, hg_w, s5_w, d, d)
    assert sum(sizes) == w_in.shape[2]
    off = [0]
    for s in sizes:
        off.append(off[-1] + s)
    ids32, ids16 = (0, 1, 4), (2, 3, 5, 6)
    tc_in = functools.reduce(math.gcd, sizes + (512,))

    def starts(ids):
        out, col = {}, 0
        for k in ids:
            out[k] = col
            col += sizes[k]
        return out

    def blocks(ids):
        return [c for k in ids for c in range(off[k] // tc_in, off[k + 1] // tc_in)]

    steps = lambda grid: grid[0] * grid[1]
    n_body = (seq // tm) * (ffn1_w_gate.shape[2] // 512)
    n_inp = steps(_inproj_grid(seq, sum(sizes[k] for k in ids16))[2])
    n_hg = steps(_hgrn_grid(seq, hg_w)[2])
    hx, nx, w32, w16 = _ffn(
        hx, None, norm_ffn1[0], norm_mix[0], w1_gate, w1_up, w1_down, emit_norm=True, head=(h0, n0),
        cast=(_cast_tile(w_in[0], n_body, tc_in, [blocks(ids32), blocks(ids16)]),),
        tm=tm, name="ffn_body")

    ident, sigm = (0.0, 1.0, 0.0, 0.0), (0.0, 0.0, 1.0, 0.0)
    rows = {0: (HG_DK ** -0.5, 0.0, 0.0, 0.0), 1: (0.0, 0.0, 0.0, 1.0), 2: ident,
            3: (hg_norm[0].astype(F32), 0.0, 0.0, 0.0), 4: ident, 5: sigm, 6: sigm}

    def coef(ids):
        return jnp.concatenate([jnp.stack([jnp.broadcast_to(jnp.asarray(v, F32), (sizes[k],))
                                           for v in rows[k]]) for k in ids], axis=1)

    s32, s16 = starts(ids32), starts(ids16)
    x32, m32 = _inproj(nx, nm, w32, coef(ids32), hg_lower_bounds, s32[1], F32)
    x16, m16, w_proj, w_glu_a, w_glu_b, w_o = _inproj(
        nx, nm, w16, coef(ids16), None, 0, BF16,
        cast=[_cast_tile(w, n_inp) for w in (hg_w_proj[0], s5_w_glu_a[0], s5_w_glu_b[0], w_out[0])])

    o, w2_gate, w2_up, w2_down = _hgrn(
        (x32, s32[0]), (x32, s32[1]), (x16, s16[2]), (x16, s16[3]),
        m32[:, s32[1]:s32[1] + hg_k], m16[:, s16[2]:s16[2] + hg_w],
        cast=[_cast_tile(w, n_hg) for w in (ffn2_w_gate[0], ffn2_w_up[0], ffn2_w_down[0])])
    m, win, wout, at = _s5_weights(s5_lam_re[0], s5_lam_im[0], s5_log_step[0], s5_b_re[0],
                                   s5_b_im[0], s5_c_re[0], s5_c_im[0], s5_d[0])
    y = _s5_scan((x32, s32[4]), m32[:, s32[4]:s32[4] + s5_w], m, win, wout, at)
    hx, nx2 = _merge(hx, o, y, (x16, s16[5]), (x16, s16[6]), w_proj, w_glu_a, w_glu_b, w_o,
                     norm_ffn2[0])
    out, = _ffn(hx, None, nx2, norm_final, w2_gate, w2_up, w2_down, emit_norm=False,
                tm=tm, name="ffn_final")
    return out[None].astype(x.dtype)
```

```python
import functools
import math

import numpy as np
import jax
import jax.numpy as jnp
from jax import lax
from jax.experimental import pallas as pl
from jax.experimental.pallas import tpu as pltpu

EPS = 1e-6
HG_DK = 128
HG_CHUNK = 64
S5_GROUP = 16
S5_STATE = 64
S5_T = 16
S5_SG = 8
LANE = 128
INPROJ_SUB = 256
VMEM_LIMIT = 56 * 1024 * 1024

F32 = jnp.float32
BF16 = jnp.bfloat16


def _cparams(sem):
    return pltpu.CompilerParams(dimension_semantics=sem, vmem_limit_bytes=VMEM_LIMIT)


def _rms(v, g):
    return v * lax.rsqrt(jnp.mean(v * v, axis=-1, keepdims=True) + EPS) * g


def _dot(a, b):
    return jnp.dot(a, b, preferred_element_type=F32)


def _dot_nt(a, b):
    return lax.dot_general(a, b, (((1,), (1,)), ((), ())), preferred_element_type=F32)


def _dot_tn(a, b):
    return lax.dot_general(a, b, (((0,), (0,)), ((), ())), preferred_element_type=F32)


def _cast_jobs(cast, grid):
    nj = grid[1]
    in_specs, out_specs, out_shapes, static = [], [], [], []
    for src, (tr, tc), dests in cast:
        rows, cols = src.shape
        ncb = cols // tc
        n_tiles = (rows // tr) * ncb
        assert rows % tr == 0 and cols % tc == 0 and n_tiles <= grid[0] * nj

        def tile(i, j, n_tiles=n_tiles, ncb=ncb):
            t = jnp.minimum(i * nj + j, n_tiles - 1)
            return t // ncb, t % ncb

        in_specs.append(pl.BlockSpec((tr, tc), tile))
        for blocks in dests:
            park, k = [], 0
            for c in range(ncb):
                k = blocks.index(c) if c in blocks else k
                park.append(k)
            jumps = [(m, park[m] - park[m - 1]) for m in range(1, ncb) if park[m] != park[m - 1]]

            def dest(i, j, tile=tile, first=park[0], jumps=jumps):
                r, c = tile(i, j)
                return r, first + sum(jnp.where(c >= m, dk, 0) for m, dk in jumps)

            out_specs.append(pl.BlockSpec((tr, tc), tile if park == list(range(ncb)) else dest))
            out_shapes.append(jax.ShapeDtypeStruct((rows, len(blocks) * tc), BF16))
        static.append((n_tiles, ncb, tuple(tuple(b) for b in dests)))
    return in_specs, out_specs, out_shapes, tuple(static)


def _n_cast_out(jobs):
    return sum(len(dests) for _, _, dests in jobs)


def _run_cast_jobs(cast_in, cast_out, jobs):
    step = pl.program_id(0) * pl.num_programs(1) + pl.program_id(1)
    outs = iter(cast_out)
    for src, (n_tiles, ncb, dests) in zip(cast_in, jobs):
        dsts = [next(outs) for _ in dests]
        tile = src[...].astype(BF16)
        if len(dsts) == 1:
            dsts[0][...] = tile
        else:
            c = jnp.minimum(step, n_tiles - 1) % ncb
            for dst, cols in zip(dsts, dests):
                owns = functools.reduce(jnp.logical_or, [c == m for m in cols])

                @pl.when(owns)
                def _(dst=dst):
                    dst[...] = tile


def _cast_tile(w, n_steps, tc=None, dests=None):
    rows, cols = w.shape
    if tc is None and rows % n_steps == 0 and (rows // n_steps) % 16 == 0:
        return w, (rows // n_steps, cols), [[0]]
    tc = math.gcd(tc or 512, cols)
    tr = 128
    while (rows // tr) * (cols // tc) > n_steps:
        tr *= 2
    return w, (tr, tc), dests or [list(range(cols // tc))]


def _ffn_kernel(*refs, tm, n_meta, normed, emit_norm, w_f32, head, jobs):
    it = iter(refs)
    take = lambda k: [next(it) for _ in range(k)]
    x_ref, = take(1)
    m_ref = take(1)[0] if n_meta else None
    g_ref, gn_ref, wg_ref, wu_ref, wd_ref = take(5)
    hx_ref, hn16_ref = take(2) if head else (None, None)
    cast_in = take(len(jobs))
    ox_ref, = take(1)
    nx_ref = take(1)[0] if emit_norm else None
    om_ref, nm_ref = take(2) if n_meta else (None, None)
    w16_out = take(3) if w_f32 else ()
    cast_out = take(_n_cast_out(jobs))
    hn_ref = g_ref if normed else take(1)[0]
    acc_ref = take(1)[0] if n_meta else ox_ref
    i = pl.program_id(0)
    j = pl.program_id(1)
    _run_cast_jobs(cast_in, cast_out, jobs)

    def body():
        @pl.when(j == 0)
        def _():
            if not normed:
                hn_ref[0:tm, :] = _rms(x_ref[...], g_ref[...]).astype(BF16)
            if n_meta:
                hn_ref[tm:tm + n_meta, :] = _rms(m_ref[...], g_ref[...]).astype(BF16)
            acc_ref[...] = jnp.zeros_like(acc_ref)

        tiles = [w[...].astype(BF16) if w_f32 else w[...] for w in (wg_ref, wu_ref, wd_ref)]
        for dst, tile in zip(w16_out, tiles):
            dst[...] = tile
        hn = hn_ref[...]
        gate = _dot(hn, tiles[0])
        up = _dot(hn, tiles[1])
        act = (gate * jax.nn.sigmoid(gate) * up).astype(BF16)
        acc_ref[...] += _dot(act, tiles[2])

        @pl.when(j == pl.num_programs(1) - 1)
        def _():
            hx = x_ref[...] + 0.5 * acc_ref[0:tm, :]
            if emit_norm:
                ox_ref[...] = hx
                nx_ref[...] = _rms(hx, gn_ref[...]).astype(BF16)
            else:
                ox_ref[...] = _rms(hx, gn_ref[...])
            if n_meta:
                hm = m_ref[...] + 0.5 * acc_ref[tm:tm + n_meta, :]
                om_ref[...] = hm
                nm_ref[...] = _rms(hm, gn_ref[...]).astype(BF16)

    if head:
        @pl.when((i == 0) & (j == 0))
        def _():
            ox_ref[...] = hx_ref[...]
            nx_ref[...] = hn16_ref[...]

        pl.when(i > 0)(body)
    else:
        body()


def _ffn(hx, hm, g, g_next, wg, wu, wd, *, emit_norm, head=None, cast=(), rows=None, name,
         tm=512, tf=512):
    n, d = hx.shape
    rows = n if rows is None else rows
    dff = wg.shape[1]
    n_meta = 0 if hm is None else hm.shape[0]
    normed = g.ndim == 2
    w_f32 = wg.dtype == F32
    grid = (rows // tm, dff // tf)
    assert not w_f32 or grid[0] == 1
    assert not (normed and n_meta)
    row = lambda i, j: (i, 0)
    const = lambda i, j: (0, 0)
    in_specs = [pl.BlockSpec((tm, d), row)]
    args = [hx]
    if n_meta:
        in_specs.append(pl.BlockSpec((n_meta, d), const))
        args.append(hm)
    col = (lambda i, j: jnp.where(i == 0, 0, j)) if head else (lambda i, j: j)
    in_specs += [pl.BlockSpec((tm, d), row) if normed else pl.BlockSpec((1, d), const),
                 pl.BlockSpec((1, d), const),
                 pl.BlockSpec((d, tf), lambda i, j: (0, col(i, j))),
                 pl.BlockSpec((d, tf), lambda i, j: (0, col(i, j))),
                 pl.BlockSpec((tf, d), lambda i, j: (col(i, j), 0))]
    args += [g if normed else g.reshape(1, d), g_next.reshape(1, d), wg, wu, wd]
    if head:
        in_specs += [pl.BlockSpec((tm, d), const, pipeline_mode=pl.Buffered(1))] * 2
        args += list(head)
    job_in, job_out, job_shapes, jobs = _cast_jobs(cast, grid)
    in_specs += job_in
    args += [src for src, _, _ in cast]
    out_shape = [jax.ShapeDtypeStruct((rows, d), F32)]
    out_specs = [pl.BlockSpec((tm, d), row)]
    if emit_norm:
        out_shape.append(jax.ShapeDtypeStruct((rows, d), BF16))
        out_specs.append(pl.BlockSpec((tm, d), row))
    if n_meta:
        out_shape += [jax.ShapeDtypeStruct((n_meta, d), F32), jax.ShapeDtypeStruct((n_meta, d), BF16)]
        out_specs += [pl.BlockSpec((n_meta, d), const)] * 2
    if w_f32:
        out_shape += [jax.ShapeDtypeStruct(w.shape, BF16) for w in (wg, wu, wd)]
        out_specs += [pl.BlockSpec((d, tf), lambda i, j: (0, j)),
                      pl.BlockSpec((d, tf), lambda i, j: (0, j)),
                      pl.BlockSpec((tf, d), lambda i, j: (j, 0))]
    out_specs += job_out
    out_shape += job_shapes
    return pl.pallas_call(
        functools.partial(_ffn_kernel, tm=tm, n_meta=n_meta, normed=normed, emit_norm=emit_norm,
                          w_f32=w_f32, head=head is not None, jobs=jobs),
        grid=grid, in_specs=in_specs, out_specs=out_specs, out_shape=out_shape,
        scratch_shapes=([] if normed else [pltpu.VMEM((tm + n_meta, d), BF16)])
        + ([pltpu.VMEM((tm + n_meta, d), F32)] if n_meta else []),
        compiler_params=_cparams(("arbitrary", "arbitrary")),
        name=name,
    )(*args)


def _inproj_kernel(*refs, tm, n_meta, with_lb, jobs):
    it = iter(refs)
    take = lambda k: [next(it) for _ in range(k)]
    x_ref, m_ref, w_ref, coef_ref = take(4)
    lb_ref = take(1)[0] if with_lb else None
    cast_in = take(len(jobs))
    ox_ref, om_ref = take(2)
    cast_out = take(_n_cast_out(jobs))
    lhs_ref, = take(1)
    _run_cast_jobs(cast_in, cast_out, jobs)

    @pl.when(pl.program_id(1) == 0)
    def _():
        lhs_ref[0:tm, :] = x_ref[...]
        lhs_ref[tm:tm + n_meta, :] = m_ref[...]

    tn = w_ref.shape[1]
    sw = math.gcd(tn, INPROJ_SUB)
    for s in range(tn // sw):
        cs = slice(s * sw, (s + 1) * sw)
        res = _dot(lhs_ref[...], w_ref[:, cs])
        sig = jax.nn.sigmoid(res)
        coef = coef_ref[:, cs]
        a, c, b, d = coef[0:1, :], coef[1:2, :], coef[2:3, :], 0.0
        if with_lb:
            lbl = lb_ref[:, cs]
            e = jnp.exp(lbl - jnp.max(lbl, axis=0, keepdims=True))
            lb = e[0:1, :] / jnp.sum(e, axis=0, keepdims=True)
            flag = coef[3:4, :]
            b = b + flag * (1.0 - lb)
            d = flag * lb
        out = res * (a * sig + c) + (b * sig + d)
        ox_ref[:, cs] = out[0:tm, :].astype(ox_ref.dtype)
        om_ref[:, cs] = out[tm:tm + n_meta, :].astype(om_ref.dtype)


def _inproj_grid(n, cols, tm=1024, tn=1792):
    tm = min(tm, n)
    tn = math.gcd(tn, cols)
    assert tn % LANE == 0 and n % tm == 0
    return tm, tn, (n // tm, cols // tn)


def _inproj(nx, nm, w, coef, lower_bounds, lb_cols, dtype, cast=()):
    n, d = nx.shape
    n_meta = nm.shape[0]
    cols = w.shape[1]
    with_lb = lower_bounds is not None
    tm, tn, grid = _inproj_grid(n, cols)
    job_in, job_out, job_shapes, jobs = _cast_jobs(cast, grid)
    in_specs = [pl.BlockSpec((tm, d), lambda i, j: (i, 0)),
                pl.BlockSpec((n_meta, d), lambda i, j: (0, 0)),
                pl.BlockSpec((d, tn), lambda i, j: (0, j)),
                pl.BlockSpec((4, tn), lambda i, j: (0, j))]
    args = [nx, nm, w, coef]
    if with_lb:
        nlb, k = lower_bounds.shape
        lb_full = jnp.pad(lower_bounds.astype(F32), ((0, 0), (lb_cols, cols - lb_cols - k)))
        in_specs.append(pl.BlockSpec((nlb, tn), lambda i, j: (0, j)))
        args.append(lb_full)
    ox, om, *copies = pl.pallas_call(
        functools.partial(_inproj_kernel, tm=tm, n_meta=n_meta, with_lb=with_lb, jobs=jobs),
        grid=grid,
        in_specs=in_specs + job_in,
        out_specs=[pl.BlockSpec((tm, tn), lambda i, j: (i, j)),
                   pl.BlockSpec((pl.Squeezed(), n_meta, tn), lambda i, j: (i, 0, j))] + job_out,
        out_shape=[jax.ShapeDtypeStruct((n, cols), dtype),
                   jax.ShapeDtypeStruct((n // tm, n_meta, cols), dtype)] + job_shapes,
        scratch_shapes=[pltpu.VMEM((tm + n_meta, d), BF16)],
        compiler_params=_cparams(("arbitrary", "arbitrary")),
        name="mixer_inproj_" + jnp.dtype(dtype).name,
    )(*args, *[src for src, _, _ in cast])
    return (ox, om[0], *copies)


def _split3(x):
    hi = x.astype(BF16)
    r1 = x - hi.astype(F32)
    mid = r1.astype(BF16)
    lo = (r1 - mid.astype(F32)).astype(BF16)
    return hi, mid, lo


def _dot3(m01, parts):
    return _dot(m01, parts[0]) + _dot(m01, parts[1]) + _dot(m01, parts[2])


def _tri(n, rep=1):
    r = lax.broadcasted_iota(jnp.int32, (n * rep, n), 0)
    c = lax.broadcasted_iota(jnp.int32, (n * rep, n), 1)
    return jnp.where(r >= c * rep, 1.0, 0.0).astype(BF16)


def _hgrn_tables(c):
    sums = [np.tril(np.ones((c, c)))]
    level = np.full((c, c), -1, np.int32)
    t_idx, s_idx = np.meshgrid(np.arange(c), np.arange(c), indexing="ij")
    halves = []
    h = c // 2
    while h >= 1:
        if h < 8:
            m = np.zeros((c, c))
            for r in range(c):
                mid = (r // (2 * h)) * 2 * h + h
                if r < mid:
                    m[r, r + 1:mid] = 1.0
                else:
                    m[r, mid:r + 1] = 1.0
            sums.append(m)
        pair = 2 * h
        level[(t_idx // pair == s_idx // pair) & (s_idx % pair < h) & (t_idx % pair >= h)] = len(halves)
        halves.append(h)
        h //= 2
    assert len(halves) % 2 == 0
    level[np.arange(c), np.arange(c)] = len(halves)
    owner = np.full((c, 2 * c), -1, np.int32)
    owner[:, :c] = np.where((level >= 0) & (level % 2 == 0), level // 2, -1)
    owner[:, c:] = np.where(level % 2 == 1, level // 2, -1)
    return np.concatenate(sums, axis=0), owner, tuple(halves)


def _hgrn_score_products(q, k, x, halves):
    C = HG_CHUNK
    b = x[0:C]
    zero = jnp.zeros((C, HG_DK), BF16)
    diag = jnp.sum(q * k, axis=-1, keepdims=True)
    qts, kts = [], []
    fine = 0
    for h in halves:
        if h >= 8:
            none = jnp.zeros((h, HG_DK), F32)
            qh, kh = [], []
            for r in range(0, C, 2 * h):
                b_mid = b[r + h - 1:r + h, :]
                kh += [k[r:r + h] * jnp.exp(b_mid - b[r:r + h]), none]
                qh += [none, q[r + h:r + 2 * h] * jnp.exp(b[r + h:r + 2 * h] - b_mid)]
            qts.append(jnp.concatenate(qh, axis=0).astype(BF16))
            kts.append(jnp.concatenate(kh, axis=0).astype(BF16))
        else:
            fine += 1
            e = jnp.exp(x[fine * C:(fine + 1) * C])
            qts.append((q * e).astype(BF16))
            kts.append((k * e).astype(BF16))
    prods = []
    for p in range(len(halves) // 2):
        lhs = jnp.concatenate([qts[2 * p], qts[2 * p + 1]], axis=1)
        rhs = jnp.concatenate([jnp.concatenate([kts[2 * p], zero], axis=1),
                               jnp.concatenate([zero, kts[2 * p + 1]], axis=1)], axis=0)
        prods.append(_dot_nt(lhs, rhs))
    return prods, diag


def _hgrn_pair(q2, f2, v2, gate2, st2, sums, owns, halves):
    C, D = HG_CHUNK, HG_DK
    halfs = (slice(0, D), slice(D, 2 * D))
    k2 = 1.0 - f2
    x2 = _dot(sums, jnp.concatenate(_split3(jnp.log(f2)), axis=0))
    yield
    b2 = x2[0:C]
    b_last = b2[C - 1:C, :]
    kdec2 = (k2 * jnp.exp(b_last - b2)).astype(BF16)
    qe2 = (q2 * jnp.exp(b2)).astype(BF16)
    st2b = st2.astype(BF16)
    scores = []
    for c in halfs:
        scores.append(_hgrn_score_products(q2[:, c], k2[:, c], x2[:, c], halves))
        yield
    zero = jnp.zeros((2 * C, D), BF16)
    os, vvts, kds = [], [], []
    for j, c in enumerate(halfs):
        prods, diag = scores[j]
        a = jnp.where(owns[len(prods)], diag, 0.0)
        for p, prod in enumerate(prods):
            a = jnp.where(owns[p], prod, a)
        v = v2[:, c]
        vvt = jnp.concatenate([v, v], axis=0).T
        os.append(_dot_nt(jnp.concatenate([qe2[:, c], a.astype(BF16)], axis=1),
                          jnp.concatenate([st2b[:, c], vvt], axis=1)))
        vvts.append(vvt)
        kd = jnp.concatenate([kdec2[:, c], jnp.zeros((C, D), BF16)], axis=0)
        kds.append(jnp.concatenate([kd, zero] if j == 0 else [zero, kd], axis=1))
        yield
    st_new = st2 * jnp.exp(b_last) + _dot(jnp.concatenate(vvts, axis=1), jnp.concatenate(kds, axis=0))
    yield
    ons = [o * lax.rsqrt(jnp.mean(o * o, axis=-1, keepdims=True) + EPS) for o in os]
    on2 = jnp.concatenate(ons, axis=1) * gate2
    return on2.astype(BF16), st_new


def _interleave(gens):
    out = [None] * len(gens)
    live = list(range(len(gens)))
    while live:
        for g in list(live):
            try:
                next(gens[g])
            except StopIteration as stop:
                out[g] = stop.value
                live.remove(g)
    return out


def _hgrn_kernel(q_ref, f_ref, v_ref, go_ref, fm_ref, vm_ref, sums_ref, owner_ref,
                 *rest, n_chunks, n_meta, hb, halves, jobs):
    cast_in, o_ref = rest[:len(jobs)], rest[len(jobs)]
    cast_out, st_ref = rest[len(jobs) + 1:-1], rest[-1]
    _run_cast_jobs(cast_in, cast_out, jobs)
    i = pl.program_id(0)
    p0 = pl.program_id(1) * (hb // 2)
    C = HG_CHUNK
    W = 2 * HG_DK

    @pl.when(i == 0)
    def _():
        tri_m = _tri(n_meta)
        for pair in range(hb // 2):
            sts = []
            for hh in (2 * pair, 2 * pair + 1):
                cols = slice(hh * HG_DK, (hh + 1) * HG_DK)
                fm = fm_ref[:, cols]
                bm = _dot3(tri_m, _split3(jnp.log(fm)))
                kdec = ((1.0 - fm) * jnp.exp(bm[n_meta - 1:n_meta, :] - bm)).astype(BF16)
                sts.append(_dot_tn(vm_ref[:, cols], kdec))
            st_ref[p0 + pair] = jnp.concatenate(sts, axis=1)

    def chunk(c, states):
        r0 = pl.multiple_of(c * C, C)
        owner = owner_ref[...]
        owns = [owner == p for p in range(len(halves) // 2 + 1)]
        sums = sums_ref[...]
        gens = []
        for pair in range(hb // 2):
            cols = slice(pair * W, (pair + 1) * W)
            gate2 = go_ref[pl.ds(r0, C), cols].astype(F32)
            gens.append(_hgrn_pair(q_ref[pl.ds(r0, C), cols], f_ref[pl.ds(r0, C), cols],
                                   v_ref[pl.ds(r0, C), cols], gate2, states[pair],
                                   sums, owns, halves))
        results = _interleave(gens)
        o_ref[pl.ds(r0, C), :] = jnp.concatenate([r[0] for r in results], axis=1)
        return tuple(r[1] for r in results)

    states = lax.fori_loop(0, n_chunks, chunk, tuple(st_ref[p0 + p] for p in range(hb // 2)))
    for pair in range(hb // 2):
        st_ref[p0 + pair] = states[pair]


def _hgrn_grid(n, width, rb=512, hb=12):
    heads = width // HG_DK
    hb = min(hb, heads)
    assert hb % 2 == 0 and heads % hb == 0
    return rb, hb, (n // rb, heads // hb)


def _hgrn(q, f, v, go, fm, vm, cast=()):
    n, width = q[0].shape[0], fm.shape[1]
    heads = width // HG_DK
    rb, hb, grid = _hgrn_grid(n, width)
    bw = hb * HG_DK
    n_meta = fm.shape[0]
    job_in, job_out, job_shapes, jobs = _cast_jobs(cast, grid)
    sums, owner, halves = _hgrn_tables(HG_CHUNK)
    sums3 = jnp.asarray(np.concatenate([sums] * 3, axis=1), BF16)
    blk = lambda i, h: (i, h)
    mblk = lambda i, h: (0, h)
    const = lambda i, h: (0, 0)

    def view(col0):
        assert col0 % bw == 0
        return pl.BlockSpec((rb, bw), lambda i, h: (i, col0 // bw + h))

    return pl.pallas_call(
        functools.partial(_hgrn_kernel, n_chunks=rb // HG_CHUNK, n_meta=n_meta, hb=hb,
                          halves=halves, jobs=jobs),
        grid=grid,
        in_specs=[view(q[1]), view(f[1]), view(v[1]), view(go[1])]
        + [pl.BlockSpec((n_meta, hb * HG_DK), mblk)] * 2
        + [pl.BlockSpec(sums3.shape, const), pl.BlockSpec(owner.shape, const)] + job_in,
        out_specs=[pl.BlockSpec((rb, hb * HG_DK), blk)] + job_out,
        out_shape=[jax.ShapeDtypeStruct((n, width), BF16)] + job_shapes,
        scratch_shapes=[pltpu.VMEM((heads // 2, HG_DK, 2 * HG_DK), F32)],
        compiler_params=_cparams(("arbitrary", "arbitrary")),
        name="hgrn2",
    )(q[0], f[0], v[0], go[0], fm, vm, sums3, jnp.asarray(owner), *[src for src, _, _ in cast])


def _s5_weights_kernel(lr_ref, li_ref, dt_ref, bre_ref, bim_ref, cre_ref, cim_ref, dsk_ref,
                       kt_ref, win_ref, wout_ref, at_ref):
    T = S5_T
    ns = lr_ref.shape[1]
    nc = cre_ref.shape[0]
    lr, li, dt = lr_ref[...], li_ref[...], dt_ref[...]

    def powers(tau):
        mag = jnp.exp(lr * dt * tau)
        ang = li * dt * tau
        return mag * jnp.cos(ang), mag * jnp.sin(ang)

    a_re, a_im = powers(1.0)
    num_re, num_im = a_re - 1.0, a_im
    den = lr * lr + li * li
    coef_re = (num_re * lr + num_im * li) / den
    coef_im = (num_im * lr - num_re * li) / den
    br, bi = bre_ref[...], bim_ref[...]
    bbar_re = coef_re * br - coef_im * bi
    bbar_im = coef_re * bi + coef_im * br

    pw = [powers(float(tau)) for tau in range(T + 1)]
    for s in range(T):
        p_re, p_im = pw[T - 1 - s]
        win_ref[s * nc:(s + 1) * nc, 0:ns] = (bbar_re * p_re - bbar_im * p_im).astype(BF16)
        win_ref[s * nc:(s + 1) * nc, ns:2 * ns] = (bbar_re * p_im + bbar_im * p_re).astype(BF16)
    at_ref[:, 0:ns] = pw[T][0]
    at_ref[:, ns:2 * ns] = pw[T][1]

    cr, ci = cre_ref[...], cim_ref[...]
    def split2(a):
        hi = a.astype(BF16)
        return hi, (a - hi.astype(F32)).astype(BF16)

    bb_hi, bb_lo = split2(jnp.concatenate([bbar_re, -bbar_im], axis=1))
    for tau in range(T + 1):
        p_re, p_im = pw[tau]
        ca_re = cr * p_re - ci * p_im
        ca_im = cr * p_im + ci * p_re
        if tau >= 1:
            t = tau - 1
            wout_ref[0:ns, t * nc:(t + 1) * nc] = ca_re.T.astype(BF16)
            wout_ref[ns:2 * ns, t * nc:(t + 1) * nc] = (-ca_im).T.astype(BF16)
        if tau < T:
            ca_hi, ca_lo = split2(jnp.concatenate([ca_re, ca_im], axis=1))
            kt = _dot_nt(bb_hi, ca_hi) + _dot_nt(bb_hi, ca_lo) + _dot_nt(bb_lo, ca_hi)
            if tau == 0:
                kt = kt + dsk_ref[...]
            kt_ref[tau] = kt.astype(BF16)


def _s5_weights(lam_re, lam_im, log_step, b_re, b_im, c_re, c_im, d_skip):
    G, N = lam_re.shape
    P = d_skip.shape[1]
    nsg = G // S5_SG
    ns, nc = S5_SG * N, S5_SG * P
    T = S5_T
    dt = jnp.exp(log_step.astype(F32))
    eye = jnp.eye(S5_SG, dtype=F32)

    def rows(a):
        return a.astype(F32).reshape(nsg, 1, ns)

    def embed_b(b):
        b = b.astype(F32).reshape(nsg, S5_SG, N, P)
        return jnp.einsum('zgnq,gh->zhqgn', b, eye).reshape(nsg, nc, ns)

    def embed_c(c):
        c = c.astype(F32).reshape(nsg, S5_SG, P, N)
        return jnp.einsum('zgpn,gh->zhpgn', c, eye).reshape(nsg, nc, ns)

    dtb = jnp.broadcast_to(dt[:, None], (G, N))
    d_diag = jax.vmap(jnp.diag)(d_skip.astype(F32).reshape(nsg, nc))
    sq = pl.Squeezed()
    spec_r = pl.BlockSpec((sq, 1, ns), lambda z: (z, 0, 0))
    spec_b = pl.BlockSpec((sq, nc, ns), lambda z: (z, 0, 0))
    return pl.pallas_call(
        _s5_weights_kernel,
        grid=(nsg,),
        in_specs=[spec_r, spec_r, spec_r, spec_b, spec_b, spec_b, spec_b,
                  pl.BlockSpec((sq, nc, nc), lambda z: (z, 0, 0))],
        out_specs=[pl.BlockSpec((sq, T, nc, nc), lambda z: (z, 0, 0, 0)),
                   pl.BlockSpec((sq, T * nc, 2 * ns), lambda z: (z, 0, 0)),
                   pl.BlockSpec((sq, 2 * ns, T * nc), lambda z: (z, 0, 0)),
                   pl.BlockSpec((sq, 1, 2 * ns), lambda z: (z, 0, 0))],
        out_shape=[jax.ShapeDtypeStruct((nsg, T, nc, nc), BF16),
                   jax.ShapeDtypeStruct((nsg, T * nc, 2 * ns), BF16),
                   jax.ShapeDtypeStruct((nsg, 2 * ns, T * nc), BF16),
                   jax.ShapeDtypeStruct((nsg, 1, 2 * ns), F32)],
        compiler_params=_cparams(("arbitrary",)),
        name="s5_weights",
    )(rows(lam_re), rows(lam_im), rows(dtb),
      embed_b(b_re), embed_b(b_im), embed_c(c_re), embed_c(c_im), d_diag)


def _s5_scan_kernel(u_ref, um_ref, kt_ref, win_ref, wout_ref, at_ref, y_ref,
                    uc_ref, v_ref, xp_ref, m_ref, *, n_chunks):
    T = S5_T
    nc = u_ref.shape[1]
    ns = at_ref.shape[1] // 2

    @pl.when(pl.program_id(0) == 0)
    def _():
        zero = jnp.zeros((nc, nc), BF16)
        for s in range(1, T):
            for t in range(s):
                m_ref[s * nc:(s + 1) * nc, t * nc:(t + 1) * nc] = zero

    for tau in range(T):
        tap = kt_ref[tau]
        for s in range(T - tau):
            m_ref[s * nc:(s + 1) * nc, (s + tau) * nc:(s + tau + 1) * nc] = tap

    for t in range(T):
        uc_ref[0:n_chunks, t * nc:(t + 1) * nc] = u_ref[pl.ds(t, n_chunks, stride=T), :].astype(BF16)
        uc_ref[n_chunks:n_chunks + 1, t * nc:(t + 1) * nc] = um_ref[t:t + 1, :].astype(BF16)
    pad = uc_ref.shape[0] - n_chunks - 1
    uc_ref[n_chunks + 1:, :] = jnp.zeros((pad, T * nc), BF16)
    uc = uc_ref[...]
    v_ref[...] = _dot(uc, win_ref[...])
    a_re, a_im = at_ref[:, 0:ns], at_ref[:, ns:2 * ns]

    x_re0 = v_ref[n_chunks:n_chunks + 1, 0:ns]
    x_im0 = v_ref[n_chunks:n_chunks + 1, ns:2 * ns]
    xp_ref[n_chunks:, :] = jnp.zeros((pad + 1, 2 * ns), F32)

    def step(c, carry):
        x_re, x_im = carry
        xp_ref[pl.ds(c, 1), 0:ns] = x_re
        xp_ref[pl.ds(c, 1), ns:2 * ns] = x_im
        v_re = v_ref[pl.ds(c, 1), 0:ns]
        v_im = v_ref[pl.ds(c, 1), ns:2 * ns]
        return (a_re * x_re - a_im * x_im + v_re, a_re * x_im + a_im * x_re + v_im)

    lax.fori_loop(0, n_chunks, step, (x_re0, x_im0))
    xp = xp_ref[...].astype(BF16)
    for p in range(T // 2):
        k_hi = (2 * p + 2) * nc
        cols = slice(2 * p * nc, k_hi)
        yc = _dot(uc_ref[:, 0:k_hi], m_ref[0:k_hi, cols]) + _dot(xp, wout_ref[:, cols])
        for t in (2 * p, 2 * p + 1):
            y_ref[pl.ds(t, n_chunks, stride=T), :] = yc[0:n_chunks, (t - 2 * p) * nc:(t - 2 * p + 1) * nc]


def _s5_scan(u, um, kt, win, wout, at):
    u, col0 = u
    n, width = u.shape[0], um.shape[1]
    T = S5_T
    nsg = kt.shape[0]
    nc = width // nsg
    ns2 = at.shape[2]
    assert um.shape[0] == T and n % T == 0 and col0 % nc == 0
    n_chunks = n // T
    rows = -(-(n_chunks + 1) // 16) * 16
    sq = pl.Squeezed()
    return pl.pallas_call(
        functools.partial(_s5_scan_kernel, n_chunks=n_chunks),
        grid=(nsg,),
        in_specs=[pl.BlockSpec((n, nc), lambda z: (0, col0 // nc + z)),
                  pl.BlockSpec((T, nc), lambda z: (0, z)),
                  pl.BlockSpec((sq, T, nc, nc), lambda z: (z, 0, 0, 0)),
                  pl.BlockSpec((sq, T * nc, ns2), lambda z: (z, 0, 0)),
                  pl.BlockSpec((sq, ns2, T * nc), lambda z: (z, 0, 0)),
                  pl.BlockSpec((sq, 1, ns2), lambda z: (z, 0, 0))],
        out_specs=pl.BlockSpec((n, nc), lambda z: (0, z)),
        out_shape=jax.ShapeDtypeStruct((n, width), F32),
        scratch_shapes=[pltpu.VMEM((rows, T * nc), BF16), pltpu.VMEM((rows, ns2), F32),
                        pltpu.VMEM((rows, ns2), F32), pltpu.VMEM((T * nc, T * nc), BF16)],
        compiler_params=_cparams(("arbitrary",)),
        name="s5_scan",
    )(u, um, kt, win, wout, at)


def _merge_kernel(*refs, pieces):
    h_ref, o_ref, y_ref = refs[0:3]
    gh_refs, gs_refs = refs[3:3 + pieces], refs[3 + pieces:3 + 2 * pieces]
    wp_ref, wa_ref, wb_ref, wo_ref, gn_ref, out_ref, norm_ref = refs[3 + 2 * pieces:]
    gh = jnp.concatenate([r[...] for r in gh_refs], axis=1).astype(F32)
    gs = jnp.concatenate([r[...] for r in gs_refs], axis=1).astype(F32)
    y_hg = _dot(o_ref[...], wp_ref[...])
    z = jax.nn.gelu(y_ref[...]).astype(BF16)
    y_s5 = _dot(z, wa_ref[...]) * jax.nn.sigmoid(_dot(z, wb_ref[...]))
    merged = gh * y_hg + gs * y_s5
    h = h_ref[...] + _dot(merged.astype(BF16), wo_ref[...])
    out_ref[...] = h
    norm_ref[...] = _rms(h, gn_ref[...]).astype(BF16)


def _merge(hx, o, y, gh, gs, wp, wa, wb, wo, g_next, *, tm=256):
    n, d = hx.shape
    row = lambda i: (i, 0)
    const = lambda i: (0, 0)
    gw = functools.reduce(math.gcd, (gh[1], gs[1], d))
    assert gw % LANE == 0
    pieces = d // gw

    def resident(shape):
        return pl.BlockSpec(shape, const, pipeline_mode=pl.Buffered(1))

    def view(col0):
        return [pl.BlockSpec((tm, gw), functools.partial(lambda i, k: (i, k), k=col0 // gw + k))
                for k in range(pieces)]

    return pl.pallas_call(
        functools.partial(_merge_kernel, pieces=pieces),
        grid=(n // tm,),
        in_specs=[pl.BlockSpec((tm, d), row), pl.BlockSpec((tm, o.shape[1]), row),
                  pl.BlockSpec((tm, y.shape[1]), row)] + view(gh[1]) + view(gs[1])
        + [resident(wp.shape), resident(wa.shape), resident(wb.shape), resident(wo.shape),
           pl.BlockSpec((1, d), const)],
        out_specs=[pl.BlockSpec((tm, d), row)] * 2,
        out_shape=[jax.ShapeDtypeStruct((n, d), F32), jax.ShapeDtypeStruct((n, d), BF16)],
        compiler_params=_cparams(("arbitrary",)),
        name="mixer_merge",
    )(hx, o, y, *([gh[0]] * pieces), *([gs[0]] * pieces), wp, wa, wb, wo, g_next.reshape(1, d))


def kernel(x, meta_tokens, norm_ffn1, ffn1_w_gate, ffn1_w_up, ffn1_w_down, norm_mix, w_in, hg_lower_bounds, hg_norm, hg_w_proj, s5_lam_re, s5_lam_im, s5_log_step, s5_b_re, s5_b_im, s5_c_re, s5_c_im, s5_d, s5_w_glu_a, s5_w_glu_b, w_out, norm_ffn2, ffn2_w_gate, ffn2_w_up, ffn2_w_down, norm_final):
    batch, seq, d = x.shape
    depth = norm_ffn1.shape[0]
    assert batch == 1 and depth == 1
    hg_k = hg_lower_bounds.shape[1]
    hg_w = hg_norm.shape[1]
    s5_w = s5_w_glu_a.shape[1]
    hx = x[0].astype(F32)
    hm = meta_tokens.astype(F32)
    tm = min(512, seq)
    h0, n0, hm, nm, w1_gate, w1_up, w1_down = _ffn(
        hx, hm, norm_ffn1[0], norm_mix[0], ffn1_w_gate[0], ffn1_w_up[0], ffn1_w_down[0],
        emit_norm=True, rows=tm, tm=tm, tf=256, name="ffn_head")
    sizes = (hg_k, hg_k, hg_w, hg_w, s5_w, d, d)
    assert sum(sizes) == w_in.shape[2]
    off = [0]
    for s in sizes:
        off.append(off[-1] + s)
    ids32, ids16 = (0, 1, 4), (2, 3, 5, 6)
    tc_in = functools.reduce(math.gcd, sizes + (512,))

    def starts(ids):
        out, col = {}, 0
        for k in ids:
            out[k] = col
            col += sizes[k]
        return out

    def blocks(ids):
        return [c for k in ids for c in range(off[k] // tc_in, off[k + 1] // tc_in)]

    steps = lambda grid: grid[0] * grid[1]
    n_body = (seq // tm) * (ffn1_w_gate.shape[2] // 512)
    n_inp = steps(_inproj_grid(seq, sum(sizes[k] for k in ids16))[2])
    n_hg = steps(_hgrn_grid(seq, hg_w)[2])
    hx, nx, w32, w16 = _ffn(
        hx, None, norm_ffn1[0], norm_mix[0], w1_gate, w1_up, w1_down, emit_norm=True, head=(h0, n0),
        cast=(_cast_tile(w_in[0], n_body, tc_in, [blocks(ids32), blocks(ids16)]),),
        tm=tm, name="ffn_body")

    ident, sigm = (0.0, 1.0, 0.0, 0.0), (0.0, 0.0, 1.0, 0.0)
    rows = {0: (HG_DK ** -0.5, 0.0, 0.0, 0.0), 1: (0.0, 0.0, 0.0, 1.0), 2: ident,
            3: (hg_norm[0].astype(F32), 0.0, 0.0, 0.0), 4: ident, 5: sigm, 6: sigm}

    def coef(ids):
        return jnp.concatenate([jnp.stack([jnp.broadcast_to(jnp.asarray(v, F32), (sizes[k],))
                                           for v in rows[k]]) for k in ids], axis=1)

    s32, s16 = starts(ids32), starts(ids16)
    x32, m32 = _inproj(nx, nm, w32, coef(ids32), hg_lower_bounds, s32[1], F32)
    x16, m16, w_proj, w_glu_a, w_glu_b, w_o = _inproj(
        nx, nm, w16, coef(ids16), None, 0, BF16,
        cast=[_cast_tile(w, n_inp) for w in (hg_w_proj[0], s5_w_glu_a[0], s5_w_glu_b[0], w_out[0])])

    o, w2_gate, w2_up, w2_down = _hgrn(
        (x32, s32[0]), (x32, s32[1]), (x16, s16[2]), (x16, s16[3]),
        m32[:, s32[1]:s32[1] + hg_k], m16[:, s16[2]:s16[2] + hg_w],
        cast=[_cast_tile(w, n_hg) for w in (ffn2_w_gate[0], ffn2_w_up[0], ffn2_w_down[0])])
    m, win, wout, at = _s5_weights(s5_lam_re[0], s5_lam_im[0], s5_log_step[0], s5_b_re[0],
                                   s5_b_im[0], s5_c_re[0], s5_c_im[0], s5_d[0])
    y = _s5_scan((x32, s32[4]), m32[:, s32[4]:s32[4] + s5_w], m, win, wout, at)
    hx, nx2 = _merge(hx, o, y, (x16, s16[5]), (x16, s16[6]), w_proj, w_glu_a, w_glu_b, w_o,
                     norm_ffn2[0])
    out, = _ffn(hx, None, nx2, norm_final, w2_gate, w2_up, w2_down, emit_norm=False,
                tm=tm, name="ffn_final")
    return out[None].astype(x.dtype)
```

```python
import functools
import math

import numpy as np
import jax
import jax.numpy as jnp
from jax import lax
from jax.experimental import pallas as pl
from jax.experimental.pallas import tpu as pltpu

EPS = 1e-6
HG_DK = 128
HG_CHUNK = 64
S5_GROUP = 16
S5_STATE = 64
S5_T = 16
S5_SG = 8
LANE = 128
INPROJ_SUB = 256
VMEM_LIMIT = 56 * 1024 * 1024

F32 = jnp.float32
BF16 = jnp.bfloat16


def _cparams(sem):
    return pltpu.CompilerParams(dimension_semantics=sem, vmem_limit_bytes=VMEM_LIMIT)


def _rms(v, g):
    return v * lax.rsqrt(jnp.mean(v * v, axis=-1, keepdims=True) + EPS) * g


def _sigmoid(x):
    return 0.5 * jnp.tanh(0.5 * x) + 0.5


def _silu(x):
    h = 0.5 * x
    return h * (jnp.tanh(h) + 1.0)


def _dot(a, b):
    return jnp.dot(a, b, preferred_element_type=F32)


def _dot_nt(a, b):
    return lax.dot_general(a, b, (((1,), (1,)), ((), ())), preferred_element_type=F32)


def _dot_tn(a, b):
    return lax.dot_general(a, b, (((0,), (0,)), ((), ())), preferred_element_type=F32)


def _cast_jobs(cast, grid):
    nj = grid[1]
    in_specs, out_specs, out_shapes, static = [], [], [], []
    for src, (tr, tc), dests in cast:
        rows, cols = src.shape
        ncb = cols // tc
        n_tiles = (rows // tr) * ncb
        assert rows % tr == 0 and cols % tc == 0 and n_tiles <= grid[0] * nj

        def tile(i, j, n_tiles=n_tiles, ncb=ncb):
            t = jnp.minimum(i * nj + j, n_tiles - 1)
            return t // ncb, t % ncb

        in_specs.append(pl.BlockSpec((tr, tc), tile))
        for blocks in dests:
            park, k = [], 0
            for c in range(ncb):
                k = blocks.index(c) if c in blocks else k
                park.append(k)
            jumps = [(m, park[m] - park[m - 1]) for m in range(1, ncb) if park[m] != park[m - 1]]

            def dest(i, j, tile=tile, first=park[0], jumps=jumps):
                r, c = tile(i, j)
                return r, first + sum(jnp.where(c >= m, dk, 0) for m, dk in jumps)

            out_specs.append(pl.BlockSpec((tr, tc), tile if park == list(range(ncb)) else dest))
            out_shapes.append(jax.ShapeDtypeStruct((rows, len(blocks) * tc), BF16))
        static.append((n_tiles, ncb, tuple(tuple(b) for b in dests)))
    return in_specs, out_specs, out_shapes, tuple(static)


def _n_cast_out(jobs):
    return sum(len(dests) for _, _, dests in jobs)


def _run_cast_jobs(cast_in, cast_out, jobs):
    step = pl.program_id(0) * pl.num_programs(1) + pl.program_id(1)
    outs = iter(cast_out)
    for src, (n_tiles, ncb, dests) in zip(cast_in, jobs):
        dsts = [next(outs) for _ in dests]
        tile = src[...].astype(BF16)
        if len(dsts) == 1:
            dsts[0][...] = tile
        else:
            c = jnp.minimum(step, n_tiles - 1) % ncb
            for dst, cols in zip(dsts, dests):
                owns = functools.reduce(jnp.logical_or, [c == m for m in cols])

                @pl.when(owns)
                def _(dst=dst):
                    dst[...] = tile


def _cast_tile(w, n_steps, tc=None, dests=None):
    rows, cols = w.shape
    if tc is None and rows % n_steps == 0 and (rows // n_steps) % 16 == 0:
        return w, (rows // n_steps, cols), [[0]]
    tc = math.gcd(tc or 512, cols)
    tr = 128
    while (rows // tr) * (cols // tc) > n_steps:
        tr *= 2
    return w, (tr, tc), dests or [list(range(cols // tc))]


def _ffn_kernel(*refs, tm, n_meta, normed, emit_norm, w_f32, head, jobs):
    it = iter(refs)
    take = lambda k: [next(it) for _ in range(k)]
    x_ref, = take(1)
    m_ref = take(1)[0] if n_meta else None
    g_ref, gn_ref, wg_ref, wu_ref, wd_ref = take(5)
    hx_ref, hn16_ref = take(2) if head else (None, None)
    cast_in = take(len(jobs))
    ox_ref, = take(1)
    nx_ref = take(1)[0] if emit_norm else None
    om_ref, nm_ref = take(2) if n_meta else (None, None)
    w16_out = take(3) if w_f32 else ()
    cast_out = take(_n_cast_out(jobs))
    hn_ref = g_ref if normed else take(1)[0]
    acc_ref = take(1)[0] if n_meta else ox_ref
    i = pl.program_id(0)
    j = pl.program_id(1)
    _run_cast_jobs(cast_in, cast_out, jobs)

    def body():
        @pl.when(j == 0)
        def _():
            if not normed:
                hn_ref[0:tm, :] = _rms(x_ref[...], g_ref[...]).astype(BF16)
            if n_meta:
                hn_ref[tm:tm + n_meta, :] = _rms(m_ref[...], g_ref[...]).astype(BF16)
            acc_ref[...] = jnp.zeros_like(acc_ref)

        tiles = [w[...].astype(BF16) if w_f32 else w[...] for w in (wg_ref, wu_ref, wd_ref)]
        for dst, tile in zip(w16_out, tiles):
            dst[...] = tile
        hn = hn_ref[...]
        gate = _dot(hn, tiles[0])
        up = _dot(hn, tiles[1])
        act = (_silu(gate) * up).astype(BF16)
        acc_ref[...] += _dot(act, tiles[2])

        @pl.when(j == pl.num_programs(1) - 1)
        def _():
            hx = x_ref[...] + 0.5 * acc_ref[0:tm, :]
            if emit_norm:
                ox_ref[...] = hx
                nx_ref[...] = _rms(hx, gn_ref[...]).astype(BF16)
            else:
                ox_ref[...] = _rms(hx, gn_ref[...])
            if n_meta:
                hm = m_ref[...] + 0.5 * acc_ref[tm:tm + n_meta, :]
                om_ref[...] = hm
                nm_ref[...] = _rms(hm, gn_ref[...]).astype(BF16)

    if head:
        @pl.when((i == 0) & (j == 0))
        def _():
            ox_ref[...] = hx_ref[...]
            nx_ref[...] = hn16_ref[...]

        pl.when(i > 0)(body)
    else:
        body()


def _ffn(hx, hm, g, g_next, wg, wu, wd, *, emit_norm, head=None, cast=(), rows=None, name,
         tm=512, tf=512):
    n, d = hx.shape
    rows = n if rows is None else rows
    dff = wg.shape[1]
    n_meta = 0 if hm is None else hm.shape[0]
    normed = g.ndim == 2
    w_f32 = wg.dtype == F32
    grid = (rows // tm, dff // tf)
    assert not w_f32 or grid[0] == 1
    assert not (normed and n_meta)
    row = lambda i, j: (i, 0)
    const = lambda i, j: (0, 0)
    in_specs = [pl.BlockSpec((tm, d), row)]
    args = [hx]
    if n_meta:
        in_specs.append(pl.BlockSpec((n_meta, d), const))
        args.append(hm)
    col = (lambda i, j: jnp.where(i == 0, 0, j)) if head else (lambda i, j: j)
    in_specs += [pl.BlockSpec((tm, d), row) if normed else pl.BlockSpec((1, d), const),
                 pl.BlockSpec((1, d), const),
                 pl.BlockSpec((d, tf), lambda i, j: (0, col(i, j))),
                 pl.BlockSpec((d, tf), lambda i, j: (0, col(i, j))),
                 pl.BlockSpec((tf, d), lambda i, j: (col(i, j), 0))]
    args += [g if normed else g.reshape(1, d), g_next.reshape(1, d), wg, wu, wd]
    if head:
        in_specs += [pl.BlockSpec((tm, d), const, pipeline_mode=pl.Buffered(1))] * 2
        args += list(head)
    job_in, job_out, job_shapes, jobs = _cast_jobs(cast, grid)
    in_specs += job_in
    args += [src for src, _, _ in cast]
    out_shape = [jax.ShapeDtypeStruct((rows, d), F32)]
    out_specs = [pl.BlockSpec((tm, d), row)]
    if emit_norm:
        out_shape.append(jax.ShapeDtypeStruct((rows, d), BF16))
        out_specs.append(pl.BlockSpec((tm, d), row))
    if n_meta:
        out_shape += [jax.ShapeDtypeStruct((n_meta, d), F32), jax.ShapeDtypeStruct((n_meta, d), BF16)]
        out_specs += [pl.BlockSpec((n_meta, d), const)] * 2
    if w_f32:
        out_shape += [jax.ShapeDtypeStruct(w.shape, BF16) for w in (wg, wu, wd)]
        out_specs += [pl.BlockSpec((d, tf), lambda i, j: (0, j)),
                      pl.BlockSpec((d, tf), lambda i, j: (0, j)),
                      pl.BlockSpec((tf, d), lambda i, j: (j, 0))]
    out_specs += job_out
    out_shape += job_shapes
    return pl.pallas_call(
        functools.partial(_ffn_kernel, tm=tm, n_meta=n_meta, normed=normed, emit_norm=emit_norm,
                          w_f32=w_f32, head=head is not None, jobs=jobs),
        grid=grid, in_specs=in_specs, out_specs=out_specs, out_shape=out_shape,
        scratch_shapes=([] if normed else [pltpu.VMEM((tm + n_meta, d), BF16)])
        + ([pltpu.VMEM((tm + n_meta, d), F32)] if n_meta else []),
        compiler_params=_cparams(("arbitrary", "arbitrary")),
        name=name,
    )(*args)


def _inproj_kernel(*refs, tm, n_meta, with_lb, jobs):
    it = iter(refs)
    take = lambda k: [next(it) for _ in range(k)]
    x_ref, m_ref, w_ref, coef_ref = take(4)
    lb_ref = take(1)[0] if with_lb else None
    cast_in = take(len(jobs))
    ox_ref, om_ref = take(2)
    cast_out = take(_n_cast_out(jobs))
    lhs_ref, = take(1)
    _run_cast_jobs(cast_in, cast_out, jobs)

    @pl.when(pl.program_id(1) == 0)
    def _():
        lhs_ref[0:tm, :] = x_ref[...]
        lhs_ref[tm:tm + n_meta, :] = m_ref[...]

    tn = w_ref.shape[1]
    sw = math.gcd(tn, INPROJ_SUB)
    for s in range(tn // sw):
        cs = slice(s * sw, (s + 1) * sw)
        res = _dot(lhs_ref[...], w_ref[:, cs])
        coef = coef_ref[:, cs]
        a, c, b, d = coef[0:1, :], coef[1:2, :], coef[2:3, :], 0.0
        if with_lb:
            lbl = lb_ref[:, cs]
            e = jnp.exp(lbl - jnp.max(lbl, axis=0, keepdims=True))
            lb = e[0:1, :] / jnp.sum(e, axis=0, keepdims=True)
            flag = coef[3:4, :]
            b = b + flag * (1.0 - lb)
            d = flag * lb
        t = jnp.tanh(0.5 * res)
        out = res * ((0.5 * a) * t + (0.5 * a + c)) + ((0.5 * b) * t + (0.5 * b + d))
        ox_ref[:, cs] = out[0:tm, :].astype(ox_ref.dtype)
        om_ref[:, cs] = out[tm:tm + n_meta, :].astype(om_ref.dtype)


def _inproj_grid(n, cols, tm=1024, tn=1792):
    tm = min(tm, n)
    tn = math.gcd(tn, cols)
    assert tn % LANE == 0 and n % tm == 0
    return tm, tn, (n // tm, cols // tn)


def _inproj(nx, nm, w, coef, lower_bounds, lb_cols, dtype, cast=()):
    n, d = nx.shape
    n_meta = nm.shape[0]
    cols = w.shape[1]
    with_lb = lower_bounds is not None
    tm, tn, grid = _inproj_grid(n, cols)
    job_in, job_out, job_shapes, jobs = _cast_jobs(cast, grid)
    in_specs = [pl.BlockSpec((tm, d), lambda i, j: (i, 0)),
                pl.BlockSpec((n_meta, d), lambda i, j: (0, 0)),
                pl.BlockSpec((d, tn), lambda i, j: (0, j)),
                pl.BlockSpec((4, tn), lambda i, j: (0, j))]
    args = [nx, nm, w, coef]
    if with_lb:
        nlb, k = lower_bounds.shape
        lb_full = jnp.pad(lower_bounds.astype(F32), ((0, 0), (lb_cols, cols - lb_cols - k)))
        in_specs.append(pl.BlockSpec((nlb, tn), lambda i, j: (0, j)))
        args.append(lb_full)
    ox, om, *copies = pl.pallas_call(
        functools.partial(_inproj_kernel, tm=tm, n_meta=n_meta, with_lb=with_lb, jobs=jobs),
        grid=grid,
        in_specs=in_specs + job_in,
        out_specs=[pl.BlockSpec((tm, tn), lambda i, j: (i, j)),
                   pl.BlockSpec((pl.Squeezed(), n_meta, tn), lambda i, j: (i, 0, j))] + job_out,
        out_shape=[jax.ShapeDtypeStruct((n, cols), dtype),
                   jax.ShapeDtypeStruct((n // tm, n_meta, cols), dtype)] + job_shapes,
        scratch_shapes=[pltpu.VMEM((tm + n_meta, d), BF16)],
        compiler_params=_cparams(("arbitrary", "arbitrary")),
        name="mixer_inproj_" + jnp.dtype(dtype).name,
    )(*args, *[src for src, _, _ in cast])
    return (ox, om[0], *copies)


def _split3(x):
    hi = x.astype(BF16)
    r1 = x - hi.astype(F32)
    mid = r1.astype(BF16)
    lo = (r1 - mid.astype(F32)).astype(BF16)
    return hi, mid, lo


def _dot3(m01, parts):
    return _dot(m01, parts[0]) + _dot(m01, parts[1]) + _dot(m01, parts[2])


def _tri(n, rep=1):
    r = lax.broadcasted_iota(jnp.int32, (n * rep, n), 0)
    c = lax.broadcasted_iota(jnp.int32, (n * rep, n), 1)
    return jnp.where(r >= c * rep, 1.0, 0.0).astype(BF16)


def _hgrn_tables(c):
    sums = [np.tril(np.ones((c, c)))]
    level = np.full((c, c), -1, np.int32)
    t_idx, s_idx = np.meshgrid(np.arange(c), np.arange(c), indexing="ij")
    halves = []
    h = c // 2
    while h >= 1:
        if h < 8:
            m = np.zeros((c, c))
            for r in range(c):
                mid = (r // (2 * h)) * 2 * h + h
                if r < mid:
                    m[r, r + 1:mid] = 1.0
                else:
                    m[r, mid:r + 1] = 1.0
            sums.append(m)
        pair = 2 * h
        level[(t_idx // pair == s_idx // pair) & (s_idx % pair < h) & (t_idx % pair >= h)] = len(halves)
        halves.append(h)
        h //= 2
    assert len(halves) % 2 == 0
    level[np.arange(c), np.arange(c)] = len(halves)
    owner = np.full((c, 2 * c), -1, np.int32)
    owner[:, :c] = np.where((level >= 0) & (level % 2 == 0), level // 2, -1)
    owner[:, c:] = np.where(level % 2 == 1, level // 2, -1)
    return np.concatenate(sums, axis=0), owner, tuple(halves)


def _hgrn_score_products(q, k, x, halves):
    C = HG_CHUNK
    b = x[0:C]
    zero = jnp.zeros((C, HG_DK), BF16)
    diag = jnp.sum(q * k, axis=-1, keepdims=True)
    qts, kts = [], []
    fine = 0
    for h in halves:
        if h >= 8:
            none = jnp.zeros((h, HG_DK), F32)
            qh, kh = [], []
            for r in range(0, C, 2 * h):
                b_mid = b[r + h - 1:r + h, :]
                kh += [k[r:r + h] * jnp.exp(b_mid - b[r:r + h]), none]
                qh += [none, q[r + h:r + 2 * h] * jnp.exp(b[r + h:r + 2 * h] - b_mid)]
            qts.append(jnp.concatenate(qh, axis=0).astype(BF16))
            kts.append(jnp.concatenate(kh, axis=0).astype(BF16))
        else:
            fine += 1
            e = jnp.exp(x[fine * C:(fine + 1) * C])
            qts.append((q * e).astype(BF16))
            kts.append((k * e).astype(BF16))
    prods = []
    for p in range(len(halves) // 2):
        lhs = jnp.concatenate([qts[2 * p], qts[2 * p + 1]], axis=1)
        rhs = jnp.concatenate([jnp.concatenate([kts[2 * p], zero], axis=1),
                               jnp.concatenate([zero, kts[2 * p + 1]], axis=1)], axis=0)
        prods.append(_dot_nt(lhs, rhs))
    return prods, diag


def _hgrn_pair(q2, f2, v2, gate2, st2, sums, owns, halves):
    C, D = HG_CHUNK, HG_DK
    halfs = (slice(0, D), slice(D, 2 * D))
    k2 = 1.0 - f2
    x2 = _dot(sums, jnp.concatenate(_split3(jnp.log(f2)), axis=0))
    yield
    b2 = x2[0:C]
    b_last = b2[C - 1:C, :]
    kdec2 = (k2 * jnp.exp(b_last - b2)).astype(BF16)
    qe2 = (q2 * jnp.exp(b2)).astype(BF16)
    st2b = st2.astype(BF16)
    scores = []
    for c in halfs:
        scores.append(_hgrn_score_products(q2[:, c], k2[:, c], x2[:, c], halves))
        yield
    zero = jnp.zeros((2 * C, D), BF16)
    os, vvts, kds = [], [], []
    for j, c in enumerate(halfs):
        prods, diag = scores[j]
        a = jnp.where(owns[len(prods)], diag, 0.0)
        for p, prod in enumerate(prods):
            a = jnp.where(owns[p], prod, a)
        v = v2[:, c]
        vvt = jnp.concatenate([v, v], axis=0).T
        os.append(_dot_nt(jnp.concatenate([qe2[:, c], a.astype(BF16)], axis=1),
                          jnp.concatenate([st2b[:, c], vvt], axis=1)))
        vvts.append(vvt)
        kd = jnp.concatenate([kdec2[:, c], jnp.zeros((C, D), BF16)], axis=0)
        kds.append(jnp.concatenate([kd, zero] if j == 0 else [zero, kd], axis=1))
        yield
    st_new = st2 * jnp.exp(b_last) + _dot(jnp.concatenate(vvts, axis=1), jnp.concatenate(kds, axis=0))
    yield
    ons = [o * lax.rsqrt(jnp.mean(o * o, axis=-1, keepdims=True) + EPS) for o in os]
    on2 = jnp.concatenate(ons, axis=1) * gate2
    return on2.astype(BF16), st_new


def _interleave(gens):
    out = [None] * len(gens)
    live = list(range(len(gens)))
    while live:
        for g in list(live):
            try:
                next(gens[g])
            except StopIteration as stop:
                out[g] = stop.value
                live.remove(g)
    return out


def _hgrn_kernel(q_ref, f_ref, v_ref, go_ref, fm_ref, vm_ref, sums_ref, owner_ref,
                 *rest, n_chunks, n_meta, hb, halves, jobs):
    cast_in, o_ref = rest[:len(jobs)], rest[len(jobs)]
    cast_out, st_ref = rest[len(jobs) + 1:-1], rest[-1]
    _run_cast_jobs(cast_in, cast_out, jobs)
    i = pl.program_id(0)
    p0 = pl.program_id(1) * (hb // 2)
    C = HG_CHUNK
    W = 2 * HG_DK

    @pl.when(i == 0)
    def _():
        tri_m = _tri(n_meta)
        for pair in range(hb // 2):
            sts = []
            for hh in (2 * pair, 2 * pair + 1):
                cols = slice(hh * HG_DK, (hh + 1) * HG_DK)
                fm = fm_ref[:, cols]
                bm = _dot3(tri_m, _split3(jnp.log(fm)))
                kdec = ((1.0 - fm) * jnp.exp(bm[n_meta - 1:n_meta, :] - bm)).astype(BF16)
                sts.append(_dot_tn(vm_ref[:, cols], kdec))
            st_ref[p0 + pair] = jnp.concatenate(sts, axis=1)

    def chunk(c, states):
        r0 = pl.multiple_of(c * C, C)
        owner = owner_ref[...]
        owns = [owner == p for p in range(len(halves) // 2 + 1)]
        sums = sums_ref[...]
        gens = []
        for pair in range(hb // 2):
            cols = slice(pair * W, (pair + 1) * W)
            gate2 = go_ref[pl.ds(r0, C), cols].astype(F32)
            gens.append(_hgrn_pair(q_ref[pl.ds(r0, C), cols], f_ref[pl.ds(r0, C), cols],
                                   v_ref[pl.ds(r0, C), cols], gate2, states[pair],
                                   sums, owns, halves))
        results = _interleave(gens)
        o_ref[pl.ds(r0, C), :] = jnp.concatenate([r[0] for r in results], axis=1)
        return tuple(r[1] for r in results)

    states = lax.fori_loop(0, n_chunks, chunk, tuple(st_ref[p0 + p] for p in range(hb // 2)))
    for pair in range(hb // 2):
        st_ref[p0 + pair] = states[pair]


def _hgrn_grid(n, width, rb=512, hb=12):
    heads = width // HG_DK
    hb = min(hb, heads)
    assert hb % 2 == 0 and heads % hb == 0
    return rb, hb, (n // rb, heads // hb)


def _hgrn(q, f, v, go, fm, vm, cast=()):
    n, width = q[0].shape[0], fm.shape[1]
    heads = width // HG_DK
    rb, hb, grid = _hgrn_grid(n, width)
    bw = hb * HG_DK
    n_meta = fm.shape[0]
    job_in, job_out, job_shapes, jobs = _cast_jobs(cast, grid)
    sums, owner, halves = _hgrn_tables(HG_CHUNK)
    sums3 = jnp.asarray(np.concatenate([sums] * 3, axis=1), BF16)
    blk = lambda i, h: (i, h)
    mblk = lambda i, h: (0, h)
    const = lambda i, h: (0, 0)

    def view(col0):
        assert col0 % bw == 0
        return pl.BlockSpec((rb, bw), lambda i, h: (i, col0 // bw + h))

    return pl.pallas_call(
        functools.partial(_hgrn_kernel, n_chunks=rb // HG_CHUNK, n_meta=n_meta, hb=hb,
                          halves=halves, jobs=jobs),
        grid=grid,
        in_specs=[view(q[1]), view(f[1]), view(v[1]), view(go[1])]
        + [pl.BlockSpec((n_meta, hb * HG_DK), mblk)] * 2
        + [pl.BlockSpec(sums3.shape, const), pl.BlockSpec(owner.shape, const)] + job_in,
        out_specs=[pl.BlockSpec((rb, hb * HG_DK), blk)] + job_out,
        out_shape=[jax.ShapeDtypeStruct((n, width), BF16)] + job_shapes,
        scratch_shapes=[pltpu.VMEM((heads // 2, HG_DK, 2 * HG_DK), F32)],
        compiler_params=_cparams(("arbitrary", "arbitrary")),
        name="hgrn2",
    )(q[0], f[0], v[0], go[0], fm, vm, sums3, jnp.asarray(owner), *[src for src, _, _ in cast])


def _s5_weights_kernel(lr_ref, li_ref, dt_ref, bre_ref, bim_ref, cre_ref, cim_ref, dsk_ref,
                       kt_ref, win_ref, wout_ref, at_ref):
    T = S5_T
    ns = lr_ref.shape[1]
    nc = cre_ref.shape[0]
    lr, li, dt = lr_ref[...], li_ref[...], dt_ref[...]

    def powers(tau):
        mag = jnp.exp(lr * dt * tau)
        ang = li * dt * tau
        return mag * jnp.cos(ang), mag * jnp.sin(ang)

    a_re, a_im = powers(1.0)
    num_re, num_im = a_re - 1.0, a_im
    den = lr * lr + li * li
    coef_re = (num_re * lr + num_im * li) / den
    coef_im = (num_im * lr - num_re * li) / den
    br, bi = bre_ref[...], bim_ref[...]
    bbar_re = coef_re * br - coef_im * bi
    bbar_im = coef_re * bi + coef_im * br

    pw = [powers(float(tau)) for tau in range(T + 1)]
    for s in range(T):
        p_re, p_im = pw[T - 1 - s]
        win_ref[s * nc:(s + 1) * nc, 0:ns] = (bbar_re * p_re - bbar_im * p_im).astype(BF16)
        win_ref[s * nc:(s + 1) * nc, ns:2 * ns] = (bbar_re * p_im + bbar_im * p_re).astype(BF16)
    at_ref[:, 0:ns] = pw[T][0]
    at_ref[:, ns:2 * ns] = pw[T][1]

    cr, ci = cre_ref[...], cim_ref[...]
    def split2(a):
        hi = a.astype(BF16)
        return hi, (a - hi.astype(F32)).astype(BF16)

    bb_hi, bb_lo = split2(jnp.concatenate([bbar_re, -bbar_im], axis=1))
    for tau in range(T + 1):
        p_re, p_im = pw[tau]
        ca_re = cr * p_re - ci * p_im
        ca_im = cr * p_im + ci * p_re
        if tau >= 1:
            t = tau - 1
            wout_ref[0:ns, t * nc:(t + 1) * nc] = ca_re.T.astype(BF16)
            wout_ref[ns:2 * ns, t * nc:(t + 1) * nc] = (-ca_im).T.astype(BF16)
        if tau < T:
            ca_hi, ca_lo = split2(jnp.concatenate([ca_re, ca_im], axis=1))
            kt = _dot_nt(bb_hi, ca_hi) + _dot_nt(bb_hi, ca_lo) + _dot_nt(bb_lo, ca_hi)
            if tau == 0:
                kt = kt + dsk_ref[...]
            kt_ref[tau] = kt.astype(BF16)


def _s5_weights(lam_re, lam_im, log_step, b_re, b_im, c_re, c_im, d_skip):
    G, N = lam_re.shape
    P = d_skip.shape[1]
    nsg = G // S5_SG
    ns, nc = S5_SG * N, S5_SG * P
    T = S5_T
    dt = jnp.exp(log_step.astype(F32))
    eye = jnp.eye(S5_SG, dtype=F32)

    def rows(a):
        return a.astype(F32).reshape(nsg, 1, ns)

    def embed_b(b):
        b = b.astype(F32).reshape(nsg, S5_SG, N, P)
        return jnp.einsum('zgnq,gh->zhqgn', b, eye).reshape(nsg, nc, ns)

    def embed_c(c):
        c = c.astype(F32).reshape(nsg, S5_SG, P, N)
        return jnp.einsum('zgpn,gh->zhpgn', c, eye).reshape(nsg, nc, ns)

    dtb = jnp.broadcast_to(dt[:, None], (G, N))
    d_diag = jax.vmap(jnp.diag)(d_skip.astype(F32).reshape(nsg, nc))
    sq = pl.Squeezed()
    spec_r = pl.BlockSpec((sq, 1, ns), lambda z: (z, 0, 0))
    spec_b = pl.BlockSpec((sq, nc, ns), lambda z: (z, 0, 0))
    return pl.pallas_call(
        _s5_weights_kernel,
        grid=(nsg,),
        in_specs=[spec_r, spec_r, spec_r, spec_b, spec_b, spec_b, spec_b,
                  pl.BlockSpec((sq, nc, nc), lambda z: (z, 0, 0))],
        out_specs=[pl.BlockSpec((sq, T, nc, nc), lambda z: (z, 0, 0, 0)),
                   pl.BlockSpec((sq, T * nc, 2 * ns), lambda z: (z, 0, 0)),
                   pl.BlockSpec((sq, 2 * ns, T * nc), lambda z: (z, 0, 0)),
                   pl.BlockSpec((sq, 1, 2 * ns), lambda z: (z, 0, 0))],
        out_shape=[jax.ShapeDtypeStruct((nsg, T, nc, nc), BF16),
                   jax.ShapeDtypeStruct((nsg, T * nc, 2 * ns), BF16),
                   jax.ShapeDtypeStruct((nsg, 2 * ns, T * nc), BF16),
                   jax.ShapeDtypeStruct((nsg, 1, 2 * ns), F32)],
        compiler_params=_cparams(("arbitrary",)),
        name="s5_weights",
    )(rows(lam_re), rows(lam_im), rows(dtb),
      embed_b(b_re), embed_b(b_im), embed_c(c_re), embed_c(c_im), d_diag)


def _s5_scan_kernel(u_ref, um_ref, kt_ref, win_ref, wout_ref, at_ref, y_ref,
                    uc_ref, v_ref, xp_ref, m_ref, *, n_chunks):
    T = S5_T
    nc = u_ref.shape[1]
    ns = at_ref.shape[1] // 2

    @pl.when(pl.program_id(0) == 0)
    def _():
        zero = jnp.zeros((nc, nc), BF16)
        for s in range(1, T):
            for t in range(s):
                m_ref[s * nc:(s + 1) * nc, t * nc:(t + 1) * nc] = zero

    for tau in range(T):
        tap = kt_ref[tau]
        for s in range(T - tau):
            m_ref[s * nc:(s + 1) * nc, (s + tau) * nc:(s + tau + 1) * nc] = tap

    for t in range(T):
        uc_ref[0:n_chunks, t * nc:(t + 1) * nc] = u_ref[pl.ds(t, n_chunks, stride=T), :].astype(BF16)
        uc_ref[n_chunks:n_chunks + 1, t * nc:(t + 1) * nc] = um_ref[t:t + 1, :].astype(BF16)
    pad = uc_ref.shape[0] - n_chunks - 1
    uc_ref[n_chunks + 1:, :] = jnp.zeros((pad, T * nc), BF16)
    uc = uc_ref[...]
    v_ref[...] = _dot(uc, win_ref[...])
    a_re, a_im = at_ref[:, 0:ns], at_ref[:, ns:2 * ns]

    x_re0 = v_ref[n_chunks:n_chunks + 1, 0:ns]
    x_im0 = v_ref[n_chunks:n_chunks + 1, ns:2 * ns]
    xp_ref[n_chunks:, :] = jnp.zeros((pad + 1, 2 * ns), F32)

    def step(c, carry):
        x_re, x_im = carry
        xp_ref[pl.ds(c, 1), 0:ns] = x_re
        xp_ref[pl.ds(c, 1), ns:2 * ns] = x_im
        v_re = v_ref[pl.ds(c, 1), 0:ns]
        v_im = v_ref[pl.ds(c, 1), ns:2 * ns]
        return (a_re * x_re - a_im * x_im + v_re, a_re * x_im + a_im * x_re + v_im)

    lax.fori_loop(0, n_chunks, step, (x_re0, x_im0))
    xp = xp_ref[...].astype(BF16)
    for p in range(T // 2):
        k_hi = (2 * p + 2) * nc
        cols = slice(2 * p * nc, k_hi)
        yc = _dot(uc_ref[:, 0:k_hi], m_ref[0:k_hi, cols]) + _dot(xp, wout_ref[:, cols])
        for t in (2 * p, 2 * p + 1):
            y_ref[pl.ds(t, n_chunks, stride=T), :] = yc[0:n_chunks, (t - 2 * p) * nc:(t - 2 * p + 1) * nc]


def _s5_scan(u, um, kt, win, wout, at):
    u, col0 = u
    n, width = u.shape[0], um.shape[1]
    T = S5_T
    nsg = kt.shape[0]
    nc = width // nsg
    ns2 = at.shape[2]
    assert um.shape[0] == T and n % T == 0 and col0 % nc == 0
    n_chunks = n // T
    rows = -(-(n_chunks + 1) // 16) * 16
    sq = pl.Squeezed()
    return pl.pallas_call(
        functools.partial(_s5_scan_kernel, n_chunks=n_chunks),
        grid=(nsg,),
        in_specs=[pl.BlockSpec((n, nc), lambda z: (0, col0 // nc + z)),
                  pl.BlockSpec((T, nc), lambda z: (0, z)),
                  pl.BlockSpec((sq, T, nc, nc), lambda z: (z, 0, 0, 0)),
                  pl.BlockSpec((sq, T * nc, ns2), lambda z: (z, 0, 0)),
                  pl.BlockSpec((sq, ns2, T * nc), lambda z: (z, 0, 0)),
                  pl.BlockSpec((sq, 1, ns2), lambda z: (z, 0, 0))],
        out_specs=pl.BlockSpec((n, nc), lambda z: (0, z)),
        out_shape=jax.ShapeDtypeStruct((n, width), F32),
        scratch_shapes=[pltpu.VMEM((rows, T * nc), BF16), pltpu.VMEM((rows, ns2), F32),
                        pltpu.VMEM((rows, ns2), F32), pltpu.VMEM((T * nc, T * nc), BF16)],
        compiler_params=_cparams(("arbitrary",)),
        name="s5_scan",
    )(u, um, kt, win, wout, at)


def _merge_kernel(*refs, pieces):
    h_ref, o_ref, y_ref = refs[0:3]
    gh_refs, gs_refs = refs[3:3 + pieces], refs[3 + pieces:3 + 2 * pieces]
    wp_ref, wa_ref, wb_ref, wo_ref, gn_ref, out_ref, norm_ref = refs[3 + 2 * pieces:]
    gh = jnp.concatenate([r[...] for r in gh_refs], axis=1).astype(F32)
    gs = jnp.concatenate([r[...] for r in gs_refs], axis=1).astype(F32)
    y_hg = _dot(o_ref[...], wp_ref[...])
    z = jax.nn.gelu(y_ref[...]).astype(BF16)
    y_s5 = _dot(z, wa_ref[...]) * _sigmoid(_dot(z, wb_ref[...]))
    merged = gh * y_hg + gs * y_s5
    h = h_ref[...] + _dot(merged.astype(BF16), wo_ref[...])
    out_ref[...] = h
    norm_ref[...] = _rms(h, gn_ref[...]).astype(BF16)


def _merge(hx, o, y, gh, gs, wp, wa, wb, wo, g_next, *, tm=256):
    n, d = hx.shape
    row = lambda i: (i, 0)
    const = lambda i: (0, 0)
    gw = functools.reduce(math.gcd, (gh[1], gs[1], d))
    assert gw % LANE == 0
    pieces = d // gw

    def resident(shape):
        return pl.BlockSpec(shape, const, pipeline_mode=pl.Buffered(1))

    def view(col0):
        return [pl.BlockSpec((tm, gw), functools.partial(lambda i, k: (i, k), k=col0 // gw + k))
                for k in range(pieces)]

    return pl.pallas_call(
        functools.partial(_merge_kernel, pieces=pieces),
        grid=(n // tm,),
        in_specs=[pl.BlockSpec((tm, d), row), pl.BlockSpec((tm, o.shape[1]), row),
                  pl.BlockSpec((tm, y.shape[1]), row)] + view(gh[1]) + view(gs[1])
        + [resident(wp.shape), resident(wa.shape), resident(wb.shape), resident(wo.shape),
           pl.BlockSpec((1, d), const)],
        out_specs=[pl.BlockSpec((tm, d), row)] * 2,
        out_shape=[jax.ShapeDtypeStruct((n, d), F32), jax.ShapeDtypeStruct((n, d), BF16)],
        compiler_params=_cparams(("arbitrary",)),
        name="mixer_merge",
    )(hx, o, y, *([gh[0]] * pieces), *([gs[0]] * pieces), wp, wa, wb, wo, g_next.reshape(1, d))


def kernel(x, meta_tokens, norm_ffn1, ffn1_w_gate, ffn1_w_up, ffn1_w_down, norm_mix, w_in, hg_lower_bounds, hg_norm, hg_w_proj, s5_lam_re, s5_lam_im, s5_log_step, s5_b_re, s5_b_im, s5_c_re, s5_c_im, s5_d, s5_w_glu_a, s5_w_glu_b, w_out, norm_ffn2, ffn2_w_gate, ffn2_w_up, ffn2_w_down, norm_final):
    batch, seq, d = x.shape
    depth = norm_ffn1.shape[0]
    assert batch == 1 and depth == 1
    hg_k = hg_lower_bounds.shape[1]
    hg_w = hg_norm.shape[1]
    s5_w = s5_w_glu_a.shape[1]
    hx = x[0].astype(F32)
    hm = meta_tokens.astype(F32)
    tm = min(512, seq)
    h0, n0, hm, nm, w1_gate, w1_up, w1_down = _ffn(
        hx, hm, norm_ffn1[0], norm_mix[0], ffn1_w_gate[0], ffn1_w_up[0], ffn1_w_down[0],
        emit_norm=True, rows=tm, tm=tm, tf=256, name="ffn_head")
    sizes = (hg_k, hg_k, hg_w, hg_w, s5_w, d, d)
    assert sum(sizes) == w_in.shape[2]
    off = [0]
    for s in sizes:
        off.append(off[-1] + s)
    ids32, ids16 = (0, 1, 4), (2, 3, 5, 6)
    tc_in = functools.reduce(math.gcd, sizes + (512,))

    def starts(ids):
        out, col = {}, 0
        for k in ids:
            out[k] = col
            col += sizes[k]
        return out

    def blocks(ids):
        return [c for k in ids for c in range(off[k] // tc_in, off[k + 1] // tc_in)]

    steps = lambda grid: grid[0] * grid[1]
    n_body = (seq // tm) * (ffn1_w_gate.shape[2] // 512)
    n_inp = steps(_inproj_grid(seq, sum(sizes[k] for k in ids16))[2])
    n_hg = steps(_hgrn_grid(seq, hg_w)[2])
    hx, nx, w32, w16 = _ffn(
        hx, None, norm_ffn1[0], norm_mix[0], w1_gate, w1_up, w1_down, emit_norm=True, head=(h0, n0),
        cast=(_cast_tile(w_in[0], n_body, tc_in, [blocks(ids32), blocks(ids16)]),),
        tm=tm, name="ffn_body")

    ident, sigm = (0.0, 1.0, 0.0, 0.0), (0.0, 0.0, 1.0, 0.0)
    rows = {0: (HG_DK ** -0.5, 0.0, 0.0, 0.0), 1: (0.0, 0.0, 0.0, 1.0), 2: ident,
            3: (hg_norm[0].astype(F32), 0.0, 0.0, 0.0), 4: ident, 5: sigm, 6: sigm}

    def coef(ids):
        return jnp.concatenate([jnp.stack([jnp.broadcast_to(jnp.asarray(v, F32), (sizes[k],))
                                           for v in rows[k]]) for k in ids], axis=1)

    s32, s16 = starts(ids32), starts(ids16)
    x32, m32 = _inproj(nx, nm, w32, coef(ids32), hg_lower_bounds, s32[1], F32)
    x16, m16, w_proj, w_glu_a, w_glu_b, w_o = _inproj(
        nx, nm, w16, coef(ids16), None, 0, BF16,
        cast=[_cast_tile(w, n_inp) for w in (hg_w_proj[0], s5_w_glu_a[0], s5_w_glu_b[0], w_out[0])])

    o, w2_gate, w2_up, w2_down = _hgrn(
        (x32, s32[0]), (x32, s32[1]), (x16, s16[2]), (x16, s16[3]),
        m32[:, s32[1]:s32[1] + hg_k], m16[:, s16[2]:s16[2] + hg_w],
        cast=[_cast_tile(w, n_hg) for w in (ffn2_w_gate[0], ffn2_w_up[0], ffn2_w_down[0])])
    m, win, wout, at = _s5_weights(s5_lam_re[0], s5_lam_im[0], s5_log_step[0], s5_b_re[0],
                                   s5_b_im[0], s5_c_re[0], s5_c_im[0], s5_d[0])
    y = _s5_scan((x32, s32[4]), m32[:, s32[4]:s32[4] + s5_w], m, win, wout, at)
    hx, nx2 = _merge(hx, o, y, (x16, s16[5]), (x16, s16[6]), w_proj, w_glu_a, w_glu_b, w_o,
                     norm_ffn2[0])
    out, = _ffn(hx, None, nx2, norm_final, w2_gate, w2_up, w2_down, emit_norm=False,
                tm=tm, name="ffn_final")
    return out[None].astype(x.dtype)
```

```python
import functools
import math

import numpy as np
import jax
import jax.numpy as jnp
from jax import lax
from jax.experimental import pallas as pl
from jax.experimental.pallas import tpu as pltpu

EPS = 1e-6
HG_DK = 128
HG_CHUNK = 64
S5_GROUP = 16
S5_STATE = 64
S5_T = 16
S5_SG = 8
LANE = 128
INPROJ_SUB = 256
VMEM_LIMIT = 56 * 1024 * 1024

F32 = jnp.float32
BF16 = jnp.bfloat16


def _cparams(sem):
    return pltpu.CompilerParams(dimension_semantics=sem, vmem_limit_bytes=VMEM_LIMIT)


def _rms(v, g):
    return v * lax.rsqrt(jnp.mean(v * v, axis=-1, keepdims=True) + EPS) * g


def _sigmoid(x):
    return 0.5 * jnp.tanh(0.5 * x) + 0.5


def _silu(x):
    h = 0.5 * x
    return h * (jnp.tanh(h) + 1.0)


def _dot(a, b):
    return jnp.dot(a, b, preferred_element_type=F32)


def _dot_nt(a, b):
    return lax.dot_general(a, b, (((1,), (1,)), ((), ())), preferred_element_type=F32)


def _dot_tn(a, b):
    return lax.dot_general(a, b, (((0,), (0,)), ((), ())), preferred_element_type=F32)


def _cast_jobs(cast, grid):
    nj = grid[1]
    in_specs, out_specs, out_shapes, static = [], [], [], []
    for src, (tr, tc), dests in cast:
        rows, cols = src.shape
        ncb = cols // tc
        n_tiles = (rows // tr) * ncb
        assert rows % tr == 0 and cols % tc == 0 and n_tiles <= grid[0] * nj

        def tile(i, j, n_tiles=n_tiles, ncb=ncb):
            t = jnp.minimum(i * nj + j, n_tiles - 1)
            return t // ncb, t % ncb

        in_specs.append(pl.BlockSpec((tr, tc), tile))
        for blocks in dests:
            park, k = [], 0
            for c in range(ncb):
                k = blocks.index(c) if c in blocks else k
                park.append(k)
            jumps = [(m, park[m] - park[m - 1]) for m in range(1, ncb) if park[m] != park[m - 1]]

            def dest(i, j, tile=tile, first=park[0], jumps=jumps):
                r, c = tile(i, j)
                return r, first + sum(jnp.where(c >= m, dk, 0) for m, dk in jumps)

            out_specs.append(pl.BlockSpec((tr, tc), tile if park == list(range(ncb)) else dest))
            out_shapes.append(jax.ShapeDtypeStruct((rows, len(blocks) * tc), BF16))
        static.append((n_tiles, ncb, tuple(tuple(b) for b in dests)))
    return in_specs, out_specs, out_shapes, tuple(static)


def _n_cast_out(jobs):
    return sum(len(dests) for _, _, dests in jobs)


def _run_cast_jobs(cast_in, cast_out, jobs):
    step = pl.program_id(0) * pl.num_programs(1) + pl.program_id(1)
    outs = iter(cast_out)
    for src, (n_tiles, ncb, dests) in zip(cast_in, jobs):
        dsts = [next(outs) for _ in dests]
        tile = src[...].astype(BF16)
        if len(dsts) == 1:
            dsts[0][...] = tile
        else:
            c = jnp.minimum(step, n_tiles - 1) % ncb
            for dst, cols in zip(dsts, dests):
                owns = functools.reduce(jnp.logical_or, [c == m for m in cols])

                @pl.when(owns)
                def _(dst=dst):
                    dst[...] = tile


def _cast_tile(w, n_steps, tc=None, dests=None):
    rows, cols = w.shape
    if tc is None and rows % n_steps == 0 and (rows // n_steps) % 16 == 0:
        return w, (rows // n_steps, cols), [[0]]
    tc = math.gcd(tc or 512, cols)
    tr = 128
    while (rows // tr) * (cols // tc) > n_steps:
        tr *= 2
    return w, (tr, tc), dests or [list(range(cols // tc))]


def _ffn_kernel(*refs, tm, n_meta, normed, emit_norm, w_f32, head, jobs):
    it = iter(refs)
    take = lambda k: [next(it) for _ in range(k)]
    x_ref, = take(1)
    m_ref = take(1)[0] if n_meta else None
    g_ref, gn_ref, wg_ref, wu_ref, wd_ref = take(5)
    hx_ref, hn16_ref = take(2) if head else (None, None)
    cast_in = take(len(jobs))
    ox_ref, = take(1)
    nx_ref = take(1)[0] if emit_norm else None
    om_ref, nm_ref = take(2) if n_meta else (None, None)
    w16_out = take(3) if w_f32 else ()
    cast_out = take(_n_cast_out(jobs))
    hn_ref = g_ref if normed else take(1)[0]
    acc_ref = take(1)[0] if n_meta else ox_ref
    i = pl.program_id(0)
    j = pl.program_id(1)
    _run_cast_jobs(cast_in, cast_out, jobs)

    def body():
        @pl.when(j == 0)
        def _():
            if not normed:
                hn_ref[0:tm, :] = _rms(x_ref[...], g_ref[...]).astype(BF16)
            if n_meta:
                hn_ref[tm:tm + n_meta, :] = _rms(m_ref[...], g_ref[...]).astype(BF16)
            acc_ref[...] = jnp.zeros_like(acc_ref)

        tiles = [w[...].astype(BF16) if w_f32 else w[...] for w in (wg_ref, wu_ref, wd_ref)]
        for dst, tile in zip(w16_out, tiles):
            dst[...] = tile
        hn = hn_ref[...]
        gate = _dot(hn, tiles[0])
        up = _dot(hn, tiles[1])
        act = (_silu(gate) * up).astype(BF16)
        acc_ref[...] += _dot(act, tiles[2])

        @pl.when(j == pl.num_programs(1) - 1)
        def _():
            hx = x_ref[...] + 0.5 * acc_ref[0:tm, :]
            if emit_norm:
                ox_ref[...] = hx
                nx_ref[...] = _rms(hx, gn_ref[...]).astype(BF16)
            else:
                ox_ref[...] = _rms(hx, gn_ref[...])
            if n_meta:
                hm = m_ref[...] + 0.5 * acc_ref[tm:tm + n_meta, :]
                om_ref[...] = hm
                nm_ref[...] = _rms(hm, gn_ref[...]).astype(BF16)

    if head:
        @pl.when((i == 0) & (j == 0))
        def _():
            ox_ref[...] = hx_ref[...]
            nx_ref[...] = hn16_ref[...]

        pl.when(i > 0)(body)
    else:
        body()


def _ffn(hx, hm, g, g_next, wg, wu, wd, *, emit_norm, head=None, cast=(), rows=None, name,
         tm=512, tf=512):
    n, d = hx.shape
    rows = n if rows is None else rows
    dff = wg.shape[1]
    n_meta = 0 if hm is None else hm.shape[0]
    normed = g.ndim == 2
    w_f32 = wg.dtype == F32
    grid = (rows // tm, dff // tf)
    assert not w_f32 or grid[0] == 1
    assert not (normed and n_meta)
    row = lambda i, j: (i, 0)
    const = lambda i, j: (0, 0)
    in_specs = [pl.BlockSpec((tm, d), row)]
    args = [hx]
    if n_meta:
        in_specs.append(pl.BlockSpec((n_meta, d), const))
        args.append(hm)
    col = (lambda i, j: jnp.where(i == 0, 0, j)) if head else (lambda i, j: j)
    in_specs += [pl.BlockSpec((tm, d), row) if normed else pl.BlockSpec((1, d), const),
                 pl.BlockSpec((1, d), const),
                 pl.BlockSpec((d, tf), lambda i, j: (0, col(i, j))),
                 pl.BlockSpec((d, tf), lambda i, j: (0, col(i, j))),
                 pl.BlockSpec((tf, d), lambda i, j: (col(i, j), 0))]
    args += [g if normed else g.reshape(1, d), g_next.reshape(1, d), wg, wu, wd]
    if head:
        in_specs += [pl.BlockSpec((tm, d), const, pipeline_mode=pl.Buffered(1))] * 2
        args += list(head)
    job_in, job_out, job_shapes, jobs = _cast_jobs(cast, grid)
    in_specs += job_in
    args += [src for src, _, _ in cast]
    out_shape = [jax.ShapeDtypeStruct((rows, d), F32)]
    out_specs = [pl.BlockSpec((tm, d), row)]
    if emit_norm:
        out_shape.append(jax.ShapeDtypeStruct((rows, d), BF16))
        out_specs.append(pl.BlockSpec((tm, d), row))
    if n_meta:
        out_shape += [jax.ShapeDtypeStruct((n_meta, d), F32), jax.ShapeDtypeStruct((n_meta, d), BF16)]
        out_specs += [pl.BlockSpec((n_meta, d), const)] * 2
    if w_f32:
        out_shape += [jax.ShapeDtypeStruct(w.shape, BF16) for w in (wg, wu, wd)]
        out_specs += [pl.BlockSpec((d, tf), lambda i, j: (0, j)),
                      pl.BlockSpec((d, tf), lambda i, j: (0, j)),
                      pl.BlockSpec((tf, d), lambda i, j: (j, 0))]
    out_specs += job_out
    out_shape += job_shapes
    return pl.pallas_call(
        functools.partial(_ffn_kernel, tm=tm, n_meta=n_meta, normed=normed, emit_norm=emit_norm,
                          w_f32=w_f32, head=head is not None, jobs=jobs),
        grid=grid, in_specs=in_specs, out_specs=out_specs, out_shape=out_shape,
        scratch_shapes=([] if normed else [pltpu.VMEM((tm + n_meta, d), BF16)])
        + ([pltpu.VMEM((tm + n_meta, d), F32)] if n_meta else []),
        compiler_params=_cparams(("arbitrary", "arbitrary")),
        name=name,
    )(*args)


def _inproj_kernel(*refs, tm, n_meta, with_lb, jobs):
    it = iter(refs)
    take = lambda k: [next(it) for _ in range(k)]
    x_ref, m_ref, w_ref, coef_ref = take(4)
    lb_ref = take(1)[0] if with_lb else None
    cast_in = take(len(jobs))
    ox_ref, om_ref = take(2)
    cast_out = take(_n_cast_out(jobs))
    lhs_ref, = take(1)
    _run_cast_jobs(cast_in, cast_out, jobs)

    @pl.when(pl.program_id(1) == 0)
    def _():
        lhs_ref[0:tm, :] = x_ref[...]
        lhs_ref[tm:tm + n_meta, :] = m_ref[...]

    tn = w_ref.shape[1]
    sw = math.gcd(tn, INPROJ_SUB)
    for s in range(tn // sw):
        cs = slice(s * sw, (s + 1) * sw)
        res = _dot(lhs_ref[...], w_ref[:, cs])
        coef = coef_ref[:, cs]
        a, c, b, d = coef[0:1, :], coef[1:2, :], coef[2:3, :], 0.0
        if with_lb:
            lbl = lb_ref[:, cs]
            e = jnp.exp(lbl - jnp.max(lbl, axis=0, keepdims=True))
            lb = e[0:1, :] / jnp.sum(e, axis=0, keepdims=True)
            flag = coef[3:4, :]
            b = b + flag * (1.0 - lb)
            d = flag * lb
        t = jnp.tanh(0.5 * res)
        rows = [0.5 * a, 0.5 * a + c, 0.5 * b, 0.5 * b + d]
        if ox_ref.dtype == BF16:
            res, t = res.astype(BF16), t.astype(BF16)
            rows = [r.astype(BF16) for r in rows]
        out = res * (rows[0] * t + rows[1]) + (rows[2] * t + rows[3])
        ox_ref[:, cs] = out[0:tm, :].astype(ox_ref.dtype)
        om_ref[:, cs] = out[tm:tm + n_meta, :].astype(om_ref.dtype)


def _inproj_grid(n, cols, tm=1024, tn=1792):
    tm = min(tm, n)
    tn = math.gcd(tn, cols)
    assert tn % LANE == 0 and n % tm == 0
    return tm, tn, (n // tm, cols // tn)


def _inproj(nx, nm, w, coef, lower_bounds, lb_cols, dtype, cast=()):
    n, d = nx.shape
    n_meta = nm.shape[0]
    cols = w.shape[1]
    with_lb = lower_bounds is not None
    tm, tn, grid = _inproj_grid(n, cols)
    job_in, job_out, job_shapes, jobs = _cast_jobs(cast, grid)
    in_specs = [pl.BlockSpec((tm, d), lambda i, j: (i, 0)),
                pl.BlockSpec((n_meta, d), lambda i, j: (0, 0)),
                pl.BlockSpec((d, tn), lambda i, j: (0, j)),
                pl.BlockSpec((4, tn), lambda i, j: (0, j))]
    args = [nx, nm, w, coef]
    if with_lb:
        nlb, k = lower_bounds.shape
        lb_full = jnp.pad(lower_bounds.astype(F32), ((0, 0), (lb_cols, cols - lb_cols - k)))
        in_specs.append(pl.BlockSpec((nlb, tn), lambda i, j: (0, j)))
        args.append(lb_full)
    ox, om, *copies = pl.pallas_call(
        functools.partial(_inproj_kernel, tm=tm, n_meta=n_meta, with_lb=with_lb, jobs=jobs),
        grid=grid,
        in_specs=in_specs + job_in,
        out_specs=[pl.BlockSpec((tm, tn), lambda i, j: (i, j)),
                   pl.BlockSpec((pl.Squeezed(), n_meta, tn), lambda i, j: (i, 0, j))] + job_out,
        out_shape=[jax.ShapeDtypeStruct((n, cols), dtype),
                   jax.ShapeDtypeStruct((n // tm, n_meta, cols), dtype)] + job_shapes,
        scratch_shapes=[pltpu.VMEM((tm + n_meta, d), BF16)],
        compiler_params=_cparams(("arbitrary", "arbitrary")),
        name="mixer_inproj_" + jnp.dtype(dtype).name,
    )(*args, *[src for src, _, _ in cast])
    return (ox, om[0], *copies)


def _split3(x):
    hi = x.astype(BF16)
    r1 = x - hi.astype(F32)
    mid = r1.astype(BF16)
    lo = (r1 - mid.astype(F32)).astype(BF16)
    return hi, mid, lo


def _dot3(m01, parts):
    return _dot(m01, parts[0]) + _dot(m01, parts[1]) + _dot(m01, parts[2])


def _tri(n, rep=1):
    r = lax.broadcasted_iota(jnp.int32, (n * rep, n), 0)
    c = lax.broadcasted_iota(jnp.int32, (n * rep, n), 1)
    return jnp.where(r >= c * rep, 1.0, 0.0).astype(BF16)


def _hgrn_tables(c):
    sums = [np.tril(np.ones((c, c)))]
    level = np.full((c, c), -1, np.int32)
    t_idx, s_idx = np.meshgrid(np.arange(c), np.arange(c), indexing="ij")
    halves = []
    h = c // 2
    while h >= 1:
        if h < 8:
            m = np.zeros((c, c))
            for r in range(c):
                mid = (r // (2 * h)) * 2 * h + h
                if r < mid:
                    m[r, r + 1:mid] = 1.0
                else:
                    m[r, mid:r + 1] = 1.0
            sums.append(m)
        pair = 2 * h
        level[(t_idx // pair == s_idx // pair) & (s_idx % pair < h) & (t_idx % pair >= h)] = len(halves)
        halves.append(h)
        h //= 2
    assert len(halves) % 2 == 0
    level[np.arange(c), np.arange(c)] = len(halves)
    owner = np.full((c, 2 * c), -1, np.int32)
    owner[:, :c] = np.where((level >= 0) & (level % 2 == 0), level // 2, -1)
    owner[:, c:] = np.where(level % 2 == 1, level // 2, -1)
    return np.concatenate(sums, axis=0), owner, tuple(halves)


def _hgrn_score_products(q, k, x, halves):
    C = HG_CHUNK
    b = x[0:C]
    zero = jnp.zeros((C, HG_DK), BF16)
    diag = jnp.sum(q * k, axis=-1, keepdims=True)
    qts, kts = [], []
    fine = 0
    for h in halves:
        if h >= 8:
            none = jnp.zeros((h, HG_DK), F32)
            qh, kh = [], []
            for r in range(0, C, 2 * h):
                b_mid = b[r + h - 1:r + h, :]
                kh += [k[r:r + h] * jnp.exp(b_mid - b[r:r + h]), none]
                qh += [none, q[r + h:r + 2 * h] * jnp.exp(b[r + h:r + 2 * h] - b_mid)]
            qts.append(jnp.concatenate(qh, axis=0).astype(BF16))
            kts.append(jnp.concatenate(kh, axis=0).astype(BF16))
        else:
            fine += 1
            e = jnp.exp(x[fine * C:(fine + 1) * C])
            qts.append((q * e).astype(BF16))
            kts.append((k * e).astype(BF16))
    prods = []
    for p in range(len(halves) // 2):
        lhs = jnp.concatenate([qts[2 * p], qts[2 * p + 1]], axis=1)
        rhs = jnp.concatenate([jnp.concatenate([kts[2 * p], zero], axis=1),
                               jnp.concatenate([zero, kts[2 * p + 1]], axis=1)], axis=0)
        prods.append(_dot_nt(lhs, rhs))
    return prods, diag


def _hgrn_pair(q2, f2, v2, gate2, st2, sums, owns, halves):
    C, D = HG_CHUNK, HG_DK
    halfs = (slice(0, D), slice(D, 2 * D))
    k2 = 1.0 - f2
    x2 = _dot(sums, jnp.concatenate(_split3(jnp.log(f2)), axis=0))
    yield
    b2 = x2[0:C]
    b_last = b2[C - 1:C, :]
    kdec2 = (k2 * jnp.exp(b_last - b2)).astype(BF16)
    qe2 = (q2 * jnp.exp(b2)).astype(BF16)
    st2b = st2.astype(BF16)
    scores = []
    for c in halfs:
        scores.append(_hgrn_score_products(q2[:, c], k2[:, c], x2[:, c], halves))
        yield
    zero = jnp.zeros((2 * C, D), BF16)
    os, vvts, kds = [], [], []
    for j, c in enumerate(halfs):
        prods, diag = scores[j]
        a = jnp.where(owns[len(prods)], diag, 0.0)
        for p, prod in enumerate(prods):
            a = jnp.where(owns[p], prod, a)
        v = v2[:, c]
        vvt = jnp.concatenate([v, v], axis=0).T
        os.append(_dot_nt(jnp.concatenate([qe2[:, c], a.astype(BF16)], axis=1),
                          jnp.concatenate([st2b[:, c], vvt], axis=1)))
        vvts.append(vvt)
        kd = jnp.concatenate([kdec2[:, c], jnp.zeros((C, D), BF16)], axis=0)
        kds.append(jnp.concatenate([kd, zero] if j == 0 else [zero, kd], axis=1))
        yield
    st_new = st2 * jnp.exp(b_last) + _dot(jnp.concatenate(vvts, axis=1), jnp.concatenate(kds, axis=0))
    yield
    ons = [o * lax.rsqrt(jnp.mean(o * o, axis=-1, keepdims=True) + EPS) for o in os]
    on2 = jnp.concatenate(ons, axis=1) * gate2
    return on2.astype(BF16), st_new


def _interleave(gens):
    out = [None] * len(gens)
    live = list(range(len(gens)))
    while live:
        for g in list(live):
            try:
                next(gens[g])
            except StopIteration as stop:
                out[g] = stop.value
                live.remove(g)
    return out


def _hgrn_kernel(q_ref, f_ref, v_ref, go_ref, fm_ref, vm_ref, sums_ref, owner_ref,
                 *rest, n_chunks, n_meta, hb, halves, jobs):
    cast_in, o_ref = rest[:len(jobs)], rest[len(jobs)]
    cast_out, st_ref = rest[len(jobs) + 1:-1], rest[-1]
    _run_cast_jobs(cast_in, cast_out, jobs)
    i = pl.program_id(0)
    p0 = pl.program_id(1) * (hb // 2)
    C = HG_CHUNK
    W = 2 * HG_DK

    @pl.when(i == 0)
    def _():
        tri_m = _tri(n_meta)
        for pair in range(hb // 2):
            sts = []
            for hh in (2 * pair, 2 * pair + 1):
                cols = slice(hh * HG_DK, (hh + 1) * HG_DK)
                fm = fm_ref[:, cols]
                bm = _dot3(tri_m, _split3(jnp.log(fm)))
                kdec = ((1.0 - fm) * jnp.exp(bm[n_meta - 1:n_meta, :] - bm)).astype(BF16)
                sts.append(_dot_tn(vm_ref[:, cols], kdec))
            st_ref[p0 + pair] = jnp.concatenate(sts, axis=1)

    def chunk(c, states):
        r0 = pl.multiple_of(c * C, C)
        owner = owner_ref[...]
        owns = [owner == p for p in range(len(halves) // 2 + 1)]
        sums = sums_ref[...]
        gens = []
        for pair in range(hb // 2):
            cols = slice(pair * W, (pair + 1) * W)
            gate2 = go_ref[pl.ds(r0, C), cols].astype(F32)
            gens.append(_hgrn_pair(q_ref[pl.ds(r0, C), cols], f_ref[pl.ds(r0, C), cols],
                                   v_ref[pl.ds(r0, C), cols], gate2, states[pair],
                                   sums, owns, halves))
        results = _interleave(gens)
        o_ref[pl.ds(r0, C), :] = jnp.concatenate([r[0] for r in results], axis=1)
        return tuple(r[1] for r in results)

    states = lax.fori_loop(0, n_chunks, chunk, tuple(st_ref[p0 + p] for p in range(hb // 2)))
    for pair in range(hb // 2):
        st_ref[p0 + pair] = states[pair]


def _hgrn_grid(n, width, rb=512, hb=12):
    heads = width // HG_DK
    hb = min(hb, heads)
    assert hb % 2 == 0 and heads % hb == 0
    return rb, hb, (n // rb, heads // hb)


def _hgrn(q, f, v, go, fm, vm, cast=()):
    n, width = q[0].shape[0], fm.shape[1]
    heads = width // HG_DK
    rb, hb, grid = _hgrn_grid(n, width)
    bw = hb * HG_DK
    n_meta = fm.shape[0]
    job_in, job_out, job_shapes, jobs = _cast_jobs(cast, grid)
    sums, owner, halves = _hgrn_tables(HG_CHUNK)
    sums3 = jnp.asarray(np.concatenate([sums] * 3, axis=1), BF16)
    blk = lambda i, h: (i, h)
    mblk = lambda i, h: (0, h)
    const = lambda i, h: (0, 0)

    def view(col0):
        assert col0 % bw == 0
        return pl.BlockSpec((rb, bw), lambda i, h: (i, col0 // bw + h))

    return pl.pallas_call(
        functools.partial(_hgrn_kernel, n_chunks=rb // HG_CHUNK, n_meta=n_meta, hb=hb,
                          halves=halves, jobs=jobs),
        grid=grid,
        in_specs=[view(q[1]), view(f[1]), view(v[1]), view(go[1])]
        + [pl.BlockSpec((n_meta, hb * HG_DK), mblk)] * 2
        + [pl.BlockSpec(sums3.shape, const), pl.BlockSpec(owner.shape, const)] + job_in,
        out_specs=[pl.BlockSpec((rb, hb * HG_DK), blk)] + job_out,
        out_shape=[jax.ShapeDtypeStruct((n, width), BF16)] + job_shapes,
        scratch_shapes=[pltpu.VMEM((heads // 2, HG_DK, 2 * HG_DK), F32)],
        compiler_params=_cparams(("arbitrary", "arbitrary")),
        name="hgrn2",
    )(q[0], f[0], v[0], go[0], fm, vm, sums3, jnp.asarray(owner), *[src for src, _, _ in cast])


def _s5_weights_kernel(lr_ref, li_ref, dt_ref, bre_ref, bim_ref, cre_ref, cim_ref, dsk_ref,
                       kt_ref, win_ref, wout_ref, at_ref):
    T = S5_T
    ns = lr_ref.shape[1]
    nc = cre_ref.shape[0]
    lr, li, dt = lr_ref[...], li_ref[...], dt_ref[...]

    def powers(tau):
        mag = jnp.exp(lr * dt * tau)
        ang = li * dt * tau
        return mag * jnp.cos(ang), mag * jnp.sin(ang)

    a_re, a_im = powers(1.0)
    num_re, num_im = a_re - 1.0, a_im
    den = lr * lr + li * li
    coef_re = (num_re * lr + num_im * li) / den
    coef_im = (num_im * lr - num_re * li) / den
    br, bi = bre_ref[...], bim_ref[...]
    bbar_re = coef_re * br - coef_im * bi
    bbar_im = coef_re * bi + coef_im * br

    pw = [powers(float(tau)) for tau in range(T + 1)]
    for s in range(T):
        p_re, p_im = pw[T - 1 - s]
        win_ref[s * nc:(s + 1) * nc, 0:ns] = (bbar_re * p_re - bbar_im * p_im).astype(BF16)
        win_ref[s * nc:(s + 1) * nc, ns:2 * ns] = (bbar_re * p_im + bbar_im * p_re).astype(BF16)
    at_ref[:, 0:ns] = pw[T][0]
    at_ref[:, ns:2 * ns] = pw[T][1]

    cr, ci = cre_ref[...], cim_ref[...]
    def split2(a):
        hi = a.astype(BF16)
        return hi, (a - hi.astype(F32)).astype(BF16)

    bb_hi, bb_lo = split2(jnp.concatenate([bbar_re, -bbar_im], axis=1))
    for tau in range(T + 1):
        p_re, p_im = pw[tau]
        ca_re = cr * p_re - ci * p_im
        ca_im = cr * p_im + ci * p_re
        if tau >= 1:
            t = tau - 1
            wout_ref[0:ns, t * nc:(t + 1) * nc] = ca_re.T.astype(BF16)
            wout_ref[ns:2 * ns, t * nc:(t + 1) * nc] = (-ca_im).T.astype(BF16)
        if tau < T:
            ca_hi, ca_lo = split2(jnp.concatenate([ca_re, ca_im], axis=1))
            kt = _dot_nt(bb_hi, ca_hi) + _dot_nt(bb_hi, ca_lo) + _dot_nt(bb_lo, ca_hi)
            if tau == 0:
                kt = kt + dsk_ref[...]
            kt_ref[tau] = kt.astype(BF16)


def _s5_weights(lam_re, lam_im, log_step, b_re, b_im, c_re, c_im, d_skip):
    G, N = lam_re.shape
    P = d_skip.shape[1]
    nsg = G // S5_SG
    ns, nc = S5_SG * N, S5_SG * P
    T = S5_T
    dt = jnp.exp(log_step.astype(F32))
    eye = jnp.eye(S5_SG, dtype=F32)

    def rows(a):
        return a.astype(F32).reshape(nsg, 1, ns)

    def embed_b(b):
        b = b.astype(F32).reshape(nsg, S5_SG, N, P)
        return jnp.einsum('zgnq,gh->zhqgn', b, eye).reshape(nsg, nc, ns)

    def embed_c(c):
        c = c.astype(F32).reshape(nsg, S5_SG, P, N)
        return jnp.einsum('zgpn,gh->zhpgn', c, eye).reshape(nsg, nc, ns)

    dtb = jnp.broadcast_to(dt[:, None], (G, N))
    d_diag = jax.vmap(jnp.diag)(d_skip.astype(F32).reshape(nsg, nc))
    sq = pl.Squeezed()
    spec_r = pl.BlockSpec((sq, 1, ns), lambda z: (z, 0, 0))
    spec_b = pl.BlockSpec((sq, nc, ns), lambda z: (z, 0, 0))
    return pl.pallas_call(
        _s5_weights_kernel,
        grid=(nsg,),
        in_specs=[spec_r, spec_r, spec_r, spec_b, spec_b, spec_b, spec_b,
                  pl.BlockSpec((sq, nc, nc), lambda z: (z, 0, 0))],
        out_specs=[pl.BlockSpec((sq, T, nc, nc), lambda z: (z, 0, 0, 0)),
                   pl.BlockSpec((sq, T * nc, 2 * ns), lambda z: (z, 0, 0)),
                   pl.BlockSpec((sq, 2 * ns, T * nc), lambda z: (z, 0, 0)),
                   pl.BlockSpec((sq, 1, 2 * ns), lambda z: (z, 0, 0))],
        out_shape=[jax.ShapeDtypeStruct((nsg, T, nc, nc), BF16),
                   jax.ShapeDtypeStruct((nsg, T * nc, 2 * ns), BF16),
                   jax.ShapeDtypeStruct((nsg, 2 * ns, T * nc), BF16),
                   jax.ShapeDtypeStruct((nsg, 1, 2 * ns), F32)],
        compiler_params=_cparams(("arbitrary",)),
        name="s5_weights",
    )(rows(lam_re), rows(lam_im), rows(dtb),
      embed_b(b_re), embed_b(b_im), embed_c(c_re), embed_c(c_im), d_diag)


def _s5_scan_kernel(u_ref, um_ref, kt_ref, win_ref, wout_ref, at_ref, y_ref,
                    uc_ref, v_ref, xp_ref, m_ref, *, n_chunks):
    T = S5_T
    nc = u_ref.shape[1]
    ns = at_ref.shape[1] // 2

    @pl.when(pl.program_id(0) == 0)
    def _():
        zero = jnp.zeros((nc, nc), BF16)
        for s in range(1, T):
            for t in range(s):
                m_ref[s * nc:(s + 1) * nc, t * nc:(t + 1) * nc] = zero

    for tau in range(T):
        tap = kt_ref[tau]
        for s in range(T - tau):
            m_ref[s * nc:(s + 1) * nc, (s + tau) * nc:(s + tau + 1) * nc] = tap

    for t in range(T):
        uc_ref[0:n_chunks, t * nc:(t + 1) * nc] = u_ref[pl.ds(t, n_chunks, stride=T), :].astype(BF16)
        uc_ref[n_chunks:n_chunks + 1, t * nc:(t + 1) * nc] = um_ref[t:t + 1, :].astype(BF16)
    pad = uc_ref.shape[0] - n_chunks - 1
    uc_ref[n_chunks + 1:, :] = jnp.zeros((pad, T * nc), BF16)
    uc = uc_ref[...]
    v_ref[...] = _dot(uc, win_ref[...])
    a_re, a_im = at_ref[:, 0:ns], at_ref[:, ns:2 * ns]

    x_re0 = v_ref[n_chunks:n_chunks + 1, 0:ns]
    x_im0 = v_ref[n_chunks:n_chunks + 1, ns:2 * ns]
    xp_ref[n_chunks:, :] = jnp.zeros((pad + 1, 2 * ns), F32)

    def step(c, carry):
        x_re, x_im = carry
        xp_ref[pl.ds(c, 1), 0:ns] = x_re
        xp_ref[pl.ds(c, 1), ns:2 * ns] = x_im
        v_re = v_ref[pl.ds(c, 1), 0:ns]
        v_im = v_ref[pl.ds(c, 1), ns:2 * ns]
        return (a_re * x_re - a_im * x_im + v_re, a_re * x_im + a_im * x_re + v_im)

    lax.fori_loop(0, n_chunks, step, (x_re0, x_im0))
    xp = xp_ref[...].astype(BF16)
    for p in range(T // 2):
        k_hi = (2 * p + 2) * nc
        cols = slice(2 * p * nc, k_hi)
        yc = _dot(uc_ref[:, 0:k_hi], m_ref[0:k_hi, cols]) + _dot(xp, wout_ref[:, cols])
        for t in (2 * p, 2 * p + 1):
            y_ref[pl.ds(t, n_chunks, stride=T), :] = yc[0:n_chunks, (t - 2 * p) * nc:(t - 2 * p + 1) * nc]


def _s5_scan(u, um, kt, win, wout, at):
    u, col0 = u
    n, width = u.shape[0], um.shape[1]
    T = S5_T
    nsg = kt.shape[0]
    nc = width // nsg
    ns2 = at.shape[2]
    assert um.shape[0] == T and n % T == 0 and col0 % nc == 0
    n_chunks = n // T
    rows = -(-(n_chunks + 1) // 16) * 16
    sq = pl.Squeezed()
    return pl.pallas_call(
        functools.partial(_s5_scan_kernel, n_chunks=n_chunks),
        grid=(nsg,),
        in_specs=[pl.BlockSpec((n, nc), lambda z: (0, col0 // nc + z)),
                  pl.BlockSpec((T, nc), lambda z: (0, z)),
                  pl.BlockSpec((sq, T, nc, nc), lambda z: (z, 0, 0, 0)),
                  pl.BlockSpec((sq, T * nc, ns2), lambda z: (z, 0, 0)),
                  pl.BlockSpec((sq, ns2, T * nc), lambda z: (z, 0, 0)),
                  pl.BlockSpec((sq, 1, ns2), lambda z: (z, 0, 0))],
        out_specs=pl.BlockSpec((n, nc), lambda z: (0, z)),
        out_shape=jax.ShapeDtypeStruct((n, width), F32),
        scratch_shapes=[pltpu.VMEM((rows, T * nc), BF16), pltpu.VMEM((rows, ns2), F32),
                        pltpu.VMEM((rows, ns2), F32), pltpu.VMEM((T * nc, T * nc), BF16)],
        compiler_params=_cparams(("arbitrary",)),
        name="s5_scan",
    )(u, um, kt, win, wout, at)


def _merge_kernel(*refs, pieces):
    h_ref, o_ref, y_ref = refs[0:3]
    gh_refs, gs_refs = refs[3:3 + pieces], refs[3 + pieces:3 + 2 * pieces]
    wp_ref, wa_ref, wb_ref, wo_ref, gn_ref, out_ref, norm_ref = refs[3 + 2 * pieces:]
    gh = jnp.concatenate([r[...] for r in gh_refs], axis=1).astype(F32)
    gs = jnp.concatenate([r[...] for r in gs_refs], axis=1).astype(F32)
    y_hg = _dot(o_ref[...], wp_ref[...])
    z = jax.nn.gelu(y_ref[...]).astype(BF16)
    y_s5 = _dot(z, wa_ref[...]) * _sigmoid(_dot(z, wb_ref[...]))
    merged = gh * y_hg + gs * y_s5
    h = h_ref[...] + _dot(merged.astype(BF16), wo_ref[...])
    out_ref[...] = h
    norm_ref[...] = _rms(h, gn_ref[...]).astype(BF16)


def _merge(hx, o, y, gh, gs, wp, wa, wb, wo, g_next, *, tm=256):
    n, d = hx.shape
    row = lambda i: (i, 0)
    const = lambda i: (0, 0)
    gw = functools.reduce(math.gcd, (gh[1], gs[1], d))
    assert gw % LANE == 0
    pieces = d // gw

    def resident(shape):
        return pl.BlockSpec(shape, const, pipeline_mode=pl.Buffered(1))

    def view(col0):
        return [pl.BlockSpec((tm, gw), functools.partial(lambda i, k: (i, k), k=col0 // gw + k))
                for k in range(pieces)]

    return pl.pallas_call(
        functools.partial(_merge_kernel, pieces=pieces),
        grid=(n // tm,),
        in_specs=[pl.BlockSpec((tm, d), row), pl.BlockSpec((tm, o.shape[1]), row),
                  pl.BlockSpec((tm, y.shape[1]), row)] + view(gh[1]) + view(gs[1])
        + [resident(wp.shape), resident(wa.shape), resident(wb.shape), resident(wo.shape),
           pl.BlockSpec((1, d), const)],
        out_specs=[pl.BlockSpec((tm, d), row)] * 2,
        out_shape=[jax.ShapeDtypeStruct((n, d), F32), jax.ShapeDtypeStruct((n, d), BF16)],
        compiler_params=_cparams(("arbitrary",)),
        name="mixer_merge",
    )(hx, o, y, *([gh[0]] * pieces), *([gs[0]] * pieces), wp, wa, wb, wo, g_next.reshape(1, d))


def kernel(x, meta_tokens, norm_ffn1, ffn1_w_gate, ffn1_w_up, ffn1_w_down, norm_mix, w_in, hg_lower_bounds, hg_norm, hg_w_proj, s5_lam_re, s5_lam_im, s5_log_step, s5_b_re, s5_b_im, s5_c_re, s5_c_im, s5_d, s5_w_glu_a, s5_w_glu_b, w_out, norm_ffn2, ffn2_w_gate, ffn2_w_up, ffn2_w_down, norm_final):
    batch, seq, d = x.shape
    depth = norm_ffn1.shape[0]
    assert batch == 1 and depth == 1
    hg_k = hg_lower_bounds.shape[1]
    hg_w = hg_norm.shape[1]
    s5_w = s5_w_glu_a.shape[1]
    hx = x[0].astype(F32)
    hm = meta_tokens.astype(F32)
    tm = min(512, seq)
    h0, n0, hm, nm, w1_gate, w1_up, w1_down = _ffn(
        hx, hm, norm_ffn1[0], norm_mix[0], ffn1_w_gate[0], ffn1_w_up[0], ffn1_w_down[0],
        emit_norm=True, rows=tm, tm=tm, tf=256, name="ffn_head")
    sizes = (hg_k, hg_k, hg_w, hg_w, s5_w, d, d)
    assert sum(sizes) == w_in.shape[2]
    off = [0]
    for s in sizes:
        off.append(off[-1] + s)
    ids32, ids16 = (0, 1, 4), (2, 3, 5, 6)
    tc_in = functools.reduce(math.gcd, sizes + (512,))

    def starts(ids):
        out, col = {}, 0
        for k in ids:
            out[k] = col
            col += sizes[k]
        return out

    def blocks(ids):
        return [c for k in ids for c in range(off[k] // tc_in, off[k + 1] // tc_in)]

    steps = lambda grid: grid[0] * grid[1]
    n_body = (seq // tm) * (ffn1_w_gate.shape[2] // 512)
    n_inp = steps(_inproj_grid(seq, sum(sizes[k] for k in ids16))[2])
    n_hg = steps(_hgrn_grid(seq, hg_w)[2])
    hx, nx, w32, w16 = _ffn(
        hx, None, norm_ffn1[0], norm_mix[0], w1_gate, w1_up, w1_down, emit_norm=True, head=(h0, n0),
        cast=(_cast_tile(w_in[0], n_body, tc_in, [blocks(ids32), blocks(ids16)]),),
        tm=tm, name="ffn_body")

    ident, sigm = (0.0, 1.0, 0.0, 0.0), (0.0, 0.0, 1.0, 0.0)
    rows = {0: (HG_DK ** -0.5, 0.0, 0.0, 0.0), 1: (0.0, 0.0, 0.0, 1.0), 2: ident,
            3: (hg_norm[0].astype(F32), 0.0, 0.0, 0.0), 4: ident, 5: sigm, 6: sigm}

    def coef(ids):
        return jnp.concatenate([jnp.stack([jnp.broadcast_to(jnp.asarray(v, F32), (sizes[k],))
                                           for v in rows[k]]) for k in ids], axis=1)

    s32, s16 = starts(ids32), starts(ids16)
    x32, m32 = _inproj(nx, nm, w32, coef(ids32), hg_lower_bounds, s32[1], F32)
    x16, m16, w_proj, w_glu_a, w_glu_b, w_o = _inproj(
        nx, nm, w16, coef(ids16), None, 0, BF16,
        cast=[_cast_tile(w, n_inp) for w in (hg_w_proj[0], s5_w_glu_a[0], s5_w_glu_b[0], w_out[0])])

    o, w2_gate, w2_up, w2_down = _hgrn(
        (x32, s32[0]), (x32, s32[1]), (x16, s16[2]), (x16, s16[3]),
        m32[:, s32[1]:s32[1] + hg_k], m16[:, s16[2]:s16[2] + hg_w],
        cast=[_cast_tile(w, n_hg) for w in (ffn2_w_gate[0], ffn2_w_up[0], ffn2_w_down[0])])
    m, win, wout, at = _s5_weights(s5_lam_re[0], s5_lam_im[0], s5_log_step[0], s5_b_re[0],
                                   s5_b_im[0], s5_c_re[0], s5_c_im[0], s5_d[0])
    y = _s5_scan((x32, s32[4]), m32[:, s32[4]:s32[4] + s5_w], m, win, wout, at)
    hx, nx2 = _merge(hx, o, y, (x16, s16[5]), (x16, s16[6]), w_proj, w_glu_a, w_glu_b, w_o,
                     norm_ffn2[0])
    out, = _ffn(hx, None, nx2, norm_final, w2_gate, w2_up, w2_down, emit_norm=False,
                tm=tm, name="ffn_final")
    return out[None].astype(x.dtype)
```

```python
import functools
import math

import numpy as np
import jax
import jax.numpy as jnp
from jax import lax
from jax.experimental import pallas as pl
from jax.experimental.pallas import tpu as pltpu

EPS = 1e-6
HG_DK = 128
HG_CHUNK = 64
S5_GROUP = 16
S5_STATE = 64
S5_T = 16
S5_SG = 8
LANE = 128
INPROJ_SUB = 256
VMEM_LIMIT = 56 * 1024 * 1024

F32 = jnp.float32
BF16 = jnp.bfloat16


def _cparams(sem):
    return pltpu.CompilerParams(dimension_semantics=sem, vmem_limit_bytes=VMEM_LIMIT)


def _rms(v, g):
    return v * lax.rsqrt(jnp.mean(v * v, axis=-1, keepdims=True) + EPS) * g


def _sigmoid(x):
    return 0.5 * jnp.tanh(0.5 * x) + 0.5


def _silu(x):
    h = 0.5 * x
    return h * (jnp.tanh(h) + 1.0)


def _dot(a, b):
    return jnp.dot(a, b, preferred_element_type=F32)


def _dot_nt(a, b):
    return lax.dot_general(a, b, (((1,), (1,)), ((), ())), preferred_element_type=F32)


def _dot_tn(a, b):
    return lax.dot_general(a, b, (((0,), (0,)), ((), ())), preferred_element_type=F32)


def _cast_jobs(cast, grid):
    nj = grid[1]
    in_specs, out_specs, out_shapes, static = [], [], [], []
    for src, (tr, tc), dests in cast:
        rows, cols = src.shape
        ncb = cols // tc
        n_tiles = (rows // tr) * ncb
        assert rows % tr == 0 and cols % tc == 0 and n_tiles <= grid[0] * nj

        def tile(i, j, n_tiles=n_tiles, ncb=ncb):
            t = jnp.minimum(i * nj + j, n_tiles - 1)
            return t // ncb, t % ncb

        in_specs.append(pl.BlockSpec((tr, tc), tile))
        for blocks in dests:
            park, k = [], 0
            for c in range(ncb):
                k = blocks.index(c) if c in blocks else k
                park.append(k)
            jumps = [(m, park[m] - park[m - 1]) for m in range(1, ncb) if park[m] != park[m - 1]]

            def dest(i, j, tile=tile, first=park[0], jumps=jumps):
                r, c = tile(i, j)
                return r, first + sum(jnp.where(c >= m, dk, 0) for m, dk in jumps)

            out_specs.append(pl.BlockSpec((tr, tc), tile if park == list(range(ncb)) else dest))
            out_shapes.append(jax.ShapeDtypeStruct((rows, len(blocks) * tc), BF16))
        static.append((n_tiles, ncb, tuple(tuple(b) for b in dests)))
    return in_specs, out_specs, out_shapes, tuple(static)


def _n_cast_out(jobs):
    return sum(len(dests) for _, _, dests in jobs)


def _run_cast_jobs(cast_in, cast_out, jobs):
    step = pl.program_id(0) * pl.num_programs(1) + pl.program_id(1)
    outs = iter(cast_out)
    for src, (n_tiles, ncb, dests) in zip(cast_in, jobs):
        dsts = [next(outs) for _ in dests]
        tile = src[...].astype(BF16)
        if len(dsts) == 1:
            dsts[0][...] = tile
        else:
            c = jnp.minimum(step, n_tiles - 1) % ncb
            for dst, cols in zip(dsts, dests):
                owns = functools.reduce(jnp.logical_or, [c == m for m in cols])

                @pl.when(owns)
                def _(dst=dst):
                    dst[...] = tile


def _cast_tile(w, n_steps, tc=None, dests=None):
    rows, cols = w.shape
    if tc is None and rows % n_steps == 0 and (rows // n_steps) % 16 == 0:
        return w, (rows // n_steps, cols), [[0]]
    tc = math.gcd(tc or 512, cols)
    tr = 128
    while (rows // tr) * (cols // tc) > n_steps:
        tr *= 2
    return w, (tr, tc), dests or [list(range(cols // tc))]


def _ffn_kernel(*refs, tm, n_meta, normed, emit_norm, w_f32, head, jobs):
    it = iter(refs)
    take = lambda k: [next(it) for _ in range(k)]
    x_ref, = take(1)
    m_ref = take(1)[0] if n_meta else None
    g_ref, gn_ref, wg_ref, wu_ref, wd_ref = take(5)
    hx_ref, hn16_ref = take(2) if head else (None, None)
    cast_in = take(len(jobs))
    ox_ref, = take(1)
    nx_ref = take(1)[0] if emit_norm else None
    om_ref, nm_ref = take(2) if n_meta else (None, None)
    w16_out = take(3) if w_f32 else ()
    cast_out = take(_n_cast_out(jobs))
    hn_ref = g_ref if normed else take(1)[0]
    acc_ref = take(1)[0] if n_meta else ox_ref
    i = pl.program_id(0)
    j = pl.program_id(1)
    _run_cast_jobs(cast_in, cast_out, jobs)

    def body():
        @pl.when(j == 0)
        def _():
            if not normed:
                hn_ref[0:tm, :] = _rms(x_ref[...], g_ref[...]).astype(BF16)
            if n_meta:
                hn_ref[tm:tm + n_meta, :] = _rms(m_ref[...], g_ref[...]).astype(BF16)
            acc_ref[...] = jnp.zeros_like(acc_ref)

        tiles = [w[...].astype(BF16) if w_f32 else w[...] for w in (wg_ref, wu_ref, wd_ref)]
        for dst, tile in zip(w16_out, tiles):
            dst[...] = tile
        hn = hn_ref[...]
        gate = _dot(hn, tiles[0])
        up = _dot(hn, tiles[1])
        act = (_silu(gate) * up).astype(BF16)
        acc_ref[...] += _dot(act, tiles[2])

        @pl.when(j == pl.num_programs(1) - 1)
        def _():
            hx = x_ref[...] + 0.5 * acc_ref[0:tm, :]
            if emit_norm:
                ox_ref[...] = hx
                nx_ref[...] = _rms(hx, gn_ref[...]).astype(BF16)
            else:
                ox_ref[...] = _rms(hx, gn_ref[...])
            if n_meta:
                hm = m_ref[...] + 0.5 * acc_ref[tm:tm + n_meta, :]
                om_ref[...] = hm
                nm_ref[...] = _rms(hm, gn_ref[...]).astype(BF16)

    if head:
        @pl.when((i == 0) & (j == 0))
        def _():
            ox_ref[...] = hx_ref[...]
            nx_ref[...] = hn16_ref[...]

        pl.when(i > 0)(body)
    else:
        body()


def _ffn(hx, hm, g, g_next, wg, wu, wd, *, emit_norm, head=None, cast=(), rows=None, name,
         tm=512, tf=512):
    n, d = hx.shape
    rows = n if rows is None else rows
    dff = wg.shape[1]
    n_meta = 0 if hm is None else hm.shape[0]
    normed = g.ndim == 2
    w_f32 = wg.dtype == F32
    grid = (rows // tm, dff // tf)
    assert not w_f32 or grid[0] == 1
    assert not (normed and n_meta)
    row = lambda i, j: (i, 0)
    const = lambda i, j: (0, 0)
    in_specs = [pl.BlockSpec((tm, d), row)]
    args = [hx]
    if n_meta:
        in_specs.append(pl.BlockSpec((n_meta, d), const))
        args.append(hm)
    col = (lambda i, j: jnp.where(i == 0, 0, j)) if head else (lambda i, j: j)
    in_specs += [pl.BlockSpec((tm, d), row) if normed else pl.BlockSpec((1, d), const),
                 pl.BlockSpec((1, d), const),
                 pl.BlockSpec((d, tf), lambda i, j: (0, col(i, j))),
                 pl.BlockSpec((d, tf), lambda i, j: (0, col(i, j))),
                 pl.BlockSpec((tf, d), lambda i, j: (col(i, j), 0))]
    args += [g if normed else g.reshape(1, d), g_next.reshape(1, d), wg, wu, wd]
    if head:
        in_specs += [pl.BlockSpec((tm, d), const, pipeline_mode=pl.Buffered(1))] * 2
        args += list(head)
    job_in, job_out, job_shapes, jobs = _cast_jobs(cast, grid)
    in_specs += job_in
    args += [src for src, _, _ in cast]
    out_shape = [jax.ShapeDtypeStruct((rows, d), F32)]
    out_specs = [pl.BlockSpec((tm, d), row)]
    if emit_norm:
        out_shape.append(jax.ShapeDtypeStruct((rows, d), BF16))
        out_specs.append(pl.BlockSpec((tm, d), row))
    if n_meta:
        out_shape += [jax.ShapeDtypeStruct((n_meta, d), F32), jax.ShapeDtypeStruct((n_meta, d), BF16)]
        out_specs += [pl.BlockSpec((n_meta, d), const)] * 2
    if w_f32:
        out_shape += [jax.ShapeDtypeStruct(w.shape, BF16) for w in (wg, wu, wd)]
        out_specs += [pl.BlockSpec((d, tf), lambda i, j: (0, j)),
                      pl.BlockSpec((d, tf), lambda i, j: (0, j)),
                      pl.BlockSpec((tf, d), lambda i, j: (j, 0))]
    out_specs += job_out
    out_shape += job_shapes
    return pl.pallas_call(
        functools.partial(_ffn_kernel, tm=tm, n_meta=n_meta, normed=normed, emit_norm=emit_norm,
                          w_f32=w_f32, head=head is not None, jobs=jobs),
        grid=grid, in_specs=in_specs, out_specs=out_specs, out_shape=out_shape,
        scratch_shapes=([] if normed else [pltpu.VMEM((tm + n_meta, d), BF16)])
        + ([pltpu.VMEM((tm + n_meta, d), F32)] if n_meta else []),
        compiler_params=_cparams(("arbitrary", "arbitrary")),
        name=name,
    )(*args)


def _inproj_kernel(*refs, tm, n_meta, with_lb, kinds, jobs):
    it = iter(refs)
    take = lambda k: [next(it) for _ in range(k)]
    x_ref, m_ref, w_ref, scale_ref = take(4)
    lb_ref = take(1)[0] if with_lb else None
    cast_in = take(len(jobs))
    ox_ref, om_ref = take(2)
    cast_out = take(_n_cast_out(jobs))
    lhs_ref, = take(1)
    _run_cast_jobs(cast_in, cast_out, jobs)

    @pl.when(pl.program_id(1) == 0)
    def _():
        lhs_ref[0:tm, :] = x_ref[...]
        lhs_ref[tm:tm + n_meta, :] = m_ref[...]

    tn = w_ref.shape[1]
    sw = tn // len(kinds[0])

    def column_step(step_kinds):
        for s, kind in enumerate(step_kinds):
            cs = slice(s * sw, (s + 1) * sw)
            res = _dot(lhs_ref[...], w_ref[:, cs])
            if kind == "ident":
                out = res
            elif kind == "silu":
                h = 0.5 * res
                out = (scale_ref[:, cs] * h) * (jnp.tanh(h) + 1.0)
            elif kind == "sigm":
                out = 0.5 * jnp.tanh(0.5 * res) + 0.5
            else:
                assert kind == "forget"
                lbl = lb_ref[:, cs]
                e = jnp.exp(lbl - jnp.max(lbl, axis=0, keepdims=True))
                lb = e[0:1, :] / jnp.sum(e, axis=0, keepdims=True)
                out = (0.5 - 0.5 * lb) * jnp.tanh(0.5 * res) + (0.5 + 0.5 * lb)
            ox_ref[:, cs] = out[0:tm, :].astype(ox_ref.dtype)
            om_ref[:, cs] = out[tm:tm + n_meta, :].astype(om_ref.dtype)

    if len(set(kinds)) == 1:
        column_step(kinds[0])
    else:
        for jj, step_kinds in enumerate(kinds):
            pl.when(pl.program_id(1) == jj)(functools.partial(column_step, step_kinds))


def _inproj_grid(n, cols, tm=1024, tn=1792):
    tm = min(tm, n)
    tn = math.gcd(tn, cols)
    assert tn % LANE == 0 and n % tm == 0
    return tm, tn, (n // tm, cols // tn)


def _inproj(nx, nm, w, segments, scale, lower_bounds, dtype, cast=()):
    n, d = nx.shape
    n_meta = nm.shape[0]
    cols = w.shape[1]
    with_lb = lower_bounds is not None
    tm, tn, grid = _inproj_grid(n, cols)
    sw = math.gcd(tn, INPROJ_SUB)
    col_kind = [kind for width, kind in segments for _ in range(width // sw)]
    assert all(width % sw == 0 for width, _ in segments) and len(col_kind) * sw == cols
    kinds = tuple(tuple(col_kind[jj * (tn // sw):(jj + 1) * (tn // sw)]) for jj in range(grid[1]))
    job_in, job_out, job_shapes, jobs = _cast_jobs(cast, grid)
    in_specs = [pl.BlockSpec((tm, d), lambda i, j: (i, 0)),
                pl.BlockSpec((n_meta, d), lambda i, j: (0, 0)),
                pl.BlockSpec((d, tn), lambda i, j: (0, j)),
                pl.BlockSpec((1, tn), lambda i, j: (0, j))]
    args = [nx, nm, w, scale]
    if with_lb:
        in_specs.append(pl.BlockSpec((lower_bounds.shape[0], tn), lambda i, j: (0, j)))
        args.append(lower_bounds)
    ox, om, *copies = pl.pallas_call(
        functools.partial(_inproj_kernel, tm=tm, n_meta=n_meta, with_lb=with_lb, kinds=kinds,
                          jobs=jobs),
        grid=grid,
        in_specs=in_specs + job_in,
        out_specs=[pl.BlockSpec((tm, tn), lambda i, j: (i, j)),
                   pl.BlockSpec((pl.Squeezed(), n_meta, tn), lambda i, j: (i, 0, j))] + job_out,
        out_shape=[jax.ShapeDtypeStruct((n, cols), dtype),
                   jax.ShapeDtypeStruct((n // tm, n_meta, cols), dtype)] + job_shapes,
        scratch_shapes=[pltpu.VMEM((tm + n_meta, d), BF16)],
        compiler_params=_cparams(("arbitrary", "arbitrary")),
        name="mixer_inproj_" + jnp.dtype(dtype).name,
    )(*args, *[src for src, _, _ in cast])
    return (ox, om[0], *copies)


def _split3(x):
    hi = x.astype(BF16)
    r1 = x - hi.astype(F32)
    mid = r1.astype(BF16)
    lo = (r1 - mid.astype(F32)).astype(BF16)
    return hi, mid, lo


def _dot3(m01, parts):
    return _dot(m01, parts[0]) + _dot(m01, parts[1]) + _dot(m01, parts[2])


def _tri(n, rep=1):
    r = lax.broadcasted_iota(jnp.int32, (n * rep, n), 0)
    c = lax.broadcasted_iota(jnp.int32, (n * rep, n), 1)
    return jnp.where(r >= c * rep, 1.0, 0.0).astype(BF16)


def _hgrn_tables(c):
    sums = [np.tril(np.ones((c, c)))]
    level = np.full((c, c), -1, np.int32)
    t_idx, s_idx = np.meshgrid(np.arange(c), np.arange(c), indexing="ij")
    halves = []
    h = c // 2
    while h >= 1:
        if h < 8:
            m = np.zeros((c, c))
            for r in range(c):
                mid = (r // (2 * h)) * 2 * h + h
                if r < mid:
                    m[r, r + 1:mid] = 1.0
                else:
                    m[r, mid:r + 1] = 1.0
            sums.append(m)
        pair = 2 * h
        level[(t_idx // pair == s_idx // pair) & (s_idx % pair < h) & (t_idx % pair >= h)] = len(halves)
        halves.append(h)
        h //= 2
    assert len(halves) % 2 == 0
    level[np.arange(c), np.arange(c)] = len(halves)
    owner = np.full((c, 2 * c), -1, np.int32)
    owner[:, :c] = np.where((level >= 0) & (level % 2 == 0), level // 2, -1)
    owner[:, c:] = np.where(level % 2 == 1, level // 2, -1)
    return np.concatenate(sums, axis=0), owner, tuple(halves)


def _hgrn_score_products(q, k, x, halves):
    C = HG_CHUNK
    b = x[0:C]
    zero = jnp.zeros((C, HG_DK), BF16)
    diag = jnp.sum(q * k, axis=-1, keepdims=True)
    qts, kts = [], []
    fine = 0
    for h in halves:
        if h >= 8:
            none = jnp.zeros((h, HG_DK), F32)
            qh, kh = [], []
            for r in range(0, C, 2 * h):
                b_mid = b[r + h - 1:r + h, :]
                kh += [k[r:r + h] * jnp.exp(b_mid - b[r:r + h]), none]
                qh += [none, q[r + h:r + 2 * h] * jnp.exp(b[r + h:r + 2 * h] - b_mid)]
            qts.append(jnp.concatenate(qh, axis=0).astype(BF16))
            kts.append(jnp.concatenate(kh, axis=0).astype(BF16))
        else:
            fine += 1
            e = jnp.exp(x[fine * C:(fine + 1) * C])
            qts.append((q * e).astype(BF16))
            kts.append((k * e).astype(BF16))
    prods = []
    for p in range(len(halves) // 2):
        lhs = jnp.concatenate([qts[2 * p], qts[2 * p + 1]], axis=1)
        rhs = jnp.concatenate([jnp.concatenate([kts[2 * p], zero], axis=1),
                               jnp.concatenate([zero, kts[2 * p + 1]], axis=1)], axis=0)
        prods.append(_dot_nt(lhs, rhs))
    return prods, diag


def _hgrn_pair(q2, f2, v2, gate2, st2, sums, owns, halves):
    C, D = HG_CHUNK, HG_DK
    halfs = (slice(0, D), slice(D, 2 * D))
    k2 = 1.0 - f2
    x2 = _dot(sums, jnp.concatenate(_split3(jnp.log(f2)), axis=0))
    yield
    b2 = x2[0:C]
    b_last = b2[C - 1:C, :]
    kdec2 = (k2 * jnp.exp(b_last - b2)).astype(BF16)
    qe2 = (q2 * jnp.exp(b2)).astype(BF16)
    st2b = st2.astype(BF16)
    scores = []
    for c in halfs:
        scores.append(_hgrn_score_products(q2[:, c], k2[:, c], x2[:, c], halves))
        yield
    zero = jnp.zeros((2 * C, D), BF16)
    os, vvts, kds = [], [], []
    for j, c in enumerate(halfs):
        prods, diag = scores[j]
        a = jnp.where(owns[len(prods)], diag, 0.0)
        for p, prod in enumerate(prods):
            a = jnp.where(owns[p], prod, a)
        v = v2[:, c]
        vvt = jnp.concatenate([v, v], axis=0).T
        os.append(_dot_nt(jnp.concatenate([qe2[:, c], a.astype(BF16)], axis=1),
                          jnp.concatenate([st2b[:, c], vvt], axis=1)))
        vvts.append(vvt)
        kd = jnp.concatenate([kdec2[:, c], jnp.zeros((C, D), BF16)], axis=0)
        kds.append(jnp.concatenate([kd, zero] if j == 0 else [zero, kd], axis=1))
        yield
    st_new = st2 * jnp.exp(b_last) + _dot(jnp.concatenate(vvts, axis=1), jnp.concatenate(kds, axis=0))
    yield
    ons = [o * lax.rsqrt(jnp.mean(o * o, axis=-1, keepdims=True) + EPS) for o in os]
    on2 = jnp.concatenate(ons, axis=1) * gate2
    return on2.astype(BF16), st_new


def _interleave(gens):
    out = [None] * len(gens)
    live = list(range(len(gens)))
    while live:
        for g in list(live):
            try:
                next(gens[g])
            except StopIteration as stop:
                out[g] = stop.value
                live.remove(g)
    return out


def _hgrn_kernel(q_ref, f_ref, v_ref, go_ref, fm_ref, vm_ref, sums_ref, owner_ref,
                 *rest, n_chunks, n_meta, hb, halves, jobs):
    cast_in, o_ref = rest[:len(jobs)], rest[len(jobs)]
    cast_out, st_ref = rest[len(jobs) + 1:-1], rest[-1]
    _run_cast_jobs(cast_in, cast_out, jobs)
    i = pl.program_id(0)
    p0 = pl.program_id(1) * (hb // 2)
    C = HG_CHUNK
    W = 2 * HG_DK

    @pl.when(i == 0)
    def _():
        tri_m = _tri(n_meta)
        for pair in range(hb // 2):
            sts = []
            for hh in (2 * pair, 2 * pair + 1):
                cols = slice(hh * HG_DK, (hh + 1) * HG_DK)
                fm = fm_ref[:, cols]
                bm = _dot3(tri_m, _split3(jnp.log(fm)))
                kdec = ((1.0 - fm) * jnp.exp(bm[n_meta - 1:n_meta, :] - bm)).astype(BF16)
                sts.append(_dot_tn(vm_ref[:, cols], kdec))
            st_ref[p0 + pair] = jnp.concatenate(sts, axis=1)

    def chunk(c, states):
        r0 = pl.multiple_of(c * C, C)
        owner = owner_ref[...]
        owns = [owner == p for p in range(len(halves) // 2 + 1)]
        sums = sums_ref[...]
        gens = []
        for pair in range(hb // 2):
            cols = slice(pair * W, (pair + 1) * W)
            gate2 = go_ref[pl.ds(r0, C), cols].astype(F32)
            gens.append(_hgrn_pair(q_ref[pl.ds(r0, C), cols], f_ref[pl.ds(r0, C), cols],
                                   v_ref[pl.ds(r0, C), cols], gate2, states[pair],
                                   sums, owns, halves))
        results = _interleave(gens)
        o_ref[pl.ds(r0, C), :] = jnp.concatenate([r[0] for r in results], axis=1)
        return tuple(r[1] for r in results)

    states = lax.fori_loop(0, n_chunks, chunk, tuple(st_ref[p0 + p] for p in range(hb // 2)))
    for pair in range(hb // 2):
        st_ref[p0 + pair] = states[pair]


def _hgrn_grid(n, width, rb=512, hb=12):
    heads = width // HG_DK
    hb = min(hb, heads)
    assert hb % 2 == 0 and heads % hb == 0
    return rb, hb, (n // rb, heads // hb)


def _hgrn(q, f, v, go, fm, vm, cast=()):
    n, width = q[0].shape[0], fm.shape[1]
    heads = width // HG_DK
    rb, hb, grid = _hgrn_grid(n, width)
    bw = hb * HG_DK
    n_meta = fm.shape[0]
    job_in, job_out, job_shapes, jobs = _cast_jobs(cast, grid)
    sums, owner, halves = _hgrn_tables(HG_CHUNK)
    sums3 = jnp.asarray(np.concatenate([sums] * 3, axis=1), BF16)
    blk = lambda i, h: (i, h)
    mblk = lambda i, h: (0, h)
    const = lambda i, h: (0, 0)

    def view(col0):
        assert col0 % bw == 0
        return pl.BlockSpec((rb, bw), lambda i, h: (i, col0 // bw + h))

    return pl.pallas_call(
        functools.partial(_hgrn_kernel, n_chunks=rb // HG_CHUNK, n_meta=n_meta, hb=hb,
                          halves=halves, jobs=jobs),
        grid=grid,
        in_specs=[view(q[1]), view(f[1]), view(v[1]), view(go[1])]
        + [pl.BlockSpec((n_meta, hb * HG_DK), mblk)] * 2
        + [pl.BlockSpec(sums3.shape, const), pl.BlockSpec(owner.shape, const)] + job_in,
        out_specs=[pl.BlockSpec((rb, hb * HG_DK), blk)] + job_out,
        out_shape=[jax.ShapeDtypeStruct((n, width), BF16)] + job_shapes,
        scratch_shapes=[pltpu.VMEM((heads // 2, HG_DK, 2 * HG_DK), F32)],
        compiler_params=_cparams(("arbitrary", "arbitrary")),
        name="hgrn2",
    )(q[0], f[0], v[0], go[0], fm, vm, sums3, jnp.asarray(owner), *[src for src, _, _ in cast])


def _s5_weights_kernel(lr_ref, li_ref, dt_ref, bre_ref, bim_ref, cre_ref, cim_ref, dsk_ref,
                       kt_ref, win_ref, wout_ref, at_ref):
    T = S5_T
    ns = lr_ref.shape[1]
    nc = cre_ref.shape[0]
    lr, li, dt = lr_ref[...], li_ref[...], dt_ref[...]

    def powers(tau):
        mag = jnp.exp(lr * dt * tau)
        ang = li * dt * tau
        return mag * jnp.cos(ang), mag * jnp.sin(ang)

    a_re, a_im = powers(1.0)
    num_re, num_im = a_re - 1.0, a_im
    den = lr * lr + li * li
    coef_re = (num_re * lr + num_im * li) / den
    coef_im = (num_im * lr - num_re * li) / den
    br, bi = bre_ref[...], bim_ref[...]
    bbar_re = coef_re * br - coef_im * bi
    bbar_im = coef_re * bi + coef_im * br

    pw = [powers(float(tau)) for tau in range(T + 1)]
    for s in range(T):
        p_re, p_im = pw[T - 1 - s]
        win_ref[s * nc:(s + 1) * nc, 0:ns] = (bbar_re * p_re - bbar_im * p_im).astype(BF16)
        win_ref[s * nc:(s + 1) * nc, ns:2 * ns] = (bbar_re * p_im + bbar_im * p_re).astype(BF16)
    at_ref[:, 0:ns] = pw[T][0]
    at_ref[:, ns:2 * ns] = pw[T][1]

    cr, ci = cre_ref[...], cim_ref[...]
    def split2(a):
        hi = a.astype(BF16)
        return hi, (a - hi.astype(F32)).astype(BF16)

    bb_hi, bb_lo = split2(jnp.concatenate([bbar_re, -bbar_im], axis=1))
    for tau in range(T + 1):
        p_re, p_im = pw[tau]
        ca_re = cr * p_re - ci * p_im
        ca_im = cr * p_im + ci * p_re
        if tau >= 1:
            t = tau - 1
            wout_ref[0:ns, t * nc:(t + 1) * nc] = ca_re.T.astype(BF16)
            wout_ref[ns:2 * ns, t * nc:(t + 1) * nc] = (-ca_im).T.astype(BF16)
        if tau < T:
            ca_hi, ca_lo = split2(jnp.concatenate([ca_re, ca_im], axis=1))
            kt = _dot_nt(bb_hi, ca_hi) + _dot_nt(bb_hi, ca_lo) + _dot_nt(bb_lo, ca_hi)
            if tau == 0:
                kt = kt + dsk_ref[...]
            kt_ref[tau] = kt.astype(BF16)


def _s5_weights(lam_re, lam_im, log_step, b_re, b_im, c_re, c_im, d_skip):
    G, N = lam_re.shape
    P = d_skip.shape[1]
    nsg = G // S5_SG
    ns, nc = S5_SG * N, S5_SG * P
    T = S5_T
    dt = jnp.exp(log_step.astype(F32))
    eye = jnp.eye(S5_SG, dtype=F32)

    def rows(a):
        return a.astype(F32).reshape(nsg, 1, ns)

    def embed_b(b):
        b = b.astype(F32).reshape(nsg, S5_SG, N, P)
        return jnp.einsum('zgnq,gh->zhqgn', b, eye).reshape(nsg, nc, ns)

    def embed_c(c):
        c = c.astype(F32).reshape(nsg, S5_SG, P, N)
        return jnp.einsum('zgpn,gh->zhpgn', c, eye).reshape(nsg, nc, ns)

    dtb = jnp.broadcast_to(dt[:, None], (G, N))
    d_diag = jax.vmap(jnp.diag)(d_skip.astype(F32).reshape(nsg, nc))
    sq = pl.Squeezed()
    spec_r = pl.BlockSpec((sq, 1, ns), lambda z: (z, 0, 0))
    spec_b = pl.BlockSpec((sq, nc, ns), lambda z: (z, 0, 0))
    return pl.pallas_call(
        _s5_weights_kernel,
        grid=(nsg,),
        in_specs=[spec_r, spec_r, spec_r, spec_b, spec_b, spec_b, spec_b,
                  pl.BlockSpec((sq, nc, nc), lambda z: (z, 0, 0))],
        out_specs=[pl.BlockSpec((sq, T, nc, nc), lambda z: (z, 0, 0, 0)),
                   pl.BlockSpec((sq, T * nc, 2 * ns), lambda z: (z, 0, 0)),
                   pl.BlockSpec((sq, 2 * ns, T * nc), lambda z: (z, 0, 0)),
                   pl.BlockSpec((sq, 1, 2 * ns), lambda z: (z, 0, 0))],
        out_shape=[jax.ShapeDtypeStruct((nsg, T, nc, nc), BF16),
                   jax.ShapeDtypeStruct((nsg, T * nc, 2 * ns), BF16),
                   jax.ShapeDtypeStruct((nsg, 2 * ns, T * nc), BF16),
                   jax.ShapeDtypeStruct((nsg, 1, 2 * ns), F32)],
        compiler_params=_cparams(("arbitrary",)),
        name="s5_weights",
    )(rows(lam_re), rows(lam_im), rows(dtb),
      embed_b(b_re), embed_b(b_im), embed_c(c_re), embed_c(c_im), d_diag)


def _s5_scan_kernel(u_ref, um_ref, kt_ref, win_ref, wout_ref, at_ref, y_ref,
                    uc_ref, v_ref, xp_ref, m_ref, *, n_chunks):
    T = S5_T
    nc = u_ref.shape[1]
    ns = at_ref.shape[1] // 2

    @pl.when(pl.program_id(0) == 0)
    def _():
        zero = jnp.zeros((nc, nc), BF16)
        for s in range(1, T):
            for t in range(s):
                m_ref[s * nc:(s + 1) * nc, t * nc:(t + 1) * nc] = zero

    for tau in range(T):
        tap = kt_ref[tau]
        for s in range(T - tau):
            m_ref[s * nc:(s + 1) * nc, (s + tau) * nc:(s + tau + 1) * nc] = tap

    for t in range(T):
        uc_ref[0:n_chunks, t * nc:(t + 1) * nc] = u_ref[pl.ds(t, n_chunks, stride=T), :].astype(BF16)
        uc_ref[n_chunks:n_chunks + 1, t * nc:(t + 1) * nc] = um_ref[t:t + 1, :].astype(BF16)
    pad = uc_ref.shape[0] - n_chunks - 1
    uc_ref[n_chunks + 1:, :] = jnp.zeros((pad, T * nc), BF16)
    uc = uc_ref[...]
    v_ref[...] = _dot(uc, win_ref[...])
    a_re, a_im = at_ref[:, 0:ns], at_ref[:, ns:2 * ns]

    x_re0 = v_ref[n_chunks:n_chunks + 1, 0:ns]
    x_im0 = v_ref[n_chunks:n_chunks + 1, ns:2 * ns]
    xp_ref[n_chunks:, :] = jnp.zeros((pad + 1, 2 * ns), F32)

    def step(c, carry):
        x_re, x_im = carry
        xp_ref[pl.ds(c, 1), 0:ns] = x_re
        xp_ref[pl.ds(c, 1), ns:2 * ns] = x_im
        v_re = v_ref[pl.ds(c, 1), 0:ns]
        v_im = v_ref[pl.ds(c, 1), ns:2 * ns]
        return (a_re * x_re - a_im * x_im + v_re, a_re * x_im + a_im * x_re + v_im)

    lax.fori_loop(0, n_chunks, step, (x_re0, x_im0))
    xp = xp_ref[...].astype(BF16)
    for p in range(T // 2):
        k_hi = (2 * p + 2) * nc
        cols = slice(2 * p * nc, k_hi)
        yc = _dot(uc_ref[:, 0:k_hi], m_ref[0:k_hi, cols]) + _dot(xp, wout_ref[:, cols])
        for t in (2 * p, 2 * p + 1):
            y_ref[pl.ds(t, n_chunks, stride=T), :] = yc[0:n_chunks, (t - 2 * p) * nc:(t - 2 * p + 1) * nc]


def _s5_scan(u, um, kt, win, wout, at):
    u, col0 = u
    n, width = u.shape[0], um.shape[1]
    T = S5_T
    nsg = kt.shape[0]
    nc = width // nsg
    ns2 = at.shape[2]
    assert um.shape[0] == T and n % T == 0 and col0 % nc == 0
    n_chunks = n // T
    rows = -(-(n_chunks + 1) // 16) * 16
    sq = pl.Squeezed()
    return pl.pallas_call(
        functools.partial(_s5_scan_kernel, n_chunks=n_chunks),
        grid=(nsg,),
        in_specs=[pl.BlockSpec((n, nc), lambda z: (0, col0 // nc + z)),
                  pl.BlockSpec((T, nc), lambda z: (0, z)),
                  pl.BlockSpec((sq, T, nc, nc), lambda z: (z, 0, 0, 0)),
                  pl.BlockSpec((sq, T * nc, ns2), lambda z: (z, 0, 0)),
                  pl.BlockSpec((sq, ns2, T * nc), lambda z: (z, 0, 0)),
                  pl.BlockSpec((sq, 1, ns2), lambda z: (z, 0, 0))],
        out_specs=pl.BlockSpec((n, nc), lambda z: (0, z)),
        out_shape=jax.ShapeDtypeStruct((n, width), F32),
        scratch_shapes=[pltpu.VMEM((rows, T * nc), BF16), pltpu.VMEM((rows, ns2), F32),
                        pltpu.VMEM((rows, ns2), F32), pltpu.VMEM((T * nc, T * nc), BF16)],
        compiler_params=_cparams(("arbitrary",)),
        name="s5_scan",
    )(u, um, kt, win, wout, at)


def _merge_kernel(*refs, pieces):
    h_ref, o_ref, y_ref = refs[0:3]
    gh_refs, gs_refs = refs[3:3 + pieces], refs[3 + pieces:3 + 2 * pieces]
    wp_ref, wa_ref, wb_ref, wo_ref, gn_ref, out_ref, norm_ref = refs[3 + 2 * pieces:]
    gh = jnp.concatenate([r[...] for r in gh_refs], axis=1).astype(F32)
    gs = jnp.concatenate([r[...] for r in gs_refs], axis=1).astype(F32)
    y_hg = _dot(o_ref[...], wp_ref[...])
    z = jax.nn.gelu(y_ref[...]).astype(BF16)
    y_s5 = _dot(z, wa_ref[...]) * _sigmoid(_dot(z, wb_ref[...]))
    merged = gh * y_hg + gs * y_s5
    h = h_ref[...] + _dot(merged.astype(BF16), wo_ref[...])
    out_ref[...] = h
    norm_ref[...] = _rms(h, gn_ref[...]).astype(BF16)


def _merge(hx, o, y, gh, gs, wp, wa, wb, wo, g_next, *, tm=256):
    n, d = hx.shape
    row = lambda i: (i, 0)
    const = lambda i: (0, 0)
    gw = functools.reduce(math.gcd, (gh[1], gs[1], d))
    assert gw % LANE == 0
    pieces = d // gw

    def resident(shape):
        return pl.BlockSpec(shape, const, pipeline_mode=pl.Buffered(1))

    def view(col0):
        return [pl.BlockSpec((tm, gw), functools.partial(lambda i, k: (i, k), k=col0 // gw + k))
                for k in range(pieces)]

    return pl.pallas_call(
        functools.partial(_merge_kernel, pieces=pieces),
        grid=(n // tm,),
        in_specs=[pl.BlockSpec((tm, d), row), pl.BlockSpec((tm, o.shape[1]), row),
                  pl.BlockSpec((tm, y.shape[1]), row)] + view(gh[1]) + view(gs[1])
        + [resident(wp.shape), resident(wa.shape), resident(wb.shape), resident(wo.shape),
           pl.BlockSpec((1, d), const)],
        out_specs=[pl.BlockSpec((tm, d), row)] * 2,
        out_shape=[jax.ShapeDtypeStruct((n, d), F32), jax.ShapeDtypeStruct((n, d), BF16)],
        compiler_params=_cparams(("arbitrary",)),
        name="mixer_merge",
    )(hx, o, y, *([gh[0]] * pieces), *([gs[0]] * pieces), wp, wa, wb, wo, g_next.reshape(1, d))


def kernel(x, meta_tokens, norm_ffn1, ffn1_w_gate, ffn1_w_up, ffn1_w_down, norm_mix, w_in, hg_lower_bounds, hg_norm, hg_w_proj, s5_lam_re, s5_lam_im, s5_log_step, s5_b_re, s5_b_im, s5_c_re, s5_c_im, s5_d, s5_w_glu_a, s5_w_glu_b, w_out, norm_ffn2, ffn2_w_gate, ffn2_w_up, ffn2_w_down, norm_final):
    batch, seq, d = x.shape
    depth = norm_ffn1.shape[0]
    assert batch == 1 and depth == 1
    hg_k = hg_lower_bounds.shape[1]
    hg_w = hg_norm.shape[1]
    s5_w = s5_w_glu_a.shape[1]
    hx = x[0].astype(F32)
    hm = meta_tokens.astype(F32)
    tm = min(512, seq)
    h0, n0, hm, nm, w1_gate, w1_up, w1_down = _ffn(
        hx, hm, norm_ffn1[0], norm_mix[0], ffn1_w_gate[0], ffn1_w_up[0], ffn1_w_down[0],
        emit_norm=True, rows=tm, tm=tm, tf=256, name="ffn_head")
    sizes = (hg_k, hg_k, hg_w, hg_w, s5_w, d, d)
    assert sum(sizes) == w_in.shape[2]
    off = [0]
    for s in sizes:
        off.append(off[-1] + s)
    ids32, ids16 = (0, 1, 4), (2, 3, 5, 6)
    tc_in = functools.reduce(math.gcd, sizes + (512,))

    def starts(ids):
        out, col = {}, 0
        for k in ids:
            out[k] = col
            col += sizes[k]
        return out

    def blocks(ids):
        return [c for k in ids for c in range(off[k] // tc_in, off[k + 1] // tc_in)]

    steps = lambda grid: grid[0] * grid[1]
    n_body = (seq // tm) * (ffn1_w_gate.shape[2] // 512)
    n_inp = steps(_inproj_grid(seq, sum(sizes[k] for k in ids16))[2])
    n_hg = steps(_hgrn_grid(seq, hg_w)[2])
    hx, nx, w32, w16 = _ffn(
        hx, None, norm_ffn1[0], norm_mix[0], w1_gate, w1_up, w1_down, emit_norm=True, head=(h0, n0),
        cast=(_cast_tile(w_in[0], n_body, tc_in, [blocks(ids32), blocks(ids16)]),),
        tm=tm, name="ffn_body")

    kind = ("silu", "forget", "ident", "silu", "ident", "sigm", "sigm")
    scale = {0: HG_DK ** -0.5, 3: hg_norm[0].astype(F32)}

    def segments(ids):
        return tuple((sizes[k], kind[k]) for k in ids)

    def scale_row(ids):
        return jnp.concatenate([jnp.broadcast_to(jnp.asarray(scale.get(k, 1.0), F32), (sizes[k],))
                                for k in ids])[None]

    s32, s16 = starts(ids32), starts(ids16)
    lb32 = jnp.pad(hg_lower_bounds.astype(F32),
                   ((0, 0), (s32[1], sum(sizes[k] for k in ids32) - s32[1] - hg_k)))
    x32, m32 = _inproj(nx, nm, w32, segments(ids32), scale_row(ids32), lb32, F32)
    x16, m16, w_proj, w_glu_a, w_glu_b, w_o = _inproj(
        nx, nm, w16, segments(ids16), scale_row(ids16), None, BF16,
        cast=[_cast_tile(w, n_inp) for w in (hg_w_proj[0], s5_w_glu_a[0], s5_w_glu_b[0], w_out[0])])

    o, w2_gate, w2_up, w2_down = _hgrn(
        (x32, s32[0]), (x32, s32[1]), (x16, s16[2]), (x16, s16[3]),
        m32[:, s32[1]:s32[1] + hg_k], m16[:, s16[2]:s16[2] + hg_w],
        cast=[_cast_tile(w, n_hg) for w in (ffn2_w_gate[0], ffn2_w_up[0], ffn2_w_down[0])])
    m, win, wout, at = _s5_weights(s5_lam_re[0], s5_lam_im[0], s5_log_step[0], s5_b_re[0],
                                   s5_b_im[0], s5_c_re[0], s5_c_im[0], s5_d[0])
    y = _s5_scan((x32, s32[4]), m32[:, s32[4]:s32[4] + s5_w], m, win, wout, at)
    hx, nx2 = _merge(hx, o, y, (x16, s16[5]), (x16, s16[6]), w_proj, w_glu_a, w_glu_b, w_o,
                     norm_ffn2[0])
    out, = _ffn(hx, None, nx2, norm_final, w2_gate, w2_up, w2_down, emit_norm=False,
                tm=tm, name="ffn_final")
    return out[None].astype(x.dtype)
```

```python
import functools
import math

import numpy as np
import jax
import jax.numpy as jnp
from jax import lax
from jax.experimental import pallas as pl
from jax.experimental.pallas import tpu as pltpu

EPS = 1e-6
HG_DK = 128
HG_CHUNK = 64
S5_GROUP = 16
S5_STATE = 64
S5_T = 16
S5_SG = 8
LANE = 128
INPROJ_SUB = 256
VMEM_LIMIT = 56 * 1024 * 1024

F32 = jnp.float32
BF16 = jnp.bfloat16


def _cparams(sem):
    return pltpu.CompilerParams(dimension_semantics=sem, vmem_limit_bytes=VMEM_LIMIT)


def _rms(v, g):
    return v * lax.rsqrt(jnp.mean(v * v, axis=-1, keepdims=True) + EPS) * g


def _sigmoid(x):
    return 0.5 * jnp.tanh(0.5 * x) + 0.5


def _silu(x):
    h = 0.5 * x
    return h * (jnp.tanh(h) + 1.0)


def _dot(a, b):
    return jnp.dot(a, b, preferred_element_type=F32)


def _dot_nt(a, b):
    return lax.dot_general(a, b, (((1,), (1,)), ((), ())), preferred_element_type=F32)


def _dot_tn(a, b):
    return lax.dot_general(a, b, (((0,), (0,)), ((), ())), preferred_element_type=F32)


def _cast_jobs(cast, grid):
    nj = grid[1]
    in_specs, out_specs, out_shapes, static = [], [], [], []
    for src, (tr, tc), dests in cast:
        rows, cols = src.shape
        ncb = cols // tc
        n_tiles = (rows // tr) * ncb
        assert rows % tr == 0 and cols % tc == 0 and n_tiles <= grid[0] * nj

        def tile(i, j, n_tiles=n_tiles, ncb=ncb):
            t = jnp.minimum(i * nj + j, n_tiles - 1)
            return t // ncb, t % ncb

        in_specs.append(pl.BlockSpec((tr, tc), tile))
        for blocks in dests:
            park, k = [], 0
            for c in range(ncb):
                k = blocks.index(c) if c in blocks else k
                park.append(k)
            jumps = [(m, park[m] - park[m - 1]) for m in range(1, ncb) if park[m] != park[m - 1]]

            def dest(i, j, tile=tile, first=park[0], jumps=jumps):
                r, c = tile(i, j)
                return r, first + sum(jnp.where(c >= m, dk, 0) for m, dk in jumps)

            out_specs.append(pl.BlockSpec((tr, tc), tile if park == list(range(ncb)) else dest))
            out_shapes.append(jax.ShapeDtypeStruct((rows, len(blocks) * tc), BF16))
        static.append((n_tiles, ncb, tuple(tuple(b) for b in dests)))
    return in_specs, out_specs, out_shapes, tuple(static)


def _n_cast_out(jobs):
    return sum(len(dests) for _, _, dests in jobs)


def _run_cast_jobs(cast_in, cast_out, jobs):
    step = pl.program_id(0) * pl.num_programs(1) + pl.program_id(1)
    outs = iter(cast_out)
    for src, (n_tiles, ncb, dests) in zip(cast_in, jobs):
        dsts = [next(outs) for _ in dests]
        tile = src[...].astype(BF16)
        if len(dsts) == 1:
            dsts[0][...] = tile
        else:
            c = jnp.minimum(step, n_tiles - 1) % ncb
            for dst, cols in zip(dsts, dests):
                owns = functools.reduce(jnp.logical_or, [c == m for m in cols])

                @pl.when(owns)
                def _(dst=dst):
                    dst[...] = tile


def _cast_tile(w, n_steps, tc=None, dests=None):
    rows, cols = w.shape
    if tc is None and rows % n_steps == 0 and (rows // n_steps) % 16 == 0:
        return w, (rows // n_steps, cols), [[0]]
    tc = math.gcd(tc or 512, cols)
    tr = 128
    while (rows // tr) * (cols // tc) > n_steps:
        tr *= 2
    return w, (tr, tc), dests or [list(range(cols // tc))]


def _ffn_kernel(*refs, tm, n_meta, emit_norm, w_f32, head, jobs):
    it = iter(refs)
    take = lambda k: [next(it) for _ in range(k)]
    x_ref, = take(1)
    m_ref = take(1)[0] if n_meta else None
    g_ref, gn_ref, wg_ref, wu_ref, wd_ref = take(5)
    hx_ref, hn16_ref = take(2) if head else (None, None)
    cast_in = take(len(jobs))
    ox_ref, = take(1)
    nx_ref = take(1)[0] if emit_norm else None
    om_ref, nm_ref = take(2) if n_meta else (None, None)
    w16_out = take(3) if w_f32 else ()
    cast_out = take(_n_cast_out(jobs))
    hn_ref, = take(1)
    acc_ref = take(1)[0] if n_meta else ox_ref
    i = pl.program_id(0)
    j = pl.program_id(1)
    _run_cast_jobs(cast_in, cast_out, jobs)

    def body():
        @pl.when(j == 0)
        def _():
            hn_ref[0:tm, :] = _rms(x_ref[...], g_ref[...]).astype(BF16)
            if n_meta:
                hn_ref[tm:tm + n_meta, :] = _rms(m_ref[...], g_ref[...]).astype(BF16)
            acc_ref[...] = jnp.zeros_like(acc_ref)

        tiles = [w[...].astype(BF16) if w_f32 else w[...] for w in (wg_ref, wu_ref, wd_ref)]
        for dst, tile in zip(w16_out, tiles):
            dst[...] = tile
        hn = hn_ref[...]
        gate = _dot(hn, tiles[0])
        up = _dot(hn, tiles[1])
        act = (_silu(gate) * up).astype(BF16)
        acc_ref[...] += _dot(act, tiles[2])

        @pl.when(j == pl.num_programs(1) - 1)
        def _():
            hx = x_ref[...] + 0.5 * acc_ref[0:tm, :]
            if emit_norm:
                ox_ref[...] = hx
                nx_ref[...] = _rms(hx, gn_ref[...]).astype(BF16)
            else:
                ox_ref[...] = _rms(hx, gn_ref[...])
            if n_meta:
                hm = m_ref[...] + 0.5 * acc_ref[tm:tm + n_meta, :]
                om_ref[...] = hm
                nm_ref[...] = _rms(hm, gn_ref[...]).astype(BF16)

    if head:
        @pl.when((i == 0) & (j == 0))
        def _():
            ox_ref[...] = hx_ref[...]
            nx_ref[...] = hn16_ref[...]

        pl.when(i > 0)(body)
    else:
        body()


def _ffn(hx, hm, g, g_next, wg, wu, wd, *, emit_norm, head=None, cast=(), rows=None, name,
         tm=512, tf=512):
    n, d = hx.shape
    rows = n if rows is None else rows
    dff = wg.shape[1]
    n_meta = 0 if hm is None else hm.shape[0]
    w_f32 = wg.dtype == F32
    grid = (rows // tm, dff // tf)
    assert not w_f32 or grid[0] == 1
    row = lambda i, j: (i, 0)
    const = lambda i, j: (0, 0)
    in_specs = [pl.BlockSpec((tm, d), row)]
    args = [hx]
    if n_meta:
        in_specs.append(pl.BlockSpec((n_meta, d), const))
        args.append(hm)
    col = (lambda i, j: jnp.where(i == 0, 0, j)) if head else (lambda i, j: j)
    in_specs += [pl.BlockSpec((1, d), const), pl.BlockSpec((1, d), const),
                 pl.BlockSpec((d, tf), lambda i, j: (0, col(i, j))),
                 pl.BlockSpec((d, tf), lambda i, j: (0, col(i, j))),
                 pl.BlockSpec((tf, d), lambda i, j: (col(i, j), 0))]
    args += [g.reshape(1, d), g_next.reshape(1, d), wg, wu, wd]
    if head:
        in_specs += [pl.BlockSpec((tm, d), const, pipeline_mode=pl.Buffered(1))] * 2
        args += list(head)
    job_in, job_out, job_shapes, jobs = _cast_jobs(cast, grid)
    in_specs += job_in
    args += [src for src, _, _ in cast]
    out_shape = [jax.ShapeDtypeStruct((rows, d), F32)]
    out_specs = [pl.BlockSpec((tm, d), row)]
    if emit_norm:
        out_shape.append(jax.ShapeDtypeStruct((rows, d), BF16))
        out_specs.append(pl.BlockSpec((tm, d), row))
    if n_meta:
        out_shape += [jax.ShapeDtypeStruct((n_meta, d), F32), jax.ShapeDtypeStruct((n_meta, d), BF16)]
        out_specs += [pl.BlockSpec((n_meta, d), const)] * 2
    if w_f32:
        out_shape += [jax.ShapeDtypeStruct(w.shape, BF16) for w in (wg, wu, wd)]
        out_specs += [pl.BlockSpec((d, tf), lambda i, j: (0, j)),
                      pl.BlockSpec((d, tf), lambda i, j: (0, j)),
                      pl.BlockSpec((tf, d), lambda i, j: (j, 0))]
    out_specs += job_out
    out_shape += job_shapes
    return pl.pallas_call(
        functools.partial(_ffn_kernel, tm=tm, n_meta=n_meta, emit_norm=emit_norm, w_f32=w_f32,
                          head=head is not None, jobs=jobs),
        grid=grid, in_specs=in_specs, out_specs=out_specs, out_shape=out_shape,
        scratch_shapes=[pltpu.VMEM((tm + n_meta, d), BF16)]
        + ([pltpu.VMEM((tm + n_meta, d), F32)] if n_meta else []),
        compiler_params=_cparams(("arbitrary", "arbitrary")),
        name=name,
    )(*args)


def _inproj_kernel(*refs, tm, n_meta, with_lb, kinds, jobs):
    it = iter(refs)
    take = lambda k: [next(it) for _ in range(k)]
    x_ref, m_ref, w_ref, scale_ref = take(4)
    lb_ref = take(1)[0] if with_lb else None
    cast_in = take(len(jobs))
    ox_ref, om_ref = take(2)
    cast_out = take(_n_cast_out(jobs))
    lhs_ref, = take(1)
    _run_cast_jobs(cast_in, cast_out, jobs)

    @pl.when(pl.program_id(1) == 0)
    def _():
        lhs_ref[0:tm, :] = x_ref[...]
        lhs_ref[tm:tm + n_meta, :] = m_ref[...]

    tn = w_ref.shape[1]
    sw = tn // len(kinds[0])

    def column_step(step_kinds):
        for s, kind in enumerate(step_kinds):
            cs = slice(s * sw, (s + 1) * sw)
            res = _dot(lhs_ref[...], w_ref[:, cs])
            if kind == "ident":
                out = res
            elif kind == "silu":
                h = 0.5 * res
                out = (scale_ref[:, cs] * h) * (jnp.tanh(h) + 1.0)
            elif kind == "sigm":
                out = 0.5 * jnp.tanh(0.5 * res) + 0.5
            else:
                assert kind == "forget"
                lbl = lb_ref[:, cs]
                e = jnp.exp(lbl - jnp.max(lbl, axis=0, keepdims=True))
                lb = e[0:1, :] / jnp.sum(e, axis=0, keepdims=True)
                out = (0.5 - 0.5 * lb) * jnp.tanh(0.5 * res) + (0.5 + 0.5 * lb)
            ox_ref[:, cs] = out[0:tm, :].astype(ox_ref.dtype)
            om_ref[:, cs] = out[tm:tm + n_meta, :].astype(om_ref.dtype)

    if len(set(kinds)) == 1:
        column_step(kinds[0])
    else:
        for jj, step_kinds in enumerate(kinds):
            pl.when(pl.program_id(1) == jj)(functools.partial(column_step, step_kinds))


def _inproj_grid(n, cols, tm=1024, tn=1792):
    tm = min(tm, n)
    tn = math.gcd(tn, cols)
    assert tn % LANE == 0 and n % tm == 0
    return tm, tn, (n // tm, cols // tn)


def _inproj(nx, nm, w, segments, scale, lower_bounds, dtype, cast=()):
    n, d = nx.shape
    n_meta = nm.shape[0]
    cols = w.shape[1]
    with_lb = lower_bounds is not None
    tm, tn, grid = _inproj_grid(n, cols)
    sw = math.gcd(tn, INPROJ_SUB)
    col_kind = [kind for width, kind in segments for _ in range(width // sw)]
    assert all(width % sw == 0 for width, _ in segments) and len(col_kind) * sw == cols
    kinds = tuple(tuple(col_kind[jj * (tn // sw):(jj + 1) * (tn // sw)]) for jj in range(grid[1]))
    job_in, job_out, job_shapes, jobs = _cast_jobs(cast, grid)
    in_specs = [pl.BlockSpec((tm, d), lambda i, j: (i, 0)),
                pl.BlockSpec((n_meta, d), lambda i, j: (0, 0)),
                pl.BlockSpec((d, tn), lambda i, j: (0, j)),
                pl.BlockSpec((1, tn), lambda i, j: (0, j))]
    args = [nx, nm, w, scale]
    if with_lb:
        in_specs.append(pl.BlockSpec((lower_bounds.shape[0], tn), lambda i, j: (0, j)))
        args.append(lower_bounds)
    ox, om, *copies = pl.pallas_call(
        functools.partial(_inproj_kernel, tm=tm, n_meta=n_meta, with_lb=with_lb, kinds=kinds,
                          jobs=jobs),
        grid=grid,
        in_specs=in_specs + job_in,
        out_specs=[pl.BlockSpec((tm, tn), lambda i, j: (i, j)),
                   pl.BlockSpec((pl.Squeezed(), n_meta, tn), lambda i, j: (i, 0, j))] + job_out,
        out_shape=[jax.ShapeDtypeStruct((n, cols), dtype),
                   jax.ShapeDtypeStruct((n // tm, n_meta, cols), dtype)] + job_shapes,
        scratch_shapes=[pltpu.VMEM((tm + n_meta, d), BF16)],
        compiler_params=_cparams(("arbitrary", "arbitrary")),
        name="mixer_inproj_" + jnp.dtype(dtype).name,
    )(*args, *[src for src, _, _ in cast])
    return (ox, om[0], *copies)


def _split3(x):
    hi = x.astype(BF16)
    r1 = x - hi.astype(F32)
    mid = r1.astype(BF16)
    lo = (r1 - mid.astype(F32)).astype(BF16)
    return hi, mid, lo


def _dot3(m01, parts):
    return _dot(m01, parts[0]) + _dot(m01, parts[1]) + _dot(m01, parts[2])


def _tri(n, rep=1):
    r = lax.broadcasted_iota(jnp.int32, (n * rep, n), 0)
    c = lax.broadcasted_iota(jnp.int32, (n * rep, n), 1)
    return jnp.where(r >= c * rep, 1.0, 0.0).astype(BF16)


def _hgrn_tables(c):
    sums = [np.tril(np.ones((c, c)))]
    level = np.full((c, c), -1, np.int32)
    t_idx, s_idx = np.meshgrid(np.arange(c), np.arange(c), indexing="ij")
    halves = []
    h = c // 2
    while h >= 1:
        if h < 8:
            m = np.zeros((c, c))
            for r in range(c):
                mid = (r // (2 * h)) * 2 * h + h
                if r < mid:
                    m[r, r + 1:mid] = 1.0
                else:
                    m[r, mid:r + 1] = 1.0
            sums.append(m)
        pair = 2 * h
        level[(t_idx // pair == s_idx // pair) & (s_idx % pair < h) & (t_idx % pair >= h)] = len(halves)
        halves.append(h)
        h //= 2
    assert len(halves) % 2 == 0
    level[np.arange(c), np.arange(c)] = len(halves)
    owner = np.full((c, 2 * c), -1, np.int32)
    owner[:, :c] = np.where((level >= 0) & (level % 2 == 0), level // 2, -1)
    owner[:, c:] = np.where(level % 2 == 1, level // 2, -1)
    return np.concatenate(sums, axis=0), owner, tuple(halves)


def _hgrn_score_products(q, k, x, halves):
    C = HG_CHUNK
    b = x[0:C]
    zero = jnp.zeros((C, HG_DK), BF16)
    diag = jnp.sum(q * k, axis=-1, keepdims=True)
    qts, kts = [], []
    fine = 0
    for h in halves:
        if h >= 8:
            none = jnp.zeros((h, HG_DK), F32)
            qh, kh = [], []
            for r in range(0, C, 2 * h):
                b_mid = b[r + h - 1:r + h, :]
                kh += [k[r:r + h] * jnp.exp(b_mid - b[r:r + h]), none]
                qh += [none, q[r + h:r + 2 * h] * jnp.exp(b[r + h:r + 2 * h] - b_mid)]
            qts.append(jnp.concatenate(qh, axis=0).astype(BF16))
            kts.append(jnp.concatenate(kh, axis=0).astype(BF16))
        else:
            fine += 1
            e = jnp.exp(x[fine * C:(fine + 1) * C])
            qts.append((q * e).astype(BF16))
            kts.append((k * e).astype(BF16))
    prods = []
    for p in range(len(halves) // 2):
        lhs = jnp.concatenate([qts[2 * p], qts[2 * p + 1]], axis=1)
        rhs = jnp.concatenate([jnp.concatenate([kts[2 * p], zero], axis=1),
                               jnp.concatenate([zero, kts[2 * p + 1]], axis=1)], axis=0)
        prods.append(_dot_nt(lhs, rhs))
    return prods, diag


def _hgrn_pair(q2, f2, v2, gate2, st2, sums, owns, halves):
    C, D = HG_CHUNK, HG_DK
    halfs = (slice(0, D), slice(D, 2 * D))
    k2 = 1.0 - f2
    x2 = _dot(sums, jnp.concatenate(_split3(jnp.log(f2)), axis=0))
    yield
    b2 = x2[0:C]
    b_last = b2[C - 1:C, :]
    kdec2 = (k2 * jnp.exp(b_last - b2)).astype(BF16)
    qe2 = (q2 * jnp.exp(b2)).astype(BF16)
    st2b = st2.astype(BF16)
    scores = []
    for c in halfs:
        scores.append(_hgrn_score_products(q2[:, c], k2[:, c], x2[:, c], halves))
        yield
    zero = jnp.zeros((2 * C, D), BF16)
    os, vvts, kds = [], [], []
    for j, c in enumerate(halfs):
        prods, diag = scores[j]
        a = jnp.where(owns[len(prods)], diag, 0.0)
        for p, prod in enumerate(prods):
            a = jnp.where(owns[p], prod, a)
        v = v2[:, c]
        vvt = jnp.concatenate([v, v], axis=0).T
        os.append(_dot_nt(jnp.concatenate([qe2[:, c], a.astype(BF16)], axis=1),
                          jnp.concatenate([st2b[:, c], vvt], axis=1)))
        vvts.append(vvt)
        kd = jnp.concatenate([kdec2[:, c], jnp.zeros((C, D), BF16)], axis=0)
        kds.append(jnp.concatenate([kd, zero] if j == 0 else [zero, kd], axis=1))
        yield
    st_new = st2 * jnp.exp(b_last) + _dot(jnp.concatenate(vvts, axis=1), jnp.concatenate(kds, axis=0))
    yield
    ons = [o * lax.rsqrt(jnp.mean(o * o, axis=-1, keepdims=True) + EPS) for o in os]
    on2 = jnp.concatenate(ons, axis=1) * gate2
    return on2.astype(BF16), st_new


def _interleave(gens):
    out = [None] * len(gens)
    live = list(range(len(gens)))
    while live:
        for g in list(live):
            try:
                next(gens[g])
            except StopIteration as stop:
                out[g] = stop.value
                live.remove(g)
    return out


def _hgrn_kernel(q_ref, f_ref, v_ref, go_ref, fm_ref, vm_ref, sums_ref, owner_ref,
                 *rest, n_chunks, n_meta, hb, halves, jobs):
    cast_in, o_ref = rest[:len(jobs)], rest[len(jobs)]
    cast_out, st_ref = rest[len(jobs) + 1:-1], rest[-1]
    _run_cast_jobs(cast_in, cast_out, jobs)
    i = pl.program_id(0)
    p0 = pl.program_id(1) * (hb // 2)
    C = HG_CHUNK
    W = 2 * HG_DK

    @pl.when(i == 0)
    def _():
        tri_m = _tri(n_meta)
        for pair in range(hb // 2):
            sts = []
            for hh in (2 * pair, 2 * pair + 1):
                cols = slice(hh * HG_DK, (hh + 1) * HG_DK)
                fm = fm_ref[:, cols]
                bm = _dot3(tri_m, _split3(jnp.log(fm)))
                kdec = ((1.0 - fm) * jnp.exp(bm[n_meta - 1:n_meta, :] - bm)).astype(BF16)
                sts.append(_dot_tn(vm_ref[:, cols], kdec))
            st_ref[p0 + pair] = jnp.concatenate(sts, axis=1)

    def chunk(c, states):
        r0 = pl.multiple_of(c * C, C)
        owner = owner_ref[...]
        owns = [owner == p for p in range(len(halves) // 2 + 1)]
        sums = sums_ref[...]
        gens = []
        for pair in range(hb // 2):
            cols = slice(pair * W, (pair + 1) * W)
            gate2 = go_ref[pl.ds(r0, C), cols].astype(F32)
            gens.append(_hgrn_pair(q_ref[pl.ds(r0, C), cols], f_ref[pl.ds(r0, C), cols],
                                   v_ref[pl.ds(r0, C), cols], gate2, states[pair],
                                   sums, owns, halves))
        results = _interleave(gens)
        o_ref[pl.ds(r0, C), :] = jnp.concatenate([r[0] for r in results], axis=1)
        return tuple(r[1] for r in results)

    states = lax.fori_loop(0, n_chunks, chunk, tuple(st_ref[p0 + p] for p in range(hb // 2)))
    for pair in range(hb // 2):
        st_ref[p0 + pair] = states[pair]


def _hgrn_grid(n, width, rb=512, hb=12):
    heads = width // HG_DK
    hb = min(hb, heads)
    assert hb % 2 == 0 and heads % hb == 0
    return rb, hb, (n // rb, heads // hb)


def _hgrn(q, f, v, go, fm, vm, cast=()):
    n, width = q[0].shape[0], fm.shape[1]
    heads = width // HG_DK
    rb, hb, grid = _hgrn_grid(n, width)
    bw = hb * HG_DK
    n_meta = fm.shape[0]
    job_in, job_out, job_shapes, jobs = _cast_jobs(cast, grid)
    sums, owner, halves = _hgrn_tables(HG_CHUNK)
    sums3 = jnp.asarray(np.concatenate([sums] * 3, axis=1), BF16)
    blk = lambda i, h: (i, h)
    mblk = lambda i, h: (0, h)
    const = lambda i, h: (0, 0)

    def view(col0):
        assert col0 % bw == 0
        return pl.BlockSpec((rb, bw), lambda i, h: (i, col0 // bw + h))

    return pl.pallas_call(
        functools.partial(_hgrn_kernel, n_chunks=rb // HG_CHUNK, n_meta=n_meta, hb=hb,
                          halves=halves, jobs=jobs),
        grid=grid,
        in_specs=[view(q[1]), view(f[1]), view(v[1]), view(go[1])]
        + [pl.BlockSpec((n_meta, hb * HG_DK), mblk)] * 2
        + [pl.BlockSpec(sums3.shape, const), pl.BlockSpec(owner.shape, const)] + job_in,
        out_specs=[pl.BlockSpec((rb, hb * HG_DK), blk)] + job_out,
        out_shape=[jax.ShapeDtypeStruct((n, width), BF16)] + job_shapes,
        scratch_shapes=[pltpu.VMEM((heads // 2, HG_DK, 2 * HG_DK), F32)],
        compiler_params=_cparams(("arbitrary", "arbitrary")),
        name="hgrn2",
    )(q[0], f[0], v[0], go[0], fm, vm, sums3, jnp.asarray(owner), *[src for src, _, _ in cast])


def _s5_weights_kernel(lr_ref, li_ref, dt_ref, bre_ref, bim_ref, cre_ref, cim_ref, dsk_ref,
                       kt_ref, win_ref, wout_ref, at_ref):
    T = S5_T
    ns = lr_ref.shape[1]
    nc = cre_ref.shape[0]
    lr, li, dt = lr_ref[...], li_ref[...], dt_ref[...]

    def powers(tau):
        mag = jnp.exp(lr * dt * tau)
        ang = li * dt * tau
        return mag * jnp.cos(ang), mag * jnp.sin(ang)

    a_re, a_im = powers(1.0)
    num_re, num_im = a_re - 1.0, a_im
    den = lr * lr + li * li
    coef_re = (num_re * lr + num_im * li) / den
    coef_im = (num_im * lr - num_re * li) / den
    br, bi = bre_ref[...], bim_ref[...]
    bbar_re = coef_re * br - coef_im * bi
    bbar_im = coef_re * bi + coef_im * br

    pw = [powers(float(tau)) for tau in range(T + 1)]
    for s in range(T):
        p_re, p_im = pw[T - 1 - s]
        win_ref[s * nc:(s + 1) * nc, 0:ns] = (bbar_re * p_re - bbar_im * p_im).astype(BF16)
        win_ref[s * nc:(s + 1) * nc, ns:2 * ns] = (bbar_re * p_im + bbar_im * p_re).astype(BF16)
    at_ref[:, 0:ns] = pw[T][0]
    at_ref[:, ns:2 * ns] = pw[T][1]

    cr, ci = cre_ref[...], cim_ref[...]
    def split2(a):
        hi = a.astype(BF16)
        return hi, (a - hi.astype(F32)).astype(BF16)

    bb_hi, bb_lo = split2(jnp.concatenate([bbar_re, -bbar_im], axis=1))
    for tau in range(T + 1):
        p_re, p_im = pw[tau]
        ca_re = cr * p_re - ci * p_im
        ca_im = cr * p_im + ci * p_re
        if tau >= 1:
            t = tau - 1
            wout_ref[0:ns, t * nc:(t + 1) * nc] = ca_re.T.astype(BF16)
            wout_ref[ns:2 * ns, t * nc:(t + 1) * nc] = (-ca_im).T.astype(BF16)
        if tau < T:
            ca_hi, ca_lo = split2(jnp.concatenate([ca_re, ca_im], axis=1))
            kt = _dot_nt(bb_hi, ca_hi) + _dot_nt(bb_hi, ca_lo) + _dot_nt(bb_lo, ca_hi)
            if tau == 0:
                kt = kt + dsk_ref[...]
            kt_ref[tau] = kt.astype(BF16)


def _s5_weights(lam_re, lam_im, log_step, b_re, b_im, c_re, c_im, d_skip):
    G, N = lam_re.shape
    P = d_skip.shape[1]
    nsg = G // S5_SG
    ns, nc = S5_SG * N, S5_SG * P
    T = S5_T
    dt = jnp.exp(log_step.astype(F32))
    eye = jnp.eye(S5_SG, dtype=F32)

    def rows(a):
        return a.astype(F32).reshape(nsg, 1, ns)

    def embed_b(b):
        b = b.astype(F32).reshape(nsg, S5_SG, N, P)
        return jnp.einsum('zgnq,gh->zhqgn', b, eye).reshape(nsg, nc, ns)

    def embed_c(c):
        c = c.astype(F32).reshape(nsg, S5_SG, P, N)
        return jnp.einsum('zgpn,gh->zhpgn', c, eye).reshape(nsg, nc, ns)

    dtb = jnp.broadcast_to(dt[:, None], (G, N))
    d_diag = jax.vmap(jnp.diag)(d_skip.astype(F32).reshape(nsg, nc))
    sq = pl.Squeezed()
    spec_r = pl.BlockSpec((sq, 1, ns), lambda z: (z, 0, 0))
    spec_b = pl.BlockSpec((sq, nc, ns), lambda z: (z, 0, 0))
    return pl.pallas_call(
        _s5_weights_kernel,
        grid=(nsg,),
        in_specs=[spec_r, spec_r, spec_r, spec_b, spec_b, spec_b, spec_b,
                  pl.BlockSpec((sq, nc, nc), lambda z: (z, 0, 0))],
        out_specs=[pl.BlockSpec((sq, T, nc, nc), lambda z: (z, 0, 0, 0)),
                   pl.BlockSpec((sq, T * nc, 2 * ns), lambda z: (z, 0, 0)),
                   pl.BlockSpec((sq, 2 * ns, T * nc), lambda z: (z, 0, 0)),
                   pl.BlockSpec((sq, 1, 2 * ns), lambda z: (z, 0, 0))],
        out_shape=[jax.ShapeDtypeStruct((nsg, T, nc, nc), BF16),
                   jax.ShapeDtypeStruct((nsg, T * nc, 2 * ns), BF16),
                   jax.ShapeDtypeStruct((nsg, 2 * ns, T * nc), BF16),
                   jax.ShapeDtypeStruct((nsg, 1, 2 * ns), F32)],
        compiler_params=_cparams(("arbitrary",)),
        name="s5_weights",
    )(rows(lam_re), rows(lam_im), rows(dtb),
      embed_b(b_re), embed_b(b_im), embed_c(c_re), embed_c(c_im), d_diag)


def _s5_scan_kernel(u_ref, um_ref, kt_ref, win_ref, wout_ref, at_ref, y_ref,
                    uc_ref, v_ref, xp_ref, m_ref, *, n_chunks):
    T = S5_T
    nc = u_ref.shape[1]
    ns = at_ref.shape[1] // 2

    @pl.when(pl.program_id(0) == 0)
    def _():
        zero = jnp.zeros((nc, nc), BF16)
        for s in range(1, T):
            for t in range(s):
                m_ref[s * nc:(s + 1) * nc, t * nc:(t + 1) * nc] = zero

    for tau in range(T):
        tap = kt_ref[tau]
        for s in range(T - tau):
            m_ref[s * nc:(s + 1) * nc, (s + tau) * nc:(s + tau + 1) * nc] = tap

    for t in range(T):
        uc_ref[0:n_chunks, t * nc:(t + 1) * nc] = u_ref[pl.ds(t, n_chunks, stride=T), :].astype(BF16)
        uc_ref[n_chunks:n_chunks + 1, t * nc:(t + 1) * nc] = um_ref[t:t + 1, :].astype(BF16)
    pad = uc_ref.shape[0] - n_chunks - 1
    uc_ref[n_chunks + 1:, :] = jnp.zeros((pad, T * nc), BF16)
    uc = uc_ref[...]
    v_ref[...] = _dot(uc, win_ref[...])
    a_re, a_im = at_ref[:, 0:ns], at_ref[:, ns:2 * ns]

    x_re0 = v_ref[n_chunks:n_chunks + 1, 0:ns]
    x_im0 = v_ref[n_chunks:n_chunks + 1, ns:2 * ns]
    xp_ref[n_chunks:, :] = jnp.zeros((pad + 1, 2 * ns), F32)

    def step(c, carry):
        x_re, x_im = carry
        xp_ref[pl.ds(c, 1), 0:ns] = x_re
        xp_ref[pl.ds(c, 1), ns:2 * ns] = x_im
        v_re = v_ref[pl.ds(c, 1), 0:ns]
        v_im = v_ref[pl.ds(c, 1), ns:2 * ns]
        return (a_re * x_re - a_im * x_im + v_re, a_re * x_im + a_im * x_re + v_im)

    lax.fori_loop(0, n_chunks, step, (x_re0, x_im0))
    xp = xp_ref[...].astype(BF16)
    for p in range(T // 2):
        k_hi = (2 * p + 2) * nc
        cols = slice(2 * p * nc, k_hi)
        yc = _dot(uc_ref[:, 0:k_hi], m_ref[0:k_hi, cols]) + _dot(xp, wout_ref[:, cols])
        for t in (2 * p, 2 * p + 1):
            y_ref[pl.ds(t, n_chunks, stride=T), :] = yc[0:n_chunks, (t - 2 * p) * nc:(t - 2 * p + 1) * nc]


def _s5_scan(u, um, kt, win, wout, at):
    u, col0 = u
    n, width = u.shape[0], um.shape[1]
    T = S5_T
    nsg = kt.shape[0]
    nc = width // nsg
    ns2 = at.shape[2]
    assert um.shape[0] == T and n % T == 0 and col0 % nc == 0
    n_chunks = n // T
    rows = -(-(n_chunks + 1) // 16) * 16
    sq = pl.Squeezed()
    return pl.pallas_call(
        functools.partial(_s5_scan_kernel, n_chunks=n_chunks),
        grid=(nsg,),
        in_specs=[pl.BlockSpec((n, nc), lambda z: (0, col0 // nc + z)),
                  pl.BlockSpec((T, nc), lambda z: (0, z)),
                  pl.BlockSpec((sq, T, nc, nc), lambda z: (z, 0, 0, 0)),
                  pl.BlockSpec((sq, T * nc, ns2), lambda z: (z, 0, 0)),
                  pl.BlockSpec((sq, ns2, T * nc), lambda z: (z, 0, 0)),
                  pl.BlockSpec((sq, 1, ns2), lambda z: (z, 0, 0))],
        out_specs=pl.BlockSpec((n, nc), lambda z: (0, z)),
        out_shape=jax.ShapeDtypeStruct((n, width), F32),
        scratch_shapes=[pltpu.VMEM((rows, T * nc), BF16), pltpu.VMEM((rows, ns2), F32),
                        pltpu.VMEM((rows, ns2), F32), pltpu.VMEM((T * nc, T * nc), BF16)],
        compiler_params=_cparams(("arbitrary",)),
        name="s5_scan",
    )(u, um, kt, win, wout, at)


def _merge_kernel(*refs, pieces):
    h_ref, o_ref, y_ref = refs[0:3]
    gh_refs, gs_refs = refs[3:3 + pieces], refs[3 + pieces:3 + 2 * pieces]
    wp_ref, wa_ref, wb_ref, wo_ref, out_ref = refs[3 + 2 * pieces:]
    gh = jnp.concatenate([r[...] for r in gh_refs], axis=1).astype(F32)
    gs = jnp.concatenate([r[...] for r in gs_refs], axis=1).astype(F32)
    y_hg = _dot(o_ref[...], wp_ref[...])
    z = jax.nn.gelu(y_ref[...]).astype(BF16)
    y_s5 = _dot(z, wa_ref[...]) * _sigmoid(_dot(z, wb_ref[...]))
    merged = gh * y_hg + gs * y_s5
    out_ref[...] = h_ref[...] + _dot(merged.astype(BF16), wo_ref[...])


def _merge(hx, o, y, gh, gs, wp, wa, wb, wo, *, tm=256):
    n, d = hx.shape
    row = lambda i: (i, 0)
    const = lambda i: (0, 0)
    gw = functools.reduce(math.gcd, (gh[1], gs[1], d))
    assert gw % LANE == 0
    pieces = d // gw

    def resident(shape):
        return pl.BlockSpec(shape, const, pipeline_mode=pl.Buffered(1))

    def view(col0):
        return [pl.BlockSpec((tm, gw), functools.partial(lambda i, k: (i, k), k=col0 // gw + k))
                for k in range(pieces)]

    return pl.pallas_call(
        functools.partial(_merge_kernel, pieces=pieces),
        grid=(n // tm,),
        in_specs=[pl.BlockSpec((tm, d), row), pl.BlockSpec((tm, o.shape[1]), row),
                  pl.BlockSpec((tm, y.shape[1]), row)] + view(gh[1]) + view(gs[1])
        + [resident(wp.shape), resident(wa.shape), resident(wb.shape), resident(wo.shape)],
        out_specs=pl.BlockSpec((tm, d), row),
        out_shape=jax.ShapeDtypeStruct((n, d), F32),
        compiler_params=_cparams(("arbitrary",)),
        name="mixer_merge",
    )(hx, o, y, *([gh[0]] * pieces), *([gs[0]] * pieces), wp, wa, wb, wo)


def kernel(x, meta_tokens, norm_ffn1, ffn1_w_gate, ffn1_w_up, ffn1_w_down, norm_mix, w_in, hg_lower_bounds, hg_norm, hg_w_proj, s5_lam_re, s5_lam_im, s5_log_step, s5_b_re, s5_b_im, s5_c_re, s5_c_im, s5_d, s5_w_glu_a, s5_w_glu_b, w_out, norm_ffn2, ffn2_w_gate, ffn2_w_up, ffn2_w_down, norm_final):
    batch, seq, d = x.shape
    depth = norm_ffn1.shape[0]
    assert batch == 1 and depth == 1
    hg_k = hg_lower_bounds.shape[1]
    hg_w = hg_norm.shape[1]
    s5_w = s5_w_glu_a.shape[1]
    hx = x[0].astype(F32)
    hm = meta_tokens.astype(F32)
    tm = min(512, seq)
    h0, n0, hm, nm, w1_gate, w1_up, w1_down = _ffn(
        hx, hm, norm_ffn1[0], norm_mix[0], ffn1_w_gate[0], ffn1_w_up[0], ffn1_w_down[0],
        emit_norm=True, rows=tm, tm=tm, tf=256, name="ffn_head")
    sizes = (hg_k, hg_k, hg_w, hg_w, s5_w, d, d)
    assert sum(sizes) == w_in.shape[2]
    off = [0]
    for s in sizes:
        off.append(off[-1] + s)
    ids32, ids16 = (0, 1, 4), (2, 3, 5, 6)
    tc_in = functools.reduce(math.gcd, sizes + (512,))

    def starts(ids):
        out, col = {}, 0
        for k in ids:
            out[k] = col
            col += sizes[k]
        return out

    def blocks(ids):
        return [c for k in ids for c in range(off[k] // tc_in, off[k + 1] // tc_in)]

    steps = lambda grid: grid[0] * grid[1]
    n_body = (seq // tm) * (ffn1_w_gate.shape[2] // 512)
    n_inp = steps(_inproj_grid(seq, sum(sizes[k] for k in ids16))[2])
    n_hg = steps(_hgrn_grid(seq, hg_w)[2])
    hx, nx, w32, w16 = _ffn(
        hx, None, norm_ffn1[0], norm_mix[0], w1_gate, w1_up, w1_down, emit_norm=True, head=(h0, n0),
        cast=(_cast_tile(w_in[0], n_body, tc_in, [blocks(ids32), blocks(ids16)]),),
        tm=tm, name="ffn_body")

    kind = ("silu", "forget", "ident", "silu", "ident", "sigm", "sigm")
    scale = {0: HG_DK ** -0.5, 3: hg_norm[0].astype(F32)}

    def segments(ids):
        return tuple((sizes[k], kind[k]) for k in ids)

    def scale_row(ids):
        return jnp.concatenate([jnp.broadcast_to(jnp.asarray(scale.get(k, 1.0), F32), (sizes[k],))
                                for k in ids])[None]

    s32, s16 = starts(ids32), starts(ids16)
    lb32 = jnp.pad(hg_lower_bounds.astype(F32),
                   ((0, 0), (s32[1], sum(sizes[k] for k in ids32) - s32[1] - hg_k)))
    x32, m32 = _inproj(nx, nm, w32, segments(ids32), scale_row(ids32), lb32, F32)
    x16, m16, w_proj, w_glu_a, w_glu_b, w_o = _inproj(
        nx, nm, w16, segments(ids16), scale_row(ids16), None, BF16,
        cast=[_cast_tile(w, n_inp) for w in (hg_w_proj[0], s5_w_glu_a[0], s5_w_glu_b[0], w_out[0])])

    o, w2_gate, w2_up, w2_down = _hgrn(
        (x32, s32[0]), (x32, s32[1]), (x16, s16[2]), (x16, s16[3]),
        m32[:, s32[1]:s32[1] + hg_k], m16[:, s16[2]:s16[2] + hg_w],
        cast=[_cast_tile(w, n_hg) for w in (ffn2_w_gate[0], ffn2_w_up[0], ffn2_w_down[0])])
    m, win, wout, at = _s5_weights(s5_lam_re[0], s5_lam_im[0], s5_log_step[0], s5_b_re[0],
                                   s5_b_im[0], s5_c_re[0], s5_c_im[0], s5_d[0])
    y = _s5_scan((x32, s32[4]), m32[:, s32[4]:s32[4] + s5_w], m, win, wout, at)
    hx = _merge(hx, o, y, (x16, s16[5]), (x16, s16[6]), w_proj, w_glu_a, w_glu_b, w_o)
    out, = _ffn(hx, None, norm_ffn2[0], norm_final, w2_gate, w2_up, w2_down, emit_norm=False,
                tm=tm, name="ffn_final")
    return out[None].astype(x.dtype)
```

```python
import functools
import math

import numpy as np
import jax
import jax.numpy as jnp
from jax import lax
from jax.experimental import pallas as pl
from jax.experimental.pallas import tpu as pltpu

EPS = 1e-6
HG_DK = 128
HG_CHUNK = 64
S5_GROUP = 16
S5_STATE = 64
S5_T = 16
S5_SG = 8
LANE = 128
INPROJ_SUB = 256
VMEM_LIMIT = 56 * 1024 * 1024

F32 = jnp.float32
BF16 = jnp.bfloat16


def _cparams(sem):
    return pltpu.CompilerParams(dimension_semantics=sem, vmem_limit_bytes=VMEM_LIMIT)


def _rms(v, g):
    return v * lax.rsqrt(jnp.mean(v * v, axis=-1, keepdims=True) + EPS) * g


def _sigmoid(x):
    return 0.5 * jnp.tanh(0.5 * x) + 0.5


def _silu(x):
    h = 0.5 * x
    return h * (jnp.tanh(h) + 1.0)


def _dot(a, b):
    return jnp.dot(a, b, preferred_element_type=F32)


def _dot_nt(a, b):
    return lax.dot_general(a, b, (((1,), (1,)), ((), ())), preferred_element_type=F32)


def _dot_tn(a, b):
    return lax.dot_general(a, b, (((0,), (0,)), ((), ())), preferred_element_type=F32)


def _cast_jobs(cast, grid):
    nj = grid[1]
    in_specs, out_specs, out_shapes, static = [], [], [], []
    for src, (tr, tc), dests in cast:
        rows, cols = src.shape
        ncb = cols // tc
        n_tiles = (rows // tr) * ncb
        assert rows % tr == 0 and cols % tc == 0 and n_tiles <= grid[0] * nj

        def tile(i, j, n_tiles=n_tiles, ncb=ncb):
            t = jnp.minimum(i * nj + j, n_tiles - 1)
            return t // ncb, t % ncb

        in_specs.append(pl.BlockSpec((tr, tc), tile))
        for blocks in dests:
            park, k = [], 0
            for c in range(ncb):
                k = blocks.index(c) if c in blocks else k
                park.append(k)
            jumps = [(m, park[m] - park[m - 1]) for m in range(1, ncb) if park[m] != park[m - 1]]

            def dest(i, j, tile=tile, first=park[0], jumps=jumps):
                r, c = tile(i, j)
                return r, first + sum(jnp.where(c >= m, dk, 0) for m, dk in jumps)

            out_specs.append(pl.BlockSpec((tr, tc), tile if park == list(range(ncb)) else dest))
            out_shapes.append(jax.ShapeDtypeStruct((rows, len(blocks) * tc), BF16))
        static.append((n_tiles, ncb, tuple(tuple(b) for b in dests)))
    return in_specs, out_specs, out_shapes, tuple(static)


def _n_cast_out(jobs):
    return sum(len(dests) for _, _, dests in jobs)


def _run_cast_jobs(cast_in, cast_out, jobs):
    step = pl.program_id(0) * pl.num_programs(1) + pl.program_id(1)
    outs = iter(cast_out)
    for src, (n_tiles, ncb, dests) in zip(cast_in, jobs):
        dsts = [next(outs) for _ in dests]
        tile = src[...].astype(BF16)
        if len(dsts) == 1:
            dsts[0][...] = tile
        else:
            c = jnp.minimum(step, n_tiles - 1) % ncb
            for dst, cols in zip(dsts, dests):
                owns = functools.reduce(jnp.logical_or, [c == m for m in cols])

                @pl.when(owns)
                def _(dst=dst):
                    dst[...] = tile


def _cast_tile(w, n_steps, tc=None, dests=None):
    rows, cols = w.shape
    if tc is None and rows % n_steps == 0 and (rows // n_steps) % 16 == 0:
        return w, (rows // n_steps, cols), [[0]]
    tc = math.gcd(tc or 512, cols)
    tr = 128
    while (rows // tr) * (cols // tc) > n_steps:
        tr *= 2
    return w, (tr, tc), dests or [list(range(cols // tc))]


def _ffn_kernel(*refs, tm, nj, n_meta, emit_norm, w_f32, head, jobs):
    it = iter(refs)
    take = lambda k: [next(it) for _ in range(k)]
    x_ref, = take(1)
    m_ref = take(1)[0] if n_meta else None
    g_ref, gn_ref, wg_ref, wu_ref, wd_ref = take(5)
    hx_ref, hn16_ref = take(2) if head else (None, None)
    cast_in = take(len(jobs))
    ox_ref, = take(1)
    nx_ref = take(1)[0] if emit_norm else None
    om_ref, nm_ref = take(2) if n_meta else (None, None)
    w16_out = take(3) if w_f32 else ()
    cast_out = take(_n_cast_out(jobs))
    hn_ref, = take(1)
    acc_ref = take(1)[0] if n_meta else ox_ref
    i = pl.program_id(0)
    j = pl.program_id(1)
    _run_cast_jobs(cast_in, cast_out, jobs)

    def step(first, last):
        if first:
            hn_ref[0:tm, :] = _rms(x_ref[...], g_ref[...]).astype(BF16)
            if n_meta:
                hn_ref[tm:tm + n_meta, :] = _rms(m_ref[...], g_ref[...]).astype(BF16)
        tiles = [w[...].astype(BF16) if w_f32 else w[...] for w in (wg_ref, wu_ref, wd_ref)]
        for dst, tile in zip(w16_out, tiles):
            dst[...] = tile
        hn = hn_ref[...]
        gate = _dot(hn, tiles[0])
        up = _dot(hn, tiles[1])
        act = (_silu(gate) * up).astype(BF16)
        if first:
            acc_ref[...] = _dot(act, tiles[2])
        else:
            acc_ref[...] += _dot(act, tiles[2])
        if last:
            hx = x_ref[...] + 0.5 * acc_ref[0:tm, :]
            if emit_norm:
                ox_ref[...] = hx
                nx_ref[...] = _rms(hx, gn_ref[...]).astype(BF16)
            else:
                ox_ref[...] = _rms(hx, gn_ref[...])
            if n_meta:
                hm = m_ref[...] + 0.5 * acc_ref[tm:tm + n_meta, :]
                om_ref[...] = hm
                nm_ref[...] = _rms(hm, gn_ref[...]).astype(BF16)

    def body():
        if nj == 1:
            step(True, True)
        else:
            pl.when(j == 0)(functools.partial(step, True, False))
            pl.when((j > 0) & (j < nj - 1))(functools.partial(step, False, False))
            pl.when(j == nj - 1)(functools.partial(step, False, True))

    if head:
        @pl.when((i == 0) & (j == 0))
        def _():
            ox_ref[...] = hx_ref[...]
            nx_ref[...] = hn16_ref[...]

        pl.when(i > 0)(body)
    else:
        body()


def _ffn(hx, hm, g, g_next, wg, wu, wd, *, emit_norm, head=None, cast=(), rows=None, name,
         tm=512, tf=512):
    n, d = hx.shape
    rows = n if rows is None else rows
    dff = wg.shape[1]
    n_meta = 0 if hm is None else hm.shape[0]
    w_f32 = wg.dtype == F32
    grid = (rows // tm, dff // tf)
    assert not w_f32 or grid[0] == 1
    row = lambda i, j: (i, 0)
    const = lambda i, j: (0, 0)
    in_specs = [pl.BlockSpec((tm, d), row)]
    args = [hx]
    if n_meta:
        in_specs.append(pl.BlockSpec((n_meta, d), const))
        args.append(hm)
    col = (lambda i, j: jnp.where(i == 0, 0, j)) if head else (lambda i, j: j)
    in_specs += [pl.BlockSpec((1, d), const), pl.BlockSpec((1, d), const),
                 pl.BlockSpec((d, tf), lambda i, j: (0, col(i, j))),
                 pl.BlockSpec((d, tf), lambda i, j: (0, col(i, j))),
                 pl.BlockSpec((tf, d), lambda i, j: (col(i, j), 0))]
    args += [g.reshape(1, d), g_next.reshape(1, d), wg, wu, wd]
    if head:
        in_specs += [pl.BlockSpec((tm, d), const, pipeline_mode=pl.Buffered(1))] * 2
        args += list(head)
    job_in, job_out, job_shapes, jobs = _cast_jobs(cast, grid)
    in_specs += job_in
    args += [src for src, _, _ in cast]
    out_shape = [jax.ShapeDtypeStruct((rows, d), F32)]
    out_specs = [pl.BlockSpec((tm, d), row)]
    if emit_norm:
        out_shape.append(jax.ShapeDtypeStruct((rows, d), BF16))
        out_specs.append(pl.BlockSpec((tm, d), row))
    if n_meta:
        out_shape += [jax.ShapeDtypeStruct((n_meta, d), F32), jax.ShapeDtypeStruct((n_meta, d), BF16)]
        out_specs += [pl.BlockSpec((n_meta, d), const)] * 2
    if w_f32:
        out_shape += [jax.ShapeDtypeStruct(w.shape, BF16) for w in (wg, wu, wd)]
        out_specs += [pl.BlockSpec((d, tf), lambda i, j: (0, j)),
                      pl.BlockSpec((d, tf), lambda i, j: (0, j)),
                      pl.BlockSpec((tf, d), lambda i, j: (j, 0))]
    out_specs += job_out
    out_shape += job_shapes
    return pl.pallas_call(
        functools.partial(_ffn_kernel, tm=tm, nj=grid[1], n_meta=n_meta, emit_norm=emit_norm,
                          w_f32=w_f32, head=head is not None, jobs=jobs),
        grid=grid, in_specs=in_specs, out_specs=out_specs, out_shape=out_shape,
        scratch_shapes=[pltpu.VMEM((tm + n_meta, d), BF16)]
        + ([pltpu.VMEM((tm + n_meta, d), F32)] if n_meta else []),
        compiler_params=_cparams(("arbitrary", "arbitrary")),
        name=name,
    )(*args)


def _inproj_kernel(*refs, tm, n_meta, with_lb, kinds, jobs):
    it = iter(refs)
    take = lambda k: [next(it) for _ in range(k)]
    x_ref, m_ref, w_ref, scale_ref = take(4)
    lb_ref = take(1)[0] if with_lb else None
    cast_in = take(len(jobs))
    ox_ref, om_ref = take(2)
    cast_out = take(_n_cast_out(jobs))
    lhs_ref, = take(1)
    _run_cast_jobs(cast_in, cast_out, jobs)

    @pl.when(pl.program_id(1) == 0)
    def _():
        lhs_ref[0:tm, :] = x_ref[...]
        lhs_ref[tm:tm + n_meta, :] = m_ref[...]

    tn = w_ref.shape[1]
    sw = tn // len(kinds[0])

    def column_step(step_kinds):
        for s, kind in enumerate(step_kinds):
            cs = slice(s * sw, (s + 1) * sw)
            res = _dot(lhs_ref[...], w_ref[:, cs])
            if kind == "ident":
                out = res
            elif kind == "silu":
                h = 0.5 * res
                out = (scale_ref[:, cs] * h) * (jnp.tanh(h) + 1.0)
            elif kind == "sigm":
                out = 0.5 * jnp.tanh(0.5 * res) + 0.5
            else:
                assert kind == "forget"
                lbl = lb_ref[:, cs]
                e = jnp.exp(lbl - jnp.max(lbl, axis=0, keepdims=True))
                lb = e[0:1, :] / jnp.sum(e, axis=0, keepdims=True)
                out = (0.5 - 0.5 * lb) * jnp.tanh(0.5 * res) + (0.5 + 0.5 * lb)
            ox_ref[:, cs] = out[0:tm, :].astype(ox_ref.dtype)
            om_ref[:, cs] = out[tm:tm + n_meta, :].astype(om_ref.dtype)

    if len(set(kinds)) == 1:
        column_step(kinds[0])
    else:
        for jj, step_kinds in enumerate(kinds):
            pl.when(pl.program_id(1) == jj)(functools.partial(column_step, step_kinds))


def _inproj_grid(n, cols, tm=1024, tn=1792):
    tm = min(tm, n)
    tn = math.gcd(tn, cols)
    assert tn % LANE == 0 and n % tm == 0
    return tm, tn, (n // tm, cols // tn)


def _inproj(nx, nm, w, segments, scale, lower_bounds, dtype, cast=()):
    n, d = nx.shape
    n_meta = nm.shape[0]
    cols = w.shape[1]
    with_lb = lower_bounds is not None
    tm, tn, grid = _inproj_grid(n, cols)
    sw = math.gcd(tn, INPROJ_SUB)
    col_kind = [kind for width, kind in segments for _ in range(width // sw)]
    assert all(width % sw == 0 for width, _ in segments) and len(col_kind) * sw == cols
    kinds = tuple(tuple(col_kind[jj * (tn // sw):(jj + 1) * (tn // sw)]) for jj in range(grid[1]))
    job_in, job_out, job_shapes, jobs = _cast_jobs(cast, grid)
    in_specs = [pl.BlockSpec((tm, d), lambda i, j: (i, 0)),
                pl.BlockSpec((n_meta, d), lambda i, j: (0, 0)),
                pl.BlockSpec((d, tn), lambda i, j: (0, j)),
                pl.BlockSpec((1, tn), lambda i, j: (0, j))]
    args = [nx, nm, w, scale]
    if with_lb:
        in_specs.append(pl.BlockSpec((lower_bounds.shape[0], tn), lambda i, j: (0, j)))
        args.append(lower_bounds)
    ox, om, *copies = pl.pallas_call(
        functools.partial(_inproj_kernel, tm=tm, n_meta=n_meta, with_lb=with_lb, kinds=kinds,
                          jobs=jobs),
        grid=grid,
        in_specs=in_specs + job_in,
        out_specs=[pl.BlockSpec((tm, tn), lambda i, j: (i, j)),
                   pl.BlockSpec((pl.Squeezed(), n_meta, tn), lambda i, j: (i, 0, j))] + job_out,
        out_shape=[jax.ShapeDtypeStruct((n, cols), dtype),
                   jax.ShapeDtypeStruct((n // tm, n_meta, cols), dtype)] + job_shapes,
        scratch_shapes=[pltpu.VMEM((tm + n_meta, d), BF16)],
        compiler_params=_cparams(("arbitrary", "arbitrary")),
        name="mixer_inproj_" + jnp.dtype(dtype).name,
    )(*args, *[src for src, _, _ in cast])
    return (ox, om[0], *copies)


def _split3(x):
    hi = x.astype(BF16)
    r1 = x - hi.astype(F32)
    mid = r1.astype(BF16)
    lo = (r1 - mid.astype(F32)).astype(BF16)
    return hi, mid, lo


def _dot3(m01, parts):
    return _dot(m01, parts[0]) + _dot(m01, parts[1]) + _dot(m01, parts[2])


def _tri(n, rep=1):
    r = lax.broadcasted_iota(jnp.int32, (n * rep, n), 0)
    c = lax.broadcasted_iota(jnp.int32, (n * rep, n), 1)
    return jnp.where(r >= c * rep, 1.0, 0.0).astype(BF16)


def _hgrn_tables(c):
    sums = [np.tril(np.ones((c, c)))]
    level = np.full((c, c), -1, np.int32)
    t_idx, s_idx = np.meshgrid(np.arange(c), np.arange(c), indexing="ij")
    halves = []
    h = c // 2
    while h >= 1:
        if h < 8:
            m = np.zeros((c, c))
            for r in range(c):
                mid = (r // (2 * h)) * 2 * h + h
                if r < mid:
                    m[r, r + 1:mid] = 1.0
                else:
                    m[r, mid:r + 1] = 1.0
            sums.append(m)
        pair = 2 * h
        level[(t_idx // pair == s_idx // pair) & (s_idx % pair < h) & (t_idx % pair >= h)] = len(halves)
        halves.append(h)
        h //= 2
    assert len(halves) % 2 == 0
    level[np.arange(c), np.arange(c)] = len(halves)
    owner = np.full((c, 2 * c), -1, np.int32)
    owner[:, :c] = np.where((level >= 0) & (level % 2 == 0), level // 2, -1)
    owner[:, c:] = np.where(level % 2 == 1, level // 2, -1)
    return np.concatenate(sums, axis=0), owner, tuple(halves)


def _hgrn_score_products(q, k, x, halves):
    C = HG_CHUNK
    b = x[0:C]
    zero = jnp.zeros((C, HG_DK), BF16)
    diag = jnp.sum(q * k, axis=-1, keepdims=True)
    qts, kts = [], []
    fine = 0
    for h in halves:
        if h >= 8:
            none = jnp.zeros((h, HG_DK), F32)
            qh, kh = [], []
            for r in range(0, C, 2 * h):
                b_mid = b[r + h - 1:r + h, :]
                kh += [k[r:r + h] * jnp.exp(b_mid - b[r:r + h]), none]
                qh += [none, q[r + h:r + 2 * h] * jnp.exp(b[r + h:r + 2 * h] - b_mid)]
            qts.append(jnp.concatenate(qh, axis=0).astype(BF16))
            kts.append(jnp.concatenate(kh, axis=0).astype(BF16))
        else:
            fine += 1
            e = jnp.exp(x[fine * C:(fine + 1) * C])
            qts.append((q * e).astype(BF16))
            kts.append((k * e).astype(BF16))
    prods = []
    for p in range(len(halves) // 2):
        lhs = jnp.concatenate([qts[2 * p], qts[2 * p + 1]], axis=1)
        rhs = jnp.concatenate([jnp.concatenate([kts[2 * p], zero], axis=1),
                               jnp.concatenate([zero, kts[2 * p + 1]], axis=1)], axis=0)
        prods.append(_dot_nt(lhs, rhs))
    return prods, diag


def _hgrn_pair(q2, f2, v2, gate2, st2, sums, owns, halves):
    C, D = HG_CHUNK, HG_DK
    halfs = (slice(0, D), slice(D, 2 * D))
    k2 = 1.0 - f2
    x2 = _dot(sums, jnp.concatenate(_split3(jnp.log(f2)), axis=0))
    yield
    b2 = x2[0:C]
    b_last = b2[C - 1:C, :]
    kdec2 = (k2 * jnp.exp(b_last - b2)).astype(BF16)
    qe2 = (q2 * jnp.exp(b2)).astype(BF16)
    st2b = st2.astype(BF16)
    scores = []
    for c in halfs:
        scores.append(_hgrn_score_products(q2[:, c], k2[:, c], x2[:, c], halves))
        yield
    zero = jnp.zeros((2 * C, D), BF16)
    os, vvts, kds = [], [], []
    for j, c in enumerate(halfs):
        prods, diag = scores[j]
        a = jnp.where(owns[len(prods)], diag, 0.0)
        for p, prod in enumerate(prods):
            a = jnp.where(owns[p], prod, a)
        v = v2[:, c]
        vvt = jnp.concatenate([v, v], axis=0).T
        os.append(_dot_nt(jnp.concatenate([qe2[:, c], a.astype(BF16)], axis=1),
                          jnp.concatenate([st2b[:, c], vvt], axis=1)))
        vvts.append(vvt)
        kd = jnp.concatenate([kdec2[:, c], jnp.zeros((C, D), BF16)], axis=0)
        kds.append(jnp.concatenate([kd, zero] if j == 0 else [zero, kd], axis=1))
        yield
    st_new = st2 * jnp.exp(b_last) + _dot(jnp.concatenate(vvts, axis=1), jnp.concatenate(kds, axis=0))
    yield
    ons = [o * lax.rsqrt(jnp.mean(o * o, axis=-1, keepdims=True) + EPS) for o in os]
    on2 = jnp.concatenate(ons, axis=1) * gate2
    return on2.astype(BF16), st_new


def _interleave(gens):
    out = [None] * len(gens)
    live = list(range(len(gens)))
    while live:
        for g in list(live):
            try:
                next(gens[g])
            except StopIteration as stop:
                out[g] = stop.value
                live.remove(g)
    return out


def _hgrn_kernel(q_ref, f_ref, v_ref, go_ref, fm_ref, vm_ref, sums_ref, owner_ref,
                 *rest, n_chunks, n_meta, hb, halves, jobs):
    cast_in, o_ref = rest[:len(jobs)], rest[len(jobs)]
    cast_out, st_ref = rest[len(jobs) + 1:-1], rest[-1]
    _run_cast_jobs(cast_in, cast_out, jobs)
    i = pl.program_id(0)
    p0 = pl.program_id(1) * (hb // 2)
    C = HG_CHUNK
    W = 2 * HG_DK

    @pl.when(i == 0)
    def _():
        tri_m = _tri(n_meta)
        for pair in range(hb // 2):
            sts = []
            for hh in (2 * pair, 2 * pair + 1):
                cols = slice(hh * HG_DK, (hh + 1) * HG_DK)
                fm = fm_ref[:, cols]
                bm = _dot3(tri_m, _split3(jnp.log(fm)))
                kdec = ((1.0 - fm) * jnp.exp(bm[n_meta - 1:n_meta, :] - bm)).astype(BF16)
                sts.append(_dot_tn(vm_ref[:, cols], kdec))
            st_ref[p0 + pair] = jnp.concatenate(sts, axis=1)

    def chunk(c, states):
        r0 = pl.multiple_of(c * C, C)
        owner = owner_ref[...]
        owns = [owner == p for p in range(len(halves) // 2 + 1)]
        sums = sums_ref[...]
        gens = []
        for pair in range(hb // 2):
            cols = slice(pair * W, (pair + 1) * W)
            gate2 = go_ref[pl.ds(r0, C), cols].astype(F32)
            gens.append(_hgrn_pair(q_ref[pl.ds(r0, C), cols], f_ref[pl.ds(r0, C), cols],
                                   v_ref[pl.ds(r0, C), cols], gate2, states[pair],
                                   sums, owns, halves))
        results = _interleave(gens)
        o_ref[pl.ds(r0, C), :] = jnp.concatenate([r[0] for r in results], axis=1)
        return tuple(r[1] for r in results)

    states = lax.fori_loop(0, n_chunks, chunk, tuple(st_ref[p0 + p] for p in range(hb // 2)))
    for pair in range(hb // 2):
        st_ref[p0 + pair] = states[pair]


def _hgrn_grid(n, width, rb=512, hb=12):
    heads = width // HG_DK
    hb = min(hb, heads)
    assert hb % 2 == 0 and heads % hb == 0
    return rb, hb, (n // rb, heads // hb)


def _hgrn(q, f, v, go, fm, vm, cast=()):
    n, width = q[0].shape[0], fm.shape[1]
    heads = width // HG_DK
    rb, hb, grid = _hgrn_grid(n, width)
    bw = hb * HG_DK
    n_meta = fm.shape[0]
    job_in, job_out, job_shapes, jobs = _cast_jobs(cast, grid)
    sums, owner, halves = _hgrn_tables(HG_CHUNK)
    sums3 = jnp.asarray(np.concatenate([sums] * 3, axis=1), BF16)
    blk = lambda i, h: (i, h)
    mblk = lambda i, h: (0, h)
    const = lambda i, h: (0, 0)

    def view(col0):
        assert col0 % bw == 0
        return pl.BlockSpec((rb, bw), lambda i, h: (i, col0 // bw + h))

    return pl.pallas_call(
        functools.partial(_hgrn_kernel, n_chunks=rb // HG_CHUNK, n_meta=n_meta, hb=hb,
                          halves=halves, jobs=jobs),
        grid=grid,
        in_specs=[view(q[1]), view(f[1]), view(v[1]), view(go[1])]
        + [pl.BlockSpec((n_meta, hb * HG_DK), mblk)] * 2
        + [pl.BlockSpec(sums3.shape, const), pl.BlockSpec(owner.shape, const)] + job_in,
        out_specs=[pl.BlockSpec((rb, hb * HG_DK), blk)] + job_out,
        out_shape=[jax.ShapeDtypeStruct((n, width), BF16)] + job_shapes,
        scratch_shapes=[pltpu.VMEM((heads // 2, HG_DK, 2 * HG_DK), F32)],
        compiler_params=_cparams(("arbitrary", "arbitrary")),
        name="hgrn2",
    )(q[0], f[0], v[0], go[0], fm, vm, sums3, jnp.asarray(owner), *[src for src, _, _ in cast])


def _s5_weights_kernel(lr_ref, li_ref, dt_ref, bre_ref, bim_ref, cre_ref, cim_ref, dsk_ref,
                       kt_ref, win_ref, wout_ref, at_ref):
    T = S5_T
    ns = lr_ref.shape[1]
    nc = cre_ref.shape[0]
    lr, li, dt = lr_ref[...], li_ref[...], dt_ref[...]

    def powers(tau):
        mag = jnp.exp(lr * dt * tau)
        ang = li * dt * tau
        return mag * jnp.cos(ang), mag * jnp.sin(ang)

    a_re, a_im = powers(1.0)
    num_re, num_im = a_re - 1.0, a_im
    den = lr * lr + li * li
    coef_re = (num_re * lr + num_im * li) / den
    coef_im = (num_im * lr - num_re * li) / den
    br, bi = bre_ref[...], bim_ref[...]
    bbar_re = coef_re * br - coef_im * bi
    bbar_im = coef_re * bi + coef_im * br

    pw = [powers(float(tau)) for tau in range(T + 1)]
    for s in range(T):
        p_re, p_im = pw[T - 1 - s]
        win_ref[s * nc:(s + 1) * nc, 0:ns] = (bbar_re * p_re - bbar_im * p_im).astype(BF16)
        win_ref[s * nc:(s + 1) * nc, ns:2 * ns] = (bbar_re * p_im + bbar_im * p_re).astype(BF16)
    at_ref[:, 0:ns] = pw[T][0]
    at_ref[:, ns:2 * ns] = pw[T][1]

    cr, ci = cre_ref[...], cim_ref[...]
    def split2(a):
        hi = a.astype(BF16)
        return hi, (a - hi.astype(F32)).astype(BF16)

    bb_hi, bb_lo = split2(jnp.concatenate([bbar_re, -bbar_im], axis=1))
    for tau in range(T + 1):
        p_re, p_im = pw[tau]
        ca_re = cr * p_re - ci * p_im
        ca_im = cr * p_im + ci * p_re
        if tau >= 1:
            t = tau - 1
            wout_ref[0:ns, t * nc:(t + 1) * nc] = ca_re.T.astype(BF16)
            wout_ref[ns:2 * ns, t * nc:(t + 1) * nc] = (-ca_im).T.astype(BF16)
        if tau < T:
            ca_hi, ca_lo = split2(jnp.concatenate([ca_re, ca_im], axis=1))
            kt = _dot_nt(bb_hi, ca_hi) + _dot_nt(bb_hi, ca_lo) + _dot_nt(bb_lo, ca_hi)
            if tau == 0:
                kt = kt + dsk_ref[...]
            kt_ref[tau] = kt.astype(BF16)


def _s5_weights(lam_re, lam_im, log_step, b_re, b_im, c_re, c_im, d_skip):
    G, N = lam_re.shape
    P = d_skip.shape[1]
    nsg = G // S5_SG
    ns, nc = S5_SG * N, S5_SG * P
    T = S5_T
    dt = jnp.exp(log_step.astype(F32))
    eye = jnp.eye(S5_SG, dtype=F32)

    def rows(a):
        return a.astype(F32).reshape(nsg, 1, ns)

    def embed_b(b):
        b = b.astype(F32).reshape(nsg, S5_SG, N, P)
        return jnp.einsum('zgnq,gh->zhqgn', b, eye).reshape(nsg, nc, ns)

    def embed_c(c):
        c = c.astype(F32).reshape(nsg, S5_SG, P, N)
        return jnp.einsum('zgpn,gh->zhpgn', c, eye).reshape(nsg, nc, ns)

    dtb = jnp.broadcast_to(dt[:, None], (G, N))
    d_diag = jax.vmap(jnp.diag)(d_skip.astype(F32).reshape(nsg, nc))
    sq = pl.Squeezed()
    spec_r = pl.BlockSpec((sq, 1, ns), lambda z: (z, 0, 0))
    spec_b = pl.BlockSpec((sq, nc, ns), lambda z: (z, 0, 0))
    return pl.pallas_call(
        _s5_weights_kernel,
        grid=(nsg,),
        in_specs=[spec_r, spec_r, spec_r, spec_b, spec_b, spec_b, spec_b,
                  pl.BlockSpec((sq, nc, nc), lambda z: (z, 0, 0))],
        out_specs=[pl.BlockSpec((sq, T, nc, nc), lambda z: (z, 0, 0, 0)),
                   pl.BlockSpec((sq, T * nc, 2 * ns), lambda z: (z, 0, 0)),
                   pl.BlockSpec((sq, 2 * ns, T * nc), lambda z: (z, 0, 0)),
                   pl.BlockSpec((sq, 1, 2 * ns), lambda z: (z, 0, 0))],
        out_shape=[jax.ShapeDtypeStruct((nsg, T, nc, nc), BF16),
                   jax.ShapeDtypeStruct((nsg, T * nc, 2 * ns), BF16),
                   jax.ShapeDtypeStruct((nsg, 2 * ns, T * nc), BF16),
                   jax.ShapeDtypeStruct((nsg, 1, 2 * ns), F32)],
        compiler_params=_cparams(("arbitrary",)),
        name="s5_weights",
    )(rows(lam_re), rows(lam_im), rows(dtb),
      embed_b(b_re), embed_b(b_im), embed_c(c_re), embed_c(c_im), d_diag)


def _s5_scan_kernel(u_ref, um_ref, kt_ref, win_ref, wout_ref, at_ref, y_ref,
                    uc_ref, v_ref, xp_ref, m_ref, *, n_chunks):
    T = S5_T
    nc = u_ref.shape[1]
    ns = at_ref.shape[1] // 2

    @pl.when(pl.program_id(0) == 0)
    def _():
        zero = jnp.zeros((nc, nc), BF16)
        for s in range(1, T):
            for t in range(s):
                m_ref[s * nc:(s + 1) * nc, t * nc:(t + 1) * nc] = zero

    for tau in range(T):
        tap = kt_ref[tau]
        for s in range(T - tau):
            m_ref[s * nc:(s + 1) * nc, (s + tau) * nc:(s + tau + 1) * nc] = tap

    for t in range(T):
        uc_ref[0:n_chunks, t * nc:(t + 1) * nc] = u_ref[pl.ds(t, n_chunks, stride=T), :].astype(BF16)
        uc_ref[n_chunks:n_chunks + 1, t * nc:(t + 1) * nc] = um_ref[t:t + 1, :].astype(BF16)
    pad = uc_ref.shape[0] - n_chunks - 1
    uc_ref[n_chunks + 1:, :] = jnp.zeros((pad, T * nc), BF16)
    uc = uc_ref[...]
    v_ref[...] = _dot(uc, win_ref[...])
    a_re, a_im = at_ref[:, 0:ns], at_ref[:, ns:2 * ns]

    x_re0 = v_ref[n_chunks:n_chunks + 1, 0:ns]
    x_im0 = v_ref[n_chunks:n_chunks + 1, ns:2 * ns]
    xp_ref[n_chunks:, :] = jnp.zeros((pad + 1, 2 * ns), F32)

    def step(c, carry):
        x_re, x_im = carry
        xp_ref[pl.ds(c, 1), 0:ns] = x_re
        xp_ref[pl.ds(c, 1), ns:2 * ns] = x_im
        v_re = v_ref[pl.ds(c, 1), 0:ns]
        v_im = v_ref[pl.ds(c, 1), ns:2 * ns]
        return (a_re * x_re - a_im * x_im + v_re, a_re * x_im + a_im * x_re + v_im)

    lax.fori_loop(0, n_chunks, step, (x_re0, x_im0))
    xp = xp_ref[...].astype(BF16)
    for p in range(T // 2):
        k_hi = (2 * p + 2) * nc
        cols = slice(2 * p * nc, k_hi)
        yc = _dot(uc_ref[:, 0:k_hi], m_ref[0:k_hi, cols]) + _dot(xp, wout_ref[:, cols])
        for t in (2 * p, 2 * p + 1):
            y_ref[pl.ds(t, n_chunks, stride=T), :] = yc[0:n_chunks, (t - 2 * p) * nc:(t - 2 * p + 1) * nc]


def _s5_scan(u, um, kt, win, wout, at):
    u, col0 = u
    n, width = u.shape[0], um.shape[1]
    T = S5_T
    nsg = kt.shape[0]
    nc = width // nsg
    ns2 = at.shape[2]
    assert um.shape[0] == T and n % T == 0 and col0 % nc == 0
    n_chunks = n // T
    rows = -(-(n_chunks + 1) // 16) * 16
    sq = pl.Squeezed()
    return pl.pallas_call(
        functools.partial(_s5_scan_kernel, n_chunks=n_chunks),
        grid=(nsg,),
        in_specs=[pl.BlockSpec((n, nc), lambda z: (0, col0 // nc + z)),
                  pl.BlockSpec((T, nc), lambda z: (0, z)),
                  pl.BlockSpec((sq, T, nc, nc), lambda z: (z, 0, 0, 0)),
                  pl.BlockSpec((sq, T * nc, ns2), lambda z: (z, 0, 0)),
                  pl.BlockSpec((sq, ns2, T * nc), lambda z: (z, 0, 0)),
                  pl.BlockSpec((sq, 1, ns2), lambda z: (z, 0, 0))],
        out_specs=pl.BlockSpec((n, nc), lambda z: (0, z)),
        out_shape=jax.ShapeDtypeStruct((n, width), F32),
        scratch_shapes=[pltpu.VMEM((rows, T * nc), BF16), pltpu.VMEM((rows, ns2), F32),
                        pltpu.VMEM((rows, ns2), F32), pltpu.VMEM((T * nc, T * nc), BF16)],
        compiler_params=_cparams(("arbitrary",)),
        name="s5_scan",
    )(u, um, kt, win, wout, at)


def _merge_kernel(*refs, pieces):
    h_ref, o_ref, y_ref = refs[0:3]
    gh_refs, gs_refs = refs[3:3 + pieces], refs[3 + pieces:3 + 2 * pieces]
    wp_ref, wa_ref, wb_ref, wo_ref, out_ref = refs[3 + 2 * pieces:]
    gh = jnp.concatenate([r[...] for r in gh_refs], axis=1).astype(F32)
    gs = jnp.concatenate([r[...] for r in gs_refs], axis=1).astype(F32)
    y_hg = _dot(o_ref[...], wp_ref[...])
    z = jax.nn.gelu(y_ref[...]).astype(BF16)
    y_s5 = _dot(z, wa_ref[...]) * _sigmoid(_dot(z, wb_ref[...]))
    merged = gh * y_hg + gs * y_s5
    out_ref[...] = h_ref[...] + _dot(merged.astype(BF16), wo_ref[...])


def _merge(hx, o, y, gh, gs, wp, wa, wb, wo, *, tm=256):
    n, d = hx.shape
    row = lambda i: (i, 0)
    const = lambda i: (0, 0)
    gw = functools.reduce(math.gcd, (gh[1], gs[1], d))
    assert gw % LANE == 0
    pieces = d // gw

    def resident(shape):
        return pl.BlockSpec(shape, const, pipeline_mode=pl.Buffered(1))

    def view(col0):
        return [pl.BlockSpec((tm, gw), functools.partial(lambda i, k: (i, k), k=col0 // gw + k))
                for k in range(pieces)]

    return pl.pallas_call(
        functools.partial(_merge_kernel, pieces=pieces),
        grid=(n // tm,),
        in_specs=[pl.BlockSpec((tm, d), row), pl.BlockSpec((tm, o.shape[1]), row),
                  pl.BlockSpec((tm, y.shape[1]), row)] + view(gh[1]) + view(gs[1])
        + [resident(wp.shape), resident(wa.shape), resident(wb.shape), resident(wo.shape)],
        out_specs=pl.BlockSpec((tm, d), row),
        out_shape=jax.ShapeDtypeStruct((n, d), F32),
        compiler_params=_cparams(("arbitrary",)),
        name="mixer_merge",
    )(hx, o, y, *([gh[0]] * pieces), *([gs[0]] * pieces), wp, wa, wb, wo)


def kernel(x, meta_tokens, norm_ffn1, ffn1_w_gate, ffn1_w_up, ffn1_w_down, norm_mix, w_in, hg_lower_bounds, hg_norm, hg_w_proj, s5_lam_re, s5_lam_im, s5_log_step, s5_b_re, s5_b_im, s5_c_re, s5_c_im, s5_d, s5_w_glu_a, s5_w_glu_b, w_out, norm_ffn2, ffn2_w_gate, ffn2_w_up, ffn2_w_down, norm_final):
    batch, seq, d = x.shape
    depth = norm_ffn1.shape[0]
    assert batch == 1 and depth == 1
    hg_k = hg_lower_bounds.shape[1]
    hg_w = hg_norm.shape[1]
    s5_w = s5_w_glu_a.shape[1]
    hx = x[0].astype(F32)
    hm = meta_tokens.astype(F32)
    tm = min(512, seq)
    h0, n0, hm, nm, w1_gate, w1_up, w1_down = _ffn(
        hx, hm, norm_ffn1[0], norm_mix[0], ffn1_w_gate[0], ffn1_w_up[0], ffn1_w_down[0],
        emit_norm=True, rows=tm, tm=tm, tf=256, name="ffn_head")
    sizes = (hg_k, hg_k, hg_w, hg_w, s5_w, d, d)
    assert sum(sizes) == w_in.shape[2]
    off = [0]
    for s in sizes:
        off.append(off[-1] + s)
    ids32, ids16 = (0, 1, 4), (2, 3, 5, 6)
    tc_in = functools.reduce(math.gcd, sizes + (512,))

    def starts(ids):
        out, col = {}, 0
        for k in ids:
            out[k] = col
            col += sizes[k]
        return out

    def blocks(ids):
        return [c for k in ids for c in range(off[k] // tc_in, off[k + 1] // tc_in)]

    steps = lambda grid: grid[0] * grid[1]
    n_body = (seq // tm) * (ffn1_w_gate.shape[2] // 512)
    n_inp = steps(_inproj_grid(seq, sum(sizes[k] for k in ids16))[2])
    n_hg = steps(_hgrn_grid(seq, hg_w)[2])
    hx, nx, w32, w16 = _ffn(
        hx, None, norm_ffn1[0], norm_mix[0], w1_gate, w1_up, w1_down, emit_norm=True, head=(h0, n0),
        cast=(_cast_tile(w_in[0], n_body, tc_in, [blocks(ids32), blocks(ids16)]),),
        tm=tm, name="ffn_body")

    kind = ("silu", "forget", "ident", "silu", "ident", "sigm", "sigm")
    scale = {0: HG_DK ** -0.5, 3: hg_norm[0].astype(F32)}

    def segments(ids):
        return tuple((sizes[k], kind[k]) for k in ids)

    def scale_row(ids):
        return jnp.concatenate([jnp.broadcast_to(jnp.asarray(scale.get(k, 1.0), F32), (sizes[k],))
                                for k in ids])[None]

    s32, s16 = starts(ids32), starts(ids16)
    lb32 = jnp.pad(hg_lower_bounds.astype(F32),
                   ((0, 0), (s32[1], sum(sizes[k] for k in ids32) - s32[1] - hg_k)))
    x32, m32 = _inproj(nx, nm, w32, segments(ids32), scale_row(ids32), lb32, F32)
    x16, m16, w_proj, w_glu_a, w_glu_b, w_o = _inproj(
        nx, nm, w16, segments(ids16), scale_row(ids16), None, BF16,
        cast=[_cast_tile(w, n_inp) for w in (hg_w_proj[0], s5_w_glu_a[0], s5_w_glu_b[0], w_out[0])])

    o, w2_gate, w2_up, w2_down = _hgrn(
        (x32, s32[0]), (x32, s32[1]), (x16, s16[2]), (x16, s16[3]),
        m32[:, s32[1]:s32[1] + hg_k], m16[:, s16[2]:s16[2] + hg_w],
        cast=[_cast_tile(w, n_hg) for w in (ffn2_w_gate[0], ffn2_w_up[0], ffn2_w_down[0])])
    m, win, wout, at = _s5_weights(s5_lam_re[0], s5_lam_im[0], s5_log_step[0], s5_b_re[0],
                                   s5_b_im[0], s5_c_re[0], s5_c_im[0], s5_d[0])
    y = _s5_scan((x32, s32[4]), m32[:, s32[4]:s32[4] + s5_w], m, win, wout, at)
    hx = _merge(hx, o, y, (x16, s16[5]), (x16, s16[6]), w_proj, w_glu_a, w_glu_b, w_o)
    out, = _ffn(hx, None, norm_ffn2[0], norm_final, w2_gate, w2_up, w2_down, emit_norm=False,
                tm=tm, name="ffn_final")
    return out[None].astype(x.dtype)
```

```python
import functools
import math

import numpy as np
import jax
import jax.numpy as jnp
from jax import lax
from jax.experimental import pallas as pl
from jax.experimental.pallas import tpu as pltpu

EPS = 1e-6
HG_DK = 128
HG_CHUNK = 64
S5_GROUP = 16
S5_STATE = 64
S5_T = 16
S5_SG = 8
LANE = 128
INPROJ_SUB = 256
VMEM_LIMIT = 56 * 1024 * 1024

F32 = jnp.float32
BF16 = jnp.bfloat16


def _cparams(sem):
    return pltpu.CompilerParams(dimension_semantics=sem, vmem_limit_bytes=VMEM_LIMIT)


def _rms(v, g):
    return v * lax.rsqrt(jnp.mean(v * v, axis=-1, keepdims=True) + EPS) * g


def _sigmoid(x):
    return 0.5 * jnp.tanh(0.5 * x) + 0.5


def _silu(x):
    h = 0.5 * x
    return h * (jnp.tanh(h) + 1.0)


def _dot(a, b):
    return jnp.dot(a, b, preferred_element_type=F32)


def _dot_nt(a, b):
    return lax.dot_general(a, b, (((1,), (1,)), ((), ())), preferred_element_type=F32)


def _dot_tn(a, b):
    return lax.dot_general(a, b, (((0,), (0,)), ((), ())), preferred_element_type=F32)


def _cast_jobs(cast, grid):
    nj = grid[1]
    in_specs, out_specs, out_shapes, static = [], [], [], []
    for src, (tr, tc), dests in cast:
        rows, cols = src.shape
        ncb = cols // tc
        n_tiles = (rows // tr) * ncb
        assert rows % tr == 0 and cols % tc == 0 and n_tiles <= grid[0] * nj

        def tile(i, j, n_tiles=n_tiles, ncb=ncb):
            t = jnp.minimum(i * nj + j, n_tiles - 1)
            return t // ncb, t % ncb

        in_specs.append(pl.BlockSpec((tr, tc), tile))
        for blocks in dests:
            park, k = [], 0
            for c in range(ncb):
                k = blocks.index(c) if c in blocks else k
                park.append(k)
            jumps = [(m, park[m] - park[m - 1]) for m in range(1, ncb) if park[m] != park[m - 1]]

            def dest(i, j, tile=tile, first=park[0], jumps=jumps):
                r, c = tile(i, j)
                return r, first + sum(jnp.where(c >= m, dk, 0) for m, dk in jumps)

            out_specs.append(pl.BlockSpec((tr, tc), tile if park == list(range(ncb)) else dest))
            out_shapes.append(jax.ShapeDtypeStruct((rows, len(blocks) * tc), BF16))
        static.append((n_tiles, ncb, tuple(tuple(b) for b in dests)))
    return in_specs, out_specs, out_shapes, tuple(static)


def _n_cast_out(jobs):
    return sum(len(dests) for _, _, dests in jobs)


def _run_cast_jobs(cast_in, cast_out, jobs):
    step = pl.program_id(0) * pl.num_programs(1) + pl.program_id(1)
    outs = iter(cast_out)
    for src, (n_tiles, ncb, dests) in zip(cast_in, jobs):
        dsts = [next(outs) for _ in dests]
        tile = src[...].astype(BF16)
        if len(dsts) == 1:
            dsts[0][...] = tile
        else:
            c = jnp.minimum(step, n_tiles - 1) % ncb
            for dst, cols in zip(dsts, dests):
                owns = functools.reduce(jnp.logical_or, [c == m for m in cols])

                @pl.when(owns)
                def _(dst=dst):
                    dst[...] = tile


def _cast_tile(w, n_steps, tc=None, dests=None):
    rows, cols = w.shape
    if tc is None and rows % n_steps == 0 and (rows // n_steps) % 16 == 0:
        return w, (rows // n_steps, cols), [[0]]
    tc = math.gcd(tc or 512, cols)
    tr = 128
    while (rows // tr) * (cols // tc) > n_steps:
        tr *= 2
    return w, (tr, tc), dests or [list(range(cols // tc))]


def _ffn_kernel(*refs, tm, nj, n_meta, emit_norm, w_f32, head, jobs):
    it = iter(refs)
    take = lambda k: [next(it) for _ in range(k)]
    x_ref, = take(1)
    m_ref = take(1)[0] if n_meta else None
    g_ref, gn_ref, wg_ref, wu_ref, wd_ref = take(5)
    hx_ref, hn16_ref = take(2) if head else (None, None)
    cast_in = take(len(jobs))
    ox_ref, = take(1)
    nx_ref = take(1)[0] if emit_norm else None
    om_ref, nm_ref = take(2) if n_meta else (None, None)
    w16_out = take(3) if w_f32 else ()
    cast_out = take(_n_cast_out(jobs))
    hn_ref, = take(1)
    acc_ref = take(1)[0] if n_meta else ox_ref
    i = pl.program_id(0)
    j = pl.program_id(1)
    _run_cast_jobs(cast_in, cast_out, jobs)

    def step(first, last):
        if first:
            hn_ref[0:tm, :] = _rms(x_ref[...], g_ref[...]).astype(BF16)
            if n_meta:
                hn_ref[tm:tm + n_meta, :] = _rms(m_ref[...], g_ref[...]).astype(BF16)
        tiles = [w[...].astype(BF16) if w_f32 else w[...] for w in (wg_ref, wu_ref, wd_ref)]
        for dst, tile in zip(w16_out, tiles):
            dst[...] = tile
        hn = hn_ref[...]
        gate = _dot(hn, tiles[0])
        up = _dot(hn, tiles[1])
        act = (_silu(gate) * up).astype(BF16)
        if first:
            acc_ref[...] = _dot(act, tiles[2])
        else:
            acc_ref[...] += _dot(act, tiles[2])
        if last:
            hx = x_ref[...] + 0.5 * acc_ref[0:tm, :]
            if emit_norm:
                ox_ref[...] = hx
                nx_ref[...] = _rms(hx, gn_ref[...]).astype(BF16)
            else:
                ox_ref[...] = _rms(hx, gn_ref[...])
            if n_meta:
                hm = m_ref[...] + 0.5 * acc_ref[tm:tm + n_meta, :]
                om_ref[...] = hm
                nm_ref[...] = _rms(hm, gn_ref[...]).astype(BF16)

    def body():
        if nj == 1:
            step(True, True)
        else:
            pl.when(j == 0)(functools.partial(step, True, False))
            pl.when((j > 0) & (j < nj - 1))(functools.partial(step, False, False))
            pl.when(j == nj - 1)(functools.partial(step, False, True))

    if head:
        @pl.when((i == 0) & (j == 0))
        def _():
            ox_ref[...] = hx_ref[...]
            nx_ref[...] = hn16_ref[...]

        pl.when(i > 0)(body)
    else:
        body()


def _ffn(hx, hm, g, g_next, wg, wu, wd, *, emit_norm, head=None, cast=(), rows=None, name,
         tm=512, tf=512):
    n, d = hx.shape
    rows = n if rows is None else rows
    dff = wg.shape[1]
    n_meta = 0 if hm is None else hm.shape[0]
    w_f32 = wg.dtype == F32
    grid = (rows // tm, dff // tf)
    assert not w_f32 or grid[0] == 1
    row = lambda i, j: (i, 0)
    const = lambda i, j: (0, 0)
    in_specs = [pl.BlockSpec((tm, d), row)]
    args = [hx]
    if n_meta:
        in_specs.append(pl.BlockSpec((n_meta, d), const))
        args.append(hm)
    col = (lambda i, j: jnp.where(i == 0, 0, j)) if head else (lambda i, j: j)
    in_specs += [pl.BlockSpec((1, d), const), pl.BlockSpec((1, d), const),
                 pl.BlockSpec((d, tf), lambda i, j: (0, col(i, j))),
                 pl.BlockSpec((d, tf), lambda i, j: (0, col(i, j))),
                 pl.BlockSpec((tf, d), lambda i, j: (col(i, j), 0))]
    args += [g.reshape(1, d), g_next.reshape(1, d), wg, wu, wd]
    if head:
        in_specs += [pl.BlockSpec((tm, d), const, pipeline_mode=pl.Buffered(1))] * 2
        args += list(head)
    job_in, job_out, job_shapes, jobs = _cast_jobs(cast, grid)
    in_specs += job_in
    args += [src for src, _, _ in cast]
    out_shape = [jax.ShapeDtypeStruct((rows, d), F32)]
    out_specs = [pl.BlockSpec((tm, d), row)]
    if emit_norm:
        out_shape.append(jax.ShapeDtypeStruct((rows, d), BF16))
        out_specs.append(pl.BlockSpec((tm, d), row))
    if n_meta:
        out_shape += [jax.ShapeDtypeStruct((n_meta, d), F32), jax.ShapeDtypeStruct((n_meta, d), BF16)]
        out_specs += [pl.BlockSpec((n_meta, d), const)] * 2
    if w_f32:
        out_shape += [jax.ShapeDtypeStruct(w.shape, BF16) for w in (wg, wu, wd)]
        out_specs += [pl.BlockSpec((d, tf), lambda i, j: (0, j)),
                      pl.BlockSpec((d, tf), lambda i, j: (0, j)),
                      pl.BlockSpec((tf, d), lambda i, j: (j, 0))]
    out_specs += job_out
    out_shape += job_shapes
    return pl.pallas_call(
        functools.partial(_ffn_kernel, tm=tm, nj=grid[1], n_meta=n_meta, emit_norm=emit_norm,
                          w_f32=w_f32, head=head is not None, jobs=jobs),
        grid=grid, in_specs=in_specs, out_specs=out_specs, out_shape=out_shape,
        scratch_shapes=[pltpu.VMEM((tm + n_meta, d), BF16)]
        + ([pltpu.VMEM((tm + n_meta, d), F32)] if n_meta else []),
        compiler_params=_cparams(("arbitrary", "arbitrary")),
        name=name,
    )(*args)


def _inproj_kernel(*refs, tm, n_meta, with_lb, kinds, jobs):
    it = iter(refs)
    take = lambda k: [next(it) for _ in range(k)]
    x_ref, m_ref, w_ref, scale_ref = take(4)
    lb_ref = take(1)[0] if with_lb else None
    cast_in = take(len(jobs))
    ox_ref, om_ref = take(2)
    cast_out = take(_n_cast_out(jobs))
    lhs_ref, = take(1)
    _run_cast_jobs(cast_in, cast_out, jobs)

    @pl.when(pl.program_id(1) == 0)
    def _():
        lhs_ref[0:tm, :] = x_ref[...]
        lhs_ref[tm:tm + n_meta, :] = m_ref[...]

    tn = w_ref.shape[1]
    sw = tn // len(kinds[0])

    def column_step(step_kinds):
        for s, kind in enumerate(step_kinds):
            cs = slice(s * sw, (s + 1) * sw)
            res = _dot(lhs_ref[...], w_ref[:, cs])
            if kind == "ident":
                out = res
            elif kind == "silu":
                h = 0.5 * res
                out = (scale_ref[:, cs] * h) * (jnp.tanh(h) + 1.0)
            elif kind == "sigm":
                out = 0.5 * jnp.tanh(0.5 * res) + 0.5
            else:
                assert kind == "forget"
                lbl = lb_ref[:, cs]
                e = jnp.exp(lbl - jnp.max(lbl, axis=0, keepdims=True))
                lb = e[0:1, :] / jnp.sum(e, axis=0, keepdims=True)
                out = (0.5 - 0.5 * lb) * jnp.tanh(0.5 * res) + (0.5 + 0.5 * lb)
            ox_ref[:, cs] = out[0:tm, :].astype(ox_ref.dtype)
            om_ref[:, cs] = out[tm:tm + n_meta, :].astype(om_ref.dtype)

    if len(set(kinds)) == 1:
        column_step(kinds[0])
    else:
        for jj, step_kinds in enumerate(kinds):
            pl.when(pl.program_id(1) == jj)(functools.partial(column_step, step_kinds))


def _inproj_grid(n, cols, tm=1024, tn=1792):
    tm = min(tm, n)
    tn = math.gcd(tn, cols)
    assert tn % LANE == 0 and n % tm == 0
    return tm, tn, (n // tm, cols // tn)


def _inproj(nx, nm, w, segments, scale, lower_bounds, dtype, cast=()):
    n, d = nx.shape
    n_meta = nm.shape[0]
    cols = w.shape[1]
    with_lb = lower_bounds is not None
    tm, tn, grid = _inproj_grid(n, cols)
    sw = math.gcd(tn, INPROJ_SUB)
    col_kind = [kind for width, kind in segments for _ in range(width // sw)]
    assert all(width % sw == 0 for width, _ in segments) and len(col_kind) * sw == cols
    kinds = tuple(tuple(col_kind[jj * (tn // sw):(jj + 1) * (tn // sw)]) for jj in range(grid[1]))
    job_in, job_out, job_shapes, jobs = _cast_jobs(cast, grid)
    in_specs = [pl.BlockSpec((tm, d), lambda i, j: (i, 0)),
                pl.BlockSpec((n_meta, d), lambda i, j: (0, 0)),
                pl.BlockSpec((d, tn), lambda i, j: (0, j)),
                pl.BlockSpec((1, tn), lambda i, j: (0, j))]
    args = [nx, nm, w, scale]
    if with_lb:
        in_specs.append(pl.BlockSpec((lower_bounds.shape[0], tn), lambda i, j: (0, j)))
        args.append(lower_bounds)
    ox, om, *copies = pl.pallas_call(
        functools.partial(_inproj_kernel, tm=tm, n_meta=n_meta, with_lb=with_lb, kinds=kinds,
                          jobs=jobs),
        grid=grid,
        in_specs=in_specs + job_in,
        out_specs=[pl.BlockSpec((tm, tn), lambda i, j: (i, j)),
                   pl.BlockSpec((pl.Squeezed(), n_meta, tn), lambda i, j: (i, 0, j))] + job_out,
        out_shape=[jax.ShapeDtypeStruct((n, cols), dtype),
                   jax.ShapeDtypeStruct((n // tm, n_meta, cols), dtype)] + job_shapes,
        scratch_shapes=[pltpu.VMEM((tm + n_meta, d), BF16)],
        compiler_params=_cparams(("arbitrary", "arbitrary")),
        name="mixer_inproj_" + jnp.dtype(dtype).name,
    )(*args, *[src for src, _, _ in cast])
    return (ox, om[0], *copies)


def _split3(x):
    hi = x.astype(BF16)
    r1 = x - hi.astype(F32)
    mid = r1.astype(BF16)
    lo = (r1 - mid.astype(F32)).astype(BF16)
    return hi, mid, lo


def _dot3(m01, parts):
    return _dot(m01, parts[0]) + _dot(m01, parts[1]) + _dot(m01, parts[2])


def _tri(n, rep=1):
    r = lax.broadcasted_iota(jnp.int32, (n * rep, n), 0)
    c = lax.broadcasted_iota(jnp.int32, (n * rep, n), 1)
    return jnp.where(r >= c * rep, 1.0, 0.0).astype(BF16)


def _hgrn_tables(c):
    sums = [np.tril(np.ones((c, c)))]
    level = np.full((c, c), -1, np.int32)
    t_idx, s_idx = np.meshgrid(np.arange(c), np.arange(c), indexing="ij")
    halves = []
    h = c // 2
    while h >= 1:
        if h < 8:
            m = np.zeros((c, c))
            for r in range(c):
                mid = (r // (2 * h)) * 2 * h + h
                if r < mid:
                    m[r, r + 1:mid] = 1.0
                else:
                    m[r, mid:r + 1] = 1.0
            sums.append(m)
        pair = 2 * h
        level[(t_idx // pair == s_idx // pair) & (s_idx % pair < h) & (t_idx % pair >= h)] = len(halves)
        halves.append(h)
        h //= 2
    assert len(halves) % 2 == 0
    level[np.arange(c), np.arange(c)] = len(halves)
    owner = np.full((c, 2 * c), -1, np.int32)
    owner[:, :c] = np.where((level >= 0) & (level % 2 == 0), level // 2, -1)
    owner[:, c:] = np.where(level % 2 == 1, level // 2, -1)
    return np.concatenate(sums, axis=0), owner, tuple(halves)


def _hgrn_score_products(q, k, x, halves):
    C = HG_CHUNK
    b = x[0:C]
    zero = jnp.zeros((C, HG_DK), BF16)
    diag = jnp.sum(q * k, axis=-1, keepdims=True)
    qts, kts = [], []
    fine = 0
    for h in halves:
        if h >= 8:
            none = jnp.zeros((h, HG_DK), F32)
            qh, kh = [], []
            for r in range(0, C, 2 * h):
                b_mid = b[r + h - 1:r + h, :]
                kh += [k[r:r + h] * jnp.exp(b_mid - b[r:r + h]), none]
                qh += [none, q[r + h:r + 2 * h] * jnp.exp(b[r + h:r + 2 * h] - b_mid)]
            qts.append(jnp.concatenate(qh, axis=0).astype(BF16))
            kts.append(jnp.concatenate(kh, axis=0).astype(BF16))
        else:
            fine += 1
            e = jnp.exp(x[fine * C:(fine + 1) * C])
            qts.append((q * e).astype(BF16))
            kts.append((k * e).astype(BF16))
    prods = []
    for p in range(len(halves) // 2):
        lhs = jnp.concatenate([qts[2 * p], qts[2 * p + 1]], axis=1)
        rhs = jnp.concatenate([jnp.concatenate([kts[2 * p], zero], axis=1),
                               jnp.concatenate([zero, kts[2 * p + 1]], axis=1)], axis=0)
        prods.append(_dot_nt(lhs, rhs))
    return prods, diag


def _hgrn_pair(q2, f2, v2, gate2, st2, sums, owns, halves):
    C, D = HG_CHUNK, HG_DK
    halfs = (slice(0, D), slice(D, 2 * D))
    k2 = 1.0 - f2
    x2 = _dot(sums, jnp.concatenate(_split3(jnp.log(f2)), axis=0))
    yield
    b2 = x2[0:C]
    b_last = b2[C - 1:C, :]
    kdec2 = (k2 * jnp.exp(b_last - b2)).astype(BF16)
    qe2 = (q2 * jnp.exp(b2)).astype(BF16)
    st2b = st2.astype(BF16)
    scores = []
    for c in halfs:
        scores.append(_hgrn_score_products(q2[:, c], k2[:, c], x2[:, c], halves))
        yield
    zero = jnp.zeros((2 * C, D), BF16)
    os, vvts, kds = [], [], []
    for j, c in enumerate(halfs):
        prods, diag = scores[j]
        a = jnp.where(owns[len(prods)], diag, 0.0)
        for p, prod in enumerate(prods):
            a = jnp.where(owns[p], prod, a)
        v = v2[:, c]
        vvt = jnp.concatenate([v, v], axis=0).T
        os.append(_dot_nt(jnp.concatenate([qe2[:, c], a.astype(BF16)], axis=1),
                          jnp.concatenate([st2b[:, c], vvt], axis=1)))
        vvts.append(vvt)
        kd = jnp.concatenate([kdec2[:, c], jnp.zeros((C, D), BF16)], axis=0)
        kds.append(jnp.concatenate([kd, zero] if j == 0 else [zero, kd], axis=1))
        yield
    st_new = st2 * jnp.exp(b_last) + _dot(jnp.concatenate(vvts, axis=1), jnp.concatenate(kds, axis=0))
    yield
    ons = [o * lax.rsqrt(jnp.mean(o * o, axis=-1, keepdims=True) + EPS) for o in os]
    on2 = jnp.concatenate(ons, axis=1) * gate2
    return on2.astype(BF16), st_new


def _interleave(gens):
    out = [None] * len(gens)
    live = list(range(len(gens)))
    while live:
        for g in list(live):
            try:
                next(gens[g])
            except StopIteration as stop:
                out[g] = stop.value
                live.remove(g)
    return out


def _hgrn_kernel(q_ref, f_ref, v_ref, go_ref, fm_ref, vm_ref, sums_ref, owner_ref,
                 *rest, n_chunks, n_meta, hb, halves, jobs):
    cast_in, o_ref = rest[:len(jobs)], rest[len(jobs)]
    cast_out, st_ref = rest[len(jobs) + 1:-1], rest[-1]
    _run_cast_jobs(cast_in, cast_out, jobs)
    i = pl.program_id(0)
    p0 = pl.program_id(1) * (hb // 2)
    C = HG_CHUNK
    W = 2 * HG_DK

    @pl.when(i == 0)
    def _():
        tri_m = _tri(n_meta)
        for pair in range(hb // 2):
            sts = []
            for hh in (2 * pair, 2 * pair + 1):
                cols = slice(hh * HG_DK, (hh + 1) * HG_DK)
                fm = fm_ref[:, cols]
                bm = _dot3(tri_m, _split3(jnp.log(fm)))
                kdec = ((1.0 - fm) * jnp.exp(bm[n_meta - 1:n_meta, :] - bm)).astype(BF16)
                sts.append(_dot_tn(vm_ref[:, cols], kdec))
            st_ref[p0 + pair] = jnp.concatenate(sts, axis=1)

    def chunk(c, states):
        r0 = pl.multiple_of(c * C, C)
        owner = owner_ref[...]
        owns = [owner == p for p in range(len(halves) // 2 + 1)]
        sums = sums_ref[...]
        gens = []
        for pair in range(hb // 2):
            cols = slice(pair * W, (pair + 1) * W)
            gate2 = go_ref[pl.ds(r0, C), cols].astype(F32)
            gens.append(_hgrn_pair(q_ref[pl.ds(r0, C), cols], f_ref[pl.ds(r0, C), cols],
                                   v_ref[pl.ds(r0, C), cols], gate2, states[pair],
                                   sums, owns, halves))
        results = _interleave(gens)
        o_ref[pl.ds(r0, C), :] = jnp.concatenate([r[0] for r in results], axis=1)
        return tuple(r[1] for r in results)

    states = lax.fori_loop(0, n_chunks, chunk, tuple(st_ref[p0 + p] for p in range(hb // 2)))
    for pair in range(hb // 2):
        st_ref[p0 + pair] = states[pair]


def _hgrn_grid(n, width, rb=512, hb=12):
    heads = width // HG_DK
    hb = min(hb, heads)
    assert hb % 2 == 0 and heads % hb == 0
    return rb, hb, (n // rb, heads // hb)


def _hgrn(q, f, v, go, fm, vm, cast=()):
    n, width = q[0].shape[0], fm.shape[1]
    heads = width // HG_DK
    rb, hb, grid = _hgrn_grid(n, width)
    bw = hb * HG_DK
    n_meta = fm.shape[0]
    job_in, job_out, job_shapes, jobs = _cast_jobs(cast, grid)
    sums, owner, halves = _hgrn_tables(HG_CHUNK)
    sums3 = jnp.asarray(np.concatenate([sums] * 3, axis=1), BF16)
    blk = lambda i, h: (i, h)
    mblk = lambda i, h: (0, h)
    const = lambda i, h: (0, 0)

    def view(col0):
        assert col0 % bw == 0
        return pl.BlockSpec((rb, bw), lambda i, h: (i, col0 // bw + h))

    return pl.pallas_call(
        functools.partial(_hgrn_kernel, n_chunks=rb // HG_CHUNK, n_meta=n_meta, hb=hb,
                          halves=halves, jobs=jobs),
        grid=grid,
        in_specs=[view(q[1]), view(f[1]), view(v[1]), view(go[1])]
        + [pl.BlockSpec((n_meta, hb * HG_DK), mblk)] * 2
        + [pl.BlockSpec(sums3.shape, const), pl.BlockSpec(owner.shape, const)] + job_in,
        out_specs=[pl.BlockSpec((rb, hb * HG_DK), blk)] + job_out,
        out_shape=[jax.ShapeDtypeStruct((n, width), BF16)] + job_shapes,
        scratch_shapes=[pltpu.VMEM((heads // 2, HG_DK, 2 * HG_DK), F32)],
        compiler_params=_cparams(("arbitrary", "arbitrary")),
        name="hgrn2",
    )(q[0], f[0], v[0], go[0], fm, vm, sums3, jnp.asarray(owner), *[src for src, _, _ in cast])


def _s5_weights_kernel(lr_ref, li_ref, dt_ref, bre_ref, bim_ref, cre_ref, cim_ref, dsk_ref,
                       kt_ref, win_ref, wout_ref, at_ref):
    T = S5_T
    ns = lr_ref.shape[1]
    nc = cre_ref.shape[0]
    lr, li, dt = lr_ref[...], li_ref[...], dt_ref[...]

    def powers(tau):
        mag = jnp.exp(lr * dt * tau)
        ang = li * dt * tau
        return mag * jnp.cos(ang), mag * jnp.sin(ang)

    a_re, a_im = powers(1.0)
    num_re, num_im = a_re - 1.0, a_im
    den = lr * lr + li * li
    coef_re = (num_re * lr + num_im * li) / den
    coef_im = (num_im * lr - num_re * li) / den
    br, bi = bre_ref[...], bim_ref[...]
    bbar_re = coef_re * br - coef_im * bi
    bbar_im = coef_re * bi + coef_im * br

    pw = [powers(float(tau)) for tau in range(T + 1)]
    for s in range(T):
        p_re, p_im = pw[T - 1 - s]
        win_ref[s * nc:(s + 1) * nc, 0:ns] = (bbar_re * p_re - bbar_im * p_im).astype(BF16)
        win_ref[s * nc:(s + 1) * nc, ns:2 * ns] = (bbar_re * p_im + bbar_im * p_re).astype(BF16)
    at_ref[:, 0:ns] = pw[T][0]
    at_ref[:, ns:2 * ns] = pw[T][1]

    cr, ci = cre_ref[...], cim_ref[...]
    def split2(a):
        hi = a.astype(BF16)
        return hi, (a - hi.astype(F32)).astype(BF16)

    bb_hi, bb_lo = split2(jnp.concatenate([bbar_re, -bbar_im], axis=1))
    for tau in range(T + 1):
        p_re, p_im = pw[tau]
        ca_re = cr * p_re - ci * p_im
        ca_im = cr * p_im + ci * p_re
        if tau >= 1:
            t = tau - 1
            wout_ref[0:ns, t * nc:(t + 1) * nc] = ca_re.T.astype(BF16)
            wout_ref[ns:2 * ns, t * nc:(t + 1) * nc] = (-ca_im).T.astype(BF16)
        if tau < T:
            ca_hi, ca_lo = split2(jnp.concatenate([ca_re, ca_im], axis=1))
            kt = _dot_nt(bb_hi, ca_hi) + _dot_nt(bb_hi, ca_lo) + _dot_nt(bb_lo, ca_hi)
            if tau == 0:
                kt = kt + dsk_ref[...]
            kt_ref[tau] = kt.astype(BF16)


def _s5_kernel(lr_ref, li_ref, dt_ref, bre_ref, bim_ref, cre_ref, cim_ref, dsk_ref, u_ref, um_ref,
               y_ref, kt_ref, win_ref, wout_ref, at_ref, uc_ref, v_ref, xp_ref, m_ref, *, n_chunks):
    _s5_weights_kernel(lr_ref, li_ref, dt_ref, bre_ref, bim_ref, cre_ref, cim_ref, dsk_ref,
                       kt_ref, win_ref, wout_ref, at_ref)
    _s5_scan_kernel(u_ref, um_ref, kt_ref, win_ref, wout_ref, at_ref, y_ref,
                    uc_ref, v_ref, xp_ref, m_ref, n_chunks=n_chunks)


def _s5(u, um, lam_re, lam_im, log_step, b_re, b_im, c_re, c_im, d_skip):
    u, col0 = u
    n = u.shape[0]
    G, N = lam_re.shape
    P = d_skip.shape[1]
    nsg = G // S5_SG
    ns, nc = S5_SG * N, S5_SG * P
    T = S5_T
    dt = jnp.exp(log_step.astype(F32))
    eye = jnp.eye(S5_SG, dtype=F32)

    def rows(a):
        return a.astype(F32).reshape(nsg, 1, ns)

    def embed_b(b):
        b = b.astype(F32).reshape(nsg, S5_SG, N, P)
        return jnp.einsum('zgnq,gh->zhqgn', b, eye).reshape(nsg, nc, ns)

    def embed_c(c):
        c = c.astype(F32).reshape(nsg, S5_SG, P, N)
        return jnp.einsum('zgpn,gh->zhpgn', c, eye).reshape(nsg, nc, ns)

    dtb = jnp.broadcast_to(dt[:, None], (G, N))
    d_diag = jax.vmap(jnp.diag)(d_skip.astype(F32).reshape(nsg, nc))
    assert um.shape == (T, nsg * nc) and n % T == 0 and col0 % nc == 0
    n_chunks = n // T
    n_rows = -(-(n_chunks + 1) // 16) * 16
    sq = pl.Squeezed()
    spec_r = pl.BlockSpec((sq, 1, ns), lambda z: (z, 0, 0))
    spec_b = pl.BlockSpec((sq, nc, ns), lambda z: (z, 0, 0))
    return pl.pallas_call(
        functools.partial(_s5_kernel, n_chunks=n_chunks),
        grid=(nsg,),
        in_specs=[spec_r, spec_r, spec_r, spec_b, spec_b, spec_b, spec_b,
                  pl.BlockSpec((sq, nc, nc), lambda z: (z, 0, 0)),
                  pl.BlockSpec((n, nc), lambda z: (0, col0 // nc + z)),
                  pl.BlockSpec((T, nc), lambda z: (0, z))],
        out_specs=pl.BlockSpec((n, nc), lambda z: (0, z)),
        out_shape=jax.ShapeDtypeStruct((n, nsg * nc), F32),
        scratch_shapes=[pltpu.VMEM((T, nc, nc), BF16), pltpu.VMEM((T * nc, 2 * ns), BF16),
                        pltpu.VMEM((2 * ns, T * nc), BF16), pltpu.VMEM((1, 2 * ns), F32),
                        pltpu.VMEM((n_rows, T * nc), BF16), pltpu.VMEM((n_rows, 2 * ns), F32),
                        pltpu.VMEM((n_rows, 2 * ns), F32), pltpu.VMEM((T * nc, T * nc), BF16)],
        compiler_params=_cparams(("arbitrary",)),
        name="s5",
    )(rows(lam_re), rows(lam_im), rows(dtb),
      embed_b(b_re), embed_b(b_im), embed_c(c_re), embed_c(c_im), d_diag, u, um)


def _s5_scan_kernel(u_ref, um_ref, kt_ref, win_ref, wout_ref, at_ref, y_ref,
                    uc_ref, v_ref, xp_ref, m_ref, *, n_chunks):
    T = S5_T
    nc = u_ref.shape[1]
    ns = at_ref.shape[1] // 2

    @pl.when(pl.program_id(0) == 0)
    def _():
        zero = jnp.zeros((nc, nc), BF16)
        for s in range(1, T):
            for t in range(s):
                m_ref[s * nc:(s + 1) * nc, t * nc:(t + 1) * nc] = zero

    for tau in range(T):
        tap = kt_ref[tau]
        for s in range(T - tau):
            m_ref[s * nc:(s + 1) * nc, (s + tau) * nc:(s + tau + 1) * nc] = tap

    for t in range(T):
        uc_ref[0:n_chunks, t * nc:(t + 1) * nc] = u_ref[pl.ds(t, n_chunks, stride=T), :].astype(BF16)
        uc_ref[n_chunks:n_chunks + 1, t * nc:(t + 1) * nc] = um_ref[t:t + 1, :].astype(BF16)
    pad = uc_ref.shape[0] - n_chunks - 1
    uc_ref[n_chunks + 1:, :] = jnp.zeros((pad, T * nc), BF16)
    uc = uc_ref[...]
    v_ref[...] = _dot(uc, win_ref[...])
    a_re, a_im = at_ref[:, 0:ns], at_ref[:, ns:2 * ns]

    x_re0 = v_ref[n_chunks:n_chunks + 1, 0:ns]
    x_im0 = v_ref[n_chunks:n_chunks + 1, ns:2 * ns]
    xp_ref[n_chunks:, :] = jnp.zeros((pad + 1, 2 * ns), F32)

    def step(c, carry):
        x_re, x_im = carry
        xp_ref[pl.ds(c, 1), 0:ns] = x_re
        xp_ref[pl.ds(c, 1), ns:2 * ns] = x_im
        v_re = v_ref[pl.ds(c, 1), 0:ns]
        v_im = v_ref[pl.ds(c, 1), ns:2 * ns]
        return (a_re * x_re - a_im * x_im + v_re, a_re * x_im + a_im * x_re + v_im)

    lax.fori_loop(0, n_chunks, step, (x_re0, x_im0))
    xp = xp_ref[...].astype(BF16)
    for p in range(T // 2):
        k_hi = (2 * p + 2) * nc
        cols = slice(2 * p * nc, k_hi)
        yc = _dot(uc_ref[:, 0:k_hi], m_ref[0:k_hi, cols]) + _dot(xp, wout_ref[:, cols])
        for t in (2 * p, 2 * p + 1):
            y_ref[pl.ds(t, n_chunks, stride=T), :] = yc[0:n_chunks, (t - 2 * p) * nc:(t - 2 * p + 1) * nc]


def _merge_kernel(*refs, pieces):
    h_ref, o_ref, y_ref = refs[0:3]
    gh_refs, gs_refs = refs[3:3 + pieces], refs[3 + pieces:3 + 2 * pieces]
    wp_ref, wa_ref, wb_ref, wo_ref, out_ref = refs[3 + 2 * pieces:]
    gh = jnp.concatenate([r[...] for r in gh_refs], axis=1).astype(F32)
    gs = jnp.concatenate([r[...] for r in gs_refs], axis=1).astype(F32)
    y_hg = _dot(o_ref[...], wp_ref[...])
    z = jax.nn.gelu(y_ref[...]).astype(BF16)
    y_s5 = _dot(z, wa_ref[...]) * _sigmoid(_dot(z, wb_ref[...]))
    merged = gh * y_hg + gs * y_s5
    out_ref[...] = h_ref[...] + _dot(merged.astype(BF16), wo_ref[...])


def _merge(hx, o, y, gh, gs, wp, wa, wb, wo, *, tm=256):
    n, d = hx.shape
    row = lambda i: (i, 0)
    const = lambda i: (0, 0)
    gw = functools.reduce(math.gcd, (gh[1], gs[1], d))
    assert gw % LANE == 0
    pieces = d // gw

    def resident(shape):
        return pl.BlockSpec(shape, const, pipeline_mode=pl.Buffered(1))

    def view(col0):
        return [pl.BlockSpec((tm, gw), functools.partial(lambda i, k: (i, k), k=col0 // gw + k))
                for k in range(pieces)]

    return pl.pallas_call(
        functools.partial(_merge_kernel, pieces=pieces),
        grid=(n // tm,),
        in_specs=[pl.BlockSpec((tm, d), row), pl.BlockSpec((tm, o.shape[1]), row),
                  pl.BlockSpec((tm, y.shape[1]), row)] + view(gh[1]) + view(gs[1])
        + [resident(wp.shape), resident(wa.shape), resident(wb.shape), resident(wo.shape)],
        out_specs=pl.BlockSpec((tm, d), row),
        out_shape=jax.ShapeDtypeStruct((n, d), F32),
        compiler_params=_cparams(("arbitrary",)),
        name="mixer_merge",
    )(hx, o, y, *([gh[0]] * pieces), *([gs[0]] * pieces), wp, wa, wb, wo)


def kernel(x, meta_tokens, norm_ffn1, ffn1_w_gate, ffn1_w_up, ffn1_w_down, norm_mix, w_in, hg_lower_bounds, hg_norm, hg_w_proj, s5_lam_re, s5_lam_im, s5_log_step, s5_b_re, s5_b_im, s5_c_re, s5_c_im, s5_d, s5_w_glu_a, s5_w_glu_b, w_out, norm_ffn2, ffn2_w_gate, ffn2_w_up, ffn2_w_down, norm_final):
    batch, seq, d = x.shape
    depth = norm_ffn1.shape[0]
    assert batch == 1 and depth == 1
    hg_k = hg_lower_bounds.shape[1]
    hg_w = hg_norm.shape[1]
    s5_w = s5_w_glu_a.shape[1]
    hx = x[0].astype(F32)
    hm = meta_tokens.astype(F32)
    tm = min(512, seq)
    h0, n0, hm, nm, w1_gate, w1_up, w1_down = _ffn(
        hx, hm, norm_ffn1[0], norm_mix[0], ffn1_w_gate[0], ffn1_w_up[0], ffn1_w_down[0],
        emit_norm=True, rows=tm, tm=tm, tf=256, name="ffn_head")
    sizes = (hg_k, hg_k, hg_w, hg_w, s5_w, d, d)
    assert sum(sizes) == w_in.shape[2]
    off = [0]
    for s in sizes:
        off.append(off[-1] + s)
    ids32, ids16 = (0, 1, 4), (2, 3, 5, 6)
    tc_in = functools.reduce(math.gcd, sizes + (512,))

    def starts(ids):
        out, col = {}, 0
        for k in ids:
            out[k] = col
            col += sizes[k]
        return out

    def blocks(ids):
        return [c for k in ids for c in range(off[k] // tc_in, off[k + 1] // tc_in)]

    steps = lambda grid: grid[0] * grid[1]
    n_body = (seq // tm) * (ffn1_w_gate.shape[2] // 512)
    n_inp = steps(_inproj_grid(seq, sum(sizes[k] for k in ids16))[2])
    n_hg = steps(_hgrn_grid(seq, hg_w)[2])
    hx, nx, w32, w16 = _ffn(
        hx, None, norm_ffn1[0], norm_mix[0], w1_gate, w1_up, w1_down, emit_norm=True, head=(h0, n0),
        cast=(_cast_tile(w_in[0], n_body, tc_in, [blocks(ids32), blocks(ids16)]),),
        tm=tm, name="ffn_body")

    kind = ("silu", "forget", "ident", "silu", "ident", "sigm", "sigm")
    scale = {0: HG_DK ** -0.5, 3: hg_norm[0].astype(F32)}

    def segments(ids):
        return tuple((sizes[k], kind[k]) for k in ids)

    def scale_row(ids):
        return jnp.concatenate([jnp.broadcast_to(jnp.asarray(scale.get(k, 1.0), F32), (sizes[k],))
                                for k in ids])[None]

    s32, s16 = starts(ids32), starts(ids16)
    lb32 = jnp.pad(hg_lower_bounds.astype(F32),
                   ((0, 0), (s32[1], sum(sizes[k] for k in ids32) - s32[1] - hg_k)))
    x32, m32 = _inproj(nx, nm, w32, segments(ids32), scale_row(ids32), lb32, F32)
    x16, m16, w_proj, w_glu_a, w_glu_b, w_o = _inproj(
        nx, nm, w16, segments(ids16), scale_row(ids16), None, BF16,
        cast=[_cast_tile(w, n_inp) for w in (hg_w_proj[0], s5_w_glu_a[0], s5_w_glu_b[0], w_out[0])])

    o, w2_gate, w2_up, w2_down = _hgrn(
        (x32, s32[0]), (x32, s32[1]), (x16, s16[2]), (x16, s16[3]),
        m32[:, s32[1]:s32[1] + hg_k], m16[:, s16[2]:s16[2] + hg_w],
        cast=[_cast_tile(w, n_hg) for w in (ffn2_w_gate[0], ffn2_w_up[0], ffn2_w_down[0])])
    y = _s5((x32, s32[4]), m32[:, s32[4]:s32[4] + s5_w], s5_lam_re[0], s5_lam_im[0], s5_log_step[0],
            s5_b_re[0], s5_b_im[0], s5_c_re[0], s5_c_im[0], s5_d[0])
    hx = _merge(hx, o, y, (x16, s16[5]), (x16, s16[6]), w_proj, w_glu_a, w_glu_b, w_o)
    out, = _ffn(hx, None, norm_ffn2[0], norm_final, w2_gate, w2_up, w2_down, emit_norm=False,
                tm=tm, name="ffn_final")
    return out[None].astype(x.dtype)
```

```python
import functools
import math

import numpy as np
import jax
import jax.numpy as jnp
from jax import lax
from jax.experimental import pallas as pl
from jax.experimental.pallas import tpu as pltpu

EPS = 1e-6
HG_DK = 128
HG_CHUNK = 64
S5_GROUP = 16
S5_STATE = 64
S5_T = 16
S5_SG = 8
LANE = 128
INPROJ_SUB = 256
VMEM_LIMIT = 56 * 1024 * 1024

F32 = jnp.float32
BF16 = jnp.bfloat16


def _cparams(sem):
    return pltpu.CompilerParams(dimension_semantics=sem, vmem_limit_bytes=VMEM_LIMIT)


def _rms(v, g):
    return v * lax.rsqrt(jnp.mean(v * v, axis=-1, keepdims=True) + EPS) * g


def _sigmoid(x):
    return 0.5 * jnp.tanh(0.5 * x) + 0.5


def _silu(x):
    h = 0.5 * x
    return h * (jnp.tanh(h) + 1.0)


def _dot(a, b):
    return jnp.dot(a, b, preferred_element_type=F32)


def _dot_nt(a, b):
    return lax.dot_general(a, b, (((1,), (1,)), ((), ())), preferred_element_type=F32)


def _dot_tn(a, b):
    return lax.dot_general(a, b, (((0,), (0,)), ((), ())), preferred_element_type=F32)


def _cast_jobs(cast, grid):
    nj = grid[1]
    in_specs, out_specs, out_shapes, static = [], [], [], []
    for src, (tr, tc), dests in cast:
        rows, cols = src.shape
        ncb = cols // tc
        n_tiles = (rows // tr) * ncb
        assert rows % tr == 0 and cols % tc == 0 and n_tiles <= grid[0] * nj

        def tile(i, j, n_tiles=n_tiles, ncb=ncb):
            t = jnp.minimum(i * nj + j, n_tiles - 1)
            return t // ncb, t % ncb

        in_specs.append(pl.BlockSpec((tr, tc), tile))
        for blocks in dests:
            park, k = [], 0
            for c in range(ncb):
                k = blocks.index(c) if c in blocks else k
                park.append(k)
            jumps = [(m, park[m] - park[m - 1]) for m in range(1, ncb) if park[m] != park[m - 1]]

            def dest(i, j, tile=tile, first=park[0], jumps=jumps):
                r, c = tile(i, j)
                return r, first + sum(jnp.where(c >= m, dk, 0) for m, dk in jumps)

            out_specs.append(pl.BlockSpec((tr, tc), tile if park == list(range(ncb)) else dest))
            out_shapes.append(jax.ShapeDtypeStruct((rows, len(blocks) * tc), BF16))
        static.append((n_tiles, ncb, tuple(tuple(b) for b in dests)))
    return in_specs, out_specs, out_shapes, tuple(static)


def _n_cast_out(jobs):
    return sum(len(dests) for _, _, dests in jobs)


def _run_cast_jobs(cast_in, cast_out, jobs):
    step = pl.program_id(0) * pl.num_programs(1) + pl.program_id(1)
    outs = iter(cast_out)
    for src, (n_tiles, ncb, dests) in zip(cast_in, jobs):
        dsts = [next(outs) for _ in dests]
        tile = src[...].astype(BF16)
        if len(dsts) == 1:
            dsts[0][...] = tile
        else:
            c = jnp.minimum(step, n_tiles - 1) % ncb
            for dst, cols in zip(dsts, dests):
                owns = functools.reduce(jnp.logical_or, [c == m for m in cols])

                @pl.when(owns)
                def _(dst=dst):
                    dst[...] = tile


def _cast_tile(w, n_steps, tc=None, dests=None):
    rows, cols = w.shape
    if tc is None and rows % n_steps == 0 and (rows // n_steps) % 16 == 0:
        return w, (rows // n_steps, cols), [[0]]
    tc = math.gcd(tc or 512, cols)
    tr = 128
    while (rows // tr) * (cols // tc) > n_steps:
        tr *= 2
    return w, (tr, tc), dests or [list(range(cols // tc))]


def _ffn_kernel(*refs, tm, nj, n_meta, emit_norm, w_f32, head, jobs):
    it = iter(refs)
    take = lambda k: [next(it) for _ in range(k)]
    x_ref, = take(1)
    m_ref = take(1)[0] if n_meta else None
    g_ref, gn_ref, wg_ref, wu_ref, wd_ref = take(5)
    hx_ref, hn16_ref = take(2) if head else (None, None)
    cast_in = take(len(jobs))
    ox_ref, = take(1)
    nx_ref = take(1)[0] if emit_norm else None
    om_ref, nm_ref = take(2) if n_meta else (None, None)
    w16_out = take(3) if w_f32 else ()
    cast_out = take(_n_cast_out(jobs))
    hn_ref, = take(1)
    acc_ref = take(1)[0] if n_meta else ox_ref
    i = pl.program_id(0)
    j = pl.program_id(1)
    _run_cast_jobs(cast_in, cast_out, jobs)

    def step(first, last):
        if first:
            hn_ref[0:tm, :] = _rms(x_ref[...], g_ref[...]).astype(BF16)
            if n_meta:
                hn_ref[tm:tm + n_meta, :] = _rms(m_ref[...], g_ref[...]).astype(BF16)
        tiles = [w[...].astype(BF16) if w_f32 else w[...] for w in (wg_ref, wu_ref, wd_ref)]
        for dst, tile in zip(w16_out, tiles):
            dst[...] = tile
        hn = hn_ref[...]
        gate = _dot(hn, tiles[0])
        up = _dot(hn, tiles[1])
        act = (_silu(gate) * up).astype(BF16)
        if first:
            acc_ref[...] = _dot(act, tiles[2])
        else:
            acc_ref[...] += _dot(act, tiles[2])
        if last:
            hx = x_ref[...] + 0.5 * acc_ref[0:tm, :]
            if emit_norm:
                ox_ref[...] = hx
                nx_ref[...] = _rms(hx, gn_ref[...]).astype(BF16)
            else:
                ox_ref[...] = _rms(hx, gn_ref[...])
            if n_meta:
                hm = m_ref[...] + 0.5 * acc_ref[tm:tm + n_meta, :]
                om_ref[...] = hm
                nm_ref[...] = _rms(hm, gn_ref[...]).astype(BF16)

    def body():
        if nj == 1:
            step(True, True)
        else:
            pl.when(j == 0)(functools.partial(step, True, False))
            pl.when((j > 0) & (j < nj - 1))(functools.partial(step, False, False))
            pl.when(j == nj - 1)(functools.partial(step, False, True))

    if head:
        @pl.when((i == 0) & (j == 0))
        def _():
            ox_ref[...] = hx_ref[...]
            nx_ref[...] = hn16_ref[...]

        pl.when(i > 0)(body)
    else:
        body()


def _ffn(hx, hm, g, g_next, wg, wu, wd, *, emit_norm, head=None, cast=(), rows=None, name,
         tm=512, tf=512):
    n, d = hx.shape
    rows = n if rows is None else rows
    dff = wg.shape[1]
    n_meta = 0 if hm is None else hm.shape[0]
    w_f32 = wg.dtype == F32
    grid = (rows // tm, dff // tf)
    assert not w_f32 or grid[0] == 1
    row = lambda i, j: (i, 0)
    const = lambda i, j: (0, 0)
    in_specs = [pl.BlockSpec((tm, d), row)]
    args = [hx]
    if n_meta:
        in_specs.append(pl.BlockSpec((n_meta, d), const))
        args.append(hm)
    col = (lambda i, j: jnp.where(i == 0, 0, j)) if head else (lambda i, j: j)
    in_specs += [pl.BlockSpec((1, d), const), pl.BlockSpec((1, d), const),
                 pl.BlockSpec((d, tf), lambda i, j: (0, col(i, j))),
                 pl.BlockSpec((d, tf), lambda i, j: (0, col(i, j))),
                 pl.BlockSpec((tf, d), lambda i, j: (col(i, j), 0))]
    args += [g.reshape(1, d), g_next.reshape(1, d), wg, wu, wd]
    if head:
        in_specs += [pl.BlockSpec((tm, d), const, pipeline_mode=pl.Buffered(1))] * 2
        args += list(head)
    job_in, job_out, job_shapes, jobs = _cast_jobs(cast, grid)
    in_specs += job_in
    args += [src for src, _, _ in cast]
    out_shape = [jax.ShapeDtypeStruct((rows, d), F32)]
    out_specs = [pl.BlockSpec((tm, d), row)]
    if emit_norm:
        out_shape.append(jax.ShapeDtypeStruct((rows, d), BF16))
        out_specs.append(pl.BlockSpec((tm, d), row))
    if n_meta:
        out_shape += [jax.ShapeDtypeStruct((n_meta, d), F32), jax.ShapeDtypeStruct((n_meta, d), BF16)]
        out_specs += [pl.BlockSpec((n_meta, d), const)] * 2
    if w_f32:
        out_shape += [jax.ShapeDtypeStruct(w.shape, BF16) for w in (wg, wu, wd)]
        out_specs += [pl.BlockSpec((d, tf), lambda i, j: (0, j)),
                      pl.BlockSpec((d, tf), lambda i, j: (0, j)),
                      pl.BlockSpec((tf, d), lambda i, j: (j, 0))]
    out_specs += job_out
    out_shape += job_shapes
    return pl.pallas_call(
        functools.partial(_ffn_kernel, tm=tm, nj=grid[1], n_meta=n_meta, emit_norm=emit_norm,
                          w_f32=w_f32, head=head is not None, jobs=jobs),
        grid=grid, in_specs=in_specs, out_specs=out_specs, out_shape=out_shape,
        scratch_shapes=[pltpu.VMEM((tm + n_meta, d), BF16)]
        + ([pltpu.VMEM((tm + n_meta, d), F32)] if n_meta else []),
        compiler_params=_cparams(("arbitrary", "arbitrary")),
        name=name,
    )(*args)


def _inproj_kernel(*refs, tm, n_meta, with_lb, kinds, jobs):
    it = iter(refs)
    take = lambda k: [next(it) for _ in range(k)]
    x_ref, m_ref, w_ref, scale_ref = take(4)
    lb_ref = take(1)[0] if with_lb else None
    cast_in = take(len(jobs))
    ox_ref, om_ref = take(2)
    cast_out = take(_n_cast_out(jobs))
    lhs_ref, = take(1)
    _run_cast_jobs(cast_in, cast_out, jobs)

    def load_rows():
        lhs_ref[0:tm, :] = x_ref[...]
        lhs_ref[tm:tm + n_meta, :] = m_ref[...]

    tn = w_ref.shape[1]
    sw = tn // len(kinds[0])

    def column_step(step_kinds, first=False):
        if first:
            load_rows()
        for s, kind in enumerate(step_kinds):
            cs = slice(s * sw, (s + 1) * sw)
            res = _dot(lhs_ref[...], w_ref[:, cs])
            if kind == "ident":
                out = res
            elif kind == "silu":
                h = 0.5 * res
                out = (scale_ref[:, cs] * h) * (jnp.tanh(h) + 1.0)
            elif kind == "sigm":
                out = 0.5 * jnp.tanh(0.5 * res) + 0.5
            else:
                assert kind == "forget"
                lbl = lb_ref[:, cs]
                e = jnp.exp(lbl - jnp.max(lbl, axis=0, keepdims=True))
                lb = e[0:1, :] / jnp.sum(e, axis=0, keepdims=True)
                out = (0.5 - 0.5 * lb) * jnp.tanh(0.5 * res) + (0.5 + 0.5 * lb)
            ox_ref[:, cs] = out[0:tm, :].astype(ox_ref.dtype)
            om_ref[:, cs] = out[tm:tm + n_meta, :].astype(om_ref.dtype)

    if len(set(kinds)) == 1:
        pl.when(pl.program_id(1) == 0)(load_rows)
        column_step(kinds[0])
    else:
        for jj, step_kinds in enumerate(kinds):
            pl.when(pl.program_id(1) == jj)(functools.partial(column_step, step_kinds, jj == 0))


def _inproj_grid(n, cols, tm=1024, tn=1792):
    tm = min(tm, n)
    tn = math.gcd(tn, cols)
    assert tn % LANE == 0 and n % tm == 0
    return tm, tn, (n // tm, cols // tn)


def _inproj(nx, nm, w, segments, scale, lower_bounds, dtype, cast=()):
    n, d = nx.shape
    n_meta = nm.shape[0]
    cols = w.shape[1]
    with_lb = lower_bounds is not None
    tm, tn, grid = _inproj_grid(n, cols)
    sw = math.gcd(tn, INPROJ_SUB)
    col_kind = [kind for width, kind in segments for _ in range(width // sw)]
    assert all(width % sw == 0 for width, _ in segments) and len(col_kind) * sw == cols
    kinds = tuple(tuple(col_kind[jj * (tn // sw):(jj + 1) * (tn // sw)]) for jj in range(grid[1]))
    job_in, job_out, job_shapes, jobs = _cast_jobs(cast, grid)
    in_specs = [pl.BlockSpec((tm, d), lambda i, j: (i, 0)),
                pl.BlockSpec((n_meta, d), lambda i, j: (0, 0)),
                pl.BlockSpec((d, tn), lambda i, j: (0, j)),
                pl.BlockSpec((1, tn), lambda i, j: (0, j))]
    args = [nx, nm, w, scale]
    if with_lb:
        in_specs.append(pl.BlockSpec((lower_bounds.shape[0], tn), lambda i, j: (0, j)))
        args.append(lower_bounds)
    ox, om, *copies = pl.pallas_call(
        functools.partial(_inproj_kernel, tm=tm, n_meta=n_meta, with_lb=with_lb, kinds=kinds,
                          jobs=jobs),
        grid=grid,
        in_specs=in_specs + job_in,
        out_specs=[pl.BlockSpec((tm, tn), lambda i, j: (i, j)),
                   pl.BlockSpec((pl.Squeezed(), n_meta, tn), lambda i, j: (i, 0, j))] + job_out,
        out_shape=[jax.ShapeDtypeStruct((n, cols), dtype),
                   jax.ShapeDtypeStruct((n // tm, n_meta, cols), dtype)] + job_shapes,
        scratch_shapes=[pltpu.VMEM((tm + n_meta, d), BF16)],
        compiler_params=_cparams(("arbitrary", "arbitrary")),
        name="mixer_inproj_" + jnp.dtype(dtype).name,
    )(*args, *[src for src, _, _ in cast])
    return (ox, om[0], *copies)


def _split3(x):
    hi = x.astype(BF16)
    r1 = x - hi.astype(F32)
    mid = r1.astype(BF16)
    lo = (r1 - mid.astype(F32)).astype(BF16)
    return hi, mid, lo


def _dot3(m01, parts):
    return _dot(m01, parts[0]) + _dot(m01, parts[1]) + _dot(m01, parts[2])


def _tri(n, rep=1):
    r = lax.broadcasted_iota(jnp.int32, (n * rep, n), 0)
    c = lax.broadcasted_iota(jnp.int32, (n * rep, n), 1)
    return jnp.where(r >= c * rep, 1.0, 0.0).astype(BF16)


def _hgrn_tables(c):
    sums = [np.tril(np.ones((c, c)))]
    level = np.full((c, c), -1, np.int32)
    t_idx, s_idx = np.meshgrid(np.arange(c), np.arange(c), indexing="ij")
    halves = []
    h = c // 2
    while h >= 1:
        if h < 8:
            m = np.zeros((c, c))
            for r in range(c):
                mid = (r // (2 * h)) * 2 * h + h
                if r < mid:
                    m[r, r + 1:mid] = 1.0
                else:
                    m[r, mid:r + 1] = 1.0
            sums.append(m)
        pair = 2 * h
        level[(t_idx // pair == s_idx // pair) & (s_idx % pair < h) & (t_idx % pair >= h)] = len(halves)
        halves.append(h)
        h //= 2
    assert len(halves) % 2 == 0
    level[np.arange(c), np.arange(c)] = len(halves)
    owner = np.full((c, 2 * c), -1, np.int32)
    owner[:, :c] = np.where((level >= 0) & (level % 2 == 0), level // 2, -1)
    owner[:, c:] = np.where(level % 2 == 1, level // 2, -1)
    return np.concatenate(sums, axis=0), owner, tuple(halves)


def _hgrn_score_products(q, k, x, halves):
    C = HG_CHUNK
    b = x[0:C]
    zero = jnp.zeros((C, HG_DK), BF16)
    diag = jnp.sum(q * k, axis=-1, keepdims=True)
    qts, kts = [], []
    fine = 0
    for h in halves:
        if h >= 8:
            none = jnp.zeros((h, HG_DK), F32)
            qh, kh = [], []
            for r in range(0, C, 2 * h):
                b_mid = b[r + h - 1:r + h, :]
                kh += [k[r:r + h] * jnp.exp(b_mid - b[r:r + h]), none]
                qh += [none, q[r + h:r + 2 * h] * jnp.exp(b[r + h:r + 2 * h] - b_mid)]
            qts.append(jnp.concatenate(qh, axis=0).astype(BF16))
            kts.append(jnp.concatenate(kh, axis=0).astype(BF16))
        else:
            fine += 1
            e = jnp.exp(x[fine * C:(fine + 1) * C])
            qts.append((q * e).astype(BF16))
            kts.append((k * e).astype(BF16))
    prods = []
    for p in range(len(halves) // 2):
        lhs = jnp.concatenate([qts[2 * p], qts[2 * p + 1]], axis=1)
        rhs = jnp.concatenate([jnp.concatenate([kts[2 * p], zero], axis=1),
                               jnp.concatenate([zero, kts[2 * p + 1]], axis=1)], axis=0)
        prods.append(_dot_nt(lhs, rhs))
    return prods, diag


def _hgrn_pair(q2, f2, v2, gate2, st2, sums, owns, halves):
    C, D = HG_CHUNK, HG_DK
    halfs = (slice(0, D), slice(D, 2 * D))
    k2 = 1.0 - f2
    x2 = _dot(sums, jnp.concatenate(_split3(jnp.log(f2)), axis=0))
    yield
    b2 = x2[0:C]
    b_last = b2[C - 1:C, :]
    kdec2 = (k2 * jnp.exp(b_last - b2)).astype(BF16)
    qe2 = (q2 * jnp.exp(b2)).astype(BF16)
    st2b = st2.astype(BF16)
    scores = []
    for c in halfs:
        scores.append(_hgrn_score_products(q2[:, c], k2[:, c], x2[:, c], halves))
        yield
    zero = jnp.zeros((2 * C, D), BF16)
    os, vvts, kds = [], [], []
    for j, c in enumerate(halfs):
        prods, diag = scores[j]
        a = jnp.where(owns[len(prods)], diag, 0.0)
        for p, prod in enumerate(prods):
            a = jnp.where(owns[p], prod, a)
        v = v2[:, c]
        vvt = jnp.concatenate([v, v], axis=0).T
        os.append(_dot_nt(jnp.concatenate([qe2[:, c], a.astype(BF16)], axis=1),
                          jnp.concatenate([st2b[:, c], vvt], axis=1)))
        vvts.append(vvt)
        kd = jnp.concatenate([kdec2[:, c], jnp.zeros((C, D), BF16)], axis=0)
        kds.append(jnp.concatenate([kd, zero] if j == 0 else [zero, kd], axis=1))
        yield
    st_new = st2 * jnp.exp(b_last) + _dot(jnp.concatenate(vvts, axis=1), jnp.concatenate(kds, axis=0))
    yield
    ons = [o * lax.rsqrt(jnp.mean(o * o, axis=-1, keepdims=True) + EPS) for o in os]
    on2 = jnp.concatenate(ons, axis=1) * gate2
    return on2.astype(BF16), st_new


def _interleave(gens):
    out = [None] * len(gens)
    live = list(range(len(gens)))
    while live:
        for g in list(live):
            try:
                next(gens[g])
            except StopIteration as stop:
                out[g] = stop.value
                live.remove(g)
    return out


def _hgrn_kernel(q_ref, f_ref, v_ref, go_ref, fm_ref, vm_ref, sums_ref, owner_ref,
                 *rest, n_chunks, n_meta, hb, halves, jobs):
    cast_in, o_ref = rest[:len(jobs)], rest[len(jobs)]
    cast_out, st_ref = rest[len(jobs) + 1:-1], rest[-1]
    _run_cast_jobs(cast_in, cast_out, jobs)
    i = pl.program_id(0)
    p0 = pl.program_id(1) * (hb // 2)
    C = HG_CHUNK
    W = 2 * HG_DK

    @pl.when(i == 0)
    def _():
        tri_m = _tri(n_meta)
        for pair in range(hb // 2):
            sts = []
            for hh in (2 * pair, 2 * pair + 1):
                cols = slice(hh * HG_DK, (hh + 1) * HG_DK)
                fm = fm_ref[:, cols]
                bm = _dot3(tri_m, _split3(jnp.log(fm)))
                kdec = ((1.0 - fm) * jnp.exp(bm[n_meta - 1:n_meta, :] - bm)).astype(BF16)
                sts.append(_dot_tn(vm_ref[:, cols], kdec))
            st_ref[p0 + pair] = jnp.concatenate(sts, axis=1)

    def chunk(c, states):
        r0 = pl.multiple_of(c * C, C)
        owner = owner_ref[...]
        owns = [owner == p for p in range(len(halves) // 2 + 1)]
        sums = sums_ref[...]
        gens = []
        for pair in range(hb // 2):
            cols = slice(pair * W, (pair + 1) * W)
            gate2 = go_ref[pl.ds(r0, C), cols].astype(F32)
            gens.append(_hgrn_pair(q_ref[pl.ds(r0, C), cols], f_ref[pl.ds(r0, C), cols],
                                   v_ref[pl.ds(r0, C), cols], gate2, states[pair],
                                   sums, owns, halves))
        results = _interleave(gens)
        o_ref[pl.ds(r0, C), :] = jnp.concatenate([r[0] for r in results], axis=1)
        return tuple(r[1] for r in results)

    states = lax.fori_loop(0, n_chunks, chunk, tuple(st_ref[p0 + p] for p in range(hb // 2)))
    for pair in range(hb // 2):
        st_ref[p0 + pair] = states[pair]


def _hgrn_grid(n, width, rb=512, hb=12):
    heads = width // HG_DK
    hb = min(hb, heads)
    assert hb % 2 == 0 and heads % hb == 0
    return rb, hb, (n // rb, heads // hb)


def _hgrn(q, f, v, go, fm, vm, cast=()):
    n, width = q[0].shape[0], fm.shape[1]
    heads = width // HG_DK
    rb, hb, grid = _hgrn_grid(n, width)
    bw = hb * HG_DK
    n_meta = fm.shape[0]
    job_in, job_out, job_shapes, jobs = _cast_jobs(cast, grid)
    sums, owner, halves = _hgrn_tables(HG_CHUNK)
    sums3 = jnp.asarray(np.concatenate([sums] * 3, axis=1), BF16)
    blk = lambda i, h: (i, h)
    mblk = lambda i, h: (0, h)
    const = lambda i, h: (0, 0)

    def view(col0):
        assert col0 % bw == 0
        return pl.BlockSpec((rb, bw), lambda i, h: (i, col0 // bw + h))

    return pl.pallas_call(
        functools.partial(_hgrn_kernel, n_chunks=rb // HG_CHUNK, n_meta=n_meta, hb=hb,
                          halves=halves, jobs=jobs),
        grid=grid,
        in_specs=[view(q[1]), view(f[1]), view(v[1]), view(go[1])]
        + [pl.BlockSpec((n_meta, hb * HG_DK), mblk)] * 2
        + [pl.BlockSpec(sums3.shape, const), pl.BlockSpec(owner.shape, const)] + job_in,
        out_specs=[pl.BlockSpec((rb, hb * HG_DK), blk)] + job_out,
        out_shape=[jax.ShapeDtypeStruct((n, width), BF16)] + job_shapes,
        scratch_shapes=[pltpu.VMEM((heads // 2, HG_DK, 2 * HG_DK), F32)],
        compiler_params=_cparams(("arbitrary", "arbitrary")),
        name="hgrn2",
    )(q[0], f[0], v[0], go[0], fm, vm, sums3, jnp.asarray(owner), *[src for src, _, _ in cast])


def _s5_weights_kernel(lr_ref, li_ref, dt_ref, bre_ref, bim_ref, cre_ref, cim_ref, dsk_ref,
                       kt_ref, win_ref, wout_ref, at_ref):
    T = S5_T
    ns = lr_ref.shape[1]
    nc = cre_ref.shape[0]
    lr, li, dt = lr_ref[...], li_ref[...], dt_ref[...]

    def powers(tau):
        mag = jnp.exp(lr * dt * tau)
        ang = li * dt * tau
        return mag * jnp.cos(ang), mag * jnp.sin(ang)

    a_re, a_im = powers(1.0)
    num_re, num_im = a_re - 1.0, a_im
    den = lr * lr + li * li
    coef_re = (num_re * lr + num_im * li) / den
    coef_im = (num_im * lr - num_re * li) / den
    br, bi = bre_ref[...], bim_ref[...]
    bbar_re = coef_re * br - coef_im * bi
    bbar_im = coef_re * bi + coef_im * br

    pw = [powers(float(tau)) for tau in range(T + 1)]
    for s in range(T):
        p_re, p_im = pw[T - 1 - s]
        win_ref[s * nc:(s + 1) * nc, 0:ns] = (bbar_re * p_re - bbar_im * p_im).astype(BF16)
        win_ref[s * nc:(s + 1) * nc, ns:2 * ns] = (bbar_re * p_im + bbar_im * p_re).astype(BF16)
    at_ref[:, 0:ns] = pw[T][0]
    at_ref[:, ns:2 * ns] = pw[T][1]

    cr, ci = cre_ref[...], cim_ref[...]
    def split2(a):
        hi = a.astype(BF16)
        return hi, (a - hi.astype(F32)).astype(BF16)

    bb_hi, bb_lo = split2(jnp.concatenate([bbar_re, -bbar_im], axis=1))
    for tau in range(T + 1):
        p_re, p_im = pw[tau]
        ca_re = cr * p_re - ci * p_im
        ca_im = cr * p_im + ci * p_re
        if tau >= 1:
            t = tau - 1
            wout_ref[0:ns, t * nc:(t + 1) * nc] = ca_re.T.astype(BF16)
            wout_ref[ns:2 * ns, t * nc:(t + 1) * nc] = (-ca_im).T.astype(BF16)
        if tau < T:
            ca_hi, ca_lo = split2(jnp.concatenate([ca_re, ca_im], axis=1))
            kt = _dot_nt(bb_hi, ca_hi) + _dot_nt(bb_hi, ca_lo) + _dot_nt(bb_lo, ca_hi)
            if tau == 0:
                kt = kt + dsk_ref[...]
            kt_ref[tau] = kt.astype(BF16)


def _s5_kernel(lr_ref, li_ref, dt_ref, bre_ref, bim_ref, cre_ref, cim_ref, dsk_ref, u_ref, um_ref,
               y_ref, kt_ref, win_ref, wout_ref, at_ref, uc_ref, v_ref, xp_ref, m_ref, *, n_chunks):
    _s5_weights_kernel(lr_ref, li_ref, dt_ref, bre_ref, bim_ref, cre_ref, cim_ref, dsk_ref,
                       kt_ref, win_ref, wout_ref, at_ref)
    _s5_scan_kernel(u_ref, um_ref, kt_ref, win_ref, wout_ref, at_ref, y_ref,
                    uc_ref, v_ref, xp_ref, m_ref, n_chunks=n_chunks)


def _s5(u, um, lam_re, lam_im, log_step, b_re, b_im, c_re, c_im, d_skip):
    u, col0 = u
    n = u.shape[0]
    G, N = lam_re.shape
    P = d_skip.shape[1]
    nsg = G // S5_SG
    ns, nc = S5_SG * N, S5_SG * P
    T = S5_T
    dt = jnp.exp(log_step.astype(F32))
    eye = jnp.eye(S5_SG, dtype=F32)

    def rows(a):
        return a.astype(F32).reshape(nsg, 1, ns)

    def embed_b(b):
        b = b.astype(F32).reshape(nsg, S5_SG, N, P)
        return jnp.einsum('zgnq,gh->zhqgn', b, eye).reshape(nsg, nc, ns)

    def embed_c(c):
        c = c.astype(F32).reshape(nsg, S5_SG, P, N)
        return jnp.einsum('zgpn,gh->zhpgn', c, eye).reshape(nsg, nc, ns)

    dtb = jnp.broadcast_to(dt[:, None], (G, N))
    d_diag = jax.vmap(jnp.diag)(d_skip.astype(F32).reshape(nsg, nc))
    assert um.shape == (T, nsg * nc) and n % T == 0 and col0 % nc == 0
    n_chunks = n // T
    n_rows = -(-(n_chunks + 1) // 16) * 16
    sq = pl.Squeezed()
    spec_r = pl.BlockSpec((sq, 1, ns), lambda z: (z, 0, 0))
    spec_b = pl.BlockSpec((sq, nc, ns), lambda z: (z, 0, 0))
    return pl.pallas_call(
        functools.partial(_s5_kernel, n_chunks=n_chunks),
        grid=(nsg,),
        in_specs=[spec_r, spec_r, spec_r, spec_b, spec_b, spec_b, spec_b,
                  pl.BlockSpec((sq, nc, nc), lambda z: (z, 0, 0)),
                  pl.BlockSpec((n, nc), lambda z: (0, col0 // nc + z)),
                  pl.BlockSpec((T, nc), lambda z: (0, z))],
        out_specs=pl.BlockSpec((n, nc), lambda z: (0, z)),
        out_shape=jax.ShapeDtypeStruct((n, nsg * nc), F32),
        scratch_shapes=[pltpu.VMEM((T, nc, nc), BF16), pltpu.VMEM((T * nc, 2 * ns), BF16),
                        pltpu.VMEM((2 * ns, T * nc), BF16), pltpu.VMEM((1, 2 * ns), F32),
                        pltpu.VMEM((n_rows, T * nc), BF16), pltpu.VMEM((n_rows, 2 * ns), F32),
                        pltpu.VMEM((n_rows, 2 * ns), F32), pltpu.VMEM((T * nc, T * nc), BF16)],
        compiler_params=_cparams(("arbitrary",)),
        name="s5",
    )(rows(lam_re), rows(lam_im), rows(dtb),
      embed_b(b_re), embed_b(b_im), embed_c(c_re), embed_c(c_im), d_diag, u, um)


def _s5_scan_kernel(u_ref, um_ref, kt_ref, win_ref, wout_ref, at_ref, y_ref,
                    uc_ref, v_ref, xp_ref, m_ref, *, n_chunks):
    T = S5_T
    nc = u_ref.shape[1]
    ns = at_ref.shape[1] // 2

    @pl.when(pl.program_id(0) == 0)
    def _():
        zero = jnp.zeros((nc, nc), BF16)
        for s in range(1, T):
            for t in range(s):
                m_ref[s * nc:(s + 1) * nc, t * nc:(t + 1) * nc] = zero

    for tau in range(T):
        tap = kt_ref[tau]
        for s in range(T - tau):
            m_ref[s * nc:(s + 1) * nc, (s + tau) * nc:(s + tau + 1) * nc] = tap

    for t in range(T):
        uc_ref[0:n_chunks, t * nc:(t + 1) * nc] = u_ref[pl.ds(t, n_chunks, stride=T), :].astype(BF16)
        uc_ref[n_chunks:n_chunks + 1, t * nc:(t + 1) * nc] = um_ref[t:t + 1, :].astype(BF16)
    pad = uc_ref.shape[0] - n_chunks - 1
    uc_ref[n_chunks + 1:, :] = jnp.zeros((pad, T * nc), BF16)
    uc = uc_ref[...]
    v_ref[...] = _dot(uc, win_ref[...])
    a_re, a_im = at_ref[:, 0:ns], at_ref[:, ns:2 * ns]

    x_re0 = v_ref[n_chunks:n_chunks + 1, 0:ns]
    x_im0 = v_ref[n_chunks:n_chunks + 1, ns:2 * ns]
    xp_ref[n_chunks:, :] = jnp.zeros((pad + 1, 2 * ns), F32)

    def step(c, carry):
        x_re, x_im = carry
        xp_ref[pl.ds(c, 1), 0:ns] = x_re
        xp_ref[pl.ds(c, 1), ns:2 * ns] = x_im
        v_re = v_ref[pl.ds(c, 1), 0:ns]
        v_im = v_ref[pl.ds(c, 1), ns:2 * ns]
        return (a_re * x_re - a_im * x_im + v_re, a_re * x_im + a_im * x_re + v_im)

    lax.fori_loop(0, n_chunks, step, (x_re0, x_im0))
    xp = xp_ref[...].astype(BF16)
    for p in range(T // 2):
        k_hi = (2 * p + 2) * nc
        cols = slice(2 * p * nc, k_hi)
        yc = _dot(uc_ref[:, 0:k_hi], m_ref[0:k_hi, cols]) + _dot(xp, wout_ref[:, cols])
        for t in (2 * p, 2 * p + 1):
            y_ref[pl.ds(t, n_chunks, stride=T), :] = yc[0:n_chunks, (t - 2 * p) * nc:(t - 2 * p + 1) * nc]


def _merge_kernel(*refs, pieces):
    h_ref, o_ref, y_ref = refs[0:3]
    gh_refs, gs_refs = refs[3:3 + pieces], refs[3 + pieces:3 + 2 * pieces]
    wp_ref, wa_ref, wb_ref, wo_ref, out_ref = refs[3 + 2 * pieces:]
    gh = jnp.concatenate([r[...] for r in gh_refs], axis=1).astype(F32)
    gs = jnp.concatenate([r[...] for r in gs_refs], axis=1).astype(F32)
    y_hg = _dot(o_ref[...], wp_ref[...])
    z = jax.nn.gelu(y_ref[...]).astype(BF16)
    y_s5 = _dot(z, wa_ref[...]) * _sigmoid(_dot(z, wb_ref[...]))
    merged = gh * y_hg + gs * y_s5
    out_ref[...] = h_ref[...] + _dot(merged.astype(BF16), wo_ref[...])


def _merge(hx, o, y, gh, gs, wp, wa, wb, wo, *, tm=256):
    n, d = hx.shape
    row = lambda i: (i, 0)
    const = lambda i: (0, 0)
    gw = functools.reduce(math.gcd, (gh[1], gs[1], d))
    assert gw % LANE == 0
    pieces = d // gw

    def resident(shape):
        return pl.BlockSpec(shape, const, pipeline_mode=pl.Buffered(1))

    def view(col0):
        return [pl.BlockSpec((tm, gw), functools.partial(lambda i, k: (i, k), k=col0 // gw + k))
                for k in range(pieces)]

    return pl.pallas_call(
        functools.partial(_merge_kernel, pieces=pieces),
        grid=(n // tm,),
        in_specs=[pl.BlockSpec((tm, d), row), pl.BlockSpec((tm, o.shape[1]), row),
                  pl.BlockSpec((tm, y.shape[1]), row)] + view(gh[1]) + view(gs[1])
        + [resident(wp.shape), resident(wa.shape), resident(wb.shape), resident(wo.shape)],
        out_specs=pl.BlockSpec((tm, d), row),
        out_shape=jax.ShapeDtypeStruct((n, d), F32),
        compiler_params=_cparams(("arbitrary",)),
        name="mixer_merge",
    )(hx, o, y, *([gh[0]] * pieces), *([gs[0]] * pieces), wp, wa, wb, wo)


def kernel(x, meta_tokens, norm_ffn1, ffn1_w_gate, ffn1_w_up, ffn1_w_down, norm_mix, w_in, hg_lower_bounds, hg_norm, hg_w_proj, s5_lam_re, s5_lam_im, s5_log_step, s5_b_re, s5_b_im, s5_c_re, s5_c_im, s5_d, s5_w_glu_a, s5_w_glu_b, w_out, norm_ffn2, ffn2_w_gate, ffn2_w_up, ffn2_w_down, norm_final):
    batch, seq, d = x.shape
    depth = norm_ffn1.shape[0]
    assert batch == 1 and depth == 1
    hg_k = hg_lower_bounds.shape[1]
    hg_w = hg_norm.shape[1]
    s5_w = s5_w_glu_a.shape[1]
    hx = x[0].astype(F32)
    hm = meta_tokens.astype(F32)
    tm = min(512, seq)
    h0, n0, hm, nm, w1_gate, w1_up, w1_down = _ffn(
        hx, hm, norm_ffn1[0], norm_mix[0], ffn1_w_gate[0], ffn1_w_up[0], ffn1_w_down[0],
        emit_norm=True, rows=tm, tm=tm, tf=256, name="ffn_head")
    sizes = (hg_k, hg_k, hg_w, hg_w, s5_w, d, d)
    assert sum(sizes) == w_in.shape[2]
    off = [0]
    for s in sizes:
        off.append(off[-1] + s)
    ids32, ids16 = (0, 1, 4), (2, 3, 5, 6)
    tc_in = functools.reduce(math.gcd, sizes + (512,))

    def starts(ids):
        out, col = {}, 0
        for k in ids:
            out[k] = col
            col += sizes[k]
        return out

    def blocks(ids):
        return [c for k in ids for c in range(off[k] // tc_in, off[k + 1] // tc_in)]

    steps = lambda grid: grid[0] * grid[1]
    n_body = (seq // tm) * (ffn1_w_gate.shape[2] // 512)
    n_inp = steps(_inproj_grid(seq, sum(sizes[k] for k in ids16))[2])
    n_hg = steps(_hgrn_grid(seq, hg_w)[2])
    hx, nx, w32, w16 = _ffn(
        hx, None, norm_ffn1[0], norm_mix[0], w1_gate, w1_up, w1_down, emit_norm=True, head=(h0, n0),
        cast=(_cast_tile(w_in[0], n_body, tc_in, [blocks(ids32), blocks(ids16)]),),
        tm=tm, name="ffn_body")

    kind = ("silu", "forget", "ident", "silu", "ident", "sigm", "sigm")
    scale = {0: HG_DK ** -0.5, 3: hg_norm[0].astype(F32)}

    def segments(ids):
        return tuple((sizes[k], kind[k]) for k in ids)

    def scale_row(ids):
        return jnp.concatenate([jnp.broadcast_to(jnp.asarray(scale.get(k, 1.0), F32), (sizes[k],))
                                for k in ids])[None]

    s32, s16 = starts(ids32), starts(ids16)
    lb32 = jnp.pad(hg_lower_bounds.astype(F32),
                   ((0, 0), (s32[1], sum(sizes[k] for k in ids32) - s32[1] - hg_k)))
    x32, m32 = _inproj(nx, nm, w32, segments(ids32), scale_row(ids32), lb32, F32)
    x16, m16, w_proj, w_glu_a, w_glu_b, w_o = _inproj(
        nx, nm, w16, segments(ids16), scale_row(ids16), None, BF16,
        cast=[_cast_tile(w, n_inp) for w in (hg_w_proj[0], s5_w_glu_a[0], s5_w_glu_b[0], w_out[0])])

    o, w2_gate, w2_up, w2_down = _hgrn(
        (x32, s32[0]), (x32, s32[1]), (x16, s16[2]), (x16, s16[3]),
        m32[:, s32[1]:s32[1] + hg_k], m16[:, s16[2]:s16[2] + hg_w],
        cast=[_cast_tile(w, n_hg) for w in (ffn2_w_gate[0], ffn2_w_up[0], ffn2_w_down[0])])
    y = _s5((x32, s32[4]), m32[:, s32[4]:s32[4] + s5_w], s5_lam_re[0], s5_lam_im[0], s5_log_step[0],
            s5_b_re[0], s5_b_im[0], s5_c_re[0], s5_c_im[0], s5_d[0])
    hx = _merge(hx, o, y, (x16, s16[5]), (x16, s16[6]), w_proj, w_glu_a, w_glu_b, w_o)
    out, = _ffn(hx, None, norm_ffn2[0], norm_final, w2_gate, w2_up, w2_down, emit_norm=False,
                tm=tm, name="ffn_final")
    return out[None].astype(x.dtype)
```
